```python
import jax
import jax.numpy as jnp
from jax import lax
import numpy as np

D_MODEL = 1024
BATCH = 8
SEQ = 4096
DEPTH = 4

GRID_W = 64
CTX_LEN = 256
EPS = 1e-6
ROPE_THETA = 10000.0
NEG_BIG = -1e30
F_FLOOR = 1e-20

HEAD_DIM = D_MODEL // 16
GROUP_HEADS = 4
GROUP_W = GROUP_HEADS * HEAD_DIM
D_MIX = 4 * GROUP_W

MLA_H = GROUP_HEADS
MLA_Q_LORA = 3 * D_MODEL // 16
MLA_KV_LORA = D_MODEL // 8
MLA_NOPE = HEAD_DIM
MLA_ROPE = HEAD_DIM // 2
MLA_V = HEAD_DIM
MLA_SCALE = (MLA_NOPE + MLA_ROPE) ** -0.5
ATTN_BLOCK = 128
NA_H = GROUP_HEADS
NA_KH_MAX = 8
NA_KW = 16
NA_SCALE = HEAD_DIM ** -0.5
RET_H = GROUP_HEADS
RET_DK = HEAD_DIM
RET_DV = HEAD_DIM
RET_CHUNK = 128
HG_H = GROUP_HEADS
HG_DK = HEAD_DIM
HG_DV = HEAD_DIM
HG_CHUNK = 16
MOE_GROUPS = 4
MOE_PER_GROUP = 8
MOE_EXPERTS = MOE_GROUPS * MOE_PER_GROUP
MOE_TOPK = 2
MOE_FF = D_MODEL // 2
MOE_BLOCK = 256

IN_SPLITS = (MLA_Q_LORA, MLA_KV_LORA, MLA_ROPE,
             NA_H * HEAD_DIM, NA_H * HEAD_DIM, NA_H * HEAD_DIM,
             RET_H * RET_DK, RET_H * RET_DK, RET_H * RET_DV, RET_H * RET_DV,
             HG_H * HG_DK, HG_H * HG_DK, HG_H * HG_DK, HG_H * HG_DV, HG_H * HG_DV)
P_TOT = sum(IN_SPLITS)

kernel_name = 'hybrid_mla_na_retnet_hgrn2_hmoe_dit'


def rms_norm(x, gain=None):
    x32 = x.astype(jnp.float32)
    y = x32 * lax.rsqrt(jnp.mean(jnp.square(x32), axis=-1, keepdims=True) + EPS)
    if gain is not None:
        y = y * gain.astype(jnp.float32)
    return y.astype(x.dtype)


def modulate(x, shift, scale):
    return rms_norm(x) * (1 + scale) + shift


def split_cols(p):
    offsets = [int(o) for o in np.cumsum(IN_SPLITS)[:-1]]
    return jnp.split(p, offsets, axis=-1)


def split_heads(t, n_heads):
    b, n, _ = t.shape
    return t.reshape(b, n, n_heads, -1).transpose(0, 2, 1, 3)


def flip_t(t):
    return jnp.flip(t, axis=2)


def gated_head_norm(o, g, gain):
    b, h, n, dv = o.shape
    return rms_norm(o.transpose(0, 2, 1, 3), gain).reshape(b, n, h * dv) * jax.nn.silu(g)


def axial_rope_tables(n_tokens):
    pos = jnp.arange(n_tokens)
    rows = (pos // GRID_W).astype(jnp.float32)
    cols = (pos % GRID_W).astype(jnp.float32)
    per_axis = MLA_ROPE // 2
    inv_freq = ROPE_THETA ** (-jnp.arange(0, per_axis, 2, dtype=jnp.float32) / per_axis)
    ang = jnp.concatenate([rows[:, None] * inv_freq, cols[:, None] * inv_freq], axis=-1)
    return jnp.cos(ang), jnp.sin(ang)


def _rotate_pairs(z, cos, sin):
    m = z.shape[-1] // 2
    z1, z2 = z[..., :m], z[..., m:]
    return jnp.concatenate([z1 * cos - z2 * sin, z2 * cos + z1 * sin], axis=-1)


def apply_axial_rope(x, cos, sin):
    half = x.shape[-1] // 2
    m = half // 2
    cos = cos.astype(x.dtype)
    sin = sin.astype(x.dtype)
    return jnp.concatenate([_rotate_pairs(x[..., :half], cos[..., :m], sin[..., :m]),
                            _rotate_pairs(x[..., half:], cos[..., m:], sin[..., m:])], axis=-1)


def mla_queries(cq, g_cq, w_uq, g_qn, g_qr):
    b, n, _ = cq.shape
    q = (rms_norm(cq, g_cq) @ w_uq).reshape(b, n, MLA_H, MLA_NOPE + MLA_ROPE)
    return rms_norm(q[..., :MLA_NOPE], g_qn), rms_norm(q[..., MLA_NOPE:], g_qr)


def mla_keys(ckv, kr, g_ckv, w_ukv, g_kn, g_kr):
    b, n, _ = ckv.shape
    kv = (rms_norm(ckv, g_ckv) @ w_ukv).reshape(b, n, MLA_H, MLA_NOPE + MLA_V)
    return rms_norm(kv[..., :MLA_NOPE], g_kn), rms_norm(kr, g_kr), kv[..., MLA_NOPE:]


def mla_scores(q_nope, q_rope, k_nope, k_rope):
    s = jnp.einsum('bqhd,bkhd->bhqk', q_nope, k_nope) + jnp.einsum('bqhr,bkr->bhqk', q_rope, k_rope)
    return s.astype(jnp.float32) * MLA_SCALE


def mla_mixer(lat, ctx, cos, sin, q_prm, k_prm, with_ctx_out):
    cq, ckv, kr = lat
    cq_c, ckv_c, kr_c = ctx
    qn, qr = mla_queries(cq, *q_prm)
    kn, kr, v = mla_keys(ckv, kr, *k_prm)
    kn_c, kr_c, v_c = mla_keys(ckv_c, kr_c, *k_prm)
    qr = apply_axial_rope(qr, cos[:, None, :], sin[:, None, :])
    kr = apply_axial_rope(kr, cos, sin)
    b, n = qn.shape[:2]
    nb = n // ATTN_BLOCK

    def to_blocks(t):
        return jnp.moveaxis(t.reshape(b, nb, ATTN_BLOCK, *t.shape[2:]), 1, 0)

    def attend_block(blk):
        qn_b, qr_b = blk
        s = jnp.concatenate([mla_scores(qn_b, qr_b, kn, kr), mla_scores(qn_b, qr_b, kn_c, kr_c)], axis=-1)
        p = jax.nn.softmax(s, axis=-1).astype(v.dtype)
        return (jnp.einsum('bhqk,bkhd->bqhd', p[..., :n], v)
                + jnp.einsum('bhqk,bkhd->bqhd', p[..., n:], v_c))

    o = lax.map(attend_block, (to_blocks(qn), to_blocks(qr)))
    y = jnp.moveaxis(o, 0, 1).reshape(b, n, MLA_H * MLA_V)
    if not with_ctx_out:
        return y, None
    qn_c, qr_c = mla_queries(cq_c, *q_prm)
    p_c = jax.nn.softmax(mla_scores(qn_c, qr_c, kn_c, kr_c), axis=-1).astype(v_c.dtype)
    y_c = jnp.einsum('bhqk,bkhd->bqhd', p_c, v_c).reshape(b, v_c.shape[1], MLA_H * MLA_V)
    return y, y_c


def na_mixer(lat, ctx, rows, g_q, g_k, rpb, with_ctx_out):
    q, k, v = lat
    qc, kc, vc = ctx
    b, n, _ = q.shape
    kh = min(NA_KH_MAX, rows)
    kw = min(NA_KW, GRID_W)

    def heads(t):
        return t.reshape(t.shape[0], t.shape[1], NA_H, HEAD_DIM)

    qg = rms_norm(heads(q), g_q).reshape(b, rows, GRID_W, NA_H, HEAD_DIM)
    kg = rms_norm(heads(k), g_k).reshape(b, rows, GRID_W, NA_H, HEAD_DIM)
    vg = heads(v).reshape(b, rows, GRID_W, NA_H, HEAD_DIM)
    k_c = rms_norm(heads(kc), g_k)
    v_c = heads(vc)
    r = jnp.arange(rows)
    w = jnp.arange(GRID_W)
    row_idx = jnp.clip(r - kh // 2, 0, rows - kh)[:, None] + jnp.arange(kh)[None, :]
    col_start = jnp.clip(w - kw // 2, 0, GRID_W - kw)
    col_valid = (w[None, :] >= col_start[:, None]) & (w[None, :] < col_start[:, None] + kw)
    k_band = kg[:, row_idx]
    v_band = vg[:, row_idx]
    dr = row_idx - r[:, None] + (NA_KH_MAX - 1)
    dc = jnp.clip(w[None, :] - w[:, None], 1 - kw, kw - 1) + (NA_KW - 1)
    bias = rpb[:, dr[:, None, :, None], dc[None, :, None, :]].astype(jnp.float32)
    s_win = jnp.einsum('brwhd,brkuhd->bhrwku', qg, k_band).astype(jnp.float32) * NA_SCALE + bias
    s_win = jnp.where(col_valid[:, None, :], s_win, NEG_BIG)
    s_ctx = jnp.einsum('brwhd,bchd->bhrwc', qg, k_c).astype(jnp.float32) * NA_SCALE
    n_win = kh * GRID_W
    s = jnp.concatenate([s_win.reshape(b, NA_H, rows, GRID_W, n_win), s_ctx], axis=-1)
    p = jax.nn.softmax(s, axis=-1).astype(vg.dtype)
    p_win = p[..., :n_win].reshape(b, NA_H, rows, GRID_W, kh, GRID_W)
    o = (jnp.einsum('bhrwku,brkuhd->brwhd', p_win, v_band)
         + jnp.einsum('bhrwc,bchd->brwhd', p[..., n_win:], v_c))
    y = o.reshape(b, n, NA_H * HEAD_DIM)
    if not with_ctx_out:
        return y, None
    q_c = rms_norm(heads(qc), g_q)
    p_c = jax.nn.softmax(jnp.einsum('bqhd,bkhd->bhqk', q_c, k_c).astype(jnp.float32) * NA_SCALE, axis=-1)
    y_c = jnp.einsum('bhqk,bkhd->bqhd', p_c.astype(v_c.dtype), v_c).reshape(b, qc.shape[1], NA_H * HEAD_DIM)
    return y, y_c


def retention_log_decays():
    j = jnp.arange(2 * RET_H, dtype=jnp.float32)
    lg = jnp.log1p(-jnp.exp2(-5.0 - j))
    return lg[0::2], lg[1::2]


def retention_scan(q, k, v, log_gamma, s0):
    b, h, n, dk = k.shape
    dv = v.shape[-1]
    dt = k.dtype
    cs = min(RET_CHUNK, n)
    nc = n // cs
    kc = k.reshape(b, h, nc, cs, dk)
    vc = v.reshape(b, h, nc, cs, dv)
    pos = jnp.arange(cs, dtype=jnp.float32)
    lg = log_gamma[:, None]
    k_w = jnp.exp((cs - 1 - pos)[None] * lg).astype(dt)
    d_state = jnp.einsum('bhncd,bhnce->bhnde', kc * k_w[None, :, None, :, None], vc)
    chunk_decay = jnp.exp(cs * log_gamma).astype(dt)[None, :, None, None]

    def step(s, ds):
        return chunk_decay * s + ds, s

    s_final, s_prev = lax.scan(step, s0, jnp.moveaxis(d_state, 2, 0))
    if q is None:
        return None, s_final
    qc = q.reshape(b, h, nc, cs, dk)
    diff = pos[:, None] - pos[None, :]
    decay = jnp.where(diff >= 0, jnp.exp(jnp.maximum(diff, 0.0)[None] * log_gamma[:, None, None]), 0.0).astype(dt)
    q_w = jnp.exp((pos + 1)[None] * lg).astype(dt)
    scores = jnp.einsum('bhntd,bhnsd->bhnts', qc, kc) * decay[None, :, None]
    o = (jnp.einsum('bhnts,bhnse->bhnte', scores, vc)
         + jnp.einsum('bhntd,bhnde->bhnte', qc * q_w[None, :, None, :, None], jnp.moveaxis(s_prev, 0, 2)))
    return o.reshape(b, h, n, dv), s_final


def retention_mixer(lat, ctx, g_out, with_ctx_out):
    q, k, v, g = lat
    qc, kc, vc, gc = ctx
    lg_f, lg_b = retention_log_decays()
    k_scale = RET_DK ** -0.5
    qh, kh, vh = split_heads(q, RET_H), split_heads(k, RET_H) * k_scale, split_heads(v, RET_H)
    qch = split_heads(qc, RET_H) if with_ctx_out else None
    kch, vch = split_heads(kc, RET_H) * k_scale, split_heads(vc, RET_H)
    zero = jnp.zeros((q.shape[0], RET_H, RET_DK, RET_DV), k.dtype)
    oc_f, s_f = retention_scan(qch, kch, vch, lg_f, zero)
    oc_b, s_b = retention_scan(None if qch is None else flip_t(qch), flip_t(kch), flip_t(vch), lg_b, zero)
    o_f, _ = retention_scan(qh, kh, vh, lg_f, s_f)
    o_b, _ = retention_scan(flip_t(qh), flip_t(kh), flip_t(vh), lg_b, s_b)
    y = gated_head_norm(o_f + flip_t(o_b), g, g_out)
    if not with_ctx_out:
        return y, None
    return y, gated_head_norm(oc_f + flip_t(oc_b), gc, g_out)


def hgrn2_log_forget(f_raw, lb):
    lb = lb.astype(jnp.float32)
    f = lb + (1.0 - lb) * jax.nn.sigmoid(f_raw.astype(jnp.float32))
    return split_heads(jnp.log(jnp.maximum(f, F_FLOOR)), HG_H)


def chunk_gla(q, k, v, log_f, s0):
    b, h, n, dk = k.shape
    dv = v.shape[-1]
    dt = k.dtype
    cs = min(HG_CHUNK, n)
    nc = n // cs
    kc = k.reshape(b, h, nc, cs, dk)
    vc = v.reshape(b, h, nc, cs, dv)
    a = jnp.cumsum(log_f.reshape(b, h, nc, cs, dk), axis=3)
    a_last = a[:, :, :, -1]
    k_state = kc * jnp.exp(a_last[:, :, :, None] - a).astype(dt)
    d_state = jnp.einsum('bhnsd,bhnse->bhnde', k_state, vc)
    chunk_decay = jnp.moveaxis(jnp.exp(a_last).astype(dt), 2, 0)[..., None]

    def step(s, inp):
        dec, ds = inp
        return dec * s + ds, s

    s_final, s_prev = lax.scan(step, s0, (chunk_decay, jnp.moveaxis(d_state, 2, 0)))
    if q is None:
        return None, s_final
    qc = q.reshape(b, h, nc, cs, dk)
    tri = jnp.tril(jnp.ones((cs, cs), bool))[:, :, None]
    rel = a[:, :, :, :, None, :] - a[:, :, :, None, :, :]
    decay = jnp.where(tri, jnp.exp(jnp.where(tri, rel, 0.0)), 0.0).astype(dt)
    scores = jnp.einsum('bhntd,bhnsd,bhntsd->bhnts', qc, kc, decay)
    o = (jnp.einsum('bhnts,bhnse->bhnte', scores, vc)
         + jnp.einsum('bhntd,bhnde->bhnte', qc * jnp.exp(a).astype(dt), jnp.moveaxis(s_prev, 0, 2)))
    return o.reshape(b, h, n, dv), s_final


def hgrn2_mixer(lat, ctx, lb, g_out, with_ctx_out):
    q, f_fw, f_bw, i, g = lat
    qc, fc_fw, fc_bw, ic, gc = ctx
    dt = q.dtype

    def gates(f_raw):
        lf = hgrn2_log_forget(f_raw, lb)
        return (-jnp.expm1(lf)).astype(dt), lf

    qh, vh = split_heads(jax.nn.silu(q), HG_H), split_heads(i, HG_H)
    k_f, lf_f = gates(f_fw)
    k_b, lf_b = gates(f_bw)
    qch = split_heads(jax.nn.silu(qc), HG_H) if with_ctx_out else None
    vch = split_heads(ic, HG_H)
    kc_f, lfc_f = gates(fc_fw)
    kc_b, lfc_b = gates(fc_bw)
    zero = jnp.zeros((q.shape[0], HG_H, HG_DK, HG_DV), dt)
    oc_f, s_f = chunk_gla(qch, kc_f, vch, lfc_f, zero)
    oc_b, s_b = chunk_gla(None if qch is None else flip_t(qch), flip_t(kc_b), flip_t(vch), flip_t(lfc_b), zero)
    o_f, _ = chunk_gla(qh, k_f, vh, lf_f, s_f)
    o_b, _ = chunk_gla(flip_t(qh), flip_t(k_b), flip_t(vh), flip_t(lf_b), s_b)
    y = gated_head_norm(o_f + flip_t(o_b), g, g_out)
    if not with_ctx_out:
        return y, None
    return y, gated_head_norm(oc_f + flip_t(oc_b), gc, g_out)


def hier_moe(h, w_rg, b_rg, w_re, b_re, w1, w3, w2):
    n, d = h.shape
    p_group = jax.nn.softmax((h @ w_rg + b_rg).astype(jnp.float32), axis=-1)
    pg_top, g_idx = lax.top_k(p_group, 1)
    fine = (h @ w_re + b_re).astype(jnp.float32).reshape(n, MOE_GROUPS, MOE_PER_GROUP)
    fine = fine[jnp.arange(n), g_idx[:, 0]]
    pe_top, e_local = lax.top_k(jax.nn.softmax(fine, axis=-1), MOE_TOPK)
    gate = pg_top * pe_top / jnp.sum(pe_top, axis=-1, keepdims=True)
    expert = g_idx * MOE_PER_GROUP + e_local

    n_assign = n * MOE_TOPK
    flat_e = expert.reshape(-1)
    flat_tok = jnp.repeat(jnp.arange(n, dtype=jnp.int32), MOE_TOPK)
    flat_gate = gate.reshape(-1).astype(h.dtype)
    order = jnp.argsort(flat_e)
    e_sorted = flat_e[order]
    counts = jnp.zeros((MOE_EXPERTS,), jnp.int32).at[flat_e].add(1)
    start = jnp.cumsum(counts) - counts
    padded = (counts + MOE_BLOCK - 1) // MOE_BLOCK * MOE_BLOCK
    pad_end = jnp.cumsum(padded)
    pad_start = pad_end - padded
    dest = pad_start[e_sorted] + jnp.arange(n_assign, dtype=jnp.int32) - start[e_sorted]
    n_blocks = -(-n_assign // MOE_BLOCK) + MOE_EXPERTS
    n_rows = n_blocks * MOE_BLOCK
    row_tok = jnp.full((n_rows,), n, jnp.int32).at[dest].set(flat_tok[order])
    row_gate = jnp.zeros((n_rows,), h.dtype).at[dest].set(flat_gate[order])
    block_expert = jnp.minimum(
        jnp.searchsorted(pad_end, jnp.arange(n_blocks, dtype=jnp.int32) * MOE_BLOCK, side='right'),
        MOE_EXPERTS - 1)
    h_pad = jnp.concatenate([h, jnp.zeros((1, d), h.dtype)], axis=0)
    x_rows = h_pad[row_tok].reshape(n_blocks, MOE_BLOCK, d)

    def expert_block(args):
        xb, e = args
        return (jax.nn.silu(xb @ w1[e]) * (xb @ w3[e])) @ w2[e]

    y_rows = lax.map(expert_block, (x_rows, block_expert)).reshape(n_rows, d) * row_gate[:, None]
    return jnp.zeros((n + 1, d), h.dtype).at[row_tok].add(y_rows)[:n]


def setup_inputs(seed: int = 0) -> dict:
    key = jax.random.key(seed)
    ks = iter(jax.random.split(key, 32))
    f32 = jnp.float32
    L, D = DEPTH, D_MODEL

    def nrm(shape, scale):
        return jax.random.normal(next(ks), shape, f32) * scale

    def gain(shape):
        return 1.0 + nrm(shape, 0.02)

    return {
        'x': nrm((BATCH, SEQ, D), 1.0),
        'c': nrm((BATCH, D), 1.0),
        'ctx': nrm((BATCH, CTX_LEN, D), 1.0),
        'c_ctx': nrm((D,), 1.0),
        'w_ada': nrm((L, D, 6 * D), 0.5 * D ** -0.5),
        'b_ada': nrm((L, 6 * D), 0.01),
        'w_in': nrm((L, D, P_TOT), D ** -0.5),
        'w_out': nrm((L, D_MIX, D), D_MIX ** -0.5),
        'mla_g_cq': gain((L, MLA_Q_LORA)),
        'mla_g_ckv': gain((L, MLA_KV_LORA)),
        'mla_w_uq': nrm((L, MLA_Q_LORA, MLA_H * (MLA_NOPE + MLA_ROPE)), MLA_Q_LORA ** -0.5),
        'mla_w_ukv': nrm((L, MLA_KV_LORA, MLA_H * (MLA_NOPE + MLA_V)), MLA_KV_LORA ** -0.5),
        'mla_g_qn': gain((L, MLA_NOPE)),
        'mla_g_qr': gain((L, MLA_ROPE)),
        'mla_g_kn': gain((L, MLA_NOPE)),
        'mla_g_kr': gain((L, MLA_ROPE)),
        'na_g_q': gain((L, HEAD_DIM)),
        'na_g_k': gain((L, HEAD_DIM)),
        'na_rpb': nrm((L, NA_H, 2 * NA_KH_MAX - 1, 2 * NA_KW - 1), 0.1),
        'ret_g_out': gain((L, RET_DV)),
        'hg_lb_raw': nrm((L, HG_H * HG_DK), 1.0),
        'hg_g_out': gain((L, HG_DV)),
        'moe_w_rg': nrm((L, D, MOE_GROUPS), D ** -0.5),
        'moe_b_rg': nrm((L, MOE_GROUPS), 0.01),
        'moe_w_re': nrm((L, D, MOE_EXPERTS), D ** -0.5),
        'moe_b_re': nrm((L, MOE_EXPERTS), 0.01),
        'moe_w1': nrm((L, MOE_EXPERTS, D, MOE_FF), D ** -0.5),
        'moe_w3': nrm((L, MOE_EXPERTS, D, MOE_FF), D ** -0.5),
        'moe_w2': nrm((L, MOE_EXPERTS, MOE_FF, D), MOE_FF ** -0.5),
    }


def reference(x, c, ctx, c_ctx, w_ada, b_ada, w_in, w_out, mla_g_cq, mla_g_ckv, mla_w_uq, mla_w_ukv,
              mla_g_qn, mla_g_qr, mla_g_kn, mla_g_kr, na_g_q, na_g_k, na_rpb, ret_g_out, hg_lb_raw,
              hg_g_out, moe_w_rg, moe_b_rg, moe_w_re, moe_b_re, moe_w1, moe_w3, moe_w2):
    b, n, d = x.shape
    n_ctx = ctx.shape[1]
    rows = n // GRID_W
    cos, sin = axial_rope_tables(n)
    lb_w = jax.nn.softmax(hg_lb_raw.astype(jnp.float32), axis=0)
    hg_lb = jnp.cumsum(lb_w, axis=0) - lb_w[0:1]
    silu_c = jax.nn.silu(c)
    silu_cc = jax.nn.silu(c_ctx)
    xc = ctx
    for l in range(DEPTH):
        last = l == DEPTH - 1
        mod = silu_c @ w_ada[l] + b_ada[l]
        mod_c = silu_cc @ w_ada[l] + b_ada[l]
        sh_a, sc_a, gt_a, sh_m, sc_m, gt_m = jnp.split(mod[:, None, :], 6, axis=-1)
        csh_a, csc_a, cgt_a, csh_m, csc_m, cgt_m = jnp.split(mod_c, 6)

        p = split_cols(modulate(x, sh_a, sc_a) @ w_in[l])
        pc = split_cols(modulate(xc, csh_a, csc_a) @ w_in[l])
        y_mla, yc_mla = mla_mixer(p[0:3], pc[0:3], cos, sin,
                                  (mla_g_cq[l], mla_w_uq[l], mla_g_qn[l], mla_g_qr[l]),
                                  (mla_g_ckv[l], mla_w_ukv[l], mla_g_kn[l], mla_g_kr[l]), not last)
        y_na, yc_na = na_mixer(p[3:6], pc[3:6], rows, na_g_q[l], na_g_k[l], na_rpb[l], not last)
        y_ret, yc_ret = retention_mixer(p[6:10], pc[6:10], ret_g_out[l], not last)
        y_hg, yc_hg = hgrn2_mixer(p[10:15], pc[10:15], hg_lb[l], hg_g_out[l], not last)
        x = x + gt_a * (jnp.concatenate([y_mla, y_na, y_ret, y_hg], axis=-1) @ w_out[l])

        moe_prm = (moe_w_rg[l], moe_b_rg[l], moe_w_re[l], moe_b_re[l], moe_w1[l], moe_w3[l], moe_w2[l])
        if last:
            h = modulate(x, sh_m, sc_m).reshape(b * n, d)
            x = x + gt_m * hier_moe(h, *moe_prm).reshape(b, n, d)
        else:
            xc = xc + cgt_a * (jnp.concatenate([yc_mla, yc_na, yc_ret, yc_hg], axis=-1) @ w_out[l])
            h = jnp.concatenate([modulate(x, sh_m, sc_m).reshape(b * n, d),
                                 modulate(xc, csh_m, csc_m).reshape(b * n_ctx, d)], axis=0)
            out = hier_moe(h, *moe_prm)
            x = x + gt_m * out[:b * n].reshape(b, n, d)
            xc = xc + cgt_m * out[b * n:].reshape(b, n_ctx, d)
    return x
```

```python
import functools

import numpy as np
import jax
import jax.numpy as jnp
from jax import lax
from jax.experimental import pallas as pl
from jax.experimental.pallas import tpu as pltpu

F32 = jnp.float32
BF16 = jnp.bfloat16

EPS = 1e-6
ROPE_THETA = 10000.0
NEG_BIG = -1e30
F_FLOOR = 1e-20
GRID_W = 64
N_HEADS = 4
HEAD_DIM = 64
LANES = 128
GROUP_W = N_HEADS * HEAD_DIM
PAD_W = N_HEADS * LANES
MLA_Q_LORA = 192
MLA_KV_LORA = 128
MLA_ROPE = 32
MLA_SCALE = (HEAD_DIM + MLA_ROPE) ** -0.5
NA_KH = 8
NA_KW = 16
NA_SCALE = HEAD_DIM ** -0.5
RET_CHUNK = 128
HG_CHUNK = 16
MOE_GROUPS = 4
MOE_PER_GROUP = 8
MOE_EXPERTS = MOE_GROUPS * MOE_PER_GROUP
MOE_TOPK = 2
MOE_BLOCK = 256
TM = 256
VMEM_LIMIT = 56 * 1024 * 1024

COL_NA_Q, COL_NA_K, COL_NA_V = 2, 4, 6
COL_RET_Q, COL_RET_K, COL_RET_V, COL_RET_G = 8, 9, 10, 11
COL_HG_Q, COL_HG_FF, COL_HG_FB, COL_HG_I, COL_HG_G = 12, 13, 14, 15, 16
P_COLS = 17 * GROUP_W


def _cparams(sem):
    return pltpu.CompilerParams(dimension_semantics=sem, vmem_limit_bytes=VMEM_LIMIT)


def _sigmoid(x):
    return 1.0 / (1.0 + jnp.exp(-x))


def _silu(x):
    return x * _sigmoid(x)


def _dot(a, b):
    return jnp.dot(a, b, preferred_element_type=F32)


def _dot_nt(a, b):
    return lax.dot_general(a, b, (((1,), (1,)), ((), ())), preferred_element_type=F32)


def _split_dot_l(x, m, n):
    acc = None
    rem = x
    for i in range(n):
        piece = rem.astype(BF16)
        d = _dot(piece, m)
        acc = d if acc is None else acc + d
        if i + 1 < n:
            rem = rem - piece.astype(F32)
    return acc


def _split_dot_r(m, x, n):
    acc = None
    rem = x
    for i in range(n):
        piece = rem.astype(BF16)
        d = _dot(m, piece)
        acc = d if acc is None else acc + d
        if i + 1 < n:
            rem = rem - piece.astype(F32)
    return acc


def _seg_rms(x, m, gain):
    return x * lax.rsqrt(_split_dot_l(x * x, m, 2) + EPS) * gain


def _ada_kernel(c_ref, w_ref, b_ref, o_ref):
    s = _silu(c_ref[...])
    o_ref[0] = jnp.dot(s, w_ref[0], preferred_element_type=F32,
                       precision=lax.Precision.HIGHEST) + b_ref[0]


def _ada_all(cc, w_ada, b_ada):
    n_layers, d, d6 = w_ada.shape
    bn = 512
    rows = cc.shape[0]
    return pl.pallas_call(
        _ada_kernel,
        grid=(n_layers, d6 // bn),
        in_specs=[pl.BlockSpec((rows, d), lambda l, j: (0, 0)),
                  pl.BlockSpec((1, d, bn), lambda l, j: (l, 0, j)),
                  pl.BlockSpec((1, 1, bn), lambda l, j: (l, 0, j))],
        out_specs=pl.BlockSpec((1, rows, bn), lambda l, j: (l, 0, j)),
        out_shape=jax.ShapeDtypeStruct((n_layers, rows, d6), F32),
        compiler_params=_cparams(("arbitrary", "arbitrary")),
    )(cc, w_ada, b_ada.reshape(n_layers, 1, d6))


def _mod_spec(d6, n_lat_tiles):
    return pl.BlockSpec((1, 1, d6), lambda b, j: (2 * b + (j >= n_lat_tiles).astype(jnp.int32), 0, 0))


def _modulate(x, shift, scale):
    xn = x * lax.rsqrt(jnp.mean(x * x, axis=-1, keepdims=True) + EPS)
    return xn * (1.0 + scale) + shift


def _inproj_kernel(x_ref, mod_ref, w_ref, o_ref, *, d):
    xm = _modulate(x_ref[0], mod_ref[0, :, 0:d], mod_ref[0, :, d:2 * d])
    o_ref[0] = _dot(xm.astype(BF16), w_ref[...])


def _inproj(xx, modsel, w_in_p, n_lat_tiles):
    b, nt, d = xx.shape
    pc = w_in_p.shape[1]
    return pl.pallas_call(
        functools.partial(_inproj_kernel, d=d),
        grid=(b, nt // TM),
        in_specs=[pl.BlockSpec((1, TM, d), lambda i, j: (i, j, 0)),
                  _mod_spec(6 * d, n_lat_tiles),
                  pl.BlockSpec((d, pc), lambda i, j: (0, 0))],
        out_specs=pl.BlockSpec((1, TM, pc), lambda i, j: (i, j, 0)),
        out_shape=jax.ShapeDtypeStruct((b, nt, pc), F32),
        compiler_params=_cparams(("arbitrary", "arbitrary")),
    )(xx, modsel, w_in_p)


def _prep_kernel(p_ref, c_ref, s_ref, wuq_ref, wk_ref, wv_ref, gcq_ref, gckv_ref, gkr_ref,
                 gq_ref, gk_ref, mq_ref, mk_ref, gnq_ref, gnk_ref, mn_ref,
                 qm_ref, km_ref, vm_ref, qn_ref, kn_ref, vn_ref):
    cq = p_ref[0, :, 0:256]
    ckv = p_ref[0, :, 256:384]
    kr = p_ref[0, :, 384:512]
    cqn = cq * lax.rsqrt(jnp.sum(cq * cq, axis=-1, keepdims=True) * (1.0 / MLA_Q_LORA) + EPS) * gcq_ref[...]
    ckvn = (ckv * lax.rsqrt(jnp.mean(ckv * ckv, axis=-1, keepdims=True) + EPS) * gckv_ref[...]).astype(BF16)
    krn = kr * lax.rsqrt(jnp.sum(kr * kr, axis=-1, keepdims=True) * (1.0 / MLA_ROPE) + EPS) * gkr_ref[...]
    q = _seg_rms(_dot(cqn.astype(BF16), wuq_ref[...]), mq_ref[...], gq_ref[...])
    kk = _seg_rms(_dot(ckvn, wk_ref[...]), mk_ref[...], gk_ref[...])
    vv = _dot(ckvn, wv_ref[...])

    cos = c_ref[...]
    sin = s_ref[...]
    lane = lax.broadcasted_iota(jnp.int32, (TM, LANES), 1)
    first = (lane % 16) < 8

    def rope(x):
        partner = jnp.where(first, pltpu.roll(x, LANES - 8, 1), pltpu.roll(x, 8, 1))
        return x * cos + partner * sin

    krr = rope(krn)
    nq = _seg_rms(p_ref[0, :, 512:1024], mn_ref[...], gnq_ref[...])
    nk = _seg_rms(p_ref[0, :, 1024:1536], mn_ref[...], gnk_ref[...])
    for h in range(N_HEADS):
        sl = slice(h * LANES, (h + 1) * LANES)
        qm_ref[0, h] = rope(q[:, sl]).astype(BF16)
        km_ref[0, h] = (kk[:, sl] + krr).astype(BF16)
        vm_ref[0, h] = vv[:, sl].astype(BF16)
        qn_ref[0, h] = nq[:, sl].astype(BF16)
        kn_ref[0, h] = nk[:, sl].astype(BF16)
        vn_ref[0, h] = p_ref[0, :, 1536 + h * LANES:1536 + (h + 1) * LANES].astype(BF16)


def _prep(p, rope_c, rope_s, pw):
    b, nt, _ = p.shape
    full = lambda a: pl.BlockSpec(a.shape, lambda i, j: (0,) * a.ndim)
    consts = [pw['wuq'], pw['wk'], pw['wv'], pw['gcq'], pw['gckv'], pw['gkr'], pw['gq'], pw['gk'],
              pw['mq'], pw['mk'], pw['gnq'], pw['gnk'], pw['mn']]
    head_spec = pl.BlockSpec((1, N_HEADS, TM, LANES), lambda i, j: (i, 0, j, 0))
    head_shape = jax.ShapeDtypeStruct((b, N_HEADS, nt, LANES), BF16)
    return pl.pallas_call(
        _prep_kernel,
        grid=(b, nt // TM),
        in_specs=[pl.BlockSpec((1, TM, 2048), lambda i, j: (i, j, 0)),
                  pl.BlockSpec((TM, LANES), lambda i, j: (j, 0)),
                  pl.BlockSpec((TM, LANES), lambda i, j: (j, 0))] + [full(a) for a in consts],
        out_specs=[head_spec] * 6,
        out_shape=[head_shape] * 6,
        compiler_params=_cparams(("arbitrary", "arbitrary")),
    )(p, rope_c, rope_s, *consts)


def _softmax_pv(s, v):
    m = jnp.max(s, axis=-1, keepdims=True)
    e = jnp.exp(s - m)
    l = jnp.sum(e, axis=-1, keepdims=True)
    return _dot(e.astype(BF16), v) / l


def _mla_kernel(q_ref, k_ref, v_ref, o_ref, *, n_lat, n_ctx):
    j = pl.program_id(2)
    q = q_ref[0, 0]

    @pl.when(j < n_lat // TM)
    def _():
        s = _dot_nt(q, k_ref[0, 0]) * MLA_SCALE
        o_ref[0] = _softmax_pv(s, v_ref[0, 0])

    @pl.when(j >= n_lat // TM)
    def _():
        s = _dot_nt(q, k_ref[0, 0, pl.ds(n_lat, n_ctx), :]) * MLA_SCALE
        o_ref[0] = _softmax_pv(s, v_ref[0, 0, pl.ds(n_lat, n_ctx), :])


def _mla_attn(qm, km, vm, n_lat, n_ctx):
    b, h, nt, _ = qm.shape
    kv_spec = pl.BlockSpec((1, 1, nt, LANES), lambda i, hh, j: (i, hh, 0, 0))
    return pl.pallas_call(
        functools.partial(_mla_kernel, n_lat=n_lat, n_ctx=n_ctx),
        grid=(b, h, nt // TM),
        in_specs=[pl.BlockSpec((1, 1, TM, LANES), lambda i, hh, j: (i, hh, j, 0)), kv_spec, kv_spec],
        out_specs=pl.BlockSpec((1, TM, LANES), lambda i, hh, j: (i, j, hh)),
        out_shape=jax.ShapeDtypeStruct((b, nt, PAD_W), F32),
        compiler_params=_cparams(("arbitrary", "arbitrary", "arbitrary")),
    )(qm, km, vm)


def _na_kernel(q_ref, k_ref, v_ref, bias_ref, o_ref, *, n_lat, n_ctx):
    j = pl.program_id(1)
    rows = n_lat // GRID_W
    rows_per_tile = TM // GRID_W
    win = NA_KH * GRID_W

    @pl.when(j < n_lat // TM)
    def _():
        for rr in range(rows_per_tile):
            r = j * rows_per_tile + rr
            start = jnp.clip(r - NA_KH // 2, 0, rows - NA_KH)
            case = start - r + (NA_KH - 1)
            tok0 = pl.multiple_of(start * GRID_W, GRID_W)
            for h in range(N_HEADS):
                q = q_ref[0, h, rr * GRID_W:(rr + 1) * GRID_W, :]
                s1 = _dot_nt(q, k_ref[0, h, pl.ds(tok0, win), :]) * NA_SCALE + bias_ref[case, h]
                s2 = _dot_nt(q, k_ref[0, h, pl.ds(n_lat, n_ctx), :]) * NA_SCALE
                m = jnp.maximum(jnp.max(s1, axis=-1, keepdims=True), jnp.max(s2, axis=-1, keepdims=True))
                e1 = jnp.exp(s1 - m)
                e2 = jnp.exp(s2 - m)
                l = jnp.sum(e1, axis=-1, keepdims=True) + jnp.sum(e2, axis=-1, keepdims=True)
                o = _dot(e1.astype(BF16), v_ref[0, h, pl.ds(tok0, win), :])
                o = o + _dot(e2.astype(BF16), v_ref[0, h, pl.ds(n_lat, n_ctx), :])
                o_ref[0, rr * GRID_W:(rr + 1) * GRID_W, h * LANES:(h + 1) * LANES] = o / l

    @pl.when(j >= n_lat // TM)
    def _():
        for h in range(N_HEADS):
            s = _dot_nt(q_ref[0, h], k_ref[0, h, pl.ds(n_lat, n_ctx), :]) * NA_SCALE
            o_ref[0, :, h * LANES:(h + 1) * LANES] = _softmax_pv(s, v_ref[0, h, pl.ds(n_lat, n_ctx), :])


def _na_attn(qn, kn, vn, bias, n_lat, n_ctx):
    b, h, nt, _ = qn.shape
    kv_spec = pl.BlockSpec((1, h, nt, LANES), lambda i, j: (i, 0, 0, 0))
    return pl.pallas_call(
        functools.partial(_na_kernel, n_lat=n_lat, n_ctx=n_ctx),
        grid=(b, nt // TM),
        in_specs=[pl.BlockSpec((1, h, TM, LANES), lambda i, j: (i, 0, j, 0)), kv_spec, kv_spec,
                  pl.BlockSpec(bias.shape, lambda i, j: (0, 0, 0, 0))],
        out_specs=pl.BlockSpec((1, TM, PAD_W), lambda i, j: (i, j, 0)),
        out_shape=jax.ShapeDtypeStruct((b, nt, PAD_W), F32),
        compiler_params=_cparams(("arbitrary", "arbitrary")),
    )(qn, kn, vn, bias)


def _head_mask(h, shape):
    return (lax.broadcasted_iota(jnp.int32, shape, 1) // HEAD_DIM) == h


def _ret_state_step(s_ref, q, k, v, qw, kw, cd, bd):
    state = s_ref[...]
    o = _dot((q * qw).astype(BF16), state.astype(BF16))
    upd = _dot((k * kw).T.astype(BF16), v.astype(BF16))
    s_ref[...] = state * cd + upd * bd
    return o


def _ret_fwd_kernel(q_ref, k_ref, v_ref, dm_ref, qw_ref, kw_ref, cd_ref, bd_ref, o_ref, s_ref):
    @pl.when(pl.program_id(1) == 0)
    def _():
        s_ref[...] = jnp.zeros_like(s_ref)

    q = q_ref[0]
    k = k_ref[0]
    v = v_ref[0]
    o = _ret_state_step(s_ref, q, k, v, qw_ref[...], kw_ref[...], cd_ref[...], bd_ref[...])
    kb = k.astype(BF16)
    for h in range(N_HEADS):
        hm = _head_mask(h, q.shape)
        sc = _dot_nt(jnp.where(hm, q, 0.0).astype(BF16), kb) * dm_ref[h]
        o = o + _dot(sc.astype(BF16), jnp.where(hm, v, 0.0).astype(BF16))
    o_ref[0] = o


def _ret_bwd_kernel(q_ref, k_ref, v_ref, g_ref, op_ref, qw_ref, kw_ref, cd_ref, bd_ref, ms_ref, go_ref,
                    y_ref, s_ref):
    @pl.when(pl.program_id(1) == 0)
    def _():
        s_ref[...] = jnp.zeros_like(s_ref)

    o = op_ref[0] + _ret_state_step(s_ref, q_ref[0], k_ref[0], v_ref[0], qw_ref[...], kw_ref[...],
                                    cd_ref[...], bd_ref[...])
    y_ref[0] = _seg_rms(o, ms_ref[...], go_ref[...]) * _silu(g_ref[0])


def _scan_order(n_lat_t, n_ctx_t, reverse):
    if reverse:
        return lambda i: jnp.where(i < n_ctx_t, n_lat_t + n_ctx_t - 1 - i, n_lat_t + n_ctx_t - 1 - i)
    return lambda i: jnp.where(i < n_ctx_t, n_lat_t + i, i - n_ctx_t)


def _retention(p, rc, go, n_lat, n_ctx):
    b, nt, _ = p.shape
    c = RET_CHUNK
    n_lat_t, n_ctx_t = n_lat // c, n_ctx // c
    fwd = _scan_order(n_lat_t, n_ctx_t, False)
    bwd = _scan_order(n_lat_t, n_ctx_t, True)
    col = lambda order, cb: pl.BlockSpec((1, c, GROUP_W), lambda i, j: (i, order(j), cb))
    full = lambda a: pl.BlockSpec(a.shape, lambda i, j: (0,) * a.ndim)
    out_shape = jax.ShapeDtypeStruct((b, nt, GROUP_W), F32)
    scratch = [pltpu.VMEM((GROUP_W, GROUP_W), F32)]
    consts_f = [rc['dm'], rc['qw_f'], rc['kw_f'], rc['cd_f'], rc['bd']]
    o_part = pl.pallas_call(
        _ret_fwd_kernel,
        grid=(b, nt // c),
        in_specs=[col(fwd, COL_RET_Q), col(fwd, COL_RET_K), col(fwd, COL_RET_V)] + [full(a) for a in consts_f],
        out_specs=pl.BlockSpec((1, c, GROUP_W), lambda i, j: (i, fwd(j), 0)),
        out_shape=out_shape,
        scratch_shapes=scratch,
        compiler_params=_cparams(("arbitrary", "arbitrary")),
    )(p, p, p, *consts_f)
    consts_b = [rc['qw_b'], rc['kw_b'], rc['cd_b'], rc['bd'], rc['ms'], go]
    return pl.pallas_call(
        _ret_bwd_kernel,
        grid=(b, nt // c),
        in_specs=[col(bwd, COL_RET_Q), col(bwd, COL_RET_K), col(bwd, COL_RET_V), col(bwd, COL_RET_G),
                  pl.BlockSpec((1, c, GROUP_W), lambda i, j: (i, bwd(j), 0))] + [full(a) for a in consts_b],
        out_specs=pl.BlockSpec((1, c, GROUP_W), lambda i, j: (i, bwd(j), 0)),
        out_shape=out_shape,
        scratch_shapes=scratch,
        compiler_params=_cparams(("arbitrary", "arbitrary")),
    )(p, p, p, p, o_part, *consts_b)


def _hg_direction(q_ref, f_ref, v_ref, lb_ref, ain_ref, aex_ref, bseg_ref, bd_ref, st_ref, *, reverse):
    n_chunks = TM // HG_CHUNK
    qh = _silu(q_ref[0])
    lb = lb_ref[...]
    f = jnp.maximum(lb + (1.0 - lb) * _sigmoid(f_ref[0]), F_FLOOR)
    lf = jnp.log(f)
    k = 1.0 - f
    v = v_ref[0]

    a_in = _split_dot_r(ain_ref[...], lf, 3)
    a_ex = _split_dot_r(aex_ref[...], lf, 3)
    qp = (qh * jnp.exp(a_in)).astype(BF16)
    kdec = k * jnp.exp(a_ex)
    lam_all = jnp.exp(a_in + a_ex)
    vt = v.T.astype(BF16)
    row_chunk = lax.broadcasted_iota(jnp.int32, (TM, 1), 0) // HG_CHUNK
    bd = bd_ref[...]
    state = st_ref[...]
    parts = [None] * n_chunks
    for c in (range(n_chunks - 1, -1, -1) if reverse else range(n_chunks)):
        r0 = c * HG_CHUNK
        parts[c] = _dot_nt(qp[r0:r0 + HG_CHUNK], state.astype(BF16))
        upd = _dot(vt, jnp.where(row_chunk == c, kdec, 0.0).astype(BF16))
        state = state * lam_all[r0:r0 + 1] + upd * bd
    st_ref[...] = state
    o = jnp.concatenate(parts, axis=0)

    pos = lax.broadcasted_iota(jnp.int32, (TM, 1), 0) % HG_CHUNK
    bseg = bseg_ref[...]
    g = jnp.zeros_like(lf)
    for dl in range(HG_CHUNK):
        sh = (TM - dl) % TM if reverse else dl
        if dl > 0:
            shl = (TM - (dl - 1)) % TM if reverse else dl - 1
            g = g + (pltpu.roll(lf, shl, 0) if shl else lf)
        ksh = pltpu.roll(k, sh, 0) if sh else k
        vsh = pltpu.roll(v, sh, 0) if sh else v
        valid = (pos <= HG_CHUNK - 1 - dl) if reverse else (pos >= dl)
        w = jnp.where(valid, qh * ksh * jnp.exp(g), 0.0)
        o = o + _dot(w.astype(BF16), bseg) * vsh
    return o


def _hg_fwd_kernel(q_ref, f_ref, v_ref, lb_ref, ain_ref, aex_ref, bseg_ref, bd_ref, o_ref, st_ref):
    @pl.when(pl.program_id(1) == 0)
    def _():
        st_ref[...] = jnp.zeros_like(st_ref)

    o_ref[0] = _hg_direction(q_ref, f_ref, v_ref, lb_ref, ain_ref, aex_ref, bseg_ref, bd_ref, st_ref,
                             reverse=False)


def _hg_bwd_kernel(q_ref, f_ref, v_ref, g_ref, op_ref, lb_ref, ain_ref, aex_ref, bseg_ref, bd_ref,
                   ms_ref, go_ref, y_ref, st_ref):
    @pl.when(pl.program_id(1) == 0)
    def _():
        st_ref[...] = jnp.zeros_like(st_ref)

    o = op_ref[0] + _hg_direction(q_ref, f_ref, v_ref, lb_ref, ain_ref, aex_ref, bseg_ref, bd_ref, st_ref,
                                  reverse=True)
    y_ref[0] = _seg_rms(o, ms_ref[...], go_ref[...]) * _silu(g_ref[0])


def _hgrn2(p, hc, lb, go, n_lat, n_ctx):
    b, nt, _ = p.shape
    n_lat_t, n_ctx_t = n_lat // TM, n_ctx // TM
    fwd = _scan_order(n_lat_t, n_ctx_t, False)
    bwd = _scan_order(n_lat_t, n_ctx_t, True)
    col = lambda order, cb: pl.BlockSpec((1, TM, GROUP_W), lambda i, j: (i, order(j), cb))
    full = lambda a: pl.BlockSpec(a.shape, lambda i, j: (0,) * a.ndim)
    out_shape = jax.ShapeDtypeStruct((b, nt, GROUP_W), F32)
    scratch = [pltpu.VMEM((GROUP_W, GROUP_W), F32)]
    consts_f = [lb, hc['lincl'], hc['uexcl'], hc['bseg'], hc['bd']]
    o_part = pl.pallas_call(
        _hg_fwd_kernel,
        grid=(b, nt // TM),
        in_specs=[col(fwd, COL_HG_Q), col(fwd, COL_HG_FF), col(fwd, COL_HG_I)] + [full(a) for a in consts_f],
        out_specs=pl.BlockSpec((1, TM, GROUP_W), lambda i, j: (i, fwd(j), 0)),
        out_shape=out_shape,
        scratch_shapes=scratch,
        compiler_params=_cparams(("arbitrary", "arbitrary")),
    )(p, p, p, *consts_f)
    consts_b = [lb, hc['uincl'], hc['lexcl'], hc['bseg'], hc['bd'], hc['ms'], go]
    return pl.pallas_call(
        _hg_bwd_kernel,
        grid=(b, nt // TM),
        in_specs=[col(bwd, COL_HG_Q), col(bwd, COL_HG_FB), col(bwd, COL_HG_I), col(bwd, COL_HG_G),
                  pl.BlockSpec((1, TM, GROUP_W), lambda i, j: (i, bwd(j), 0))] + [full(a) for a in consts_b],
        out_specs=pl.BlockSpec((1, TM, GROUP_W), lambda i, j: (i, bwd(j), 0)),
        out_shape=out_shape,
        scratch_shapes=scratch,
        compiler_params=_cparams(("arbitrary", "arbitrary")),
    )(p, p, p, p, o_part, *consts_b)


def _outproj_kernel(x_ref, ym_ref, yn_ref, yr_ref, yh_ref, mod_ref, wm_ref, wn_ref, wr_ref, wh_ref, o_ref, *, d):
    acc = _dot(ym_ref[0].astype(BF16), wm_ref[...])
    acc = acc + _dot(yn_ref[0].astype(BF16), wn_ref[...])
    acc = acc + _dot(yr_ref[0].astype(BF16), wr_ref[...])
    acc = acc + _dot(yh_ref[0].astype(BF16), wh_ref[...])
    o_ref[0] = x_ref[0] + mod_ref[0, :, 2 * d:3 * d] * acc


def _outproj(xx, y_mla, y_na, y_ret, y_hg, modsel, ow, n_lat_tiles):
    b, nt, d = xx.shape
    tile = lambda w: pl.BlockSpec((1, TM, w), lambda i, j: (i, j, 0))
    full = lambda a: pl.BlockSpec(a.shape, lambda i, j: (0, 0))
    ws = [ow['mla'], ow['na'], ow['ret'], ow['hg']]
    return pl.pallas_call(
        functools.partial(_outproj_kernel, d=d),
        grid=(b, nt // TM),
        in_specs=[tile(d), tile(PAD_W), tile(PAD_W), tile(GROUP_W), tile(GROUP_W),
                  _mod_spec(6 * d, n_lat_tiles)] + [full(a) for a in ws],
        out_specs=tile(d),
        out_shape=jax.ShapeDtypeStruct((b, nt, d), F32),
        compiler_params=_cparams(("arbitrary", "arbitrary")),
    )(xx, y_mla, y_na, y_ret, y_hg, modsel, *ws)


def _router_kernel(x_ref, mod_ref, whi_ref, wlo_ref, br_ref, h_ref, r_ref, *, d):
    h = _modulate(x_ref[0], mod_ref[0, :, 3 * d:4 * d], mod_ref[0, :, 4 * d:5 * d])
    h_ref[...] = h
    h_hi = h.astype(BF16)
    h_lo = (h - h_hi.astype(F32)).astype(BF16)
    lg = _dot(h_hi, whi_ref[...]) + _dot(h_lo, whi_ref[...]) + _dot(h_hi, wlo_ref[...]) + br_ref[...]

    lane = lax.broadcasted_iota(jnp.int32, lg.shape, 1).astype(F32)
    far = 1e9

    def first_argmax(vals, vmax):
        return jnp.min(jnp.where(vals == vmax, lane, far), axis=-1, keepdims=True)

    gl = jnp.where(lane < MOE_GROUPS, lg, NEG_BIG)
    gmax = jnp.max(gl, axis=-1, keepdims=True)
    pg_top = 1.0 / jnp.sum(jnp.exp(gl - gmax), axis=-1, keepdims=True)
    lo = MOE_GROUPS + MOE_PER_GROUP * first_argmax(gl, gmax)
    fl = jnp.where((lane >= lo) & (lane < lo + MOE_PER_GROUP), lg, NEG_BIG)
    fmax = jnp.max(fl, axis=-1, keepdims=True)
    fsum = jnp.sum(jnp.exp(fl - fmax), axis=-1, keepdims=True)
    i1 = first_argmax(fl, fmax)
    fl2 = jnp.where(lane == i1, NEG_BIG, fl)
    f2max = jnp.max(fl2, axis=-1, keepdims=True)
    i2 = first_argmax(fl2, f2max)
    p1 = 1.0 / fsum
    p2 = jnp.exp(f2max - fmax) / fsum
    g1 = pg_top * p1 / (p1 + p2)
    g2 = pg_top * p2 / (p1 + p2)
    out = jnp.where(lane == 0, i1 - MOE_GROUPS, 0.0)
    out = jnp.where(lane == 1, i2 - MOE_GROUPS, out)
    out = jnp.where(lane == 2, g1, out)
    out = jnp.where(lane == 3, g2, out)
    r_ref[...] = out


def _router(xx, modsel, rw, n_lat_tiles):
    b, nt, d = xx.shape
    tiles = nt // TM
    full = lambda a: pl.BlockSpec(a.shape, lambda i, j: (0, 0))
    ws = [rw['hi'], rw['lo'], rw['b']]
    return pl.pallas_call(
        functools.partial(_router_kernel, d=d),
        grid=(b, tiles),
        in_specs=[pl.BlockSpec((1, TM, d), lambda i, j: (i, j, 0)), _mod_spec(6 * d, n_lat_tiles)]
                 + [full(a) for a in ws],
        out_specs=[pl.BlockSpec((TM, d), lambda i, j: (i * tiles + j, 0)),
                   pl.BlockSpec((TM, LANES), lambda i, j: (i * tiles + j, 0))],
        out_shape=[jax.ShapeDtypeStruct((b * nt, d), F32), jax.ShapeDtypeStruct((b * nt, LANES), F32)],
        compiler_params=_cparams(("arbitrary", "arbitrary")),
    )(xx, modsel, *ws)


def _moe_plan(route, n_tok):
    expert = route[:, 0:MOE_TOPK].astype(jnp.int32)
    gate = route[:, MOE_TOPK:2 * MOE_TOPK]
    n_assign = n_tok * MOE_TOPK
    flat_e = expert.reshape(-1)
    flat_tok = jnp.repeat(jnp.arange(n_tok, dtype=jnp.int32), MOE_TOPK)
    flat_gate = gate.reshape(-1)
    order = jnp.argsort(flat_e)
    e_sorted = flat_e[order]
    counts = jnp.zeros((MOE_EXPERTS,), jnp.int32).at[flat_e].add(1)
    start = jnp.cumsum(counts) - counts
    padded = (counts + MOE_BLOCK - 1) // MOE_BLOCK * MOE_BLOCK
    pad_end = jnp.cumsum(padded)
    pad_start = pad_end - padded
    dest = pad_start[e_sorted] + jnp.arange(n_assign, dtype=jnp.int32) - start[e_sorted]
    n_blocks = -(-n_assign // MOE_BLOCK) + MOE_EXPERTS
    n_rows = n_blocks * MOE_BLOCK
    row_tok = jnp.zeros((n_rows,), jnp.int32).at[dest].set(flat_tok[order])
    row_gate = jnp.zeros((n_rows,), F32).at[dest].set(flat_gate[order])
    block_expert = jnp.minimum(
        jnp.searchsorted(pad_end, jnp.arange(n_blocks, dtype=jnp.int32) * MOE_BLOCK, side='right'),
        MOE_EXPERTS - 1).astype(jnp.int32)
    used = (pad_end[-1] // MOE_BLOCK).astype(jnp.int32).reshape(1)
    dest_of = jnp.zeros((n_assign,), jnp.int32).at[order].set(dest).reshape(n_tok, MOE_TOPK)
    return row_tok, row_gate, block_expert, used, dest_of, n_blocks


def _row_gather(src_hbm, idx_ref, dst_ref, sem, n):
    def issue(r, carry):
        pltpu.make_async_copy(src_hbm.at[pl.ds(idx_ref[0, 0, r], 1)], dst_ref.at[pl.ds(r, 1)], sem).start()
        return carry

    lax.fori_loop(0, n, issue, 0)


def _row_gather_wait(src_hbm, dst_ref, sem, n):
    pltpu.make_async_copy(src_hbm.at[pl.ds(0, n)], dst_ref, sem).wait()


def _ffn_kernel(be_ref, used_ref, tok_ref, gate_ref, h_hbm, w1_ref, w3_ref, w2_ref, y_ref, xs_ref, sem):
    i = pl.program_id(0)

    @pl.when(i < used_ref[0])
    def _():
        _row_gather(h_hbm, tok_ref, xs_ref, sem, MOE_BLOCK)
        _row_gather_wait(h_hbm, xs_ref, sem, MOE_BLOCK)
        x = xs_ref[...].astype(BF16)
        mid = _silu(_dot(x, w1_ref[0])) * _dot(x, w3_ref[0])
        y_ref[...] = _dot(mid.astype(BF16), w2_ref[0]) * gate_ref[...]

    @pl.when(i >= used_ref[0])
    def _():
        y_ref[...] = jnp.zeros_like(y_ref)


def _moe_ffn(h_flat, row_tok, row_gate, block_expert, used, n_blocks, w1, w3, w2):
    d = h_flat.shape[1]
    ff = w1.shape[2]
    n_rows = n_blocks * MOE_BLOCK
    grid_spec = pltpu.PrefetchScalarGridSpec(
        num_scalar_prefetch=2,
        grid=(n_blocks,),
        in_specs=[pl.BlockSpec((1, 1, MOE_BLOCK), lambda i, be, nu: (i, 0, 0), memory_space=pltpu.SMEM),
                  pl.BlockSpec((MOE_BLOCK, 1), lambda i, be, nu: (i, 0)),
                  pl.BlockSpec(memory_space=pl.ANY),
                  pl.BlockSpec((1, d, ff), lambda i, be, nu: (be[i], 0, 0)),
                  pl.BlockSpec((1, d, ff), lambda i, be, nu: (be[i], 0, 0)),
                  pl.BlockSpec((1, ff, d), lambda i, be, nu: (be[i], 0, 0))],
        out_specs=pl.BlockSpec((MOE_BLOCK, d), lambda i, be, nu: (i, 0)),
        scratch_shapes=[pltpu.VMEM((MOE_BLOCK, d), F32), pltpu.SemaphoreType.DMA],
    )
    return pl.pallas_call(
        _ffn_kernel,
        grid_spec=grid_spec,
        out_shape=jax.ShapeDtypeStruct((n_rows, d), F32),
        compiler_params=_cparams(("arbitrary",)),
    )(block_expert, used, row_tok.reshape(n_blocks, 1, MOE_BLOCK), row_gate.reshape(n_rows, 1), h_flat, w1, w3, w2)


def _combine_kernel(d0_ref, d1_ref, x_ref, mod_ref, y_hbm, o_ref, y0_ref, y1_ref, sem0, sem1, *, d):
    _row_gather(y_hbm, d0_ref, y0_ref, sem0, TM)
    _row_gather(y_hbm, d1_ref, y1_ref, sem1, TM)
    _row_gather_wait(y_hbm, y0_ref, sem0, TM)
    _row_gather_wait(y_hbm, y1_ref, sem1, TM)
    o_ref[0] = x_ref[0] + mod_ref[0, :, 5 * d:6 * d] * (y0_ref[...] + y1_ref[...])


def _combine(xx, modsel, y_rows, dest_of, n_lat_tiles, out_tiles):
    b, nt, d = xx.shape
    tiles = nt // TM
    d0 = dest_of[:, 0].reshape(b * tiles, 1, TM)
    d1 = dest_of[:, 1].reshape(b * tiles, 1, TM)
    idx_spec = pl.BlockSpec((1, 1, TM), lambda i, j: (i * tiles + j, 0, 0), memory_space=pltpu.SMEM)
    return pl.pallas_call(
        functools.partial(_combine_kernel, d=d),
        grid=(b, out_tiles),
        in_specs=[idx_spec, idx_spec, pl.BlockSpec((1, TM, d), lambda i, j: (i, j, 0)),
                  _mod_spec(6 * d, n_lat_tiles), pl.BlockSpec(memory_space=pl.ANY)],
        out_specs=pl.BlockSpec((1, TM, d), lambda i, j: (i, j, 0)),
        out_shape=jax.ShapeDtypeStruct((b, out_tiles * TM, d), F32),
        scratch_shapes=[pltpu.VMEM((TM, d), F32), pltpu.VMEM((TM, d), F32),
                        pltpu.SemaphoreType.DMA, pltpu.SemaphoreType.DMA],
        compiler_params=_cparams(("arbitrary", "arbitrary")),
    )(d0, d1, xx, modsel, y_rows)


def _pad_heads_cols(w):
    lead = w.shape[:-1]
    w = w.reshape(*lead, N_HEADS, HEAD_DIM)
    w = jnp.concatenate([w, jnp.zeros_like(w)], axis=-1)
    return w.reshape(*lead, PAD_W)


def _pad_heads_rows(w):
    return _pad_heads_cols(w.T).T


def _seg_mean_matrix(width, segments):
    m = np.zeros((width, width), np.float32)
    for g in range(width // LANES):
        for start, length in segments:
            a = g * LANES + start
            m[a:a + length, a:a + length] = 1.0 / length
    return jnp.asarray(m, BF16)


def _rope_tables(n_lat, n_ctx):
    pos = jnp.arange(n_lat)
    rows = (pos // GRID_W).astype(F32)
    cols = (pos % GRID_W).astype(F32)
    per_axis = MLA_ROPE // 2
    inv_freq = ROPE_THETA ** (-jnp.arange(0, per_axis, 2, dtype=F32) / per_axis)
    ang = jnp.concatenate([rows[:, None] * inv_freq, cols[:, None] * inv_freq], axis=-1)
    i = np.arange(MLA_ROPE)
    src = (i // 16) * 8 + (i % 8)
    sign = np.where((i % 16) < 8, -1.0, 1.0).astype(np.float32)
    cos = jnp.ones((n_lat, LANES), F32).at[:, HEAD_DIM:HEAD_DIM + MLA_ROPE].set(jnp.cos(ang)[:, src])
    sin = jnp.zeros((n_lat, LANES), F32).at[:, HEAD_DIM:HEAD_DIM + MLA_ROPE].set(jnp.sin(ang)[:, src] * sign)
    cos = jnp.concatenate([cos, jnp.ones((n_ctx, LANES), F32)], axis=0)
    sin = jnp.concatenate([sin, jnp.zeros((n_ctx, LANES), F32)], axis=0)
    return cos, sin


def _na_bias_table(rpb):
    case = np.arange(NA_KH)[:, None]
    dr = case + np.arange(NA_KH)[None, :]
    w = np.arange(GRID_W)
    col_start = np.clip(w - NA_KW // 2, 0, GRID_W - NA_KW)
    valid = (w[None, :] >= col_start[:, None]) & (w[None, :] < col_start[:, None] + NA_KW)
    dc = np.clip(w[None, :] - w[:, None], 1 - NA_KW, NA_KW - 1) + (NA_KW - 1)
    bias = rpb[:, dr[:, None, :, None], dc[None, :, None, :]].astype(F32)
    bias = jnp.where(jnp.asarray(valid)[None, None, :, None, :], bias, NEG_BIG)
    return jnp.transpose(bias, (1, 0, 2, 3, 4)).reshape(NA_KH, N_HEADS, GRID_W, NA_KH * GRID_W)


def _block_diag_mask(block):
    i = np.arange(GROUP_W) // block
    return (i[:, None] == i[None, :]).astype(np.float32)


def _retention_consts():
    c = RET_CHUNK
    j = np.arange(2 * N_HEADS, dtype=np.float64)
    lg = np.log1p(-np.exp2(-5.0 - j))
    lg_f, lg_b = lg[0::2], lg[1::2]
    pos = np.arange(c, dtype=np.float64)
    diff = pos[:, None] - pos[None, :]
    k_scale = HEAD_DIM ** -0.5
    dm = np.zeros((N_HEADS, c, c))
    for h in range(N_HEADS):
        dm[h] = (np.where(diff >= 0, np.exp(np.maximum(diff, 0.0) * lg_f[h]), 0.0)
                 + np.where(diff <= 0, np.exp(np.maximum(-diff, 0.0) * lg_b[h]), 0.0)) * k_scale
    lanes = lambda per_head: np.repeat(per_head, HEAD_DIM, axis=-1)
    out = {
        'dm': dm,
        'qw_f': lanes(np.exp((pos + 1)[:, None] * lg_f[None, :])),
        'kw_f': lanes(np.exp((c - 1 - pos)[:, None] * lg_f[None, :])) * k_scale,
        'cd_f': lanes(np.exp(c * lg_f)[None, :]),
        'qw_b': lanes(np.exp((c - pos)[:, None] * lg_b[None, :])),
        'kw_b': lanes(np.exp(pos[:, None] * lg_b[None, :])) * k_scale,
        'cd_b': lanes(np.exp(c * lg_b)[None, :]),
        'bd': _block_diag_mask(HEAD_DIM),
    }
    out = {k: jnp.asarray(v, F32) for k, v in out.items()}
    out['ms'] = jnp.asarray(_block_diag_mask(HEAD_DIM) / HEAD_DIM, BF16)
    return out


def _hgrn_consts():
    t = np.arange(TM)
    same = (t[:, None] // HG_CHUNK) == (t[None, :] // HG_CHUNK)
    lincl = same & (t[None, :] <= t[:, None])
    lexcl = same & (t[None, :] < t[:, None])
    return {
        'lincl': jnp.asarray(lincl, BF16), 'lexcl': jnp.asarray(lexcl, BF16),
        'uincl': jnp.asarray(lincl.T, BF16), 'uexcl': jnp.asarray(lexcl.T, BF16),
        'bseg': jnp.asarray(_block_diag_mask(HEAD_DIM), BF16),
        'bd': jnp.asarray(_block_diag_mask(HEAD_DIM), F32),
        'ms': jnp.asarray(_block_diag_mask(HEAD_DIM) / HEAD_DIM, BF16),
    }


def _layer_weights(l, w_in, w_out, mla_g_cq, mla_g_ckv, mla_w_uq, mla_w_ukv, mla_g_qn, mla_g_qr, mla_g_kn,
                   mla_g_kr, na_g_q, na_g_k, moe_w_rg, moe_b_rg, moe_w_re, moe_b_re):
    d = w_in.shape[1]
    w = w_in[l]
    z = lambda n: jnp.zeros((d, n), F32)
    o = 0
    cq, o = w[:, o:o + MLA_Q_LORA], o + MLA_Q_LORA
    ckv, o = w[:, o:o + MLA_KV_LORA], o + MLA_KV_LORA
    kr, o = w[:, o:o + MLA_ROPE], o + MLA_ROPE
    naq, o = w[:, o:o + GROUP_W], o + GROUP_W
    nak, o = w[:, o:o + GROUP_W], o + GROUP_W
    nav, o = w[:, o:o + GROUP_W], o + GROUP_W
    rest = w[:, o:]
    w_in_p = jnp.concatenate([cq, z(GROUP_W - MLA_Q_LORA), ckv, z(HEAD_DIM), kr, z(LANES - HEAD_DIM - MLA_ROPE),
                              _pad_heads_cols(naq), _pad_heads_cols(nak), _pad_heads_cols(nav), rest],
                             axis=1).astype(BF16)

    qk_dim = HEAD_DIM + MLA_ROPE
    wuq = mla_w_uq[l].reshape(MLA_Q_LORA, N_HEADS, qk_dim)
    wuq = jnp.concatenate([wuq, jnp.zeros((MLA_Q_LORA, N_HEADS, LANES - qk_dim), F32)], axis=-1)
    wuq = jnp.concatenate([wuq.reshape(MLA_Q_LORA, PAD_W), jnp.zeros((GROUP_W - MLA_Q_LORA, PAD_W), F32)], axis=0)
    wukv = mla_w_ukv[l].reshape(MLA_KV_LORA, N_HEADS, 2 * HEAD_DIM)
    pad64 = jnp.zeros((MLA_KV_LORA, N_HEADS, HEAD_DIM), F32)
    wk = jnp.concatenate([wukv[:, :, :HEAD_DIM], pad64], axis=-1).reshape(MLA_KV_LORA, PAD_W)
    wv = jnp.concatenate([wukv[:, :, HEAD_DIM:], pad64], axis=-1).reshape(MLA_KV_LORA, PAD_W)

    def per_head(parts):
        row = jnp.concatenate(parts + [jnp.zeros((LANES - sum(p.shape[0] for p in parts),), F32)])
        return jnp.tile(row, N_HEADS)[None, :]

    prep = {
        'wuq': wuq.astype(BF16), 'wk': wk.astype(BF16), 'wv': wv.astype(BF16),
        'gcq': jnp.concatenate([mla_g_cq[l], jnp.zeros((GROUP_W - MLA_Q_LORA,), F32)])[None, :],
        'gckv': mla_g_ckv[l][None, :],
        'gkr': jnp.concatenate([jnp.zeros((HEAD_DIM,), F32), mla_g_kr[l],
                                jnp.zeros((LANES - HEAD_DIM - MLA_ROPE,), F32)])[None, :],
        'gq': per_head([mla_g_qn[l], mla_g_qr[l]]),
        'gk': per_head([mla_g_kn[l]]),
        'mq': _seg_mean_matrix(PAD_W, [(0, HEAD_DIM), (HEAD_DIM, MLA_ROPE)]),
        'mk': _seg_mean_matrix(PAD_W, [(0, HEAD_DIM)]),
        'gnq': per_head([na_g_q[l]]),
        'gnk': per_head([na_g_k[l]]),
        'mn': _seg_mean_matrix(PAD_W, [(0, HEAD_DIM)]),
    }
    wo = w_out[l]
    ow = {
        'mla': _pad_heads_rows(wo[0:GROUP_W]).astype(BF16),
        'na': _pad_heads_rows(wo[GROUP_W:2 * GROUP_W]).astype(BF16),
        'ret': wo[2 * GROUP_W:3 * GROUP_W].astype(BF16),
        'hg': wo[3 * GROUP_W:4 * GROUP_W].astype(BF16),
    }
    n_r = MOE_GROUPS + MOE_EXPERTS
    wr = jnp.concatenate([moe_w_rg[l], moe_w_re[l], jnp.zeros((d, LANES - n_r), F32)], axis=1)
    wr_hi = wr.astype(BF16)
    rw = {
        'hi': wr_hi, 'lo': (wr - wr_hi.astype(F32)).astype(BF16),
        'b': jnp.concatenate([moe_b_rg[l], moe_b_re[l], jnp.zeros((LANES - n_r,), F32)])[None, :],
    }
    return w_in_p, prep, ow, rw


def _layer(xx, modsel, lw, rope_c, rope_s, na_bias, rc, hc, hg_lb_l, ret_go, hg_go, w1, w3, w2,
           n_lat, n_ctx, last):
    w_in_p, prep_w, ow, rw = lw
    b, nt, d = xx.shape
    n_lat_tiles = n_lat // TM
    p = _inproj(xx, modsel, w_in_p, n_lat_tiles)
    qm, km, vm, qn, kn, vn = _prep(p, rope_c, rope_s, prep_w)
    y_mla = _mla_attn(qm, km, vm, n_lat, n_ctx)
    y_na = _na_attn(qn, kn, vn, na_bias, n_lat, n_ctx)
    y_ret = _retention(p, rc, ret_go, n_lat, n_ctx)
    y_hg = _hgrn2(p, hc, hg_lb_l, hg_go, n_lat, n_ctx)
    xx = _outproj(xx, y_mla, y_na, y_ret, y_hg, modsel, ow, n_lat_tiles)
    h_flat, route = _router(xx, modsel, rw, n_lat_tiles)
    row_tok, row_gate, block_expert, used, dest_of, n_blocks = _moe_plan(route, b * nt)
    y_rows = _moe_ffn(h_flat, row_tok, row_gate, block_expert, used, n_blocks, w1, w3, w2)
    return _combine(xx, modsel, y_rows, dest_of, n_lat_tiles, n_lat_tiles if last else nt // TM)


def kernel(x, c, ctx, c_ctx, w_ada, b_ada, w_in, w_out, mla_g_cq, mla_g_ckv, mla_w_uq, mla_w_ukv, mla_g_qn, mla_g_qr, mla_g_kn, mla_g_kr, na_g_q, na_g_k, na_rpb, ret_g_out, hg_lb_raw, hg_g_out, moe_w_rg, moe_b_rg, moe_w_re, moe_b_re, moe_w1, moe_w3, moe_w2):
    b, n_lat, d = x.shape
    n_ctx = ctx.shape[1]
    depth = w_in.shape[0]
    assert n_lat % TM == 0 and n_ctx % TM == 0 and n_lat // GRID_W >= NA_KH and TM % GRID_W == 0
    assert w_in.shape[2] == MLA_Q_LORA + MLA_KV_LORA + MLA_ROPE + 12 * GROUP_W

    cc = jnp.concatenate([c, c_ctx[None, :], jnp.zeros((16 - b - 1, d), F32)], axis=0)
    mods = _ada_all(cc, w_ada, b_ada)
    rope_c, rope_s = _rope_tables(n_lat, n_ctx)
    rc = _retention_consts()
    hc = _hgrn_consts()
    lb_w = jax.nn.softmax(hg_lb_raw.astype(F32), axis=0)
    hg_lb = jnp.cumsum(lb_w, axis=0) - lb_w[0:1]

    xx = jnp.concatenate([x, ctx], axis=1)
    tile_go = lambda g: jnp.tile(g, N_HEADS)[None, :]
    for l in range(depth):
        modsel = jnp.stack([mods[l, :b], jnp.broadcast_to(mods[l, b], (b, 6 * d))], axis=1).reshape(2 * b, 1, 6 * d)
        lw = _layer_weights(l, w_in, w_out, mla_g_cq, mla_g_ckv, mla_w_uq, mla_w_ukv, mla_g_qn, mla_g_qr,
                            mla_g_kn, mla_g_kr, na_g_q, na_g_k, moe_w_rg, moe_b_rg, moe_w_re, moe_b_re)
        xx = _layer(xx, modsel, lw, rope_c, rope_s, _na_bias_table(na_rpb[l]), rc, hc, hg_lb[l][None, :],
                    tile_go(ret_g_out[l]), tile_go(hg_g_out[l]),
                    moe_w1[l].astype(BF16), moe_w3[l].astype(BF16), moe_w2[l].astype(BF16),
                    n_lat, n_ctx, l == depth - 1)
    return xx
```

```python
import functools

import numpy as np
import jax
import jax.numpy as jnp
from jax import lax
from jax.experimental import pallas as pl
from jax.experimental.pallas import tpu as pltpu

F32 = jnp.float32
BF16 = jnp.bfloat16

EPS = 1e-6
ROPE_THETA = 10000.0
NEG_BIG = -1e30
F_FLOOR = 1e-20
GRID_W = 64
N_HEADS = 4
HEAD_DIM = 64
LANES = 128
GROUP_W = N_HEADS * HEAD_DIM
PAD_W = N_HEADS * LANES
MLA_Q_LORA = 192
MLA_KV_LORA = 128
MLA_ROPE = 32
MLA_SCALE = (HEAD_DIM + MLA_ROPE) ** -0.5
NA_KH = 8
NA_KW = 16
NA_SCALE = HEAD_DIM ** -0.5
RET_CHUNK = 128
HG_CHUNK = 16
MOE_GROUPS = 4
MOE_PER_GROUP = 8
MOE_EXPERTS = MOE_GROUPS * MOE_PER_GROUP
MOE_TOPK = 2
MOE_BLOCK = 256
TM = 256
VMEM_LIMIT = 56 * 1024 * 1024

COL_NA_Q, COL_NA_K, COL_NA_V = 2, 4, 6
COL_RET_Q, COL_RET_K, COL_RET_V, COL_RET_G = 8, 9, 10, 11
COL_HG_Q, COL_HG_FF, COL_HG_FB, COL_HG_I, COL_HG_G = 12, 13, 14, 15, 16
P_COLS = 17 * GROUP_W


def _cparams(sem):
    return pltpu.CompilerParams(dimension_semantics=sem, vmem_limit_bytes=VMEM_LIMIT)


def _sigmoid(x):
    return 1.0 / (1.0 + jnp.exp(-x))


def _silu(x):
    return x * _sigmoid(x)


def _dot(a, b):
    return jnp.dot(a, b, preferred_element_type=F32)


def _dot_nt(a, b):
    return lax.dot_general(a, b, (((1,), (1,)), ((), ())), preferred_element_type=F32)


def _split_dot_l(x, m, n):
    acc = None
    rem = x
    for i in range(n):
        piece = rem.astype(BF16)
        d = _dot(piece, m)
        acc = d if acc is None else acc + d
        if i + 1 < n:
            rem = rem - piece.astype(F32)
    return acc


def _split_dot_r(m, x, n):
    acc = None
    rem = x
    for i in range(n):
        piece = rem.astype(BF16)
        d = _dot(m, piece)
        acc = d if acc is None else acc + d
        if i + 1 < n:
            rem = rem - piece.astype(F32)
    return acc


def _seg_rms(x, m, gain):
    return x * lax.rsqrt(_split_dot_l(x * x, m, 2) + EPS) * gain


def _ada_kernel(c_ref, w_ref, b_ref, o_ref):
    s = _silu(c_ref[...])
    o_ref[0] = jnp.dot(s, w_ref[0], preferred_element_type=F32,
                       precision=lax.Precision.HIGHEST) + b_ref[0]


def _ada_all(cc, w_ada, b_ada):
    n_layers, d, d6 = w_ada.shape
    bn = 512
    rows = cc.shape[0]
    return pl.pallas_call(
        _ada_kernel,
        grid=(n_layers, d6 // bn),
        in_specs=[pl.BlockSpec((rows, d), lambda l, j: (0, 0)),
                  pl.BlockSpec((1, d, bn), lambda l, j: (l, 0, j)),
                  pl.BlockSpec((1, 1, bn), lambda l, j: (l, 0, j))],
        out_specs=pl.BlockSpec((1, rows, bn), lambda l, j: (l, 0, j)),
        out_shape=jax.ShapeDtypeStruct((n_layers, rows, d6), F32),
        compiler_params=_cparams(("arbitrary", "arbitrary")),
    )(cc, w_ada, b_ada.reshape(n_layers, 1, d6))


def _mod_spec(d6, n_lat_tiles):
    return pl.BlockSpec((1, 1, d6), lambda b, j: (2 * b + (j >= n_lat_tiles).astype(jnp.int32), 0, 0))


def _modulate(x, shift, scale):
    xn = x * lax.rsqrt(jnp.mean(x * x, axis=-1, keepdims=True) + EPS)
    return xn * (1.0 + scale) + shift


def _inproj_kernel(x_ref, mod_ref, w_ref, o_ref, *, d):
    xm = _modulate(x_ref[0], mod_ref[0, :, 0:d], mod_ref[0, :, d:2 * d])
    o_ref[0] = _dot(xm.astype(BF16), w_ref[...])


def _inproj(xx, modsel, w_in_p, n_lat_tiles):
    b, nt, d = xx.shape
    pc = w_in_p.shape[1]
    return pl.pallas_call(
        functools.partial(_inproj_kernel, d=d),
        grid=(b, nt // TM),
        in_specs=[pl.BlockSpec((1, TM, d), lambda i, j: (i, j, 0)),
                  _mod_spec(6 * d, n_lat_tiles),
                  pl.BlockSpec((d, pc), lambda i, j: (0, 0))],
        out_specs=pl.BlockSpec((1, TM, pc), lambda i, j: (i, j, 0)),
        out_shape=jax.ShapeDtypeStruct((b, nt, pc), F32),
        compiler_params=_cparams(("arbitrary", "arbitrary")),
    )(xx, modsel, w_in_p)


def _prep_kernel(p_ref, c_ref, s_ref, wuq_ref, wk_ref, wv_ref, gcq_ref, gckv_ref, gkr_ref,
                 gq_ref, gk_ref, mq_ref, mk_ref, gnq_ref, gnk_ref, mn_ref,
                 qm_ref, km_ref, vm_ref, qn_ref, kn_ref, vn_ref):
    cq = p_ref[0, :, 0:256]
    ckv = p_ref[0, :, 256:384]
    kr = p_ref[0, :, 384:512]
    cqn = cq * lax.rsqrt(jnp.sum(cq * cq, axis=-1, keepdims=True) * (1.0 / MLA_Q_LORA) + EPS) * gcq_ref[...]
    ckvn = (ckv * lax.rsqrt(jnp.mean(ckv * ckv, axis=-1, keepdims=True) + EPS) * gckv_ref[...]).astype(BF16)
    krn = kr * lax.rsqrt(jnp.sum(kr * kr, axis=-1, keepdims=True) * (1.0 / MLA_ROPE) + EPS) * gkr_ref[...]
    q = _seg_rms(_dot(cqn.astype(BF16), wuq_ref[...]), mq_ref[...], gq_ref[...])
    kk = _seg_rms(_dot(ckvn, wk_ref[...]), mk_ref[...], gk_ref[...])
    vv = _dot(ckvn, wv_ref[...])

    cos = c_ref[...]
    sin = s_ref[...]
    lane = lax.broadcasted_iota(jnp.int32, (TM, LANES), 1)
    first = (lane % 16) < 8

    def rope(x):
        partner = jnp.where(first, pltpu.roll(x, LANES - 8, 1), pltpu.roll(x, 8, 1))
        return x * cos + partner * sin

    krr = rope(krn)
    nq = _seg_rms(p_ref[0, :, 512:1024], mn_ref[...], gnq_ref[...])
    nk = _seg_rms(p_ref[0, :, 1024:1536], mn_ref[...], gnk_ref[...])
    for h in range(N_HEADS):
        sl = slice(h * LANES, (h + 1) * LANES)
        qm_ref[0, h] = rope(q[:, sl]).astype(BF16)
        km_ref[0, h] = (kk[:, sl] + krr).astype(BF16)
        vm_ref[0, h] = vv[:, sl].T.astype(BF16)
        qn_ref[0, h] = nq[:, sl].astype(BF16)
        kn_ref[0, h] = nk[:, sl].astype(BF16)
        vn_ref[0, h] = p_ref[0, :, 1536 + h * LANES:1536 + (h + 1) * LANES].astype(BF16)


def _prep(p, rope_c, rope_s, pw):
    b, nt, _ = p.shape
    full = lambda a: pl.BlockSpec(a.shape, lambda i, j: (0,) * a.ndim)
    consts = [pw['wuq'], pw['wk'], pw['wv'], pw['gcq'], pw['gckv'], pw['gkr'], pw['gq'], pw['gk'],
              pw['mq'], pw['mk'], pw['gnq'], pw['gnk'], pw['mn']]
    head_spec = pl.BlockSpec((1, N_HEADS, TM, LANES), lambda i, j: (i, 0, j, 0))
    head_shape = jax.ShapeDtypeStruct((b, N_HEADS, nt, LANES), BF16)
    head_t_spec = pl.BlockSpec((1, N_HEADS, LANES, TM), lambda i, j: (i, 0, 0, j))
    head_t_shape = jax.ShapeDtypeStruct((b, N_HEADS, LANES, nt), BF16)
    return pl.pallas_call(
        _prep_kernel,
        grid=(b, nt // TM),
        in_specs=[pl.BlockSpec((1, TM, 2048), lambda i, j: (i, j, 0)),
                  pl.BlockSpec((TM, LANES), lambda i, j: (j, 0)),
                  pl.BlockSpec((TM, LANES), lambda i, j: (j, 0))] + [full(a) for a in consts],
        out_specs=[head_spec, head_spec, head_t_spec, head_spec, head_spec, head_spec],
        out_shape=[head_shape, head_shape, head_t_shape, head_shape, head_shape, head_shape],
        compiler_params=_cparams(("arbitrary", "arbitrary")),
    )(p, rope_c, rope_s, *consts)


def _softmax_pv(s, v):
    m = jnp.max(s, axis=-1, keepdims=True)
    e = jnp.exp(s - m)
    l = jnp.sum(e, axis=-1, keepdims=True)
    return _dot(e.astype(BF16), v) / l


def _softmax_pv_t(st, vt):
    m = jnp.max(st, axis=0, keepdims=True)
    e = jnp.exp(st - m)
    l = jnp.sum(e, axis=0, keepdims=True)
    return (_dot(vt, e.astype(BF16)) / l).T


def _mla_kernel(q_ref, k_ref, vt_ref, o_ref, *, n_lat, n_ctx):
    j = pl.program_id(2)
    q = q_ref[0, 0]

    @pl.when(j < n_lat // TM)
    def _():
        st = _dot_nt(k_ref[0, 0], q) * MLA_SCALE
        o_ref[0] = _softmax_pv_t(st, vt_ref[0, 0])

    @pl.when(j >= n_lat // TM)
    def _():
        st = _dot_nt(k_ref[0, 0, pl.ds(n_lat, n_ctx), :], q) * MLA_SCALE
        o_ref[0] = _softmax_pv_t(st, vt_ref[0, 0, :, pl.ds(n_lat, n_ctx)])


def _mla_attn(qm, km, vmt, n_lat, n_ctx):
    b, h, nt, _ = qm.shape
    kv_spec = pl.BlockSpec((1, 1, nt, LANES), lambda i, hh, j: (i, hh, 0, 0))
    vt_spec = pl.BlockSpec((1, 1, LANES, nt), lambda i, hh, j: (i, hh, 0, 0))
    return pl.pallas_call(
        functools.partial(_mla_kernel, n_lat=n_lat, n_ctx=n_ctx),
        grid=(b, h, nt // TM),
        in_specs=[pl.BlockSpec((1, 1, TM, LANES), lambda i, hh, j: (i, hh, j, 0)), kv_spec, vt_spec],
        out_specs=pl.BlockSpec((1, TM, LANES), lambda i, hh, j: (i, j, hh)),
        out_shape=jax.ShapeDtypeStruct((b, nt, PAD_W), F32),
        compiler_params=_cparams(("arbitrary", "arbitrary", "arbitrary")),
    )(qm, km, vmt)


def _na_kernel(q_ref, k_ref, v_ref, bias_ref, o_ref, *, n_lat, n_ctx):
    j = pl.program_id(1)
    rows = n_lat // GRID_W
    rows_per_tile = TM // GRID_W
    win = NA_KH * GRID_W

    @pl.when(j < n_lat // TM)
    def _():
        for rr in range(rows_per_tile):
            r = j * rows_per_tile + rr
            start = jnp.clip(r - NA_KH // 2, 0, rows - NA_KH)
            case = start - r + (NA_KH - 1)
            tok0 = pl.multiple_of(start * GRID_W, GRID_W)
            for h in range(N_HEADS):
                q = q_ref[0, h, rr * GRID_W:(rr + 1) * GRID_W, :]
                s1 = _dot_nt(q, k_ref[0, h, pl.ds(tok0, win), :]) * NA_SCALE + bias_ref[case, h]
                s2 = _dot_nt(q, k_ref[0, h, pl.ds(n_lat, n_ctx), :]) * NA_SCALE
                m = jnp.maximum(jnp.max(s1, axis=-1, keepdims=True), jnp.max(s2, axis=-1, keepdims=True))
                e1 = jnp.exp(s1 - m)
                e2 = jnp.exp(s2 - m)
                l = jnp.sum(e1, axis=-1, keepdims=True) + jnp.sum(e2, axis=-1, keepdims=True)
                o = _dot(e1.astype(BF16), v_ref[0, h, pl.ds(tok0, win), :])
                o = o + _dot(e2.astype(BF16), v_ref[0, h, pl.ds(n_lat, n_ctx), :])
                o_ref[0, rr * GRID_W:(rr + 1) * GRID_W, h * LANES:(h + 1) * LANES] = o / l

    @pl.when(j >= n_lat // TM)
    def _():
        for h in range(N_HEADS):
            s = _dot_nt(q_ref[0, h], k_ref[0, h, pl.ds(n_lat, n_ctx), :]) * NA_SCALE
            o_ref[0, :, h * LANES:(h + 1) * LANES] = _softmax_pv(s, v_ref[0, h, pl.ds(n_lat, n_ctx), :])


def _na_attn(qn, kn, vn, bias, n_lat, n_ctx):
    b, h, nt, _ = qn.shape
    kv_spec = pl.BlockSpec((1, h, nt, LANES), lambda i, j: (i, 0, 0, 0))
    return pl.pallas_call(
        functools.partial(_na_kernel, n_lat=n_lat, n_ctx=n_ctx),
        grid=(b, nt // TM),
        in_specs=[pl.BlockSpec((1, h, TM, LANES), lambda i, j: (i, 0, j, 0)), kv_spec, kv_spec,
                  pl.BlockSpec(bias.shape, lambda i, j: (0, 0, 0, 0))],
        out_specs=pl.BlockSpec((1, TM, PAD_W), lambda i, j: (i, j, 0)),
        out_shape=jax.ShapeDtypeStruct((b, nt, PAD_W), F32),
        compiler_params=_cparams(("arbitrary", "arbitrary")),
    )(qn, kn, vn, bias)


def _head_mask(h, shape):
    return (lax.broadcasted_iota(jnp.int32, shape, 1) // HEAD_DIM) == h


def _ret_state_step(s_ref, q, k, v, qw, kw, cd, bd):
    state = s_ref[...]
    o = _dot((q * qw).astype(BF16), state.astype(BF16))
    upd = _dot((k * kw).T.astype(BF16), v.astype(BF16))
    s_ref[...] = state * cd + upd * bd
    return o


def _ret_fwd_kernel(q_ref, k_ref, v_ref, dm_ref, qw_ref, kw_ref, cd_ref, bd_ref, o_ref, s_ref):
    @pl.when(pl.program_id(1) == 0)
    def _():
        s_ref[...] = jnp.zeros_like(s_ref)

    q = q_ref[0]
    k = k_ref[0]
    v = v_ref[0]
    o = _ret_state_step(s_ref, q, k, v, qw_ref[...], kw_ref[...], cd_ref[...], bd_ref[...])
    kb = k.astype(BF16)
    for h in range(N_HEADS):
        hm = _head_mask(h, q.shape)
        sc = _dot_nt(jnp.where(hm, q, 0.0).astype(BF16), kb) * dm_ref[h]
        o = o + _dot(sc.astype(BF16), jnp.where(hm, v, 0.0).astype(BF16))
    o_ref[0] = o


def _ret_bwd_kernel(q_ref, k_ref, v_ref, g_ref, op_ref, qw_ref, kw_ref, cd_ref, bd_ref, ms_ref, go_ref,
                    y_ref, s_ref):
    @pl.when(pl.program_id(1) == 0)
    def _():
        s_ref[...] = jnp.zeros_like(s_ref)

    o = op_ref[0] + _ret_state_step(s_ref, q_ref[0], k_ref[0], v_ref[0], qw_ref[...], kw_ref[...],
                                    cd_ref[...], bd_ref[...])
    y_ref[0] = _seg_rms(o, ms_ref[...], go_ref[...]) * _silu(g_ref[0])


def _scan_order(n_lat_t, n_ctx_t, reverse):
    if reverse:
        return lambda i: jnp.where(i < n_ctx_t, n_lat_t + n_ctx_t - 1 - i, n_lat_t + n_ctx_t - 1 - i)
    return lambda i: jnp.where(i < n_ctx_t, n_lat_t + i, i - n_ctx_t)


def _retention(p, rc, go, n_lat, n_ctx):
    b, nt, _ = p.shape
    c = RET_CHUNK
    n_lat_t, n_ctx_t = n_lat // c, n_ctx // c
    fwd = _scan_order(n_lat_t, n_ctx_t, False)
    bwd = _scan_order(n_lat_t, n_ctx_t, True)
    col = lambda order, cb: pl.BlockSpec((1, c, GROUP_W), lambda i, j: (i, order(j), cb))
    full = lambda a: pl.BlockSpec(a.shape, lambda i, j: (0,) * a.ndim)
    out_shape = jax.ShapeDtypeStruct((b, nt, GROUP_W), F32)
    scratch = [pltpu.VMEM((GROUP_W, GROUP_W), F32)]
    consts_f = [rc['dm'], rc['qw_f'], rc['kw_f'], rc['cd_f'], rc['bd']]
    o_part = pl.pallas_call(
        _ret_fwd_kernel,
        grid=(b, nt // c),
        in_specs=[col(fwd, COL_RET_Q), col(fwd, COL_RET_K), col(fwd, COL_RET_V)] + [full(a) for a in consts_f],
        out_specs=pl.BlockSpec((1, c, GROUP_W), lambda i, j: (i, fwd(j), 0)),
        out_shape=out_shape,
        scratch_shapes=scratch,
        compiler_params=_cparams(("arbitrary", "arbitrary")),
    )(p, p, p, *consts_f)
    consts_b = [rc['qw_b'], rc['kw_b'], rc['cd_b'], rc['bd'], rc['ms'], go]
    return pl.pallas_call(
        _ret_bwd_kernel,
        grid=(b, nt // c),
        in_specs=[col(bwd, COL_RET_Q), col(bwd, COL_RET_K), col(bwd, COL_RET_V), col(bwd, COL_RET_G),
                  pl.BlockSpec((1, c, GROUP_W), lambda i, j: (i, bwd(j), 0))] + [full(a) for a in consts_b],
        out_specs=pl.BlockSpec((1, c, GROUP_W), lambda i, j: (i, bwd(j), 0)),
        out_shape=out_shape,
        scratch_shapes=scratch,
        compiler_params=_cparams(("arbitrary", "arbitrary")),
    )(p, p, p, p, o_part, *consts_b)


def _hg_direction(q_ref, f_ref, v_ref, lb_ref, ain_ref, aex_ref, bseg_ref, bd_ref, st_ref, *, reverse):
    n_chunks = TM // HG_CHUNK
    qh = _silu(q_ref[0])
    lb = lb_ref[...]
    f = jnp.maximum(lb + (1.0 - lb) * _sigmoid(f_ref[0]), F_FLOOR)
    lf = jnp.log(f)
    k = 1.0 - f
    v = v_ref[0]

    a_in = _split_dot_r(ain_ref[...], lf, 3)
    a_ex = _split_dot_r(aex_ref[...], lf, 3)
    qp = (qh * jnp.exp(a_in)).astype(BF16)
    kdec = k * jnp.exp(a_ex)
    lam_all = jnp.exp(a_in + a_ex)
    vt = v.T.astype(BF16)
    row_chunk = lax.broadcasted_iota(jnp.int32, (TM, 1), 0) // HG_CHUNK
    bd = bd_ref[...]
    state = st_ref[...]
    parts = [None] * n_chunks
    for c in (range(n_chunks - 1, -1, -1) if reverse else range(n_chunks)):
        r0 = c * HG_CHUNK
        parts[c] = _dot_nt(qp[r0:r0 + HG_CHUNK], state.astype(BF16))
        upd = _dot(vt, jnp.where(row_chunk == c, kdec, 0.0).astype(BF16))
        state = state * lam_all[r0:r0 + 1] + upd * bd
    st_ref[...] = state
    o = jnp.concatenate(parts, axis=0)

    pos = lax.broadcasted_iota(jnp.int32, (TM, 1), 0) % HG_CHUNK
    bseg = bseg_ref[...]
    g = jnp.zeros_like(lf)
    for dl in range(HG_CHUNK):
        sh = (TM - dl) % TM if reverse else dl
        if dl > 0:
            shl = (TM - (dl - 1)) % TM if reverse else dl - 1
            g = g + (pltpu.roll(lf, shl, 0) if shl else lf)
        ksh = pltpu.roll(k, sh, 0) if sh else k
        vsh = pltpu.roll(v, sh, 0) if sh else v
        valid = (pos <= HG_CHUNK - 1 - dl) if reverse else (pos >= dl)
        w = jnp.where(valid, qh * ksh * jnp.exp(g), 0.0)
        o = o + _dot(w.astype(BF16), bseg) * vsh
    return o


def _hg_fwd_kernel(q_ref, f_ref, v_ref, lb_ref, ain_ref, aex_ref, bseg_ref, bd_ref, o_ref, st_ref):
    @pl.when(pl.program_id(1) == 0)
    def _():
        st_ref[...] = jnp.zeros_like(st_ref)

    o_ref[0] = _hg_direction(q_ref, f_ref, v_ref, lb_ref, ain_ref, aex_ref, bseg_ref, bd_ref, st_ref,
                             reverse=False)


def _hg_bwd_kernel(q_ref, f_ref, v_ref, g_ref, op_ref, lb_ref, ain_ref, aex_ref, bseg_ref, bd_ref,
                   ms_ref, go_ref, y_ref, st_ref):
    @pl.when(pl.program_id(1) == 0)
    def _():
        st_ref[...] = jnp.zeros_like(st_ref)

    o = op_ref[0] + _hg_direction(q_ref, f_ref, v_ref, lb_ref, ain_ref, aex_ref, bseg_ref, bd_ref, st_ref,
                                  reverse=True)
    y_ref[0] = _seg_rms(o, ms_ref[...], go_ref[...]) * _silu(g_ref[0])


def _hgrn2(p, hc, lb, go, n_lat, n_ctx):
    b, nt, _ = p.shape
    n_lat_t, n_ctx_t = n_lat // TM, n_ctx // TM
    fwd = _scan_order(n_lat_t, n_ctx_t, False)
    bwd = _scan_order(n_lat_t, n_ctx_t, True)
    col = lambda order, cb: pl.BlockSpec((1, TM, GROUP_W), lambda i, j: (i, order(j), cb))
    full = lambda a: pl.BlockSpec(a.shape, lambda i, j: (0,) * a.ndim)
    out_shape = jax.ShapeDtypeStruct((b, nt, GROUP_W), F32)
    scratch = [pltpu.VMEM((GROUP_W, GROUP_W), F32)]
    consts_f = [lb, hc['lincl'], hc['uexcl'], hc['bseg'], hc['bd']]
    o_part = pl.pallas_call(
        _hg_fwd_kernel,
        grid=(b, nt // TM),
        in_specs=[col(fwd, COL_HG_Q), col(fwd, COL_HG_FF), col(fwd, COL_HG_I)] + [full(a) for a in consts_f],
        out_specs=pl.BlockSpec((1, TM, GROUP_W), lambda i, j: (i, fwd(j), 0)),
        out_shape=out_shape,
        scratch_shapes=scratch,
        compiler_params=_cparams(("arbitrary", "arbitrary")),
    )(p, p, p, *consts_f)
    consts_b = [lb, hc['uincl'], hc['lexcl'], hc['bseg'], hc['bd'], hc['ms'], go]
    return pl.pallas_call(
        _hg_bwd_kernel,
        grid=(b, nt // TM),
        in_specs=[col(bwd, COL_HG_Q), col(bwd, COL_HG_FB), col(bwd, COL_HG_I), col(bwd, COL_HG_G),
                  pl.BlockSpec((1, TM, GROUP_W), lambda i, j: (i, bwd(j), 0))] + [full(a) for a in consts_b],
        out_specs=pl.BlockSpec((1, TM, GROUP_W), lambda i, j: (i, bwd(j), 0)),
        out_shape=out_shape,
        scratch_shapes=scratch,
        compiler_params=_cparams(("arbitrary", "arbitrary")),
    )(p, p, p, p, o_part, *consts_b)


def _outproj_kernel(x_ref, ym_ref, yn_ref, yr_ref, yh_ref, mod_ref, wm_ref, wn_ref, wr_ref, wh_ref, o_ref, *, d):
    acc = _dot(ym_ref[0].astype(BF16), wm_ref[...])
    acc = acc + _dot(yn_ref[0].astype(BF16), wn_ref[...])
    acc = acc + _dot(yr_ref[0].astype(BF16), wr_ref[...])
    acc = acc + _dot(yh_ref[0].astype(BF16), wh_ref[...])
    o_ref[0] = x_ref[0] + mod_ref[0, :, 2 * d:3 * d] * acc


def _outproj(xx, y_mla, y_na, y_ret, y_hg, modsel, ow, n_lat_tiles):
    b, nt, d = xx.shape
    tile = lambda w: pl.BlockSpec((1, TM, w), lambda i, j: (i, j, 0))
    full = lambda a: pl.BlockSpec(a.shape, lambda i, j: (0, 0))
    ws = [ow['mla'], ow['na'], ow['ret'], ow['hg']]
    return pl.pallas_call(
        functools.partial(_outproj_kernel, d=d),
        grid=(b, nt // TM),
        in_specs=[tile(d), tile(PAD_W), tile(PAD_W), tile(GROUP_W), tile(GROUP_W),
                  _mod_spec(6 * d, n_lat_tiles)] + [full(a) for a in ws],
        out_specs=tile(d),
        out_shape=jax.ShapeDtypeStruct((b, nt, d), F32),
        compiler_params=_cparams(("arbitrary", "arbitrary")),
    )(xx, y_mla, y_na, y_ret, y_hg, modsel, *ws)


def _router_kernel(x_ref, mod_ref, whi_ref, wlo_ref, br_ref, ltri_ref, r_ref, cnt_ref, *, d):
    @pl.when((pl.program_id(0) == 0) & (pl.program_id(1) == 0))
    def _():
        cnt_ref[...] = jnp.zeros_like(cnt_ref)

    h = _modulate(x_ref[0], mod_ref[0, :, 3 * d:4 * d], mod_ref[0, :, 4 * d:5 * d])
    h_hi = h.astype(BF16)
    h_lo = (h - h_hi.astype(F32)).astype(BF16)
    lg = _dot(h_hi, whi_ref[...]) + _dot(h_lo, whi_ref[...]) + _dot(h_hi, wlo_ref[...]) + br_ref[...]

    lane = lax.broadcasted_iota(jnp.int32, lg.shape, 1).astype(F32)
    far = 1e9

    def first_argmax(vals, vmax):
        return jnp.min(jnp.where(vals == vmax, lane, far), axis=-1, keepdims=True)

    gl = jnp.where(lane < MOE_GROUPS, lg, NEG_BIG)
    gmax = jnp.max(gl, axis=-1, keepdims=True)
    pg_top = 1.0 / jnp.sum(jnp.exp(gl - gmax), axis=-1, keepdims=True)
    lo = MOE_GROUPS + MOE_PER_GROUP * first_argmax(gl, gmax)
    fl = jnp.where((lane >= lo) & (lane < lo + MOE_PER_GROUP), lg, NEG_BIG)
    fmax = jnp.max(fl, axis=-1, keepdims=True)
    fsum = jnp.sum(jnp.exp(fl - fmax), axis=-1, keepdims=True)
    i1 = first_argmax(fl, fmax)
    fl2 = jnp.where(lane == i1, NEG_BIG, fl)
    f2max = jnp.max(fl2, axis=-1, keepdims=True)
    i2 = first_argmax(fl2, f2max)
    p1 = 1.0 / fsum
    p2 = jnp.exp(f2max - fmax) / fsum
    g1 = pg_top * p1 / (p1 + p2)
    g2 = pg_top * p2 / (p1 + p2)
    e1 = i1 - MOE_GROUPS
    e2 = i2 - MOE_GROUPS

    onehot = jnp.where(lane == e1, 1.0, 0.0) + jnp.where(lane == e2, 1.0, 0.0)
    before = cnt_ref[...] + _dot(ltri_ref[...], onehot.astype(BF16))
    r1 = jnp.sum(jnp.where(lane == e1, before, 0.0), axis=-1, keepdims=True)
    r2 = jnp.sum(jnp.where(lane == e2, before, 0.0), axis=-1, keepdims=True)
    cnt_ref[...] += jnp.sum(onehot, axis=0, keepdims=True)

    out = jnp.zeros_like(lg)
    for col, val in enumerate((e1, e2, g1, g2, r1, r2)):
        out = jnp.where(lane == col, val, out)
    r_ref[...] = out


ROUTE_E, ROUTE_G, ROUTE_R = 0, 2, 4


def _router(xx, modsel, rw, n_lat_tiles):
    b, nt, d = xx.shape
    tiles = nt // TM
    full = lambda a: pl.BlockSpec(a.shape, lambda i, j: (0, 0))
    ltri = jnp.asarray(np.tril(np.ones((TM, TM), np.float32), -1), BF16)
    ws = [rw['hi'], rw['lo'], rw['b'], ltri]
    return pl.pallas_call(
        functools.partial(_router_kernel, d=d),
        grid=(b, tiles),
        in_specs=[pl.BlockSpec((1, TM, d), lambda i, j: (i, j, 0)), _mod_spec(6 * d, n_lat_tiles)]
                 + [full(a) for a in ws],
        out_specs=[pl.BlockSpec((TM, LANES), lambda i, j: (i * tiles + j, 0)),
                   pl.BlockSpec((1, LANES), lambda i, j: (0, 0))],
        out_shape=[jax.ShapeDtypeStruct((b * nt, LANES), F32), jax.ShapeDtypeStruct((1, LANES), F32)],
        compiler_params=_cparams(("arbitrary", "arbitrary")),
    )(xx, modsel, *ws)


def _moe_plan(route, counts_f, n_tok):
    counts = counts_f[0, :MOE_EXPERTS].astype(jnp.int32)
    padded = (counts + MOE_BLOCK - 1) // MOE_BLOCK * MOE_BLOCK
    pad_end = jnp.cumsum(padded)
    pad_start = pad_end - padded
    n_blocks = -(-(n_tok * MOE_TOPK) // MOE_BLOCK) + MOE_EXPERTS
    blk0 = jnp.arange(n_blocks, dtype=jnp.int32) * MOE_BLOCK
    block_expert = jnp.minimum(jnp.sum((pad_end[None, :] <= blk0[:, None]).astype(jnp.int32), axis=1),
                               MOE_EXPERTS - 1)
    used = (pad_end[-1] // MOE_BLOCK).reshape(1)
    expert = route[:, ROUTE_E:ROUTE_E + MOE_TOPK].astype(jnp.int32)
    rank = route[:, ROUTE_R:ROUTE_R + MOE_TOPK].astype(jnp.int32)
    start_of = jnp.sum(jnp.where(expert[..., None] == jnp.arange(MOE_EXPERTS, dtype=jnp.int32), pad_start, 0), axis=-1)
    return start_of + rank, block_expert, used, n_blocks


def _idx_blocks(dest, k, n_tiles):
    return dest[:, k].reshape(n_tiles, 1, TM)


def _dispatch_kernel(d0_ref, d1_ref, x_ref, mod_ref, rows_in, rows_out, h_ref, sem, *, d):
    del rows_in
    h_ref[...] = _modulate(x_ref[0], mod_ref[0, :, 3 * d:4 * d], mod_ref[0, :, 4 * d:5 * d])

    def issue(r, carry):
        src = h_ref.at[pl.ds(r, 1)]
        pltpu.make_async_copy(src, rows_out.at[pl.ds(d0_ref[0, 0, r], 1)], sem).start()
        pltpu.make_async_copy(src, rows_out.at[pl.ds(d1_ref[0, 0, r], 1)], sem).start()
        return carry

    lax.fori_loop(0, TM, issue, 0)
    for _ in range(MOE_TOPK):
        pltpu.make_async_copy(h_ref, rows_out.at[pl.ds(0, TM)], sem).wait()


def _dispatch(xx, modsel, dest, n_rows, n_lat_tiles):
    b, nt, d = xx.shape
    tiles = nt // TM
    idx_spec = pl.BlockSpec((1, 1, TM), lambda i, j: (i * tiles + j, 0, 0), memory_space=pltpu.SMEM)
    return pl.pallas_call(
        functools.partial(_dispatch_kernel, d=d),
        grid=(b, tiles),
        in_specs=[idx_spec, idx_spec, pl.BlockSpec((1, TM, d), lambda i, j: (i, j, 0)),
                  _mod_spec(6 * d, n_lat_tiles), pl.BlockSpec(memory_space=pl.ANY)],
        out_specs=pl.BlockSpec(memory_space=pl.ANY),
        out_shape=jax.ShapeDtypeStruct((n_rows, d), F32),
        scratch_shapes=[pltpu.VMEM((TM, d), F32), pltpu.SemaphoreType.DMA],
        input_output_aliases={4: 0},
        compiler_params=_cparams(("arbitrary", "arbitrary")),
    )(_idx_blocks(dest, 0, b * tiles), _idx_blocks(dest, 1, b * tiles), xx, modsel, jnp.zeros((n_rows, d), F32))


def _ffn_kernel(be_ref, used_ref, x_ref, w1_ref, w3_ref, w2_ref, y_ref, w1b_ref, w3b_ref, w2b_ref):
    i = pl.program_id(0)

    @pl.when((i == 0) | (be_ref[i] != be_ref[jnp.maximum(i - 1, 0)]))
    def _():
        w1b_ref[...] = w1_ref[0, 0].astype(BF16)
        w3b_ref[...] = w3_ref[0, 0].astype(BF16)
        w2b_ref[...] = w2_ref[0, 0].astype(BF16)

    @pl.when(i < used_ref[0])
    def _():
        x = x_ref[...].astype(BF16)
        mid = _silu(_dot(x, w1b_ref[...])) * _dot(x, w3b_ref[...])
        y_ref[...] = _dot(mid.astype(BF16), w2b_ref[...])

    @pl.when(i >= used_ref[0])
    def _():
        y_ref[...] = jnp.zeros_like(y_ref)


def _moe_ffn(x_rows, block_expert, used, n_blocks, layer, w1, w3, w2):
    d = x_rows.shape[1]
    ff = w1.shape[3]
    grid_spec = pltpu.PrefetchScalarGridSpec(
        num_scalar_prefetch=2,
        grid=(n_blocks,),
        in_specs=[pl.BlockSpec((MOE_BLOCK, d), lambda i, be, nu: (i, 0)),
                  pl.BlockSpec((1, 1, d, ff), lambda i, be, nu: (layer, be[i], 0, 0)),
                  pl.BlockSpec((1, 1, d, ff), lambda i, be, nu: (layer, be[i], 0, 0)),
                  pl.BlockSpec((1, 1, ff, d), lambda i, be, nu: (layer, be[i], 0, 0))],
        out_specs=pl.BlockSpec((MOE_BLOCK, d), lambda i, be, nu: (i, 0)),
        scratch_shapes=[pltpu.VMEM((d, ff), BF16), pltpu.VMEM((d, ff), BF16), pltpu.VMEM((ff, d), BF16)],
    )
    return pl.pallas_call(
        _ffn_kernel,
        grid_spec=grid_spec,
        out_shape=jax.ShapeDtypeStruct(x_rows.shape, F32),
        compiler_params=_cparams(("arbitrary",)),
    )(block_expert, used, x_rows, w1, w3, w2)


def _row_gather(src_hbm, idx_ref, dst_ref, sem, n):
    def issue(r, carry):
        pltpu.make_async_copy(src_hbm.at[pl.ds(idx_ref[0, 0, r], 1)], dst_ref.at[pl.ds(r, 1)], sem).start()
        return carry

    lax.fori_loop(0, n, issue, 0)


def _row_gather_wait(src_hbm, dst_ref, sem, n):
    pltpu.make_async_copy(src_hbm.at[pl.ds(0, n)], dst_ref, sem).wait()


def _combine_kernel(d0_ref, d1_ref, x_ref, mod_ref, r_ref, y_hbm, o_ref, y0_ref, y1_ref, sem0, sem1, *, d):
    _row_gather(y_hbm, d0_ref, y0_ref, sem0, TM)
    _row_gather(y_hbm, d1_ref, y1_ref, sem1, TM)
    route = r_ref[...]
    lane = lax.broadcasted_iota(jnp.int32, route.shape, 1)
    g0 = jnp.sum(jnp.where(lane == ROUTE_G, route, 0.0), axis=-1, keepdims=True)
    g1 = jnp.sum(jnp.where(lane == ROUTE_G + 1, route, 0.0), axis=-1, keepdims=True)
    _row_gather_wait(y_hbm, y0_ref, sem0, TM)
    _row_gather_wait(y_hbm, y1_ref, sem1, TM)
    o_ref[0] = x_ref[0] + mod_ref[0, :, 5 * d:6 * d] * (y0_ref[...] * g0 + y1_ref[...] * g1)


def _combine(xx, modsel, route, y_rows, dest, n_lat_tiles, out_tiles):
    b, nt, d = xx.shape
    tiles = nt // TM
    idx_spec = pl.BlockSpec((1, 1, TM), lambda i, j: (i * tiles + j, 0, 0), memory_space=pltpu.SMEM)
    return pl.pallas_call(
        functools.partial(_combine_kernel, d=d),
        grid=(b, out_tiles),
        in_specs=[idx_spec, idx_spec, pl.BlockSpec((1, TM, d), lambda i, j: (i, j, 0)),
                  _mod_spec(6 * d, n_lat_tiles),
                  pl.BlockSpec((TM, LANES), lambda i, j: (i * tiles + j, 0)),
                  pl.BlockSpec(memory_space=pl.ANY)],
        out_specs=pl.BlockSpec((1, TM, d), lambda i, j: (i, j, 0)),
        out_shape=jax.ShapeDtypeStruct((b, out_tiles * TM, d), F32),
        scratch_shapes=[pltpu.VMEM((TM, d), F32), pltpu.VMEM((TM, d), F32),
                        pltpu.SemaphoreType.DMA, pltpu.SemaphoreType.DMA],
        compiler_params=_cparams(("arbitrary", "arbitrary")),
    )(_idx_blocks(dest, 0, b * tiles), _idx_blocks(dest, 1, b * tiles), xx, modsel, route, y_rows)


def _pad_heads_cols(w):
    lead = w.shape[:-1]
    w = w.reshape(*lead, N_HEADS, HEAD_DIM)
    w = jnp.concatenate([w, jnp.zeros_like(w)], axis=-1)
    return w.reshape(*lead, PAD_W)


def _pad_heads_rows(w):
    return _pad_heads_cols(w.T).T


def _seg_mean_matrix(width, segments):
    m = np.zeros((width, width), np.float32)
    for g in range(width // LANES):
        for start, length in segments:
            a = g * LANES + start
            m[a:a + length, a:a + length] = 1.0 / length
    return jnp.asarray(m, BF16)


def _rope_tables(n_lat, n_ctx):
    pos = jnp.arange(n_lat)
    rows = (pos // GRID_W).astype(F32)
    cols = (pos % GRID_W).astype(F32)
    per_axis = MLA_ROPE // 2
    inv_freq = ROPE_THETA ** (-jnp.arange(0, per_axis, 2, dtype=F32) / per_axis)
    ang = jnp.concatenate([rows[:, None] * inv_freq, cols[:, None] * inv_freq], axis=-1)
    i = np.arange(MLA_ROPE)
    src = (i // 16) * 8 + (i % 8)
    sign = np.where((i % 16) < 8, -1.0, 1.0).astype(np.float32)
    cos = jnp.ones((n_lat, LANES), F32).at[:, HEAD_DIM:HEAD_DIM + MLA_ROPE].set(jnp.cos(ang)[:, src])
    sin = jnp.zeros((n_lat, LANES), F32).at[:, HEAD_DIM:HEAD_DIM + MLA_ROPE].set(jnp.sin(ang)[:, src] * sign)
    cos = jnp.concatenate([cos, jnp.ones((n_ctx, LANES), F32)], axis=0)
    sin = jnp.concatenate([sin, jnp.zeros((n_ctx, LANES), F32)], axis=0)
    return cos, sin


def _na_bias_table(rpb):
    w = np.arange(GRID_W)
    col_start = np.clip(w - NA_KW // 2, 0, GRID_W - NA_KW)
    valid = (w[None, :] >= col_start[:, None]) & (w[None, :] < col_start[:, None] + NA_KW)
    dc = np.clip(w[None, :] - w[:, None], 1 - NA_KW, NA_KW - 1) + (NA_KW - 1)
    onehot = jnp.asarray(dc[None, :, :] == np.arange(2 * NA_KW - 1)[:, None, None], F32)
    t = jnp.einsum('hrd,dqk->hrqk', rpb.astype(F32), onehot, precision=lax.Precision.HIGHEST)
    t = jnp.where(jnp.asarray(valid)[None, None, :, :], t, NEG_BIG)
    cases = [jnp.transpose(t[:, c:c + NA_KH], (0, 2, 1, 3)).reshape(N_HEADS, GRID_W, NA_KH * GRID_W)
             for c in range(NA_KH)]
    return jnp.stack(cases, axis=0)


def _block_diag_mask(block):
    i = np.arange(GROUP_W) // block
    return (i[:, None] == i[None, :]).astype(np.float32)


def _retention_consts():
    c = RET_CHUNK
    j = np.arange(2 * N_HEADS, dtype=np.float64)
    lg = np.log1p(-np.exp2(-5.0 - j))
    lg_f, lg_b = lg[0::2], lg[1::2]
    pos = np.arange(c, dtype=np.float64)
    diff = pos[:, None] - pos[None, :]
    k_scale = HEAD_DIM ** -0.5
    dm = np.zeros((N_HEADS, c, c))
    for h in range(N_HEADS):
        dm[h] = (np.where(diff >= 0, np.exp(np.maximum(diff, 0.0) * lg_f[h]), 0.0)
                 + np.where(diff <= 0, np.exp(np.maximum(-diff, 0.0) * lg_b[h]), 0.0)) * k_scale
    lanes = lambda per_head: np.repeat(per_head, HEAD_DIM, axis=-1)
    out = {
        'dm': dm,
        'qw_f': lanes(np.exp((pos + 1)[:, None] * lg_f[None, :])),
        'kw_f': lanes(np.exp((c - 1 - pos)[:, None] * lg_f[None, :])) * k_scale,
        'cd_f': lanes(np.exp(c * lg_f)[None, :]),
        'qw_b': lanes(np.exp((c - pos)[:, None] * lg_b[None, :])),
        'kw_b': lanes(np.exp(pos[:, None] * lg_b[None, :])) * k_scale,
        'cd_b': lanes(np.exp(c * lg_b)[None, :]),
        'bd': _block_diag_mask(HEAD_DIM),
    }
    out = {k: jnp.asarray(v, F32) for k, v in out.items()}
    out['ms'] = jnp.asarray(_block_diag_mask(HEAD_DIM) / HEAD_DIM, BF16)
    return out


def _hgrn_consts():
    t = np.arange(TM)
    same = (t[:, None] // HG_CHUNK) == (t[None, :] // HG_CHUNK)
    lincl = same & (t[None, :] <= t[:, None])
    lexcl = same & (t[None, :] < t[:, None])
    return {
        'lincl': jnp.asarray(lincl, BF16), 'lexcl': jnp.asarray(lexcl, BF16),
        'uincl': jnp.asarray(lincl.T, BF16), 'uexcl': jnp.asarray(lexcl.T, BF16),
        'bseg': jnp.asarray(_block_diag_mask(HEAD_DIM), BF16),
        'bd': jnp.asarray(_block_diag_mask(HEAD_DIM), F32),
        'ms': jnp.asarray(_block_diag_mask(HEAD_DIM) / HEAD_DIM, BF16),
    }


def _layer_weights(l, w_in, w_out, mla_g_cq, mla_g_ckv, mla_w_uq, mla_w_ukv, mla_g_qn, mla_g_qr, mla_g_kn,
                   mla_g_kr, na_g_q, na_g_k, moe_w_rg, moe_b_rg, moe_w_re, moe_b_re):
    d = w_in.shape[1]
    w = w_in[l]
    z = lambda n: jnp.zeros((d, n), F32)
    o = 0
    cq, o = w[:, o:o + MLA_Q_LORA], o + MLA_Q_LORA
    ckv, o = w[:, o:o + MLA_KV_LORA], o + MLA_KV_LORA
    kr, o = w[:, o:o + MLA_ROPE], o + MLA_ROPE
    naq, o = w[:, o:o + GROUP_W], o + GROUP_W
    nak, o = w[:, o:o + GROUP_W], o + GROUP_W
    nav, o = w[:, o:o + GROUP_W], o + GROUP_W
    rest = w[:, o:]
    w_in_p = jnp.concatenate([cq, z(GROUP_W - MLA_Q_LORA), ckv, z(HEAD_DIM), kr, z(LANES - HEAD_DIM - MLA_ROPE),
                              _pad_heads_cols(naq), _pad_heads_cols(nak), _pad_heads_cols(nav), rest],
                             axis=1).astype(BF16)

    qk_dim = HEAD_DIM + MLA_ROPE
    wuq = mla_w_uq[l].reshape(MLA_Q_LORA, N_HEADS, qk_dim)
    wuq = jnp.concatenate([wuq, jnp.zeros((MLA_Q_LORA, N_HEADS, LANES - qk_dim), F32)], axis=-1)
    wuq = jnp.concatenate([wuq.reshape(MLA_Q_LORA, PAD_W), jnp.zeros((GROUP_W - MLA_Q_LORA, PAD_W), F32)], axis=0)
    wukv = mla_w_ukv[l].reshape(MLA_KV_LORA, N_HEADS, 2 * HEAD_DIM)
    pad64 = jnp.zeros((MLA_KV_LORA, N_HEADS, HEAD_DIM), F32)
    wk = jnp.concatenate([wukv[:, :, :HEAD_DIM], pad64], axis=-1).reshape(MLA_KV_LORA, PAD_W)
    wv = jnp.concatenate([wukv[:, :, HEAD_DIM:], pad64], axis=-1).reshape(MLA_KV_LORA, PAD_W)

    def per_head(parts):
        row = jnp.concatenate(parts + [jnp.zeros((LANES - sum(p.shape[0] for p in parts),), F32)])
        return jnp.tile(row, N_HEADS)[None, :]

    prep = {
        'wuq': wuq.astype(BF16), 'wk': wk.astype(BF16), 'wv': wv.astype(BF16),
        'gcq': jnp.concatenate([mla_g_cq[l], jnp.zeros((GROUP_W - MLA_Q_LORA,), F32)])[None, :],
        'gckv': mla_g_ckv[l][None, :],
        'gkr': jnp.concatenate([jnp.zeros((HEAD_DIM,), F32), mla_g_kr[l],
                                jnp.zeros((LANES - HEAD_DIM - MLA_ROPE,), F32)])[None, :],
        'gq': per_head([mla_g_qn[l], mla_g_qr[l]]),
        'gk': per_head([mla_g_kn[l]]),
        'mq': _seg_mean_matrix(PAD_W, [(0, HEAD_DIM), (HEAD_DIM, MLA_ROPE)]),
        'mk': _seg_mean_matrix(PAD_W, [(0, HEAD_DIM)]),
        'gnq': per_head([na_g_q[l]]),
        'gnk': per_head([na_g_k[l]]),
        'mn': _seg_mean_matrix(PAD_W, [(0, HEAD_DIM)]),
    }
    wo = w_out[l]
    ow = {
        'mla': _pad_heads_rows(wo[0:GROUP_W]).astype(BF16),
        'na': _pad_heads_rows(wo[GROUP_W:2 * GROUP_W]).astype(BF16),
        'ret': wo[2 * GROUP_W:3 * GROUP_W].astype(BF16),
        'hg': wo[3 * GROUP_W:4 * GROUP_W].astype(BF16),
    }
    n_r = MOE_GROUPS + MOE_EXPERTS
    wr = jnp.concatenate([moe_w_rg[l], moe_w_re[l], jnp.zeros((d, LANES - n_r), F32)], axis=1)
    wr_hi = wr.astype(BF16)
    rw = {
        'hi': wr_hi, 'lo': (wr - wr_hi.astype(F32)).astype(BF16),
        'b': jnp.concatenate([moe_b_rg[l], moe_b_re[l], jnp.zeros((LANES - n_r,), F32)])[None, :],
    }
    return w_in_p, prep, ow, rw


def _layer(xx, modsel, lw, rope_c, rope_s, na_bias, rc, hc, hg_lb_l, ret_go, hg_go, layer, w1, w3, w2,
           n_lat, n_ctx, last):
    w_in_p, prep_w, ow, rw = lw
    b, nt, d = xx.shape
    n_lat_tiles = n_lat // TM
    p = _inproj(xx, modsel, w_in_p, n_lat_tiles)
    qm, km, vm, qn, kn, vn = _prep(p, rope_c, rope_s, prep_w)
    y_mla = _mla_attn(qm, km, vm, n_lat, n_ctx)
    y_na = _na_attn(qn, kn, vn, na_bias, n_lat, n_ctx)
    y_ret = _retention(p, rc, ret_go, n_lat, n_ctx)
    y_hg = _hgrn2(p, hc, hg_lb_l, hg_go, n_lat, n_ctx)
    xx = _outproj(xx, y_mla, y_na, y_ret, y_hg, modsel, ow, n_lat_tiles)
    route, counts = _router(xx, modsel, rw, n_lat_tiles)
    dest, block_expert, used, n_blocks = _moe_plan(route, counts, b * nt)
    x_rows = _dispatch(xx, modsel, dest, n_blocks * MOE_BLOCK, n_lat_tiles)
    y_rows = _moe_ffn(x_rows, block_expert, used, n_blocks, layer, w1, w3, w2)
    return _combine(xx, modsel, route, y_rows, dest, n_lat_tiles, n_lat_tiles if last else nt // TM)


def kernel(x, c, ctx, c_ctx, w_ada, b_ada, w_in, w_out, mla_g_cq, mla_g_ckv, mla_w_uq, mla_w_ukv, mla_g_qn, mla_g_qr, mla_g_kn, mla_g_kr, na_g_q, na_g_k, na_rpb, ret_g_out, hg_lb_raw, hg_g_out, moe_w_rg, moe_b_rg, moe_w_re, moe_b_re, moe_w1, moe_w3, moe_w2):
    b, n_lat, d = x.shape
    n_ctx = ctx.shape[1]
    depth = w_in.shape[0]
    assert n_lat % TM == 0 and n_ctx % TM == 0 and n_lat // GRID_W >= NA_KH and TM % GRID_W == 0
    assert w_in.shape[2] == MLA_Q_LORA + MLA_KV_LORA + MLA_ROPE + 12 * GROUP_W

    cc = jnp.concatenate([c, c_ctx[None, :], jnp.zeros((16 - b - 1, d), F32)], axis=0)
    mods = _ada_all(cc, w_ada, b_ada)
    rope_c, rope_s = _rope_tables(n_lat, n_ctx)
    rc = _retention_consts()
    hc = _hgrn_consts()
    lb_w = jax.nn.softmax(hg_lb_raw.astype(F32), axis=0)
    hg_lb = jnp.cumsum(lb_w, axis=0) - lb_w[0:1]

    xx = jnp.concatenate([x, ctx], axis=1)
    tile_go = lambda g: jnp.tile(g, N_HEADS)[None, :]
    for l in range(depth):
        modsel = jnp.stack([mods[l, :b], jnp.broadcast_to(mods[l, b], (b, 6 * d))], axis=1).reshape(2 * b, 1, 6 * d)
        lw = _layer_weights(l, w_in, w_out, mla_g_cq, mla_g_ckv, mla_w_uq, mla_w_ukv, mla_g_qn, mla_g_qr,
                            mla_g_kn, mla_g_kr, na_g_q, na_g_k, moe_w_rg, moe_b_rg, moe_w_re, moe_b_re)
        xx = _layer(xx, modsel, lw, rope_c, rope_s, _na_bias_table(na_rpb[l]), rc, hc, hg_lb[l][None, :],
                    tile_go(ret_g_out[l]), tile_go(hg_g_out[l]),
                    l, moe_w1, moe_w3, moe_w2,
                    n_lat, n_ctx, l == depth - 1)
    return xx
```

```python
import functools

import numpy as np
import jax
import jax.numpy as jnp
from jax import lax
from jax.experimental import pallas as pl
from jax.experimental.pallas import tpu as pltpu

F32 = jnp.float32
BF16 = jnp.bfloat16

EPS = 1e-6
ROPE_THETA = 10000.0
NEG_BIG = -1e30
F_FLOOR = 1e-20
GRID_W = 64
N_HEADS = 4
HEAD_DIM = 64
LANES = 128
GROUP_W = N_HEADS * HEAD_DIM
PAD_W = N_HEADS * LANES
MLA_Q_LORA = 192
MLA_KV_LORA = 128
MLA_ROPE = 32
MLA_SCALE = (HEAD_DIM + MLA_ROPE) ** -0.5
NA_KH = 8
NA_KW = 16
NA_SCALE = HEAD_DIM ** -0.5
RET_CHUNK = 128
HG_CHUNK = 16
MOE_GROUPS = 4
MOE_PER_GROUP = 8
MOE_EXPERTS = MOE_GROUPS * MOE_PER_GROUP
MOE_TOPK = 2
MOE_BLOCK = 256
TM = 256
NA_TILE_ROWS = TM // GRID_W
NA_WIN_ROWS = NA_TILE_ROWS + NA_KH
VMEM_LIMIT = 56 * 1024 * 1024
DMA_ISSUE_UNROLL = 8

COL_NA_Q, COL_NA_K, COL_NA_V = 2, 4, 6
COL_RET_Q, COL_RET_K, COL_RET_V, COL_RET_G = 8, 9, 10, 11
COL_HG_Q, COL_HG_FF, COL_HG_FB, COL_HG_I, COL_HG_G = 12, 13, 14, 15, 16
P_COLS = 17 * GROUP_W


def _cparams(sem):
    return pltpu.CompilerParams(dimension_semantics=sem, vmem_limit_bytes=VMEM_LIMIT)


def _sigmoid(x):
    return 1.0 / (1.0 + jnp.exp(-x))


def _silu(x):
    return x * _sigmoid(x)


def _dot(a, b):
    return jnp.dot(a, b, preferred_element_type=F32)


def _dot_nt(a, b):
    return lax.dot_general(a, b, (((1,), (1,)), ((), ())), preferred_element_type=F32)


def _split_dot_l(x, m, n):
    acc = None
    rem = x
    for i in range(n):
        piece = rem.astype(BF16)
        d = _dot(piece, m)
        acc = d if acc is None else acc + d
        if i + 1 < n:
            rem = rem - piece.astype(F32)
    return acc


def _split_dot_r(m, x, n):
    acc = None
    rem = x
    for i in range(n):
        piece = rem.astype(BF16)
        d = _dot(m, piece)
        acc = d if acc is None else acc + d
        if i + 1 < n:
            rem = rem - piece.astype(F32)
    return acc


def _seg_rms(x, m, gain):
    return x * lax.rsqrt(_split_dot_l(x * x, m, 2) + EPS) * gain


def _ada_kernel(c_ref, w_ref, b_ref, o_ref):
    s = _silu(c_ref[...])
    o_ref[0] = jnp.dot(s, w_ref[0], preferred_element_type=F32,
                       precision=lax.Precision.HIGHEST) + b_ref[0]


def _ada_all(cc, w_ada, b_ada):
    n_layers, d, d6 = w_ada.shape
    bn = 512
    rows = cc.shape[0]
    return pl.pallas_call(
        _ada_kernel,
        grid=(n_layers, d6 // bn),
        in_specs=[pl.BlockSpec((rows, d), lambda l, j: (0, 0)),
                  pl.BlockSpec((1, d, bn), lambda l, j: (l, 0, j)),
                  pl.BlockSpec((1, 1, bn), lambda l, j: (l, 0, j))],
        out_specs=pl.BlockSpec((1, rows, bn), lambda l, j: (l, 0, j)),
        out_shape=jax.ShapeDtypeStruct((n_layers, rows, d6), F32),
        compiler_params=_cparams(("arbitrary", "arbitrary")),
    )(cc, w_ada, b_ada.reshape(n_layers, 1, d6))


def _mod_spec(d6, n_lat_tiles):
    return pl.BlockSpec((1, 1, d6), lambda b, j: (2 * b + (j >= n_lat_tiles).astype(jnp.int32), 0, 0))


def _modulate(x, shift, scale):
    xn = x * lax.rsqrt(jnp.mean(x * x, axis=-1, keepdims=True) + EPS)
    return xn * (1.0 + scale) + shift


def _inproj_kernel(x_ref, mod_ref, w_ref, o_ref, *, d):
    xm = _modulate(x_ref[0], mod_ref[0, :, 0:d], mod_ref[0, :, d:2 * d])
    o_ref[0] = _dot(xm.astype(BF16), w_ref[...])


def _inproj(xx, modsel, w_in_p, n_lat_tiles):
    b, nt, d = xx.shape
    pc = w_in_p.shape[1]
    return pl.pallas_call(
        functools.partial(_inproj_kernel, d=d),
        grid=(b, nt // TM),
        in_specs=[pl.BlockSpec((1, TM, d), lambda i, j: (i, j, 0)),
                  _mod_spec(6 * d, n_lat_tiles),
                  pl.BlockSpec((d, pc), lambda i, j: (0, 0))],
        out_specs=pl.BlockSpec((1, TM, pc), lambda i, j: (i, j, 0)),
        out_shape=jax.ShapeDtypeStruct((b, nt, pc), F32),
        compiler_params=_cparams(("arbitrary", "arbitrary")),
    )(xx, modsel, w_in_p)


def _prep_kernel(p_ref, c_ref, s_ref, wuq_ref, wk_ref, wv_ref, gcq_ref, gckv_ref, gkr_ref,
                 gq_ref, gk_ref, mq_ref, mk_ref, gnq_ref, gnk_ref, mn_ref,
                 qm_ref, km_ref, vm_ref, qn_ref, kn_ref, vn_ref):
    cq = p_ref[0, :, 0:256]
    ckv = p_ref[0, :, 256:384]
    kr = p_ref[0, :, 384:512]
    cqn = cq * lax.rsqrt(jnp.sum(cq * cq, axis=-1, keepdims=True) * (1.0 / MLA_Q_LORA) + EPS) * gcq_ref[...]
    ckvn = (ckv * lax.rsqrt(jnp.mean(ckv * ckv, axis=-1, keepdims=True) + EPS) * gckv_ref[...]).astype(BF16)
    krn = kr * lax.rsqrt(jnp.sum(kr * kr, axis=-1, keepdims=True) * (1.0 / MLA_ROPE) + EPS) * gkr_ref[...]
    q = _seg_rms(_dot(cqn.astype(BF16), wuq_ref[...]), mq_ref[...], gq_ref[...])
    kk = _seg_rms(_dot(ckvn, wk_ref[...]), mk_ref[...], gk_ref[...])
    vv = _dot(ckvn, wv_ref[...])

    cos = c_ref[...]
    sin = s_ref[...]
    lane = lax.broadcasted_iota(jnp.int32, (TM, LANES), 1)
    first = (lane % 16) < 8

    def rope(x):
        partner = jnp.where(first, pltpu.roll(x, LANES - 8, 1), pltpu.roll(x, 8, 1))
        return x * cos + partner * sin

    krr = rope(krn)
    nq = _seg_rms(p_ref[0, :, 512:1024], mn_ref[...], gnq_ref[...])
    nk = _seg_rms(p_ref[0, :, 1024:1536], mn_ref[...], gnk_ref[...])
    for h in range(N_HEADS):
        sl = slice(h * LANES, (h + 1) * LANES)
        qm_ref[0, h] = rope(q[:, sl]).astype(BF16)
        km_ref[0, h] = (kk[:, sl] + krr).astype(BF16)
        vm_ref[0, h] = vv[:, sl].T.astype(BF16)
        qn_ref[0, h] = nq[:, sl].astype(BF16)
        kn_ref[0, h] = nk[:, sl].astype(BF16)
        vn_ref[0, h] = p_ref[0, :, 1536 + h * LANES:1536 + (h + 1) * LANES].astype(BF16)


def _prep(p, rope_c, rope_s, pw):
    b, nt, _ = p.shape
    full = lambda a: pl.BlockSpec(a.shape, lambda i, j: (0,) * a.ndim)
    consts = [pw['wuq'], pw['wk'], pw['wv'], pw['gcq'], pw['gckv'], pw['gkr'], pw['gq'], pw['gk'],
              pw['mq'], pw['mk'], pw['gnq'], pw['gnk'], pw['mn']]
    head_spec = pl.BlockSpec((1, N_HEADS, TM, LANES), lambda i, j: (i, 0, j, 0))
    head_shape = jax.ShapeDtypeStruct((b, N_HEADS, nt, LANES), BF16)
    head_t_spec = pl.BlockSpec((1, N_HEADS, LANES, TM), lambda i, j: (i, 0, 0, j))
    head_t_shape = jax.ShapeDtypeStruct((b, N_HEADS, LANES, nt), BF16)
    return pl.pallas_call(
        _prep_kernel,
        grid=(b, nt // TM),
        in_specs=[pl.BlockSpec((1, TM, 2048), lambda i, j: (i, j, 0)),
                  pl.BlockSpec((TM, LANES), lambda i, j: (j, 0)),
                  pl.BlockSpec((TM, LANES), lambda i, j: (j, 0))] + [full(a) for a in consts],
        out_specs=[head_spec, head_spec, head_t_spec, head_spec, head_spec, head_spec],
        out_shape=[head_shape, head_shape, head_t_shape, head_shape, head_shape, head_shape],
        compiler_params=_cparams(("arbitrary", "arbitrary")),
    )(p, rope_c, rope_s, *consts)


def _softmax_pv(s, v):
    m = jnp.max(s, axis=-1, keepdims=True)
    e = jnp.exp(s - m)
    l = jnp.sum(e, axis=-1, keepdims=True)
    return _dot(e.astype(BF16), v) / l


MLA_KEY_BLOCK = 512


def _mla_attend(q, k_ref, vt_ref, k0, k1):
    m = l = acc = None
    for s0 in range(k0, k1, MLA_KEY_BLOCK):
        n = min(MLA_KEY_BLOCK, k1 - s0)
        st = _dot_nt(k_ref[0, 0, s0:s0 + n, :], q) * MLA_SCALE
        bm = jnp.max(st, axis=0, keepdims=True)
        if m is None:
            m = bm
            e = jnp.exp(st - m)
            l = jnp.sum(e, axis=0, keepdims=True)
            acc = _dot(vt_ref[0, 0, :, s0:s0 + n], e.astype(BF16))
        else:
            m_new = jnp.maximum(m, bm)
            alpha = jnp.exp(m - m_new)
            e = jnp.exp(st - m_new)
            l = l * alpha + jnp.sum(e, axis=0, keepdims=True)
            acc = acc * alpha + _dot(vt_ref[0, 0, :, s0:s0 + n], e.astype(BF16))
            m = m_new
    return (acc / l).T


def _mla_kernel(q_ref, k_ref, vt_ref, o_ref, *, n_lat, n_ctx):
    j = pl.program_id(2)
    q = q_ref[0, 0]

    @pl.when(j < n_lat // TM)
    def _():
        o_ref[0] = _mla_attend(q, k_ref, vt_ref, 0, n_lat + n_ctx)

    @pl.when(j >= n_lat // TM)
    def _():
        o_ref[0] = _mla_attend(q, k_ref, vt_ref, n_lat, n_lat + n_ctx)


def _mla_attn(qm, km, vmt, n_lat, n_ctx):
    b, h, nt, _ = qm.shape
    kv_spec = pl.BlockSpec((1, 1, nt, LANES), lambda i, hh, j: (i, hh, 0, 0))
    vt_spec = pl.BlockSpec((1, 1, LANES, nt), lambda i, hh, j: (i, hh, 0, 0))
    return pl.pallas_call(
        functools.partial(_mla_kernel, n_lat=n_lat, n_ctx=n_ctx),
        grid=(b, h, nt // TM),
        in_specs=[pl.BlockSpec((1, 1, TM, LANES), lambda i, hh, j: (i, hh, j, 0)), kv_spec, vt_spec],
        out_specs=pl.BlockSpec((1, TM, LANES), lambda i, hh, j: (i, j, hh)),
        out_shape=jax.ShapeDtypeStruct((b, nt, PAD_W), F32),
        compiler_params=_cparams(("arbitrary", "arbitrary", "arbitrary")),
    )(qm, km, vmt)


def _na_kernel(q_ref, k_ref, v_ref, bias_ref, o_ref, *, n_lat, n_ctx):
    j = pl.program_id(1)
    rows = n_lat // GRID_W
    n_tiles = n_lat // TM
    win = NA_WIN_ROWS * GRID_W

    @pl.when(j < n_tiles)
    def _():
        start = jnp.clip(j * NA_TILE_ROWS - NA_KH // 2, 0, rows - NA_WIN_ROWS)
        case = jnp.where(j == 0, 0, jnp.where(j == n_tiles - 1, 2, 1))
        tok0 = pl.multiple_of(start * GRID_W, GRID_W)
        for h in range(N_HEADS):
            q = q_ref[0, h]
            s1 = _dot_nt(q, k_ref[0, h, pl.ds(tok0, win), :]) * NA_SCALE + bias_ref[case, h]
            s2 = _dot_nt(q, k_ref[0, h, pl.ds(n_lat, n_ctx), :]) * NA_SCALE
            m = jnp.maximum(jnp.max(s1, axis=-1, keepdims=True), jnp.max(s2, axis=-1, keepdims=True))
            e1 = jnp.exp(s1 - m)
            e2 = jnp.exp(s2 - m)
            l = jnp.sum(e1, axis=-1, keepdims=True) + jnp.sum(e2, axis=-1, keepdims=True)
            o = _dot(e1.astype(BF16), v_ref[0, h, pl.ds(tok0, win), :])
            o = o + _dot(e2.astype(BF16), v_ref[0, h, pl.ds(n_lat, n_ctx), :])
            o_ref[0, :, h * LANES:(h + 1) * LANES] = o / l

    @pl.when(j >= n_lat // TM)
    def _():
        for h in range(N_HEADS):
            s = _dot_nt(q_ref[0, h], k_ref[0, h, pl.ds(n_lat, n_ctx), :]) * NA_SCALE
            o_ref[0, :, h * LANES:(h + 1) * LANES] = _softmax_pv(s, v_ref[0, h, pl.ds(n_lat, n_ctx), :])


def _na_attn(qn, kn, vn, bias, n_lat, n_ctx):
    b, h, nt, _ = qn.shape
    kv_spec = pl.BlockSpec((1, h, nt, LANES), lambda i, j: (i, 0, 0, 0))
    return pl.pallas_call(
        functools.partial(_na_kernel, n_lat=n_lat, n_ctx=n_ctx),
        grid=(b, nt // TM),
        in_specs=[pl.BlockSpec((1, h, TM, LANES), lambda i, j: (i, 0, j, 0)), kv_spec, kv_spec,
                  pl.BlockSpec(bias.shape, lambda i, j: (0, 0, 0, 0))],
        out_specs=pl.BlockSpec((1, TM, PAD_W), lambda i, j: (i, j, 0)),
        out_shape=jax.ShapeDtypeStruct((b, nt, PAD_W), F32),
        compiler_params=_cparams(("arbitrary", "arbitrary")),
    )(qn, kn, vn, bias)


def _head_mask(h, shape):
    return (lax.broadcasted_iota(jnp.int32, shape, 1) // HEAD_DIM) == h


def _ret_state_step(s_ref, q, k, v, qw, kw, cd, bd):
    state = s_ref[...]
    o = _dot((q * qw).astype(BF16), state.astype(BF16))
    upd = _dot((k * kw).T.astype(BF16), v.astype(BF16))
    s_ref[...] = state * cd + upd * bd
    return o


def _ret_fwd_kernel(q_ref, k_ref, v_ref, dm_ref, qw_ref, kw_ref, cd_ref, bd_ref, o_ref, s_ref):
    @pl.when(pl.program_id(1) == 0)
    def _():
        s_ref[...] = jnp.zeros_like(s_ref)

    q = q_ref[0]
    k = k_ref[0]
    v = v_ref[0]
    o = _ret_state_step(s_ref, q, k, v, qw_ref[...], kw_ref[...], cd_ref[...], bd_ref[...])
    kb = k.astype(BF16)
    for h in range(N_HEADS):
        hm = _head_mask(h, q.shape)
        sc = _dot_nt(jnp.where(hm, q, 0.0).astype(BF16), kb) * dm_ref[h]
        o = o + _dot(sc.astype(BF16), jnp.where(hm, v, 0.0).astype(BF16))
    o_ref[0] = o


def _ret_bwd_kernel(q_ref, k_ref, v_ref, g_ref, op_ref, qw_ref, kw_ref, cd_ref, bd_ref, ms_ref, go_ref,
                    y_ref, s_ref):
    @pl.when(pl.program_id(1) == 0)
    def _():
        s_ref[...] = jnp.zeros_like(s_ref)

    o = op_ref[0] + _ret_state_step(s_ref, q_ref[0], k_ref[0], v_ref[0], qw_ref[...], kw_ref[...],
                                    cd_ref[...], bd_ref[...])
    y_ref[0] = _seg_rms(o, ms_ref[...], go_ref[...]) * _silu(g_ref[0])


def _scan_order(n_lat_t, n_ctx_t, reverse):
    if reverse:
        return lambda i: jnp.where(i < n_ctx_t, n_lat_t + n_ctx_t - 1 - i, n_lat_t + n_ctx_t - 1 - i)
    return lambda i: jnp.where(i < n_ctx_t, n_lat_t + i, i - n_ctx_t)


def _retention(p, rc, go, n_lat, n_ctx):
    b, nt, _ = p.shape
    c = RET_CHUNK
    n_lat_t, n_ctx_t = n_lat // c, n_ctx // c
    fwd = _scan_order(n_lat_t, n_ctx_t, False)
    bwd = _scan_order(n_lat_t, n_ctx_t, True)
    col = lambda order, cb: pl.BlockSpec((1, c, GROUP_W), lambda i, j: (i, order(j), cb))
    full = lambda a: pl.BlockSpec(a.shape, lambda i, j: (0,) * a.ndim)
    out_shape = jax.ShapeDtypeStruct((b, nt, GROUP_W), F32)
    scratch = [pltpu.VMEM((GROUP_W, GROUP_W), F32)]
    consts_f = [rc['dm'], rc['qw_f'], rc['kw_f'], rc['cd_f'], rc['bd']]
    o_part = pl.pallas_call(
        _ret_fwd_kernel,
        grid=(b, nt // c),
        in_specs=[col(fwd, COL_RET_Q), col(fwd, COL_RET_K), col(fwd, COL_RET_V)] + [full(a) for a in consts_f],
        out_specs=pl.BlockSpec((1, c, GROUP_W), lambda i, j: (i, fwd(j), 0)),
        out_shape=out_shape,
        scratch_shapes=scratch,
        compiler_params=_cparams(("arbitrary", "arbitrary")),
    )(p, p, p, *consts_f)
    consts_b = [rc['qw_b'], rc['kw_b'], rc['cd_b'], rc['bd'], rc['ms'], go]
    return pl.pallas_call(
        _ret_bwd_kernel,
        grid=(b, nt // c),
        in_specs=[col(bwd, COL_RET_Q), col(bwd, COL_RET_K), col(bwd, COL_RET_V), col(bwd, COL_RET_G),
                  pl.BlockSpec((1, c, GROUP_W), lambda i, j: (i, bwd(j), 0))] + [full(a) for a in consts_b],
        out_specs=pl.BlockSpec((1, c, GROUP_W), lambda i, j: (i, bwd(j), 0)),
        out_shape=out_shape,
        scratch_shapes=scratch,
        compiler_params=_cparams(("arbitrary", "arbitrary")),
    )(p, p, p, p, o_part, *consts_b)


def _hg_direction(q_ref, f_ref, v_ref, lb_ref, ain_ref, aex_ref, bseg_ref, bd_ref, st_ref, *, reverse):
    n_chunks = TM // HG_CHUNK
    qh = _silu(q_ref[0])
    lb = lb_ref[...]
    f = jnp.maximum(lb + (1.0 - lb) * _sigmoid(f_ref[0]), F_FLOOR)
    lf = jnp.log(f)
    k = 1.0 - f
    v = v_ref[0]

    a_in = _split_dot_r(ain_ref[...], lf, 3)
    a_ex = _split_dot_r(aex_ref[...], lf, 3)
    qp = (qh * jnp.exp(a_in)).astype(BF16)
    kdec = k * jnp.exp(a_ex)
    lam_all = jnp.exp(a_in + a_ex)
    vt = v.T.astype(BF16)
    row_chunk = lax.broadcasted_iota(jnp.int32, (TM, 1), 0) // HG_CHUNK
    bd = bd_ref[...]
    state = st_ref[...]
    parts = [None] * n_chunks
    for c in (range(n_chunks - 1, -1, -1) if reverse else range(n_chunks)):
        r0 = c * HG_CHUNK
        parts[c] = _dot_nt(qp[r0:r0 + HG_CHUNK], state.astype(BF16))
        upd = _dot(vt, jnp.where(row_chunk == c, kdec, 0.0).astype(BF16))
        state = state * lam_all[r0:r0 + 1] + upd * bd
    st_ref[...] = state
    o = jnp.concatenate(parts, axis=0)

    pos = lax.broadcasted_iota(jnp.int32, (TM, 1), 0) % HG_CHUNK
    bseg = bseg_ref[...]
    g = jnp.zeros_like(lf)
    for dl in range(HG_CHUNK):
        sh = (TM - dl) % TM if reverse else dl
        if dl > 0:
            shl = (TM - (dl - 1)) % TM if reverse else dl - 1
            g = g + (pltpu.roll(lf, shl, 0) if shl else lf)
        ksh = pltpu.roll(k, sh, 0) if sh else k
        vsh = pltpu.roll(v, sh, 0) if sh else v
        valid = (pos <= HG_CHUNK - 1 - dl) if reverse else (pos >= dl)
        w = jnp.where(valid, qh * ksh * jnp.exp(g), 0.0)
        o = o + _dot(w.astype(BF16), bseg) * vsh
    return o


def _hg_fwd_kernel(q_ref, f_ref, v_ref, lb_ref, ain_ref, aex_ref, bseg_ref, bd_ref, o_ref, st_ref):
    @pl.when(pl.program_id(1) == 0)
    def _():
        st_ref[...] = jnp.zeros_like(st_ref)

    o_ref[0] = _hg_direction(q_ref, f_ref, v_ref, lb_ref, ain_ref, aex_ref, bseg_ref, bd_ref, st_ref,
                             reverse=False)


def _hg_bwd_kernel(q_ref, f_ref, v_ref, g_ref, op_ref, lb_ref, ain_ref, aex_ref, bseg_ref, bd_ref,
                   ms_ref, go_ref, y_ref, st_ref):
    @pl.when(pl.program_id(1) == 0)
    def _():
        st_ref[...] = jnp.zeros_like(st_ref)

    o = op_ref[0] + _hg_direction(q_ref, f_ref, v_ref, lb_ref, ain_ref, aex_ref, bseg_ref, bd_ref, st_ref,
                                  reverse=True)
    y_ref[0] = _seg_rms(o, ms_ref[...], go_ref[...]) * _silu(g_ref[0])


def _hgrn2(p, hc, lb, go, n_lat, n_ctx):
    b, nt, _ = p.shape
    n_lat_t, n_ctx_t = n_lat // TM, n_ctx // TM
    fwd = _scan_order(n_lat_t, n_ctx_t, False)
    bwd = _scan_order(n_lat_t, n_ctx_t, True)
    col = lambda order, cb: pl.BlockSpec((1, TM, GROUP_W), lambda i, j: (i, order(j), cb))
    full = lambda a: pl.BlockSpec(a.shape, lambda i, j: (0,) * a.ndim)
    out_shape = jax.ShapeDtypeStruct((b, nt, GROUP_W), F32)
    scratch = [pltpu.VMEM((GROUP_W, GROUP_W), F32)]
    consts_f = [lb, hc['lincl'], hc['uexcl'], hc['bseg'], hc['bd']]
    o_part = pl.pallas_call(
        _hg_fwd_kernel,
        grid=(b, nt // TM),
        in_specs=[col(fwd, COL_HG_Q), col(fwd, COL_HG_FF), col(fwd, COL_HG_I)] + [full(a) for a in consts_f],
        out_specs=pl.BlockSpec((1, TM, GROUP_W), lambda i, j: (i, fwd(j), 0)),
        out_shape=out_shape,
        scratch_shapes=scratch,
        compiler_params=_cparams(("arbitrary", "arbitrary")),
    )(p, p, p, *consts_f)
    consts_b = [lb, hc['uincl'], hc['lexcl'], hc['bseg'], hc['bd'], hc['ms'], go]
    return pl.pallas_call(
        _hg_bwd_kernel,
        grid=(b, nt // TM),
        in_specs=[col(bwd, COL_HG_Q), col(bwd, COL_HG_FB), col(bwd, COL_HG_I), col(bwd, COL_HG_G),
                  pl.BlockSpec((1, TM, GROUP_W), lambda i, j: (i, bwd(j), 0))] + [full(a) for a in consts_b],
        out_specs=pl.BlockSpec((1, TM, GROUP_W), lambda i, j: (i, bwd(j), 0)),
        out_shape=out_shape,
        scratch_shapes=scratch,
        compiler_params=_cparams(("arbitrary", "arbitrary")),
    )(p, p, p, p, o_part, *consts_b)


def _outproj_kernel(x_ref, ym_ref, yn_ref, yr_ref, yh_ref, mod_ref, wm_ref, wn_ref, wr_ref, wh_ref, o_ref, *, d):
    acc = _dot(ym_ref[0].astype(BF16), wm_ref[...])
    acc = acc + _dot(yn_ref[0].astype(BF16), wn_ref[...])
    acc = acc + _dot(yr_ref[0].astype(BF16), wr_ref[...])
    acc = acc + _dot(yh_ref[0].astype(BF16), wh_ref[...])
    o_ref[0] = x_ref[0] + mod_ref[0, :, 2 * d:3 * d] * acc


def _outproj(xx, y_mla, y_na, y_ret, y_hg, modsel, ow, n_lat_tiles):
    b, nt, d = xx.shape
    tile = lambda w: pl.BlockSpec((1, TM, w), lambda i, j: (i, j, 0))
    full = lambda a: pl.BlockSpec(a.shape, lambda i, j: (0, 0))
    ws = [ow['mla'], ow['na'], ow['ret'], ow['hg']]
    return pl.pallas_call(
        functools.partial(_outproj_kernel, d=d),
        grid=(b, nt // TM),
        in_specs=[tile(d), tile(PAD_W), tile(PAD_W), tile(GROUP_W), tile(GROUP_W),
                  _mod_spec(6 * d, n_lat_tiles)] + [full(a) for a in ws],
        out_specs=tile(d),
        out_shape=jax.ShapeDtypeStruct((b, nt, d), F32),
        compiler_params=_cparams(("arbitrary", "arbitrary")),
    )(xx, y_mla, y_na, y_ret, y_hg, modsel, *ws)


def _router_kernel(x_ref, mod_ref, whi_ref, wlo_ref, br_ref, ltri_ref, r_ref, cnt_ref, *, d):
    @pl.when((pl.program_id(0) == 0) & (pl.program_id(1) == 0))
    def _():
        cnt_ref[...] = jnp.zeros_like(cnt_ref)

    h = _modulate(x_ref[0], mod_ref[0, :, 3 * d:4 * d], mod_ref[0, :, 4 * d:5 * d])
    h_hi = h.astype(BF16)
    h_lo = (h - h_hi.astype(F32)).astype(BF16)
    lg = _dot(h_hi, whi_ref[...]) + _dot(h_lo, whi_ref[...]) + _dot(h_hi, wlo_ref[...]) + br_ref[...]

    lane = lax.broadcasted_iota(jnp.int32, lg.shape, 1).astype(F32)
    far = 1e9

    def first_argmax(vals, vmax):
        return jnp.min(jnp.where(vals == vmax, lane, far), axis=-1, keepdims=True)

    gl = jnp.where(lane < MOE_GROUPS, lg, NEG_BIG)
    gmax = jnp.max(gl, axis=-1, keepdims=True)
    pg_top = 1.0 / jnp.sum(jnp.exp(gl - gmax), axis=-1, keepdims=True)
    lo = MOE_GROUPS + MOE_PER_GROUP * first_argmax(gl, gmax)
    fl = jnp.where((lane >= lo) & (lane < lo + MOE_PER_GROUP), lg, NEG_BIG)
    fmax = jnp.max(fl, axis=-1, keepdims=True)
    fsum = jnp.sum(jnp.exp(fl - fmax), axis=-1, keepdims=True)
    i1 = first_argmax(fl, fmax)
    fl2 = jnp.where(lane == i1, NEG_BIG, fl)
    f2max = jnp.max(fl2, axis=-1, keepdims=True)
    i2 = first_argmax(fl2, f2max)
    p1 = 1.0 / fsum
    p2 = jnp.exp(f2max - fmax) / fsum
    g1 = pg_top * p1 / (p1 + p2)
    g2 = pg_top * p2 / (p1 + p2)
    e1 = i1 - MOE_GROUPS
    e2 = i2 - MOE_GROUPS

    onehot = jnp.where(lane == e1, 1.0, 0.0) + jnp.where(lane == e2, 1.0, 0.0)
    before = cnt_ref[...] + _dot(ltri_ref[...], onehot.astype(BF16))
    r1 = jnp.sum(jnp.where(lane == e1, before, 0.0), axis=-1, keepdims=True)
    r2 = jnp.sum(jnp.where(lane == e2, before, 0.0), axis=-1, keepdims=True)
    cnt_ref[...] += jnp.sum(onehot, axis=0, keepdims=True)

    out = jnp.zeros_like(lg)
    for col, val in enumerate((e1, e2, g1, g2, r1, r2)):
        out = jnp.where(lane == col, val, out)
    r_ref[...] = out


ROUTE_E, ROUTE_G, ROUTE_R = 0, 2, 4


def _router(xx, modsel, rw, n_lat_tiles):
    b, nt, d = xx.shape
    tiles = nt // TM
    full = lambda a: pl.BlockSpec(a.shape, lambda i, j: (0, 0))
    ltri = jnp.asarray(np.tril(np.ones((TM, TM), np.float32), -1), BF16)
    ws = [rw['hi'], rw['lo'], rw['b'], ltri]
    return pl.pallas_call(
        functools.partial(_router_kernel, d=d),
        grid=(b, tiles),
        in_specs=[pl.BlockSpec((1, TM, d), lambda i, j: (i, j, 0)), _mod_spec(6 * d, n_lat_tiles)]
                 + [full(a) for a in ws],
        out_specs=[pl.BlockSpec((TM, LANES), lambda i, j: (i * tiles + j, 0)),
                   pl.BlockSpec((1, LANES), lambda i, j: (0, 0))],
        out_shape=[jax.ShapeDtypeStruct((b * nt, LANES), F32), jax.ShapeDtypeStruct((1, LANES), F32)],
        compiler_params=_cparams(("arbitrary", "arbitrary")),
    )(xx, modsel, *ws)


def _moe_plan(route, counts_f, n_tok):
    counts = counts_f[0, :MOE_EXPERTS].astype(jnp.int32)
    padded = (counts + MOE_BLOCK - 1) // MOE_BLOCK * MOE_BLOCK
    pad_end = jnp.cumsum(padded)
    pad_start = pad_end - padded
    n_blocks = -(-(n_tok * MOE_TOPK) // MOE_BLOCK) + MOE_EXPERTS
    blk0 = jnp.arange(n_blocks, dtype=jnp.int32) * MOE_BLOCK
    block_expert = jnp.minimum(jnp.sum((pad_end[None, :] <= blk0[:, None]).astype(jnp.int32), axis=1),
                               MOE_EXPERTS - 1)
    used = (pad_end[-1] // MOE_BLOCK).reshape(1)
    expert = route[:, ROUTE_E:ROUTE_E + MOE_TOPK].astype(jnp.int32)
    rank = route[:, ROUTE_R:ROUTE_R + MOE_TOPK].astype(jnp.int32)
    start_of = jnp.sum(jnp.where(expert[..., None] == jnp.arange(MOE_EXPERTS, dtype=jnp.int32), pad_start, 0), axis=-1)
    return start_of + rank, block_expert, used, n_blocks


def _idx_blocks(dest, k, n_tiles):
    return dest[:, k].reshape(n_tiles, 1, TM)


def _dispatch_kernel(d0_ref, d1_ref, x_ref, mod_ref, rows_in, rows_out, h_ref, sem, *, d):
    del rows_in
    h_ref[...] = _modulate(x_ref[0], mod_ref[0, :, 3 * d:4 * d], mod_ref[0, :, 4 * d:5 * d])

    def issue(r, carry):
        src = h_ref.at[pl.ds(r, 1)]
        pltpu.make_async_copy(src, rows_out.at[pl.ds(d0_ref[0, 0, r], 1)], sem).start()
        pltpu.make_async_copy(src, rows_out.at[pl.ds(d1_ref[0, 0, r], 1)], sem).start()
        return carry

    lax.fori_loop(0, TM, issue, 0, unroll=DMA_ISSUE_UNROLL)
    for _ in range(MOE_TOPK):
        pltpu.make_async_copy(h_ref, rows_out.at[pl.ds(0, TM)], sem).wait()


def _dispatch(xx, modsel, dest, n_rows, n_lat_tiles):
    b, nt, d = xx.shape
    tiles = nt // TM
    idx_spec = pl.BlockSpec((1, 1, TM), lambda i, j: (i * tiles + j, 0, 0), memory_space=pltpu.SMEM)
    return pl.pallas_call(
        functools.partial(_dispatch_kernel, d=d),
        grid=(b, tiles),
        in_specs=[idx_spec, idx_spec, pl.BlockSpec((1, TM, d), lambda i, j: (i, j, 0)),
                  _mod_spec(6 * d, n_lat_tiles), pl.BlockSpec(memory_space=pl.ANY)],
        out_specs=pl.BlockSpec(memory_space=pl.ANY),
        out_shape=jax.ShapeDtypeStruct((n_rows, d), F32),
        scratch_shapes=[pltpu.VMEM((TM, d), F32), pltpu.SemaphoreType.DMA],
        input_output_aliases={4: 0},
        compiler_params=_cparams(("arbitrary", "arbitrary")),
    )(_idx_blocks(dest, 0, b * tiles), _idx_blocks(dest, 1, b * tiles), xx, modsel, jnp.zeros((n_rows, d), F32))


def _ffn_kernel(be_ref, used_ref, x_ref, w1_ref, w3_ref, w2_ref, y_ref, w1b_ref, w3b_ref, w2b_ref):
    i = pl.program_id(0)

    @pl.when((i == 0) | (be_ref[i] != be_ref[jnp.maximum(i - 1, 0)]))
    def _():
        w1b_ref[...] = w1_ref[0, 0].astype(BF16)
        w3b_ref[...] = w3_ref[0, 0].astype(BF16)
        w2b_ref[...] = w2_ref[0, 0].astype(BF16)

    @pl.when(i < used_ref[0])
    def _():
        x = x_ref[...].astype(BF16)
        mid = _silu(_dot(x, w1b_ref[...])) * _dot(x, w3b_ref[...])
        y_ref[...] = _dot(mid.astype(BF16), w2b_ref[...])

    @pl.when(i >= used_ref[0])
    def _():
        y_ref[...] = jnp.zeros_like(y_ref)


def _moe_ffn(x_rows, block_expert, used, n_blocks, layer, w1, w3, w2):
    d = x_rows.shape[1]
    ff = w1.shape[3]
    grid_spec = pltpu.PrefetchScalarGridSpec(
        num_scalar_prefetch=2,
        grid=(n_blocks,),
        in_specs=[pl.BlockSpec((MOE_BLOCK, d), lambda i, be, nu: (i, 0)),
                  pl.BlockSpec((1, 1, d, ff), lambda i, be, nu: (layer, be[i], 0, 0)),
                  pl.BlockSpec((1, 1, d, ff), lambda i, be, nu: (layer, be[i], 0, 0)),
                  pl.BlockSpec((1, 1, ff, d), lambda i, be, nu: (layer, be[i], 0, 0))],
        out_specs=pl.BlockSpec((MOE_BLOCK, d), lambda i, be, nu: (i, 0)),
        scratch_shapes=[pltpu.VMEM((d, ff), BF16), pltpu.VMEM((d, ff), BF16), pltpu.VMEM((ff, d), BF16)],
    )
    return pl.pallas_call(
        _ffn_kernel,
        grid_spec=grid_spec,
        out_shape=jax.ShapeDtypeStruct(x_rows.shape, F32),
        compiler_params=_cparams(("arbitrary",)),
    )(block_expert, used, x_rows, w1, w3, w2)


def _row_gather(src_hbm, idx_ref, dst_ref, sem, n):
    def issue(r, carry):
        pltpu.make_async_copy(src_hbm.at[pl.ds(idx_ref[0, 0, r], 1)], dst_ref.at[pl.ds(r, 1)], sem).start()
        return carry

    lax.fori_loop(0, n, issue, 0, unroll=DMA_ISSUE_UNROLL)


def _row_gather_wait(src_hbm, dst_ref, sem, n):
    pltpu.make_async_copy(src_hbm.at[pl.ds(0, n)], dst_ref, sem).wait()


def _combine_kernel(d0_ref, d1_ref, x_ref, mod_ref, r_ref, y_hbm, o_ref, y0_ref, y1_ref, sem0, sem1, *, d):
    _row_gather(y_hbm, d0_ref, y0_ref, sem0, TM)
    _row_gather(y_hbm, d1_ref, y1_ref, sem1, TM)
    route = r_ref[...]
    lane = lax.broadcasted_iota(jnp.int32, route.shape, 1)
    g0 = jnp.sum(jnp.where(lane == ROUTE_G, route, 0.0), axis=-1, keepdims=True)
    g1 = jnp.sum(jnp.where(lane == ROUTE_G + 1, route, 0.0), axis=-1, keepdims=True)
    _row_gather_wait(y_hbm, y0_ref, sem0, TM)
    _row_gather_wait(y_hbm, y1_ref, sem1, TM)
    o_ref[0] = x_ref[0] + mod_ref[0, :, 5 * d:6 * d] * (y0_ref[...] * g0 + y1_ref[...] * g1)


def _combine(xx, modsel, route, y_rows, dest, n_lat_tiles, out_tiles):
    b, nt, d = xx.shape
    tiles = nt // TM
    idx_spec = pl.BlockSpec((1, 1, TM), lambda i, j: (i * tiles + j, 0, 0), memory_space=pltpu.SMEM)
    return pl.pallas_call(
        functools.partial(_combine_kernel, d=d),
        grid=(b, out_tiles),
        in_specs=[idx_spec, idx_spec, pl.BlockSpec((1, TM, d), lambda i, j: (i, j, 0)),
                  _mod_spec(6 * d, n_lat_tiles),
                  pl.BlockSpec((TM, LANES), lambda i, j: (i * tiles + j, 0)),
                  pl.BlockSpec(memory_space=pl.ANY)],
        out_specs=pl.BlockSpec((1, TM, d), lambda i, j: (i, j, 0)),
        out_shape=jax.ShapeDtypeStruct((b, out_tiles * TM, d), F32),
        scratch_shapes=[pltpu.VMEM((TM, d), F32), pltpu.VMEM((TM, d), F32),
                        pltpu.SemaphoreType.DMA, pltpu.SemaphoreType.DMA],
        compiler_params=_cparams(("arbitrary", "arbitrary")),
    )(_idx_blocks(dest, 0, b * tiles), _idx_blocks(dest, 1, b * tiles), xx, modsel, route, y_rows)


def _pad_heads_cols(w):
    lead = w.shape[:-1]
    w = w.reshape(*lead, N_HEADS, HEAD_DIM)
    w = jnp.concatenate([w, jnp.zeros_like(w)], axis=-1)
    return w.reshape(*lead, PAD_W)


def _pad_heads_rows(w):
    return _pad_heads_cols(w.T).T


def _seg_mean_matrix(width, segments):
    m = np.zeros((width, width), np.float32)
    for g in range(width // LANES):
        for start, length in segments:
            a = g * LANES + start
            m[a:a + length, a:a + length] = 1.0 / length
    return jnp.asarray(m, BF16)


def _rope_tables(n_lat, n_ctx):
    pos = jnp.arange(n_lat)
    rows = (pos // GRID_W).astype(F32)
    cols = (pos % GRID_W).astype(F32)
    per_axis = MLA_ROPE // 2
    inv_freq = ROPE_THETA ** (-jnp.arange(0, per_axis, 2, dtype=F32) / per_axis)
    ang = jnp.concatenate([rows[:, None] * inv_freq, cols[:, None] * inv_freq], axis=-1)
    i = np.arange(MLA_ROPE)
    src = (i // 16) * 8 + (i % 8)
    sign = np.where((i % 16) < 8, -1.0, 1.0).astype(np.float32)
    cos = jnp.ones((n_lat, LANES), F32).at[:, HEAD_DIM:HEAD_DIM + MLA_ROPE].set(jnp.cos(ang)[:, src])
    sin = jnp.zeros((n_lat, LANES), F32).at[:, HEAD_DIM:HEAD_DIM + MLA_ROPE].set(jnp.sin(ang)[:, src] * sign)
    cos = jnp.concatenate([cos, jnp.ones((n_ctx, LANES), F32)], axis=0)
    sin = jnp.concatenate([sin, jnp.zeros((n_ctx, LANES), F32)], axis=0)
    return cos, sin


def _na_bias_table(rpb):
    w = np.arange(GRID_W)
    col_start = np.clip(w - NA_KW // 2, 0, GRID_W - NA_KW)
    valid = (w[None, :] >= col_start[:, None]) & (w[None, :] < col_start[:, None] + NA_KW)
    dc = np.clip(w[None, :] - w[:, None], 1 - NA_KW, NA_KW - 1) + (NA_KW - 1)
    onehot = jnp.asarray(dc[None, :, :] == np.arange(2 * NA_KW - 1)[:, None, None], F32)
    t = jnp.einsum('hrd,dqk->hrqk', rpb.astype(F32), onehot, precision=lax.Precision.HIGHEST)
    t = jnp.where(jnp.asarray(valid)[None, None, :, :], t, NEG_BIG)
    masked = jnp.full((N_HEADS, GRID_W, GRID_W), NEG_BIG, F32)
    q_off = (0, NA_KH // 2, NA_KH)
    first = ([0] * NA_TILE_ROWS, list(range(NA_TILE_ROWS)), [NA_KH // 2] * NA_TILE_ROWS)
    cases = []
    for c in range(3):
        row_blocks = []
        for rr in range(NA_TILE_ROWS):
            blocks = []
            for jj in range(NA_WIN_ROWS):
                live = first[c][rr] <= jj < first[c][rr] + NA_KH
                dr = jj - (q_off[c] + rr) + (NA_KH - 1)
                blocks.append(t[:, dr] if live else masked)
            row_blocks.append(jnp.concatenate(blocks, axis=-1))
        cases.append(jnp.concatenate(row_blocks, axis=-2))
    return jnp.stack(cases, axis=0)


def _block_diag_mask(block):
    i = np.arange(GROUP_W) // block
    return (i[:, None] == i[None, :]).astype(np.float32)


def _retention_consts():
    c = RET_CHUNK
    j = np.arange(2 * N_HEADS, dtype=np.float64)
    lg = np.log1p(-np.exp2(-5.0 - j))
    lg_f, lg_b = lg[0::2], lg[1::2]
    pos = np.arange(c, dtype=np.float64)
    diff = pos[:, None] - pos[None, :]
    k_scale = HEAD_DIM ** -0.5
    dm = np.zeros((N_HEADS, c, c))
    for h in range(N_HEADS):
        dm[h] = (np.where(diff >= 0, np.exp(np.maximum(diff, 0.0) * lg_f[h]), 0.0)
                 + np.where(diff <= 0, np.exp(np.maximum(-diff, 0.0) * lg_b[h]), 0.0)) * k_scale
    lanes = lambda per_head: np.repeat(per_head, HEAD_DIM, axis=-1)
    out = {
        'dm': dm,
        'qw_f': lanes(np.exp((pos + 1)[:, None] * lg_f[None, :])),
        'kw_f': lanes(np.exp((c - 1 - pos)[:, None] * lg_f[None, :])) * k_scale,
        'cd_f': lanes(np.exp(c * lg_f)[None, :]),
        'qw_b': lanes(np.exp((c - pos)[:, None] * lg_b[None, :])),
        'kw_b': lanes(np.exp(pos[:, None] * lg_b[None, :])) * k_scale,
        'cd_b': lanes(np.exp(c * lg_b)[None, :]),
        'bd': _block_diag_mask(HEAD_DIM),
    }
    out = {k: jnp.asarray(v, F32) for k, v in out.items()}
    out['ms'] = jnp.asarray(_block_diag_mask(HEAD_DIM) / HEAD_DIM, BF16)
    return out


def _hgrn_consts():
    t = np.arange(TM)
    same = (t[:, None] // HG_CHUNK) == (t[None, :] // HG_CHUNK)
    lincl = same & (t[None, :] <= t[:, None])
    lexcl = same & (t[None, :] < t[:, None])
    return {
        'lincl': jnp.asarray(lincl, BF16), 'lexcl': jnp.asarray(lexcl, BF16),
        'uincl': jnp.asarray(lincl.T, BF16), 'uexcl': jnp.asarray(lexcl.T, BF16),
        'bseg': jnp.asarray(_block_diag_mask(HEAD_DIM), BF16),
        'bd': jnp.asarray(_block_diag_mask(HEAD_DIM), F32),
        'ms': jnp.asarray(_block_diag_mask(HEAD_DIM) / HEAD_DIM, BF16),
    }


def _layer_weights(l, w_in, w_out, mla_g_cq, mla_g_ckv, mla_w_uq, mla_w_ukv, mla_g_qn, mla_g_qr, mla_g_kn,
                   mla_g_kr, na_g_q, na_g_k, moe_w_rg, moe_b_rg, moe_w_re, moe_b_re):
    d = w_in.shape[1]
    w = w_in[l]
    z = lambda n: jnp.zeros((d, n), F32)
    o = 0
    cq, o = w[:, o:o + MLA_Q_LORA], o + MLA_Q_LORA
    ckv, o = w[:, o:o + MLA_KV_LORA], o + MLA_KV_LORA
    kr, o = w[:, o:o + MLA_ROPE], o + MLA_ROPE
    naq, o = w[:, o:o + GROUP_W], o + GROUP_W
    nak, o = w[:, o:o + GROUP_W], o + GROUP_W
    nav, o = w[:, o:o + GROUP_W], o + GROUP_W
    rest = w[:, o:]
    w_in_p = jnp.concatenate([cq, z(GROUP_W - MLA_Q_LORA), ckv, z(HEAD_DIM), kr, z(LANES - HEAD_DIM - MLA_ROPE),
                              _pad_heads_cols(naq), _pad_heads_cols(nak), _pad_heads_cols(nav), rest],
                             axis=1).astype(BF16)

    qk_dim = HEAD_DIM + MLA_ROPE
    wuq = mla_w_uq[l].reshape(MLA_Q_LORA, N_HEADS, qk_dim)
    wuq = jnp.concatenate([wuq, jnp.zeros((MLA_Q_LORA, N_HEADS, LANES - qk_dim), F32)], axis=-1)
    wuq = jnp.concatenate([wuq.reshape(MLA_Q_LORA, PAD_W), jnp.zeros((GROUP_W - MLA_Q_LORA, PAD_W), F32)], axis=0)
    wukv = mla_w_ukv[l].reshape(MLA_KV_LORA, N_HEADS, 2 * HEAD_DIM)
    pad64 = jnp.zeros((MLA_KV_LORA, N_HEADS, HEAD_DIM), F32)
    wk = jnp.concatenate([wukv[:, :, :HEAD_DIM], pad64], axis=-1).reshape(MLA_KV_LORA, PAD_W)
    wv = jnp.concatenate([wukv[:, :, HEAD_DIM:], pad64], axis=-1).reshape(MLA_KV_LORA, PAD_W)

    def per_head(parts):
        row = jnp.concatenate(parts + [jnp.zeros((LANES - sum(p.shape[0] for p in parts),), F32)])
        return jnp.tile(row, N_HEADS)[None, :]

    prep = {
        'wuq': wuq.astype(BF16), 'wk': wk.astype(BF16), 'wv': wv.astype(BF16),
        'gcq': jnp.concatenate([mla_g_cq[l], jnp.zeros((GROUP_W - MLA_Q_LORA,), F32)])[None, :],
        'gckv': mla_g_ckv[l][None, :],
        'gkr': jnp.concatenate([jnp.zeros((HEAD_DIM,), F32), mla_g_kr[l],
                                jnp.zeros((LANES - HEAD_DIM - MLA_ROPE,), F32)])[None, :],
        'gq': per_head([mla_g_qn[l], mla_g_qr[l]]),
        'gk': per_head([mla_g_kn[l]]),
        'mq': _seg_mean_matrix(PAD_W, [(0, HEAD_DIM), (HEAD_DIM, MLA_ROPE)]),
        'mk': _seg_mean_matrix(PAD_W, [(0, HEAD_DIM)]),
        'gnq': per_head([na_g_q[l]]),
        'gnk': per_head([na_g_k[l]]),
        'mn': _seg_mean_matrix(PAD_W, [(0, HEAD_DIM)]),
    }
    wo = w_out[l]
    ow = {
        'mla': _pad_heads_rows(wo[0:GROUP_W]).astype(BF16),
        'na': _pad_heads_rows(wo[GROUP_W:2 * GROUP_W]).astype(BF16),
        'ret': wo[2 * GROUP_W:3 * GROUP_W].astype(BF16),
        'hg': wo[3 * GROUP_W:4 * GROUP_W].astype(BF16),
    }
    n_r = MOE_GROUPS + MOE_EXPERTS
    wr = jnp.concatenate([moe_w_rg[l], moe_w_re[l], jnp.zeros((d, LANES - n_r), F32)], axis=1)
    wr_hi = wr.astype(BF16)
    rw = {
        'hi': wr_hi, 'lo': (wr - wr_hi.astype(F32)).astype(BF16),
        'b': jnp.concatenate([moe_b_rg[l], moe_b_re[l], jnp.zeros((LANES - n_r,), F32)])[None, :],
    }
    return w_in_p, prep, ow, rw


def _layer(xx, modsel, lw, rope_c, rope_s, na_bias, rc, hc, hg_lb_l, ret_go, hg_go, layer, w1, w3, w2,
           n_lat, n_ctx, last):
    w_in_p, prep_w, ow, rw = lw
    b, nt, d = xx.shape
    n_lat_tiles = n_lat // TM
    p = _inproj(xx, modsel, w_in_p, n_lat_tiles)
    qm, km, vm, qn, kn, vn = _prep(p, rope_c, rope_s, prep_w)
    y_mla = _mla_attn(qm, km, vm, n_lat, n_ctx)
    y_na = _na_attn(qn, kn, vn, na_bias, n_lat, n_ctx)
    y_ret = _retention(p, rc, ret_go, n_lat, n_ctx)
    y_hg = _hgrn2(p, hc, hg_lb_l, hg_go, n_lat, n_ctx)
    xx = _outproj(xx, y_mla, y_na, y_ret, y_hg, modsel, ow, n_lat_tiles)
    route, counts = _router(xx, modsel, rw, n_lat_tiles)
    dest, block_expert, used, n_blocks = _moe_plan(route, counts, b * nt)
    x_rows = _dispatch(xx, modsel, dest, n_blocks * MOE_BLOCK, n_lat_tiles)
    y_rows = _moe_ffn(x_rows, block_expert, used, n_blocks, layer, w1, w3, w2)
    return _combine(xx, modsel, route, y_rows, dest, n_lat_tiles, n_lat_tiles if last else nt // TM)


def kernel(x, c, ctx, c_ctx, w_ada, b_ada, w_in, w_out, mla_g_cq, mla_g_ckv, mla_w_uq, mla_w_ukv, mla_g_qn, mla_g_qr, mla_g_kn, mla_g_kr, na_g_q, na_g_k, na_rpb, ret_g_out, hg_lb_raw, hg_g_out, moe_w_rg, moe_b_rg, moe_w_re, moe_b_re, moe_w1, moe_w3, moe_w2):
    b, n_lat, d = x.shape
    n_ctx = ctx.shape[1]
    depth = w_in.shape[0]
    assert n_lat % TM == 0 and n_ctx % TM == 0 and TM % GRID_W == 0
    assert n_lat // TM >= 3 and n_lat // GRID_W >= NA_WIN_ROWS
    assert w_in.shape[2] == MLA_Q_LORA + MLA_KV_LORA + MLA_ROPE + 12 * GROUP_W

    cc = jnp.concatenate([c, c_ctx[None, :], jnp.zeros((16 - b - 1, d), F32)], axis=0)
    mods = _ada_all(cc, w_ada, b_ada)
    rope_c, rope_s = _rope_tables(n_lat, n_ctx)
    rc = _retention_consts()
    hc = _hgrn_consts()
    lb_w = jax.nn.softmax(hg_lb_raw.astype(F32), axis=0)
    hg_lb = jnp.cumsum(lb_w, axis=0) - lb_w[0:1]

    xx = jnp.concatenate([x, ctx], axis=1)
    tile_go = lambda g: jnp.tile(g, N_HEADS)[None, :]
    for l in range(depth):
        modsel = jnp.stack([mods[l, :b], jnp.broadcast_to(mods[l, b], (b, 6 * d))], axis=1).reshape(2 * b, 1, 6 * d)
        lw = _layer_weights(l, w_in, w_out, mla_g_cq, mla_g_ckv, mla_w_uq, mla_w_ukv, mla_g_qn, mla_g_qr,
                            mla_g_kn, mla_g_kr, na_g_q, na_g_k, moe_w_rg, moe_b_rg, moe_w_re, moe_b_re)
        xx = _layer(xx, modsel, lw, rope_c, rope_s, _na_bias_table(na_rpb[l]), rc, hc, hg_lb[l][None, :],
                    tile_go(ret_g_out[l]), tile_go(hg_g_out[l]),
                    l, moe_w1, moe_w3, moe_w2,
                    n_lat, n_ctx, l == depth - 1)
    return xx
```

```python
import functools

import numpy as np
import jax
import jax.numpy as jnp
from jax import lax
from jax.experimental import pallas as pl
from jax.experimental.pallas import tpu as pltpu

F32 = jnp.float32
BF16 = jnp.bfloat16

EPS = 1e-6
ROPE_THETA = 10000.0
NEG_BIG = -1e30
F_FLOOR = 1e-20
GRID_W = 64
N_HEADS = 4
HEAD_DIM = 64
LANES = 128
GROUP_W = N_HEADS * HEAD_DIM
PAD_W = N_HEADS * LANES
MLA_Q_LORA = 192
MLA_KV_LORA = 128
MLA_ROPE = 32
MLA_SCALE = (HEAD_DIM + MLA_ROPE) ** -0.5
LOG2E = 1.4426950408889634
NA_KH = 8
NA_KW = 16
NA_SCALE = HEAD_DIM ** -0.5
RET_CHUNK = 128
HG_CHUNK = 16
MOE_GROUPS = 4
MOE_PER_GROUP = 8
MOE_EXPERTS = MOE_GROUPS * MOE_PER_GROUP
MOE_TOPK = 2
MOE_BLOCK = 256
TM = 256
NA_TILE_ROWS = TM // GRID_W
NA_WIN_ROWS = NA_TILE_ROWS + NA_KH
VMEM_LIMIT = 56 * 1024 * 1024
DMA_ISSUE_UNROLL = 8

COL_NA_Q, COL_NA_K, COL_NA_V = 2, 4, 6
COL_RET_Q, COL_RET_K, COL_RET_V, COL_RET_G = 8, 9, 10, 11
COL_HG_Q, COL_HG_FF, COL_HG_FB, COL_HG_I, COL_HG_G = 12, 13, 14, 15, 16
P_COLS = 17 * GROUP_W


def _cparams(sem):
    return pltpu.CompilerParams(dimension_semantics=sem, vmem_limit_bytes=VMEM_LIMIT)


def _sigmoid(x):
    return 1.0 / (1.0 + jnp.exp(-x))


def _silu(x):
    return x * _sigmoid(x)


def _dot(a, b):
    return jnp.dot(a, b, preferred_element_type=F32)


def _dot_nt(a, b):
    return lax.dot_general(a, b, (((1,), (1,)), ((), ())), preferred_element_type=F32)


def _split_dot_l(x, m, n):
    acc = None
    rem = x
    for i in range(n):
        piece = rem.astype(BF16)
        d = _dot(piece, m)
        acc = d if acc is None else acc + d
        if i + 1 < n:
            rem = rem - piece.astype(F32)
    return acc


def _split_dot_r(m, x, n):
    acc = None
    rem = x
    for i in range(n):
        piece = rem.astype(BF16)
        d = _dot(m, piece)
        acc = d if acc is None else acc + d
        if i + 1 < n:
            rem = rem - piece.astype(F32)
    return acc


def _seg_rms(x, m, gain):
    return x * lax.rsqrt(_split_dot_l(x * x, m, 2) + EPS) * gain


def _ada_kernel(c_ref, w_ref, b_ref, o_ref):
    s = _silu(c_ref[...])
    o_ref[0] = jnp.dot(s, w_ref[0], preferred_element_type=F32,
                       precision=lax.Precision.HIGHEST) + b_ref[0]


def _ada_all(cc, w_ada, b_ada):
    n_layers, d, d6 = w_ada.shape
    bn = 512
    rows = cc.shape[0]
    return pl.pallas_call(
        _ada_kernel,
        grid=(n_layers, d6 // bn),
        in_specs=[pl.BlockSpec((rows, d), lambda l, j: (0, 0)),
                  pl.BlockSpec((1, d, bn), lambda l, j: (l, 0, j)),
                  pl.BlockSpec((1, 1, bn), lambda l, j: (l, 0, j))],
        out_specs=pl.BlockSpec((1, rows, bn), lambda l, j: (l, 0, j)),
        out_shape=jax.ShapeDtypeStruct((n_layers, rows, d6), F32),
        compiler_params=_cparams(("arbitrary", "arbitrary")),
    )(cc, w_ada, b_ada.reshape(n_layers, 1, d6))


def _mod_spec(d6, n_lat_tiles):
    return pl.BlockSpec((1, 1, d6), lambda b, j: (2 * b + (j >= n_lat_tiles).astype(jnp.int32), 0, 0))


def _modulate(x, shift, scale):
    xn = x * lax.rsqrt(jnp.mean(x * x, axis=-1, keepdims=True) + EPS)
    return xn * (1.0 + scale) + shift


def _inproj_kernel(x_ref, mod_ref, w_ref, o_ref, *, d):
    xm = _modulate(x_ref[0], mod_ref[0, :, 0:d], mod_ref[0, :, d:2 * d])
    o_ref[0] = _dot(xm.astype(BF16), w_ref[...])


def _inproj(xx, modsel, w_in_p, n_lat_tiles):
    b, nt, d = xx.shape
    pc = w_in_p.shape[1]
    return pl.pallas_call(
        functools.partial(_inproj_kernel, d=d),
        grid=(b, nt // TM),
        in_specs=[pl.BlockSpec((1, TM, d), lambda i, j: (i, j, 0)),
                  _mod_spec(6 * d, n_lat_tiles),
                  pl.BlockSpec((d, pc), lambda i, j: (0, 0))],
        out_specs=pl.BlockSpec((1, TM, pc), lambda i, j: (i, j, 0)),
        out_shape=jax.ShapeDtypeStruct((b, nt, pc), F32),
        compiler_params=_cparams(("arbitrary", "arbitrary")),
    )(xx, modsel, w_in_p)


def _prep_kernel(p_ref, c_ref, s_ref, wuq_ref, wk_ref, wv_ref, gcq_ref, gckv_ref, gkr_ref,
                 gq_ref, gk_ref, mq_ref, mk_ref, gnq_ref, gnk_ref, mn_ref,
                 qm_ref, km_ref, vm_ref, qn_ref, kn_ref, vn_ref):
    cq = p_ref[0, :, 0:256]
    ckv = p_ref[0, :, 256:384]
    kr = p_ref[0, :, 384:512]
    cqn = cq * lax.rsqrt(jnp.sum(cq * cq, axis=-1, keepdims=True) * (1.0 / MLA_Q_LORA) + EPS) * gcq_ref[...]
    ckvn = (ckv * lax.rsqrt(jnp.mean(ckv * ckv, axis=-1, keepdims=True) + EPS) * gckv_ref[...]).astype(BF16)
    krn = kr * lax.rsqrt(jnp.sum(kr * kr, axis=-1, keepdims=True) * (1.0 / MLA_ROPE) + EPS) * gkr_ref[...]
    q = _seg_rms(_dot(cqn.astype(BF16), wuq_ref[...]), mq_ref[...], gq_ref[...])
    kk = _seg_rms(_dot(ckvn, wk_ref[...]), mk_ref[...], gk_ref[...])
    vv = _dot(ckvn, wv_ref[...])

    cos = c_ref[...]
    sin = s_ref[...]
    lane = lax.broadcasted_iota(jnp.int32, (TM, LANES), 1)
    first = (lane % 16) < 8

    def rope(x):
        partner = jnp.where(first, pltpu.roll(x, LANES - 8, 1), pltpu.roll(x, 8, 1))
        return x * cos + partner * sin

    krr = rope(krn)
    nq = _seg_rms(p_ref[0, :, 512:1024], mn_ref[...], gnq_ref[...])
    nk = _seg_rms(p_ref[0, :, 1024:1536], mn_ref[...], gnk_ref[...])
    for h in range(N_HEADS):
        sl = slice(h * LANES, (h + 1) * LANES)
        qm_ref[0, h] = (rope(q[:, sl]) * (MLA_SCALE * LOG2E)).astype(BF16)
        km_ref[0, h] = (kk[:, sl] + krr).astype(BF16)
        vm_ref[0, h] = vv[:, sl].T.astype(BF16)
        qn_ref[0, h] = nq[:, sl].astype(BF16)
        kn_ref[0, h] = nk[:, sl].astype(BF16)
        vn_ref[0, h] = p_ref[0, :, 1536 + h * LANES:1536 + (h + 1) * LANES].astype(BF16)


def _prep(p, rope_c, rope_s, pw):
    b, nt, _ = p.shape
    full = lambda a: pl.BlockSpec(a.shape, lambda i, j: (0,) * a.ndim)
    consts = [pw['wuq'], pw['wk'], pw['wv'], pw['gcq'], pw['gckv'], pw['gkr'], pw['gq'], pw['gk'],
              pw['mq'], pw['mk'], pw['gnq'], pw['gnk'], pw['mn']]
    head_spec = pl.BlockSpec((1, N_HEADS, TM, LANES), lambda i, j: (i, 0, j, 0))
    head_shape = jax.ShapeDtypeStruct((b, N_HEADS, nt, LANES), BF16)
    head_t_spec = pl.BlockSpec((1, N_HEADS, LANES, TM), lambda i, j: (i, 0, 0, j))
    head_t_shape = jax.ShapeDtypeStruct((b, N_HEADS, LANES, nt), BF16)
    return pl.pallas_call(
        _prep_kernel,
        grid=(b, nt // TM),
        in_specs=[pl.BlockSpec((1, TM, 2048), lambda i, j: (i, j, 0)),
                  pl.BlockSpec((TM, LANES), lambda i, j: (j, 0)),
                  pl.BlockSpec((TM, LANES), lambda i, j: (j, 0))] + [full(a) for a in consts],
        out_specs=[head_spec, head_spec, head_t_spec, head_spec, head_spec, head_spec],
        out_shape=[head_shape, head_shape, head_t_shape, head_shape, head_shape, head_shape],
        compiler_params=_cparams(("arbitrary", "arbitrary")),
    )(p, rope_c, rope_s, *consts)


def _softmax_pv(s, v):
    m = jnp.max(s, axis=-1, keepdims=True)
    e = jnp.exp(s - m)
    l = jnp.sum(e, axis=-1, keepdims=True)
    return _dot(e.astype(BF16), v) / l


MLA_KEY_BLOCK = 2176


MLA_HEADS_PER_STEP = 4


def _mla_attend(q_ref, k_ref, vt_ref, o_ref, k0, k1):
    items = [(hh, s0) for hh in range(MLA_HEADS_PER_STEP) for s0 in range(k0, k1, MLA_KEY_BLOCK)]
    score = lambda hh, s0: _dot_nt(k_ref[0, hh, s0:min(s0 + MLA_KEY_BLOCK, k1), :], q_ref[0, hh])
    st = score(*items[0])
    m = l = acc = None
    for i, (hh, s0) in enumerate(items):
        st_next = score(*items[i + 1]) if i + 1 < len(items) else None
        vt_blk = vt_ref[0, hh, :, s0:min(s0 + MLA_KEY_BLOCK, k1)]
        bm = jnp.max(st, axis=0, keepdims=True)
        if s0 == k0:
            m = bm
            e = jnp.exp2(st - m)
            l = jnp.sum(e, axis=0, keepdims=True)
            acc = _dot(vt_blk, e.astype(BF16))
        else:
            m_new = jnp.maximum(m, bm)
            alpha = jnp.exp2(m - m_new)
            e = jnp.exp2(st - m_new)
            l = l * alpha + jnp.sum(e, axis=0, keepdims=True)
            acc = acc * alpha + _dot(vt_blk, e.astype(BF16))
            m = m_new
        if s0 + MLA_KEY_BLOCK >= k1:
            o_ref[0, :, hh * LANES:(hh + 1) * LANES] = (acc / l).T
        st = st_next


def _mla_kernel(q_ref, k_ref, vt_ref, o_ref, *, n_lat, n_ctx):
    j = pl.program_id(2)

    @pl.when(j < n_lat // TM)
    def _():
        _mla_attend(q_ref, k_ref, vt_ref, o_ref, 0, n_lat + n_ctx)

    @pl.when(j >= n_lat // TM)
    def _():
        _mla_attend(q_ref, k_ref, vt_ref, o_ref, n_lat, n_lat + n_ctx)


def _mla_attn(qm, km, vmt, n_lat, n_ctx):
    b, h, nt, _ = qm.shape
    hp = MLA_HEADS_PER_STEP
    kv_spec = pl.BlockSpec((1, hp, nt, LANES), lambda i, hh, j: (i, hh, 0, 0))
    vt_spec = pl.BlockSpec((1, hp, LANES, nt), lambda i, hh, j: (i, hh, 0, 0))
    return pl.pallas_call(
        functools.partial(_mla_kernel, n_lat=n_lat, n_ctx=n_ctx),
        grid=(b, h // hp, nt // TM),
        in_specs=[pl.BlockSpec((1, hp, TM, LANES), lambda i, hh, j: (i, hh, j, 0)), kv_spec, vt_spec],
        out_specs=pl.BlockSpec((1, TM, hp * LANES), lambda i, hh, j: (i, j, hh)),
        out_shape=jax.ShapeDtypeStruct((b, nt, PAD_W), F32),
        compiler_params=_cparams(("arbitrary", "arbitrary", "arbitrary")),
    )(qm, km, vmt)


def _na_kernel(q_ref, k_ref, v_ref, bias_ref, o_ref, *, n_lat, n_ctx):
    j = pl.program_id(1)
    rows = n_lat // GRID_W
    n_tiles = n_lat // TM
    win = NA_WIN_ROWS * GRID_W

    @pl.when(j < n_tiles)
    def _():
        start = jnp.clip(j * NA_TILE_ROWS - NA_KH // 2, 0, rows - NA_WIN_ROWS)
        case = jnp.where(j == 0, 0, jnp.where(j == n_tiles - 1, 2, 1))
        tok0 = pl.multiple_of(start * GRID_W, GRID_W)
        for h in range(N_HEADS):
            q = q_ref[0, h]
            s1 = _dot_nt(q, k_ref[0, h, pl.ds(tok0, win), :]) * NA_SCALE + bias_ref[case, h]
            s2 = _dot_nt(q, k_ref[0, h, pl.ds(n_lat, n_ctx), :]) * NA_SCALE
            m = jnp.maximum(jnp.max(s1, axis=-1, keepdims=True), jnp.max(s2, axis=-1, keepdims=True))
            e1 = jnp.exp(s1 - m)
            e2 = jnp.exp(s2 - m)
            l = jnp.sum(e1, axis=-1, keepdims=True) + jnp.sum(e2, axis=-1, keepdims=True)
            o = _dot(e1.astype(BF16), v_ref[0, h, pl.ds(tok0, win), :])
            o = o + _dot(e2.astype(BF16), v_ref[0, h, pl.ds(n_lat, n_ctx), :])
            o_ref[0, :, h * LANES:(h + 1) * LANES] = o / l

    @pl.when(j >= n_lat // TM)
    def _():
        for h in range(N_HEADS):
            s = _dot_nt(q_ref[0, h], k_ref[0, h, pl.ds(n_lat, n_ctx), :]) * NA_SCALE
            o_ref[0, :, h * LANES:(h + 1) * LANES] = _softmax_pv(s, v_ref[0, h, pl.ds(n_lat, n_ctx), :])


def _na_attn(qn, kn, vn, bias, n_lat, n_ctx):
    b, h, nt, _ = qn.shape
    kv_spec = pl.BlockSpec((1, h, nt, LANES), lambda i, j: (i, 0, 0, 0))
    return pl.pallas_call(
        functools.partial(_na_kernel, n_lat=n_lat, n_ctx=n_ctx),
        grid=(b, nt // TM),
        in_specs=[pl.BlockSpec((1, h, TM, LANES), lambda i, j: (i, 0, j, 0)), kv_spec, kv_spec,
                  pl.BlockSpec(bias.shape, lambda i, j: (0, 0, 0, 0))],
        out_specs=pl.BlockSpec((1, TM, PAD_W), lambda i, j: (i, j, 0)),
        out_shape=jax.ShapeDtypeStruct((b, nt, PAD_W), F32),
        compiler_params=_cparams(("arbitrary", "arbitrary")),
    )(qn, kn, vn, bias)


def _head_mask(h, shape):
    return (lax.broadcasted_iota(jnp.int32, shape, 1) // HEAD_DIM) == h


def _ret_state_step(s_ref, q, k, v, qw, kw, cd, bd):
    state = s_ref[...]
    o = _dot((q * qw).astype(BF16), state.astype(BF16))
    upd = _dot((k * kw).T.astype(BF16), v.astype(BF16))
    s_ref[...] = state * cd + upd * bd
    return o


def _ret_fwd_kernel(q_ref, k_ref, v_ref, dm_ref, qw_ref, kw_ref, cd_ref, bd_ref, o_ref, s_ref):
    @pl.when(pl.program_id(1) == 0)
    def _():
        s_ref[...] = jnp.zeros_like(s_ref)

    q = q_ref[0]
    k = k_ref[0]
    v = v_ref[0]
    o = _ret_state_step(s_ref, q, k, v, qw_ref[...], kw_ref[...], cd_ref[...], bd_ref[...])
    kb = k.astype(BF16)
    for h in range(N_HEADS):
        hm = _head_mask(h, q.shape)
        sc = _dot_nt(jnp.where(hm, q, 0.0).astype(BF16), kb) * dm_ref[h]
        o = o + _dot(sc.astype(BF16), jnp.where(hm, v, 0.0).astype(BF16))
    o_ref[0] = o


def _ret_bwd_kernel(q_ref, k_ref, v_ref, g_ref, op_ref, qw_ref, kw_ref, cd_ref, bd_ref, ms_ref, go_ref,
                    y_ref, s_ref):
    @pl.when(pl.program_id(1) == 0)
    def _():
        s_ref[...] = jnp.zeros_like(s_ref)

    o = op_ref[0] + _ret_state_step(s_ref, q_ref[0], k_ref[0], v_ref[0], qw_ref[...], kw_ref[...],
                                    cd_ref[...], bd_ref[...])
    y_ref[0] = _seg_rms(o, ms_ref[...], go_ref[...]) * _silu(g_ref[0])


def _scan_order(n_lat_t, n_ctx_t, reverse):
    if reverse:
        return lambda i: jnp.where(i < n_ctx_t, n_lat_t + n_ctx_t - 1 - i, n_lat_t + n_ctx_t - 1 - i)
    return lambda i: jnp.where(i < n_ctx_t, n_lat_t + i, i - n_ctx_t)


def _retention(p, rc, go, n_lat, n_ctx):
    b, nt, _ = p.shape
    c = RET_CHUNK
    n_lat_t, n_ctx_t = n_lat // c, n_ctx // c
    fwd = _scan_order(n_lat_t, n_ctx_t, False)
    bwd = _scan_order(n_lat_t, n_ctx_t, True)
    col = lambda order, cb: pl.BlockSpec((1, c, GROUP_W), lambda i, j: (i, order(j), cb))
    full = lambda a: pl.BlockSpec(a.shape, lambda i, j: (0,) * a.ndim)
    out_shape = jax.ShapeDtypeStruct((b, nt, GROUP_W), F32)
    scratch = [pltpu.VMEM((GROUP_W, GROUP_W), F32)]
    consts_f = [rc['dm'], rc['qw_f'], rc['kw_f'], rc['cd_f'], rc['bd']]
    o_part = pl.pallas_call(
        _ret_fwd_kernel,
        grid=(b, nt // c),
        in_specs=[col(fwd, COL_RET_Q), col(fwd, COL_RET_K), col(fwd, COL_RET_V)] + [full(a) for a in consts_f],
        out_specs=pl.BlockSpec((1, c, GROUP_W), lambda i, j: (i, fwd(j), 0)),
        out_shape=out_shape,
        scratch_shapes=scratch,
        compiler_params=_cparams(("arbitrary", "arbitrary")),
    )(p, p, p, *consts_f)
    consts_b = [rc['qw_b'], rc['kw_b'], rc['cd_b'], rc['bd'], rc['ms'], go]
    return pl.pallas_call(
        _ret_bwd_kernel,
        grid=(b, nt // c),
        in_specs=[col(bwd, COL_RET_Q), col(bwd, COL_RET_K), col(bwd, COL_RET_V), col(bwd, COL_RET_G),
                  pl.BlockSpec((1, c, GROUP_W), lambda i, j: (i, bwd(j), 0))] + [full(a) for a in consts_b],
        out_specs=pl.BlockSpec((1, c, GROUP_W), lambda i, j: (i, bwd(j), 0)),
        out_shape=out_shape,
        scratch_shapes=scratch,
        compiler_params=_cparams(("arbitrary", "arbitrary")),
    )(p, p, p, p, o_part, *consts_b)


def _hg_direction(q_ref, f_ref, v_ref, lb_ref, ain_ref, aex_ref, bseg_ref, bd_ref, st_ref, sh_ref, *, reverse):
    n_chunks = TM // HG_CHUNK
    assert n_chunks == HG_CHUNK
    qh = _silu(q_ref[0])
    lb = lb_ref[...]
    f = jnp.maximum(lb + (1.0 - lb) * _sigmoid(f_ref[0]), F_FLOOR)
    lf = jnp.log(f) * LOG2E
    k = 1.0 - f
    v = v_ref[0]
    row = lax.broadcasted_iota(jnp.int32, (TM, 1), 0)
    pos = row % HG_CHUNK
    row_chunk = row // HG_CHUNK

    margin = jnp.zeros((HG_CHUNK, v.shape[1]), F32)
    for slot, val in enumerate((k, lf, v)):
        sh_ref[slot, 0:HG_CHUNK] = margin
        sh_ref[slot, HG_CHUNK:HG_CHUNK + TM] = val
        sh_ref[slot, HG_CHUNK + TM:] = margin

    def from_row(slot, dist):
        start = HG_CHUNK + dist if reverse else HG_CHUNK - dist
        return sh_ref[slot, start:start + TM]

    a_in = _split_dot_r(ain_ref[...], lf, 3)
    a_ex = _split_dot_r(aex_ref[...], lf, 3)
    qp = (qh * jnp.exp2(a_in)).astype(BF16)
    kdec = k * jnp.exp2(a_ex)
    lam_all = jnp.exp2(a_in + a_ex)
    vt = v.T.astype(BF16)
    bd = bd_ref[...]
    bseg = bseg_ref[...]
    state = st_ref[...]
    parts = [None] * n_chunks
    o_band = jnp.zeros_like(v)
    g = jnp.zeros_like(lf)
    for step in range(n_chunks):
        c = n_chunks - 1 - step if reverse else step
        r0 = c * HG_CHUNK
        parts[c] = _dot_nt(qp[r0:r0 + HG_CHUNK], state.astype(BF16))
        upd = _dot(vt, jnp.where(row_chunk == c, kdec, 0.0).astype(BF16))
        state = state * lam_all[r0:r0 + 1] + upd * bd
        dl = step
        if dl > 0:
            g = g + from_row(1, dl - 1)
        valid = (pos <= HG_CHUNK - 1 - dl) if reverse else (pos >= dl)
        w = jnp.where(valid, qh * from_row(0, dl) * jnp.exp2(g), 0.0)
        o_band = o_band + _dot(w.astype(BF16), bseg) * from_row(2, dl)
    st_ref[...] = state
    return jnp.concatenate(parts, axis=0) + o_band


def _hg_fwd_kernel(q_ref, f_ref, v_ref, lb_ref, ain_ref, aex_ref, bseg_ref, bd_ref, o_ref, st_ref, sh_ref):
    @pl.when(pl.program_id(1) == 0)
    def _():
        st_ref[...] = jnp.zeros_like(st_ref)

    o_ref[0] = _hg_direction(q_ref, f_ref, v_ref, lb_ref, ain_ref, aex_ref, bseg_ref, bd_ref, st_ref, sh_ref,
                             reverse=False)


def _hg_bwd_kernel(q_ref, f_ref, v_ref, g_ref, op_ref, lb_ref, ain_ref, aex_ref, bseg_ref, bd_ref,
                   ms_ref, go_ref, y_ref, st_ref, sh_ref):
    @pl.when(pl.program_id(1) == 0)
    def _():
        st_ref[...] = jnp.zeros_like(st_ref)

    o = op_ref[0] + _hg_direction(q_ref, f_ref, v_ref, lb_ref, ain_ref, aex_ref, bseg_ref, bd_ref, st_ref, sh_ref,
                                  reverse=True)
    y_ref[0] = _seg_rms(o, ms_ref[...], go_ref[...]) * _silu(g_ref[0])


def _hgrn2(p, hc, lb, go, n_lat, n_ctx):
    b, nt, _ = p.shape
    n_lat_t, n_ctx_t = n_lat // TM, n_ctx // TM
    fwd = _scan_order(n_lat_t, n_ctx_t, False)
    bwd = _scan_order(n_lat_t, n_ctx_t, True)
    col = lambda order, cb: pl.BlockSpec((1, TM, GROUP_W), lambda i, j: (i, order(j), cb))
    full = lambda a: pl.BlockSpec(a.shape, lambda i, j: (0,) * a.ndim)
    out_shape = jax.ShapeDtypeStruct((b, nt, GROUP_W), F32)
    scratch = [pltpu.VMEM((GROUP_W, GROUP_W), F32), pltpu.VMEM((3, TM + 2 * HG_CHUNK, GROUP_W), F32)]
    consts_f = [lb, hc['lincl'], hc['uexcl'], hc['bseg'], hc['bd']]
    o_part = pl.pallas_call(
        _hg_fwd_kernel,
        grid=(b, nt // TM),
        in_specs=[col(fwd, COL_HG_Q), col(fwd, COL_HG_FF), col(fwd, COL_HG_I)] + [full(a) for a in consts_f],
        out_specs=pl.BlockSpec((1, TM, GROUP_W), lambda i, j: (i, fwd(j), 0)),
        out_shape=out_shape,
        scratch_shapes=scratch,
        compiler_params=_cparams(("arbitrary", "arbitrary")),
    )(p, p, p, *consts_f)
    consts_b = [lb, hc['uincl'], hc['lexcl'], hc['bseg'], hc['bd'], hc['ms'], go]
    return pl.pallas_call(
        _hg_bwd_kernel,
        grid=(b, nt // TM),
        in_specs=[col(bwd, COL_HG_Q), col(bwd, COL_HG_FB), col(bwd, COL_HG_I), col(bwd, COL_HG_G),
                  pl.BlockSpec((1, TM, GROUP_W), lambda i, j: (i, bwd(j), 0))] + [full(a) for a in consts_b],
        out_specs=pl.BlockSpec((1, TM, GROUP_W), lambda i, j: (i, bwd(j), 0)),
        out_shape=out_shape,
        scratch_shapes=scratch,
        compiler_params=_cparams(("arbitrary", "arbitrary")),
    )(p, p, p, p, o_part, *consts_b)


def _outproj_kernel(x_ref, ym_ref, yn_ref, yr_ref, yh_ref, mod_ref, wm_ref, wn_ref, wr_ref, wh_ref, o_ref, *, d):
    acc = _dot(ym_ref[0].astype(BF16), wm_ref[...])
    acc = acc + _dot(yn_ref[0].astype(BF16), wn_ref[...])
    acc = acc + _dot(yr_ref[0].astype(BF16), wr_ref[...])
    acc = acc + _dot(yh_ref[0].astype(BF16), wh_ref[...])
    o_ref[0] = x_ref[0] + mod_ref[0, :, 2 * d:3 * d] * acc


def _outproj(xx, y_mla, y_na, y_ret, y_hg, modsel, ow, n_lat_tiles):
    b, nt, d = xx.shape
    tile = lambda w: pl.BlockSpec((1, TM, w), lambda i, j: (i, j, 0))
    full = lambda a: pl.BlockSpec(a.shape, lambda i, j: (0, 0))
    ws = [ow['mla'], ow['na'], ow['ret'], ow['hg']]
    return pl.pallas_call(
        functools.partial(_outproj_kernel, d=d),
        grid=(b, nt // TM),
        in_specs=[tile(d), tile(PAD_W), tile(PAD_W), tile(GROUP_W), tile(GROUP_W),
                  _mod_spec(6 * d, n_lat_tiles)] + [full(a) for a in ws],
        out_specs=tile(d),
        out_shape=jax.ShapeDtypeStruct((b, nt, d), F32),
        compiler_params=_cparams(("arbitrary", "arbitrary")),
    )(xx, y_mla, y_na, y_ret, y_hg, modsel, *ws)


def _router_kernel(x_ref, mod_ref, whi_ref, wlo_ref, br_ref, ltri_ref, r_ref, cnt_ref, *, d):
    @pl.when((pl.program_id(0) == 0) & (pl.program_id(1) == 0))
    def _():
        cnt_ref[...] = jnp.zeros_like(cnt_ref)

    h = _modulate(x_ref[0], mod_ref[0, :, 3 * d:4 * d], mod_ref[0, :, 4 * d:5 * d])
    h_hi = h.astype(BF16)
    h_lo = (h - h_hi.astype(F32)).astype(BF16)
    lg = _dot(h_hi, whi_ref[...]) + _dot(h_lo, whi_ref[...]) + _dot(h_hi, wlo_ref[...]) + br_ref[...]

    lane = lax.broadcasted_iota(jnp.int32, lg.shape, 1).astype(F32)
    far = 1e9

    def first_argmax(vals, vmax):
        return jnp.min(jnp.where(vals == vmax, lane, far), axis=-1, keepdims=True)

    gl = jnp.where(lane < MOE_GROUPS, lg, NEG_BIG)
    gmax = jnp.max(gl, axis=-1, keepdims=True)
    pg_top = 1.0 / jnp.sum(jnp.exp(gl - gmax), axis=-1, keepdims=True)
    lo = MOE_GROUPS + MOE_PER_GROUP * first_argmax(gl, gmax)
    fl = jnp.where((lane >= lo) & (lane < lo + MOE_PER_GROUP), lg, NEG_BIG)
    fmax = jnp.max(fl, axis=-1, keepdims=True)
    fsum = jnp.sum(jnp.exp(fl - fmax), axis=-1, keepdims=True)
    i1 = first_argmax(fl, fmax)
    fl2 = jnp.where(lane == i1, NEG_BIG, fl)
    f2max = jnp.max(fl2, axis=-1, keepdims=True)
    i2 = first_argmax(fl2, f2max)
    p1 = 1.0 / fsum
    p2 = jnp.exp(f2max - fmax) / fsum
    g1 = pg_top * p1 / (p1 + p2)
    g2 = pg_top * p2 / (p1 + p2)
    e1 = i1 - MOE_GROUPS
    e2 = i2 - MOE_GROUPS

    onehot = jnp.where(lane == e1, 1.0, 0.0) + jnp.where(lane == e2, 1.0, 0.0)
    before = cnt_ref[...] + _dot(ltri_ref[...], onehot.astype(BF16))
    r1 = jnp.sum(jnp.where(lane == e1, before, 0.0), axis=-1, keepdims=True)
    r2 = jnp.sum(jnp.where(lane == e2, before, 0.0), axis=-1, keepdims=True)
    cnt_ref[...] += jnp.sum(onehot, axis=0, keepdims=True)

    out = jnp.zeros_like(lg)
    for col, val in enumerate((e1, e2, g1, g2, r1, r2)):
        out = jnp.where(lane == col, val, out)
    r_ref[...] = out


ROUTE_E, ROUTE_G, ROUTE_R = 0, 2, 4


def _router(xx, modsel, rw, n_lat_tiles):
    b, nt, d = xx.shape
    tiles = nt // TM
    full = lambda a: pl.BlockSpec(a.shape, lambda i, j: (0, 0))
    ltri = jnp.asarray(np.tril(np.ones((TM, TM), np.float32), -1), BF16)
    ws = [rw['hi'], rw['lo'], rw['b'], ltri]
    return pl.pallas_call(
        functools.partial(_router_kernel, d=d),
        grid=(b, tiles),
        in_specs=[pl.BlockSpec((1, TM, d), lambda i, j: (i, j, 0)), _mod_spec(6 * d, n_lat_tiles)]
                 + [full(a) for a in ws],
        out_specs=[pl.BlockSpec((TM, LANES), lambda i, j: (i * tiles + j, 0)),
                   pl.BlockSpec((1, LANES), lambda i, j: (0, 0))],
        out_shape=[jax.ShapeDtypeStruct((b * nt, LANES), F32), jax.ShapeDtypeStruct((1, LANES), F32)],
        compiler_params=_cparams(("arbitrary", "arbitrary")),
    )(xx, modsel, *ws)


def _moe_plan(route, counts_f, n_tok):
    counts = counts_f[0, :MOE_EXPERTS].astype(jnp.int32)
    padded = (counts + MOE_BLOCK - 1) // MOE_BLOCK * MOE_BLOCK
    pad_end = jnp.cumsum(padded)
    pad_start = pad_end - padded
    n_blocks = -(-(n_tok * MOE_TOPK) // MOE_BLOCK) + MOE_EXPERTS
    blk0 = jnp.arange(n_blocks, dtype=jnp.int32) * MOE_BLOCK
    block_expert = jnp.minimum(jnp.sum((pad_end[None, :] <= blk0[:, None]).astype(jnp.int32), axis=1),
                               MOE_EXPERTS - 1)
    used = (pad_end[-1] // MOE_BLOCK).reshape(1)
    expert = route[:, ROUTE_E:ROUTE_E + MOE_TOPK].astype(jnp.int32)
    rank = route[:, ROUTE_R:ROUTE_R + MOE_TOPK].astype(jnp.int32)
    start_of = jnp.sum(jnp.where(expert[..., None] == jnp.arange(MOE_EXPERTS, dtype=jnp.int32), pad_start, 0), axis=-1)
    return start_of + rank, block_expert, used, n_blocks


def _idx_blocks(dest, k, n_tiles):
    return dest[:, k].reshape(n_tiles, 1, TM)


def _dispatch_kernel(d0_ref, d1_ref, x_ref, mod_ref, rows_in, rows_out, h_ref, sem, *, d):
    del rows_in
    step = pl.program_id(0) * pl.num_programs(1) + pl.program_id(1)
    n_steps = pl.num_programs(0) * pl.num_programs(1)
    slot = step % 2

    def wait_slot(s):
        for _ in range(MOE_TOPK):
            pltpu.make_async_copy(h_ref.at[s], rows_out.at[pl.ds(0, TM)], sem.at[s]).wait()

    h_ref[slot] = _modulate(x_ref[0], mod_ref[0, :, 3 * d:4 * d], mod_ref[0, :, 4 * d:5 * d])

    def issue(r, carry):
        src = h_ref.at[slot, pl.ds(r, 1)]
        pltpu.make_async_copy(src, rows_out.at[pl.ds(d0_ref[0, 0, r], 1)], sem.at[slot]).start()
        pltpu.make_async_copy(src, rows_out.at[pl.ds(d1_ref[0, 0, r], 1)], sem.at[slot]).start()
        return carry

    lax.fori_loop(0, TM, issue, 0, unroll=DMA_ISSUE_UNROLL)

    @pl.when(step > 0)
    def _():
        wait_slot(1 - slot)

    @pl.when(step == n_steps - 1)
    def _():
        wait_slot(slot)


def _dispatch(xx, modsel, dest, n_rows, n_lat_tiles):
    b, nt, d = xx.shape
    tiles = nt // TM
    idx_spec = pl.BlockSpec((1, 1, TM), lambda i, j: (i * tiles + j, 0, 0), memory_space=pltpu.SMEM)
    return pl.pallas_call(
        functools.partial(_dispatch_kernel, d=d),
        grid=(b, tiles),
        in_specs=[idx_spec, idx_spec, pl.BlockSpec((1, TM, d), lambda i, j: (i, j, 0)),
                  _mod_spec(6 * d, n_lat_tiles), pl.BlockSpec(memory_space=pl.ANY)],
        out_specs=pl.BlockSpec(memory_space=pl.ANY),
        out_shape=jax.ShapeDtypeStruct((n_rows, d), F32),
        scratch_shapes=[pltpu.VMEM((2, TM, d), F32), pltpu.SemaphoreType.DMA((2,))],
        input_output_aliases={4: 0},
        compiler_params=_cparams(("arbitrary", "arbitrary")),
    )(_idx_blocks(dest, 0, b * tiles), _idx_blocks(dest, 1, b * tiles), xx, modsel, jnp.zeros((n_rows, d), F32))


def _ffn_kernel(be_ref, used_ref, x_ref, w1_ref, w3_ref, w2_ref, y_ref, w1b_ref, w3b_ref, w2b_ref):
    i = pl.program_id(0)

    @pl.when((i == 0) | (be_ref[i] != be_ref[jnp.maximum(i - 1, 0)]))
    def _():
        w1b_ref[...] = w1_ref[0, 0].astype(BF16)
        w3b_ref[...] = w3_ref[0, 0].astype(BF16)
        w2b_ref[...] = w2_ref[0, 0].astype(BF16)

    @pl.when(i < used_ref[0])
    def _():
        x = x_ref[...].astype(BF16)
        mid = _silu(_dot(x, w1b_ref[...])) * _dot(x, w3b_ref[...])
        y_ref[...] = _dot(mid.astype(BF16), w2b_ref[...])

    @pl.when(i >= used_ref[0])
    def _():
        y_ref[...] = jnp.zeros_like(y_ref)


def _moe_ffn(x_rows, block_expert, used, n_blocks, layer, w1, w3, w2):
    d = x_rows.shape[1]
    ff = w1.shape[3]
    grid_spec = pltpu.PrefetchScalarGridSpec(
        num_scalar_prefetch=2,
        grid=(n_blocks,),
        in_specs=[pl.BlockSpec((MOE_BLOCK, d), lambda i, be, nu: (i, 0)),
                  pl.BlockSpec((1, 1, d, ff), lambda i, be, nu: (layer, be[i], 0, 0)),
                  pl.BlockSpec((1, 1, d, ff), lambda i, be, nu: (layer, be[i], 0, 0)),
                  pl.BlockSpec((1, 1, ff, d), lambda i, be, nu: (layer, be[i], 0, 0))],
        out_specs=pl.BlockSpec((MOE_BLOCK, d), lambda i, be, nu: (i, 0)),
        scratch_shapes=[pltpu.VMEM((d, ff), BF16), pltpu.VMEM((d, ff), BF16), pltpu.VMEM((ff, d), BF16)],
    )
    return pl.pallas_call(
        _ffn_kernel,
        grid_spec=grid_spec,
        out_shape=jax.ShapeDtypeStruct(x_rows.shape, F32),
        compiler_params=_cparams(("arbitrary",)),
    )(block_expert, used, x_rows, w1, w3, w2)


def _row_gather(src_hbm, idx_ref, dst_ref, sem, n):
    def issue(r, carry):
        pltpu.make_async_copy(src_hbm.at[pl.ds(idx_ref[0, 0, r], 1)], dst_ref.at[pl.ds(r, 1)], sem).start()
        return carry

    lax.fori_loop(0, n, issue, 0, unroll=DMA_ISSUE_UNROLL)


def _row_gather_wait(src_hbm, dst_ref, sem, n):
    pltpu.make_async_copy(src_hbm.at[pl.ds(0, n)], dst_ref, sem).wait()


def _combine_kernel(d0_ref, d1_ref, n0_ref, n1_ref, x_ref, mod_ref, r_ref, y_hbm, o_ref, y_ref, sem, *, d):
    step = pl.program_id(0) * pl.num_programs(1) + pl.program_id(1)
    n_steps = pl.num_programs(0) * pl.num_programs(1)
    slot = step % 2

    def gather(idx_refs, s):
        for k, idx_ref in enumerate(idx_refs):
            _row_gather(y_hbm, idx_ref, y_ref.at[s, k], sem.at[s, k], TM)

    @pl.when(step == 0)
    def _():
        gather((d0_ref, d1_ref), slot)

    @pl.when(step + 1 < n_steps)
    def _():
        gather((n0_ref, n1_ref), 1 - slot)

    route = r_ref[...]
    lane = lax.broadcasted_iota(jnp.int32, route.shape, 1)
    g0 = jnp.sum(jnp.where(lane == ROUTE_G, route, 0.0), axis=-1, keepdims=True)
    g1 = jnp.sum(jnp.where(lane == ROUTE_G + 1, route, 0.0), axis=-1, keepdims=True)
    for k in range(MOE_TOPK):
        _row_gather_wait(y_hbm, y_ref.at[slot, k], sem.at[slot, k], TM)
    o_ref[0] = x_ref[0] + mod_ref[0, :, 5 * d:6 * d] * (y_ref[slot, 0] * g0 + y_ref[slot, 1] * g1)


def _combine(xx, modsel, route, y_rows, dest, n_lat_tiles, out_tiles):
    b, nt, d = xx.shape
    tiles = nt // TM
    idx_spec = pl.BlockSpec((1, 1, TM), lambda i, j: (i * tiles + j, 0, 0), memory_space=pltpu.SMEM)

    def next_block(i, j):
        wrap = j + 1 >= out_tiles
        return (jnp.where(wrap, jnp.minimum(i + 1, b - 1) * tiles, i * tiles + j + 1), 0, 0)

    next_spec = pl.BlockSpec((1, 1, TM), next_block, memory_space=pltpu.SMEM)
    d0, d1 = _idx_blocks(dest, 0, b * tiles), _idx_blocks(dest, 1, b * tiles)
    return pl.pallas_call(
        functools.partial(_combine_kernel, d=d),
        grid=(b, out_tiles),
        in_specs=[idx_spec, idx_spec, next_spec, next_spec, pl.BlockSpec((1, TM, d), lambda i, j: (i, j, 0)),
                  _mod_spec(6 * d, n_lat_tiles),
                  pl.BlockSpec((TM, LANES), lambda i, j: (i * tiles + j, 0)),
                  pl.BlockSpec(memory_space=pl.ANY)],
        out_specs=pl.BlockSpec((1, TM, d), lambda i, j: (i, j, 0)),
        out_shape=jax.ShapeDtypeStruct((b, out_tiles * TM, d), F32),
        scratch_shapes=[pltpu.VMEM((2, MOE_TOPK, TM, d), F32), pltpu.SemaphoreType.DMA((2, MOE_TOPK))],
        compiler_params=_cparams(("arbitrary", "arbitrary")),
    )(d0, d1, d0, d1, xx, modsel, route, y_rows)


def _pad_heads_cols(w):
    lead = w.shape[:-1]
    w = w.reshape(*lead, N_HEADS, HEAD_DIM)
    w = jnp.concatenate([w, jnp.zeros_like(w)], axis=-1)
    return w.reshape(*lead, PAD_W)


def _pad_heads_rows(w):
    return _pad_heads_cols(w.T).T


def _seg_mean_matrix(width, segments):
    m = np.zeros((width, width), np.float32)
    for g in range(width // LANES):
        for start, length in segments:
            a = g * LANES + start
            m[a:a + length, a:a + length] = 1.0 / length
    return jnp.asarray(m, BF16)


def _rope_tables(n_lat, n_ctx):
    pos = jnp.arange(n_lat)
    rows = (pos // GRID_W).astype(F32)
    cols = (pos % GRID_W).astype(F32)
    per_axis = MLA_ROPE // 2
    inv_freq = ROPE_THETA ** (-jnp.arange(0, per_axis, 2, dtype=F32) / per_axis)
    ang = jnp.concatenate([rows[:, None] * inv_freq, cols[:, None] * inv_freq], axis=-1)
    i = np.arange(MLA_ROPE)
    src = (i // 16) * 8 + (i % 8)
    sign = np.where((i % 16) < 8, -1.0, 1.0).astype(np.float32)
    cos = jnp.ones((n_lat, LANES), F32).at[:, HEAD_DIM:HEAD_DIM + MLA_ROPE].set(jnp.cos(ang)[:, src])
    sin = jnp.zeros((n_lat, LANES), F32).at[:, HEAD_DIM:HEAD_DIM + MLA_ROPE].set(jnp.sin(ang)[:, src] * sign)
    cos = jnp.concatenate([cos, jnp.ones((n_ctx, LANES), F32)], axis=0)
    sin = jnp.concatenate([sin, jnp.zeros((n_ctx, LANES), F32)], axis=0)
    return cos, sin


def _na_bias_table(rpb):
    w = np.arange(GRID_W)
    col_start = np.clip(w - NA_KW // 2, 0, GRID_W - NA_KW)
    valid = (w[None, :] >= col_start[:, None]) & (w[None, :] < col_start[:, None] + NA_KW)
    dc = np.clip(w[None, :] - w[:, None], 1 - NA_KW, NA_KW - 1) + (NA_KW - 1)
    onehot = jnp.asarray(dc[None, :, :] == np.arange(2 * NA_KW - 1)[:, None, None], F32)
    t = jnp.einsum('hrd,dqk->hrqk', rpb.astype(F32), onehot, precision=lax.Precision.HIGHEST)
    t = jnp.where(jnp.asarray(valid)[None, None, :, :], t, NEG_BIG)
    masked = jnp.full((N_HEADS, GRID_W, GRID_W), NEG_BIG, F32)
    q_off = (0, NA_KH // 2, NA_KH)
    first = ([0] * NA_TILE_ROWS, list(range(NA_TILE_ROWS)), [NA_KH // 2] * NA_TILE_ROWS)
    cases = []
    for c in range(3):
        row_blocks = []
        for rr in range(NA_TILE_ROWS):
            blocks = []
            for jj in range(NA_WIN_ROWS):
                live = first[c][rr] <= jj < first[c][rr] + NA_KH
                dr = jj - (q_off[c] + rr) + (NA_KH - 1)
                blocks.append(t[:, dr] if live else masked)
            row_blocks.append(jnp.concatenate(blocks, axis=-1))
        cases.append(jnp.concatenate(row_blocks, axis=-2))
    return jnp.stack(cases, axis=0)


def _block_diag_mask(block):
    i = np.arange(GROUP_W) // block
    return (i[:, None] == i[None, :]).astype(np.float32)


def _retention_consts():
    c = RET_CHUNK
    j = np.arange(2 * N_HEADS, dtype=np.float64)
    lg = np.log1p(-np.exp2(-5.0 - j))
    lg_f, lg_b = lg[0::2], lg[1::2]
    pos = np.arange(c, dtype=np.float64)
    diff = pos[:, None] - pos[None, :]
    k_scale = HEAD_DIM ** -0.5
    dm = np.zeros((N_HEADS, c, c))
    for h in range(N_HEADS):
        dm[h] = (np.where(diff >= 0, np.exp(np.maximum(diff, 0.0) * lg_f[h]), 0.0)
                 + np.where(diff <= 0, np.exp(np.maximum(-diff, 0.0) * lg_b[h]), 0.0)) * k_scale
    lanes = lambda per_head: np.repeat(per_head, HEAD_DIM, axis=-1)
    out = {
        'dm': dm,
        'qw_f': lanes(np.exp((pos + 1)[:, None] * lg_f[None, :])),
        'kw_f': lanes(np.exp((c - 1 - pos)[:, None] * lg_f[None, :])) * k_scale,
        'cd_f': lanes(np.exp(c * lg_f)[None, :]),
        'qw_b': lanes(np.exp((c - pos)[:, None] * lg_b[None, :])),
        'kw_b': lanes(np.exp(pos[:, None] * lg_b[None, :])) * k_scale,
        'cd_b': lanes(np.exp(c * lg_b)[None, :]),
        'bd': _block_diag_mask(HEAD_DIM),
    }
    out = {k: jnp.asarray(v, F32) for k, v in out.items()}
    out['ms'] = jnp.asarray(_block_diag_mask(HEAD_DIM) / HEAD_DIM, BF16)
    return out


def _hgrn_consts():
    t = np.arange(TM)
    same = (t[:, None] // HG_CHUNK) == (t[None, :] // HG_CHUNK)
    lincl = same & (t[None, :] <= t[:, None])
    lexcl = same & (t[None, :] < t[:, None])
    return {
        'lincl': jnp.asarray(lincl, BF16), 'lexcl': jnp.asarray(lexcl, BF16),
        'uincl': jnp.asarray(lincl.T, BF16), 'uexcl': jnp.asarray(lexcl.T, BF16),
        'bseg': jnp.asarray(_block_diag_mask(HEAD_DIM), BF16),
        'bd': jnp.asarray(_block_diag_mask(HEAD_DIM), F32),
        'ms': jnp.asarray(_block_diag_mask(HEAD_DIM) / HEAD_DIM, BF16),
    }


def _layer_weights(l, w_in, w_out, mla_g_cq, mla_g_ckv, mla_w_uq, mla_w_ukv, mla_g_qn, mla_g_qr, mla_g_kn,
                   mla_g_kr, na_g_q, na_g_k, moe_w_rg, moe_b_rg, moe_w_re, moe_b_re):
    d = w_in.shape[1]
    w = w_in[l]
    z = lambda n: jnp.zeros((d, n), F32)
    o = 0
    cq, o = w[:, o:o + MLA_Q_LORA], o + MLA_Q_LORA
    ckv, o = w[:, o:o + MLA_KV_LORA], o + MLA_KV_LORA
    kr, o = w[:, o:o + MLA_ROPE], o + MLA_ROPE
    naq, o = w[:, o:o + GROUP_W], o + GROUP_W
    nak, o = w[:, o:o + GROUP_W], o + GROUP_W
    nav, o = w[:, o:o + GROUP_W], o + GROUP_W
    rest = w[:, o:]
    w_in_p = jnp.concatenate([cq, z(GROUP_W - MLA_Q_LORA), ckv, z(HEAD_DIM), kr, z(LANES - HEAD_DIM - MLA_ROPE),
                              _pad_heads_cols(naq), _pad_heads_cols(nak), _pad_heads_cols(nav), rest],
                             axis=1).astype(BF16)

    qk_dim = HEAD_DIM + MLA_ROPE
    wuq = mla_w_uq[l].reshape(MLA_Q_LORA, N_HEADS, qk_dim)
    wuq = jnp.concatenate([wuq, jnp.zeros((MLA_Q_LORA, N_HEADS, LANES - qk_dim), F32)], axis=-1)
    wuq = jnp.concatenate([wuq.reshape(MLA_Q_LORA, PAD_W), jnp.zeros((GROUP_W - MLA_Q_LORA, PAD_W), F32)], axis=0)
    wukv = mla_w_ukv[l].reshape(MLA_KV_LORA, N_HEADS, 2 * HEAD_DIM)
    pad64 = jnp.zeros((MLA_KV_LORA, N_HEADS, HEAD_DIM), F32)
    wk = jnp.concatenate([wukv[:, :, :HEAD_DIM], pad64], axis=-1).reshape(MLA_KV_LORA, PAD_W)
    wv = jnp.concatenate([wukv[:, :, HEAD_DIM:], pad64], axis=-1).reshape(MLA_KV_LORA, PAD_W)

    def per_head(parts):
        row = jnp.concatenate(parts + [jnp.zeros((LANES - sum(p.shape[0] for p in parts),), F32)])
        return jnp.tile(row, N_HEADS)[None, :]

    prep = {
        'wuq': wuq.astype(BF16), 'wk': wk.astype(BF16), 'wv': wv.astype(BF16),
        'gcq': jnp.concatenate([mla_g_cq[l], jnp.zeros((GROUP_W - MLA_Q_LORA,), F32)])[None, :],
        'gckv': mla_g_ckv[l][None, :],
        'gkr': jnp.concatenate([jnp.zeros((HEAD_DIM,), F32), mla_g_kr[l],
                                jnp.zeros((LANES - HEAD_DIM - MLA_ROPE,), F32)])[None, :],
        'gq': per_head([mla_g_qn[l], mla_g_qr[l]]),
        'gk': per_head([mla_g_kn[l]]),
        'mq': _seg_mean_matrix(PAD_W, [(0, HEAD_DIM), (HEAD_DIM, MLA_ROPE)]),
        'mk': _seg_mean_matrix(PAD_W, [(0, HEAD_DIM)]),
        'gnq': per_head([na_g_q[l]]),
        'gnk': per_head([na_g_k[l]]),
        'mn': _seg_mean_matrix(PAD_W, [(0, HEAD_DIM)]),
    }
    wo = w_out[l]
    ow = {
        'mla': _pad_heads_rows(wo[0:GROUP_W]).astype(BF16),
        'na': _pad_heads_rows(wo[GROUP_W:2 * GROUP_W]).astype(BF16),
        'ret': wo[2 * GROUP_W:3 * GROUP_W].astype(BF16),
        'hg': wo[3 * GROUP_W:4 * GROUP_W].astype(BF16),
    }
    n_r = MOE_GROUPS + MOE_EXPERTS
    wr = jnp.concatenate([moe_w_rg[l], moe_w_re[l], jnp.zeros((d, LANES - n_r), F32)], axis=1)
    wr_hi = wr.astype(BF16)
    rw = {
        'hi': wr_hi, 'lo': (wr - wr_hi.astype(F32)).astype(BF16),
        'b': jnp.concatenate([moe_b_rg[l], moe_b_re[l], jnp.zeros((LANES - n_r,), F32)])[None, :],
    }
    return w_in_p, prep, ow, rw


def _layer(xx, modsel, lw, rope_c, rope_s, na_bias, rc, hc, hg_lb_l, ret_go, hg_go, layer, w1, w3, w2,
           n_lat, n_ctx, last):
    w_in_p, prep_w, ow, rw = lw
    b, nt, d = xx.shape
    n_lat_tiles = n_lat // TM
    p = _inproj(xx, modsel, w_in_p, n_lat_tiles)
    qm, km, vm, qn, kn, vn = _prep(p, rope_c, rope_s, prep_w)
    y_mla = _mla_attn(qm, km, vm, n_lat, n_ctx)
    y_na = _na_attn(qn, kn, vn, na_bias, n_lat, n_ctx)
    y_ret = _retention(p, rc, ret_go, n_lat, n_ctx)
    y_hg = _hgrn2(p, hc, hg_lb_l, hg_go, n_lat, n_ctx)
    xx = _outproj(xx, y_mla, y_na, y_ret, y_hg, modsel, ow, n_lat_tiles)
    route, counts = _router(xx, modsel, rw, n_lat_tiles)
    dest, block_expert, used, n_blocks = _moe_plan(route, counts, b * nt)
    x_rows = _dispatch(xx, modsel, dest, n_blocks * MOE_BLOCK, n_lat_tiles)
    y_rows = _moe_ffn(x_rows, block_expert, used, n_blocks, layer, w1, w3, w2)
    return _combine(xx, modsel, route, y_rows, dest, n_lat_tiles, n_lat_tiles if last else nt // TM)


def kernel(x, c, ctx, c_ctx, w_ada, b_ada, w_in, w_out, mla_g_cq, mla_g_ckv, mla_w_uq, mla_w_ukv, mla_g_qn, mla_g_qr, mla_g_kn, mla_g_kr, na_g_q, na_g_k, na_rpb, ret_g_out, hg_lb_raw, hg_g_out, moe_w_rg, moe_b_rg, moe_w_re, moe_b_re, moe_w1, moe_w3, moe_w2):
    b, n_lat, d = x.shape
    n_ctx = ctx.shape[1]
    depth = w_in.shape[0]
    assert n_lat % TM == 0 and n_ctx % TM == 0 and TM % GRID_W == 0
    assert n_lat // TM >= 3 and n_lat // GRID_W >= NA_WIN_ROWS
    assert w_in.shape[2] == MLA_Q_LORA + MLA_KV_LORA + MLA_ROPE + 12 * GROUP_W

    cc = jnp.concatenate([c, c_ctx[None, :], jnp.zeros((16 - b - 1, d), F32)], axis=0)
    mods = _ada_all(cc, w_ada, b_ada)
    rope_c, rope_s = _rope_tables(n_lat, n_ctx)
    rc = _retention_consts()
    hc = _hgrn_consts()
    lb_w = jax.nn.softmax(hg_lb_raw.astype(F32), axis=0)
    hg_lb = jnp.cumsum(lb_w, axis=0) - lb_w[0:1]

    xx = jnp.concatenate([x, ctx], axis=1)
    tile_go = lambda g: jnp.tile(g, N_HEADS)[None, :]
    for l in range(depth):
        modsel = jnp.stack([mods[l, :b], jnp.broadcast_to(mods[l, b], (b, 6 * d))], axis=1).reshape(2 * b, 1, 6 * d)
        lw = _layer_weights(l, w_in, w_out, mla_g_cq, mla_g_ckv, mla_w_uq, mla_w_ukv, mla_g_qn, mla_g_qr,
                            mla_g_kn, mla_g_kr, na_g_q, na_g_k, moe_w_rg, moe_b_rg, moe_w_re, moe_b_re)
        xx = _layer(xx, modsel, lw, rope_c, rope_s, _na_bias_table(na_rpb[l]), rc, hc, hg_lb[l][None, :],
                    tile_go(ret_g_out[l]), tile_go(hg_g_out[l]),
                    l, moe_w1, moe_w3, moe_w2,
                    n_lat, n_ctx, l == depth - 1)
    return xx
```

```python
import functools

import numpy as np
import jax
import jax.numpy as jnp
from jax import lax
from jax.experimental import pallas as pl
from jax.experimental.pallas import tpu as pltpu

F32 = jnp.float32
BF16 = jnp.bfloat16

EPS = 1e-6
ROPE_THETA = 10000.0
NEG_BIG = -1e30
F_FLOOR = 1e-20
GRID_W = 64
N_HEADS = 4
HEAD_DIM = 64
LANES = 128
GROUP_W = N_HEADS * HEAD_DIM
PAD_W = N_HEADS * LANES
MLA_Q_LORA = 192
MLA_KV_LORA = 128
MLA_ROPE = 32
MLA_SCALE = (HEAD_DIM + MLA_ROPE) ** -0.5
LOG2E = 1.4426950408889634
NA_KH = 8
NA_KW = 16
NA_SCALE = HEAD_DIM ** -0.5
RET_CHUNK = 128
HG_CHUNK = 16
MOE_GROUPS = 4
MOE_PER_GROUP = 8
MOE_EXPERTS = MOE_GROUPS * MOE_PER_GROUP
MOE_TOPK = 2
MOE_BLOCK = 256
TM = 256
NA_TILE_ROWS = TM // GRID_W
NA_WIN_ROWS = NA_TILE_ROWS + NA_KH
VMEM_LIMIT = 56 * 1024 * 1024
DMA_ISSUE_UNROLL = 8

COL_NA_Q, COL_NA_K, COL_NA_V = 2, 4, 6
COL_RET_Q, COL_RET_K, COL_RET_V, COL_RET_G = 8, 9, 10, 11
COL_HG_Q, COL_HG_FF, COL_HG_FB, COL_HG_I, COL_HG_G = 12, 13, 14, 15, 16
P_COLS = 17 * GROUP_W


def _cparams(sem):
    return pltpu.CompilerParams(dimension_semantics=sem, vmem_limit_bytes=VMEM_LIMIT)


def _sigmoid(x):
    return 1.0 / (1.0 + jnp.exp(-x))


def _silu(x):
    return x * _sigmoid(x)


def _dot(a, b):
    return jnp.dot(a, b, preferred_element_type=F32)


def _dot_nt(a, b):
    return lax.dot_general(a, b, (((1,), (1,)), ((), ())), preferred_element_type=F32)


def _split_dot_l(x, m, n):
    acc = None
    rem = x
    for i in range(n):
        piece = rem.astype(BF16)
        d = _dot(piece, m)
        acc = d if acc is None else acc + d
        if i + 1 < n:
            rem = rem - piece.astype(F32)
    return acc


def _split_dot_r(m, x, n):
    acc = None
    rem = x
    for i in range(n):
        piece = rem.astype(BF16)
        d = _dot(m, piece)
        acc = d if acc is None else acc + d
        if i + 1 < n:
            rem = rem - piece.astype(F32)
    return acc


def _pack_bf16_pairs(x):
    half = x.shape[1] // 2
    bits = lax.bitcast_convert_type(x.astype(BF16).astype(F32), jnp.uint32)
    return (bits[:, :half] >> 16) | (bits[:, half:] & jnp.uint32(0xFFFF0000))


def _unpack_bf16_pairs(p):
    lo = lax.bitcast_convert_type(p << 16, F32)
    hi = lax.bitcast_convert_type(p & jnp.uint32(0xFFFF0000), F32)
    return jnp.concatenate([lo, hi], axis=1).astype(BF16)


def _seg_rms(x, m, gain):
    return x * lax.rsqrt(_split_dot_l(x * x, m, 2) + EPS) * gain


def _ada_kernel(c_ref, w_ref, b_ref, o_ref):
    s = _silu(c_ref[...])
    o_ref[0] = jnp.dot(s, w_ref[0], preferred_element_type=F32,
                       precision=lax.Precision.HIGHEST) + b_ref[0]


def _ada_all(cc, w_ada, b_ada):
    n_layers, d, d6 = w_ada.shape
    bn = 512
    rows = cc.shape[0]
    return pl.pallas_call(
        _ada_kernel,
        grid=(n_layers, d6 // bn),
        in_specs=[pl.BlockSpec((rows, d), lambda l, j: (0, 0)),
                  pl.BlockSpec((1, d, bn), lambda l, j: (l, 0, j)),
                  pl.BlockSpec((1, 1, bn), lambda l, j: (l, 0, j))],
        out_specs=pl.BlockSpec((1, rows, bn), lambda l, j: (l, 0, j)),
        out_shape=jax.ShapeDtypeStruct((n_layers, rows, d6), F32),
        compiler_params=_cparams(("arbitrary", "arbitrary")),
    )(cc, w_ada, b_ada.reshape(n_layers, 1, d6))


def _mod_spec(d6, n_lat_tiles):
    return pl.BlockSpec((1, 1, d6), lambda b, j: (2 * b + (j >= n_lat_tiles).astype(jnp.int32), 0, 0))


def _modulate(x, shift, scale):
    xn = x * lax.rsqrt(jnp.mean(x * x, axis=-1, keepdims=True) + EPS)
    return xn * (1.0 + scale) + shift


def _inproj_kernel(x_ref, mod_ref, w_ref, o_ref, *, d):
    xm = _modulate(x_ref[0], mod_ref[0, :, 0:d], mod_ref[0, :, d:2 * d])
    o_ref[0] = _dot(xm.astype(BF16), w_ref[...])


def _inproj(xx, modsel, w_in_p, n_lat_tiles):
    b, nt, d = xx.shape
    pc = w_in_p.shape[1]
    return pl.pallas_call(
        functools.partial(_inproj_kernel, d=d),
        grid=(b, nt // TM),
        in_specs=[pl.BlockSpec((1, TM, d), lambda i, j: (i, j, 0)),
                  _mod_spec(6 * d, n_lat_tiles),
                  pl.BlockSpec((d, pc), lambda i, j: (0, 0))],
        out_specs=pl.BlockSpec((1, TM, pc), lambda i, j: (i, j, 0)),
        out_shape=jax.ShapeDtypeStruct((b, nt, pc), F32),
        compiler_params=_cparams(("arbitrary", "arbitrary")),
    )(xx, modsel, w_in_p)


def _prep_kernel(p_ref, c_ref, s_ref, wuq_ref, wk_ref, wv_ref, gcq_ref, gckv_ref, gkr_ref,
                 gq_ref, gk_ref, mq_ref, mk_ref, gnq_ref, gnk_ref, mn_ref,
                 qm_ref, km_ref, vm_ref, qn_ref, kn_ref, vn_ref):
    cq = p_ref[0, :, 0:256]
    ckv = p_ref[0, :, 256:384]
    kr = p_ref[0, :, 384:512]
    cqn = cq * lax.rsqrt(jnp.sum(cq * cq, axis=-1, keepdims=True) * (1.0 / MLA_Q_LORA) + EPS) * gcq_ref[...]
    ckvn = (ckv * lax.rsqrt(jnp.mean(ckv * ckv, axis=-1, keepdims=True) + EPS) * gckv_ref[...]).astype(BF16)
    krn = kr * lax.rsqrt(jnp.sum(kr * kr, axis=-1, keepdims=True) * (1.0 / MLA_ROPE) + EPS) * gkr_ref[...]
    q = _seg_rms(_dot(cqn.astype(BF16), wuq_ref[...]), mq_ref[...], gq_ref[...])
    kk = _seg_rms(_dot(ckvn, wk_ref[...]), mk_ref[...], gk_ref[...])
    vv = _dot(ckvn, wv_ref[...])

    cos = c_ref[...]
    sin = s_ref[...]
    lane = lax.broadcasted_iota(jnp.int32, (TM, LANES), 1)
    first = (lane % 16) < 8

    def rope(x):
        partner = jnp.where(first, pltpu.roll(x, LANES - 8, 1), pltpu.roll(x, 8, 1))
        return x * cos + partner * sin

    krr = rope(krn)
    nq = _seg_rms(p_ref[0, :, 512:1024], mn_ref[...], gnq_ref[...])
    nk = _seg_rms(p_ref[0, :, 1024:1536], mn_ref[...], gnk_ref[...])
    for h in range(N_HEADS):
        sl = slice(h * LANES, (h + 1) * LANES)
        qm_ref[0, h] = (rope(q[:, sl]) * (MLA_SCALE * LOG2E)).astype(BF16)
        km_ref[0, h] = (kk[:, sl] + krr).astype(BF16)
        vm_ref[0, h] = vv[:, sl].T.astype(BF16)
        qn_ref[0, h] = (nq[:, sl] * (NA_SCALE * LOG2E)).astype(BF16)
        kn_ref[0, h] = nk[:, sl].astype(BF16)
        vn_ref[0, h] = p_ref[0, :, 1536 + h * LANES:1536 + (h + 1) * LANES].astype(BF16)


def _prep(p, rope_c, rope_s, pw):
    b, nt, _ = p.shape
    full = lambda a: pl.BlockSpec(a.shape, lambda i, j: (0,) * a.ndim)
    consts = [pw['wuq'], pw['wk'], pw['wv'], pw['gcq'], pw['gckv'], pw['gkr'], pw['gq'], pw['gk'],
              pw['mq'], pw['mk'], pw['gnq'], pw['gnk'], pw['mn']]
    head_spec = pl.BlockSpec((1, N_HEADS, TM, LANES), lambda i, j: (i, 0, j, 0))
    head_shape = jax.ShapeDtypeStruct((b, N_HEADS, nt, LANES), BF16)
    head_t_spec = pl.BlockSpec((1, N_HEADS, LANES, TM), lambda i, j: (i, 0, 0, j))
    head_t_shape = jax.ShapeDtypeStruct((b, N_HEADS, LANES, nt), BF16)
    return pl.pallas_call(
        _prep_kernel,
        grid=(b, nt // TM),
        in_specs=[pl.BlockSpec((1, TM, 2048), lambda i, j: (i, j, 0)),
                  pl.BlockSpec((TM, LANES), lambda i, j: (j, 0)),
                  pl.BlockSpec((TM, LANES), lambda i, j: (j, 0))] + [full(a) for a in consts],
        out_specs=[head_spec, head_spec, head_t_spec, head_spec, head_spec, head_spec],
        out_shape=[head_shape, head_shape, head_t_shape, head_shape, head_shape, head_shape],
        compiler_params=_cparams(("arbitrary", "arbitrary")),
    )(p, rope_c, rope_s, *consts)


def _softmax2_pv(s, v):
    m = jnp.max(s, axis=-1, keepdims=True)
    e = jnp.exp2(s - m)
    l = jnp.sum(e, axis=-1, keepdims=True)
    return _dot(e.astype(BF16), v) / l


MLA_KEY_BLOCK = 2176
MLA_Q_TILE = 512


MLA_HEADS_PER_STEP = 4


def _mla_attend(q_ref, k_ref, vt_ref, o_ref, n_q, k0, k1):
    items = [(hh, s0) for hh in range(MLA_HEADS_PER_STEP) for s0 in range(k0, k1, MLA_KEY_BLOCK)]
    score = lambda hh, s0: _dot_nt(k_ref[0, hh, s0:min(s0 + MLA_KEY_BLOCK, k1), :], q_ref[0, hh, 0:n_q, :])
    st = score(*items[0])
    m = l = acc = None
    for i, (hh, s0) in enumerate(items):
        st_next = score(*items[i + 1]) if i + 1 < len(items) else None
        vt_blk = vt_ref[0, hh, :, s0:min(s0 + MLA_KEY_BLOCK, k1)]
        bm = jnp.max(st, axis=0, keepdims=True)
        if s0 == k0:
            m = bm
            e = jnp.exp2(st - m)
            l = jnp.sum(e, axis=0, keepdims=True)
            acc = _dot(vt_blk, e.astype(BF16))
        else:
            m_new = jnp.maximum(m, bm)
            alpha = jnp.exp2(m - m_new)
            e = jnp.exp2(st - m_new)
            l = l * alpha + jnp.sum(e, axis=0, keepdims=True)
            acc = acc * alpha + _dot(vt_blk, e.astype(BF16))
            m = m_new
        if s0 + MLA_KEY_BLOCK >= k1:
            o_ref[0, 0:n_q, hh * LANES:(hh + 1) * LANES] = (acc / l).T
        st = st_next


def _mla_kernel(q_ref, k_ref, vt_ref, o_ref, *, n_lat, n_ctx):
    j = pl.program_id(2)

    @pl.when(j < n_lat // MLA_Q_TILE)
    def _():
        _mla_attend(q_ref, k_ref, vt_ref, o_ref, MLA_Q_TILE, 0, n_lat + n_ctx)

    @pl.when(j >= n_lat // MLA_Q_TILE)
    def _():
        _mla_attend(q_ref, k_ref, vt_ref, o_ref, n_ctx, n_lat, n_lat + n_ctx)


def _mla_attn(qm, km, vmt, n_lat, n_ctx):
    b, h, nt, _ = qm.shape
    hp = MLA_HEADS_PER_STEP
    tq = MLA_Q_TILE
    assert n_lat % tq == 0 and n_ctx <= tq
    kv_spec = pl.BlockSpec((1, hp, nt, LANES), lambda i, hh, j: (i, hh, 0, 0))
    vt_spec = pl.BlockSpec((1, hp, LANES, nt), lambda i, hh, j: (i, hh, 0, 0))
    return pl.pallas_call(
        functools.partial(_mla_kernel, n_lat=n_lat, n_ctx=n_ctx),
        grid=(b, h // hp, n_lat // tq + 1),
        in_specs=[pl.BlockSpec((1, hp, tq, LANES), lambda i, hh, j: (i, hh, j, 0)), kv_spec, vt_spec],
        out_specs=pl.BlockSpec((1, tq, hp * LANES), lambda i, hh, j: (i, j, hh)),
        out_shape=jax.ShapeDtypeStruct((b, nt, PAD_W), F32),
        compiler_params=_cparams(("arbitrary", "arbitrary", "arbitrary")),
    )(qm, km, vmt)


def _na_kernel(q_ref, k_ref, v_ref, bias_ref, o_ref, *, n_lat, n_ctx):
    j = pl.program_id(1)
    rows = n_lat // GRID_W
    n_tiles = n_lat // TM
    win = NA_WIN_ROWS * GRID_W

    @pl.when(j < n_tiles)
    def _():
        start = jnp.clip(j * NA_TILE_ROWS - NA_KH // 2, 0, rows - NA_WIN_ROWS)
        case = jnp.where(j == 0, 0, jnp.where(j == n_tiles - 1, 2, 1))
        tok0 = pl.multiple_of(start * GRID_W, GRID_W)
        for h in range(N_HEADS):
            q = q_ref[0, h]
            s1 = _dot_nt(q, k_ref[0, h, pl.ds(tok0, win), :]) + bias_ref[case, h]
            s2 = _dot_nt(q, k_ref[0, h, pl.ds(n_lat, n_ctx), :])
            m = jnp.maximum(jnp.max(s1, axis=-1, keepdims=True), jnp.max(s2, axis=-1, keepdims=True))
            e1 = jnp.exp2(s1 - m)
            e2 = jnp.exp2(s2 - m)
            l = jnp.sum(e1, axis=-1, keepdims=True) + jnp.sum(e2, axis=-1, keepdims=True)
            o = _dot(e1.astype(BF16), v_ref[0, h, pl.ds(tok0, win), :])
            o = o + _dot(e2.astype(BF16), v_ref[0, h, pl.ds(n_lat, n_ctx), :])
            o_ref[0, :, h * LANES:(h + 1) * LANES] = o / l

    @pl.when(j >= n_lat // TM)
    def _():
        for h in range(N_HEADS):
            s = _dot_nt(q_ref[0, h], k_ref[0, h, pl.ds(n_lat, n_ctx), :])
            o_ref[0, :, h * LANES:(h + 1) * LANES] = _softmax2_pv(s, v_ref[0, h, pl.ds(n_lat, n_ctx), :])


def _na_attn(qn, kn, vn, bias, n_lat, n_ctx):
    b, h, nt, _ = qn.shape
    kv_spec = pl.BlockSpec((1, h, nt, LANES), lambda i, j: (i, 0, 0, 0))
    return pl.pallas_call(
        functools.partial(_na_kernel, n_lat=n_lat, n_ctx=n_ctx),
        grid=(b, nt // TM),
        in_specs=[pl.BlockSpec((1, h, TM, LANES), lambda i, j: (i, 0, j, 0)), kv_spec, kv_spec,
                  pl.BlockSpec(bias.shape, lambda i, j: (0, 0, 0, 0))],
        out_specs=pl.BlockSpec((1, TM, PAD_W), lambda i, j: (i, j, 0)),
        out_shape=jax.ShapeDtypeStruct((b, nt, PAD_W), F32),
        compiler_params=_cparams(("arbitrary", "arbitrary")),
    )(qn, kn, vn, bias)


def _head_mask(h, shape):
    return (lax.broadcasted_iota(jnp.int32, shape, 1) // HEAD_DIM) == h


def _ret_state_step(s_ref, q, k, v, qw, kw, cd, bd):
    state = s_ref[...]
    o = _dot((q * qw).astype(BF16), state.astype(BF16))
    upd = _dot((k * kw).T.astype(BF16), v.astype(BF16))
    s_ref[...] = state * cd + upd * bd
    return o


def _ret_fwd_kernel(q_ref, k_ref, v_ref, dm_ref, qw_ref, kw_ref, cd_ref, bd_ref, o_ref, s_ref):
    @pl.when(pl.program_id(1) == 0)
    def _():
        s_ref[...] = jnp.zeros_like(s_ref)

    for c in range(TM // RET_CHUNK):
        rows = slice(c * RET_CHUNK, (c + 1) * RET_CHUNK)
        q = q_ref[0, rows]
        k = k_ref[0, rows]
        v = v_ref[0, rows]
        o = _ret_state_step(s_ref, q, k, v, qw_ref[...], kw_ref[...], cd_ref[...], bd_ref[...])
        kb = k.astype(BF16)
        for h in range(N_HEADS):
            hm = _head_mask(h, q.shape)
            sc = _dot_nt(jnp.where(hm, q, 0.0).astype(BF16), kb) * dm_ref[h]
            o = o + _dot(sc.astype(BF16), jnp.where(hm, v, 0.0).astype(BF16))
        o_ref[0, rows] = o


def _ret_bwd_kernel(q_ref, k_ref, v_ref, g_ref, op_ref, qw_ref, kw_ref, cd_ref, bd_ref, ms_ref, go_ref,
                    y_ref, s_ref):
    @pl.when(pl.program_id(1) == 0)
    def _():
        s_ref[...] = jnp.zeros_like(s_ref)

    for c in reversed(range(TM // RET_CHUNK)):
        rows = slice(c * RET_CHUNK, (c + 1) * RET_CHUNK)
        o = op_ref[0, rows] + _ret_state_step(s_ref, q_ref[0, rows], k_ref[0, rows], v_ref[0, rows], qw_ref[...],
                                              kw_ref[...], cd_ref[...], bd_ref[...])
        y_ref[0, rows] = _seg_rms(o, ms_ref[...], go_ref[...]) * _silu(g_ref[0, rows])


def _scan_order(n_lat_t, n_ctx_t, reverse):
    if reverse:
        return lambda i: jnp.where(i < n_ctx_t, n_lat_t + n_ctx_t - 1 - i, n_lat_t + n_ctx_t - 1 - i)
    return lambda i: jnp.where(i < n_ctx_t, n_lat_t + i, i - n_ctx_t)


def _retention(p, rc, go, n_lat, n_ctx):
    b, nt, _ = p.shape
    c = TM
    n_lat_t, n_ctx_t = n_lat // c, n_ctx // c
    fwd = _scan_order(n_lat_t, n_ctx_t, False)
    bwd = _scan_order(n_lat_t, n_ctx_t, True)
    col = lambda order, cb: pl.BlockSpec((1, c, GROUP_W), lambda i, j: (i, order(j), cb))
    full = lambda a: pl.BlockSpec(a.shape, lambda i, j: (0,) * a.ndim)
    out_shape = jax.ShapeDtypeStruct((b, nt, GROUP_W), F32)
    scratch = [pltpu.VMEM((GROUP_W, GROUP_W), F32)]
    consts_f = [rc['dm'], rc['qw_f'], rc['kw_f'], rc['cd_f'], rc['bd']]
    o_part = pl.pallas_call(
        _ret_fwd_kernel,
        grid=(b, nt // c),
        in_specs=[col(fwd, COL_RET_Q), col(fwd, COL_RET_K), col(fwd, COL_RET_V)] + [full(a) for a in consts_f],
        out_specs=pl.BlockSpec((1, c, GROUP_W), lambda i, j: (i, fwd(j), 0)),
        out_shape=out_shape,
        scratch_shapes=scratch,
        compiler_params=_cparams(("arbitrary", "arbitrary")),
    )(p, p, p, *consts_f)
    consts_b = [rc['qw_b'], rc['kw_b'], rc['cd_b'], rc['bd'], rc['ms'], go]
    return pl.pallas_call(
        _ret_bwd_kernel,
        grid=(b, nt // c),
        in_specs=[col(bwd, COL_RET_Q), col(bwd, COL_RET_K), col(bwd, COL_RET_V), col(bwd, COL_RET_G),
                  pl.BlockSpec((1, c, GROUP_W), lambda i, j: (i, bwd(j), 0))] + [full(a) for a in consts_b],
        out_specs=pl.BlockSpec((1, c, GROUP_W), lambda i, j: (i, bwd(j), 0)),
        out_shape=out_shape,
        scratch_shapes=scratch,
        compiler_params=_cparams(("arbitrary", "arbitrary")),
    )(p, p, p, p, o_part, *consts_b)


def _hg_direction(q_ref, f_ref, v_ref, lb_ref, ain_ref, aex_ref, bseg_ref, bd_ref, st_ref, sh_ref, *, reverse):
    n_chunks = TM // HG_CHUNK
    assert n_chunks == HG_CHUNK
    qh = _silu(q_ref[0])
    lb = lb_ref[...]
    f = jnp.maximum(lb + (1.0 - lb) * _sigmoid(f_ref[0]), F_FLOOR)
    lf = jnp.log(f) * LOG2E
    k = 1.0 - f
    v = v_ref[0]
    row = lax.broadcasted_iota(jnp.int32, (TM, 1), 0)
    pos = row % HG_CHUNK
    row_chunk = row // HG_CHUNK

    a_in = _split_dot_r(ain_ref[...], lf, 3)
    a_ex = _split_dot_r(aex_ref[...], lf, 3)
    width = v.shape[1]
    for slot, val in enumerate((k, a_in, v)):
        sh_ref[slot] = val.reshape(n_chunks, HG_CHUNK, width)

    def key_row(slot, s):
        return jnp.broadcast_to(sh_ref[slot, :, s:s + 1, :], (n_chunks, HG_CHUNK, width)).reshape(TM, width)
    qp = (qh * jnp.exp2(a_in)).astype(BF16)
    kdec = k * jnp.exp2(a_ex)
    lam_all = jnp.exp2(a_in + a_ex)
    vt = v.T.astype(BF16)
    bd = bd_ref[...]
    bseg = bseg_ref[...]
    state = st_ref[...]
    parts = [None] * n_chunks
    o_band = jnp.zeros_like(v)
    for step in range(n_chunks):
        c = n_chunks - 1 - step if reverse else step
        r0 = c * HG_CHUNK
        parts[c] = _dot_nt(qp[r0:r0 + HG_CHUNK], state.astype(BF16))
        upd = _dot(vt, jnp.where(row_chunk == c, kdec, 0.0).astype(BF16))
        state = state * lam_all[r0:r0 + 1] + upd * bd
        s = step
        valid = (pos <= s) if reverse else (pos >= s)
        w = jnp.where(valid, qh * key_row(0, s) * jnp.exp2(a_in - key_row(1, s)), 0.0)
        o_band = o_band + _dot(w.astype(BF16), bseg) * key_row(2, s)
    st_ref[...] = state
    return jnp.concatenate(parts, axis=0) + o_band


def _hg_fwd_kernel(q_ref, f_ref, v_ref, lb_ref, ain_ref, aex_ref, bseg_ref, bd_ref, o_ref, st_ref, sh_ref):
    @pl.when(pl.program_id(1) == 0)
    def _():
        st_ref[...] = jnp.zeros_like(st_ref)

    o_ref[0] = _hg_direction(q_ref, f_ref, v_ref, lb_ref, ain_ref, aex_ref, bseg_ref, bd_ref, st_ref, sh_ref,
                             reverse=False)


def _hg_bwd_kernel(q_ref, f_ref, v_ref, g_ref, op_ref, lb_ref, ain_ref, aex_ref, bseg_ref, bd_ref,
                   ms_ref, go_ref, y_ref, st_ref, sh_ref):
    @pl.when(pl.program_id(1) == 0)
    def _():
        st_ref[...] = jnp.zeros_like(st_ref)

    o = op_ref[0] + _hg_direction(q_ref, f_ref, v_ref, lb_ref, ain_ref, aex_ref, bseg_ref, bd_ref, st_ref, sh_ref,
                                  reverse=True)
    y_ref[0] = _seg_rms(o, ms_ref[...], go_ref[...]) * _silu(g_ref[0])


def _hgrn2(p, hc, lb, go, n_lat, n_ctx):
    b, nt, _ = p.shape
    n_lat_t, n_ctx_t = n_lat // TM, n_ctx // TM
    fwd = _scan_order(n_lat_t, n_ctx_t, False)
    bwd = _scan_order(n_lat_t, n_ctx_t, True)
    col = lambda order, cb: pl.BlockSpec((1, TM, GROUP_W), lambda i, j: (i, order(j), cb))
    full = lambda a: pl.BlockSpec(a.shape, lambda i, j: (0,) * a.ndim)
    out_shape = jax.ShapeDtypeStruct((b, nt, GROUP_W), F32)
    scratch = [pltpu.VMEM((GROUP_W, GROUP_W), F32), pltpu.VMEM((3, TM // HG_CHUNK, HG_CHUNK, GROUP_W), F32)]
    consts_f = [lb, hc['lincl'], hc['uexcl'], hc['bseg'], hc['bd']]
    o_part = pl.pallas_call(
        _hg_fwd_kernel,
        grid=(b, nt // TM),
        in_specs=[col(fwd, COL_HG_Q), col(fwd, COL_HG_FF), col(fwd, COL_HG_I)] + [full(a) for a in consts_f],
        out_specs=pl.BlockSpec((1, TM, GROUP_W), lambda i, j: (i, fwd(j), 0)),
        out_shape=out_shape,
        scratch_shapes=scratch,
        compiler_params=_cparams(("arbitrary", "arbitrary")),
    )(p, p, p, *consts_f)
    consts_b = [lb, hc['uincl'], hc['lexcl'], hc['bseg'], hc['bd'], hc['ms'], go]
    return pl.pallas_call(
        _hg_bwd_kernel,
        grid=(b, nt // TM),
        in_specs=[col(bwd, COL_HG_Q), col(bwd, COL_HG_FB), col(bwd, COL_HG_I), col(bwd, COL_HG_G),
                  pl.BlockSpec((1, TM, GROUP_W), lambda i, j: (i, bwd(j), 0))] + [full(a) for a in consts_b],
        out_specs=pl.BlockSpec((1, TM, GROUP_W), lambda i, j: (i, bwd(j), 0)),
        out_shape=out_shape,
        scratch_shapes=scratch,
        compiler_params=_cparams(("arbitrary", "arbitrary")),
    )(p, p, p, p, o_part, *consts_b)


def _outproj_kernel(x_ref, ym_ref, yn_ref, yr_ref, yh_ref, mod_ref, wm_ref, wn_ref, wr_ref, wh_ref, o_ref, *, d):
    acc = _dot(ym_ref[0].astype(BF16), wm_ref[...])
    acc = acc + _dot(yn_ref[0].astype(BF16), wn_ref[...])
    acc = acc + _dot(yr_ref[0].astype(BF16), wr_ref[...])
    acc = acc + _dot(yh_ref[0].astype(BF16), wh_ref[...])
    o_ref[0] = x_ref[0] + mod_ref[0, :, 2 * d:3 * d] * acc


def _outproj(xx, y_mla, y_na, y_ret, y_hg, modsel, ow, n_lat_tiles):
    b, nt, d = xx.shape
    tile = lambda w: pl.BlockSpec((1, TM, w), lambda i, j: (i, j, 0))
    full = lambda a: pl.BlockSpec(a.shape, lambda i, j: (0, 0))
    ws = [ow['mla'], ow['na'], ow['ret'], ow['hg']]
    return pl.pallas_call(
        functools.partial(_outproj_kernel, d=d),
        grid=(b, nt // TM),
        in_specs=[tile(d), tile(PAD_W), tile(PAD_W), tile(GROUP_W), tile(GROUP_W),
                  _mod_spec(6 * d, n_lat_tiles)] + [full(a) for a in ws],
        out_specs=tile(d),
        out_shape=jax.ShapeDtypeStruct((b, nt, d), F32),
        compiler_params=_cparams(("arbitrary", "arbitrary")),
    )(xx, y_mla, y_na, y_ret, y_hg, modsel, *ws)


def _router_kernel(x_ref, mod_ref, whi_ref, wlo_ref, br_ref, ltri_ref, r_ref, cnt_ref, *, d):
    @pl.when((pl.program_id(0) == 0) & (pl.program_id(1) == 0))
    def _():
        cnt_ref[...] = jnp.zeros_like(cnt_ref)

    h = _modulate(x_ref[0], mod_ref[0, :, 3 * d:4 * d], mod_ref[0, :, 4 * d:5 * d])
    h_hi = h.astype(BF16)
    h_lo = (h - h_hi.astype(F32)).astype(BF16)
    lg = _dot(h_hi, whi_ref[...]) + _dot(h_lo, whi_ref[...]) + _dot(h_hi, wlo_ref[...]) + br_ref[...]

    lane = lax.broadcasted_iota(jnp.int32, lg.shape, 1).astype(F32)
    far = 1e9

    def first_argmax(vals, vmax):
        return jnp.min(jnp.where(vals == vmax, lane, far), axis=-1, keepdims=True)

    gl = jnp.where(lane < MOE_GROUPS, lg, NEG_BIG)
    gmax = jnp.max(gl, axis=-1, keepdims=True)
    pg_top = 1.0 / jnp.sum(jnp.exp(gl - gmax), axis=-1, keepdims=True)
    lo = MOE_GROUPS + MOE_PER_GROUP * first_argmax(gl, gmax)
    fl = jnp.where((lane >= lo) & (lane < lo + MOE_PER_GROUP), lg, NEG_BIG)
    fmax = jnp.max(fl, axis=-1, keepdims=True)
    fsum = jnp.sum(jnp.exp(fl - fmax), axis=-1, keepdims=True)
    i1 = first_argmax(fl, fmax)
    fl2 = jnp.where(lane == i1, NEG_BIG, fl)
    f2max = jnp.max(fl2, axis=-1, keepdims=True)
    i2 = first_argmax(fl2, f2max)
    p1 = 1.0 / fsum
    p2 = jnp.exp(f2max - fmax) / fsum
    g1 = pg_top * p1 / (p1 + p2)
    g2 = pg_top * p2 / (p1 + p2)
    e1 = i1 - MOE_GROUPS
    e2 = i2 - MOE_GROUPS

    onehot = jnp.where(lane == e1, 1.0, 0.0) + jnp.where(lane == e2, 1.0, 0.0)
    before = cnt_ref[...] + _dot(ltri_ref[...], onehot.astype(BF16))
    r1 = jnp.sum(jnp.where(lane == e1, before, 0.0), axis=-1, keepdims=True)
    r2 = jnp.sum(jnp.where(lane == e2, before, 0.0), axis=-1, keepdims=True)
    cnt_ref[...] += jnp.sum(onehot, axis=0, keepdims=True)

    out = jnp.zeros_like(lg)
    for col, val in enumerate((e1, e2, g1, g2, r1, r2)):
        out = jnp.where(lane == col, val, out)
    r_ref[...] = out


ROUTE_E, ROUTE_G, ROUTE_R = 0, 2, 4


def _router(xx, modsel, rw, n_lat_tiles):
    b, nt, d = xx.shape
    tiles = nt // TM
    full = lambda a: pl.BlockSpec(a.shape, lambda i, j: (0, 0))
    ltri = jnp.asarray(np.tril(np.ones((TM, TM), np.float32), -1), BF16)
    ws = [rw['hi'], rw['lo'], rw['b'], ltri]
    return pl.pallas_call(
        functools.partial(_router_kernel, d=d),
        grid=(b, tiles),
        in_specs=[pl.BlockSpec((1, TM, d), lambda i, j: (i, j, 0)), _mod_spec(6 * d, n_lat_tiles)]
                 + [full(a) for a in ws],
        out_specs=[pl.BlockSpec((TM, LANES), lambda i, j: (i * tiles + j, 0)),
                   pl.BlockSpec((1, LANES), lambda i, j: (0, 0))],
        out_shape=[jax.ShapeDtypeStruct((b * nt, LANES), F32), jax.ShapeDtypeStruct((1, LANES), F32)],
        compiler_params=_cparams(("arbitrary", "arbitrary")),
    )(xx, modsel, *ws)


def _moe_plan(route, counts_f, n_tok):
    counts = counts_f[0, :MOE_EXPERTS].astype(jnp.int32)
    padded = (counts + MOE_BLOCK - 1) // MOE_BLOCK * MOE_BLOCK
    pad_end = jnp.cumsum(padded)
    pad_start = pad_end - padded
    n_blocks = -(-(n_tok * MOE_TOPK) // MOE_BLOCK) + MOE_EXPERTS
    blk0 = jnp.arange(n_blocks, dtype=jnp.int32) * MOE_BLOCK
    block_expert = jnp.minimum(jnp.sum((pad_end[None, :] <= blk0[:, None]).astype(jnp.int32), axis=1),
                               MOE_EXPERTS - 1)
    used = (pad_end[-1] // MOE_BLOCK).reshape(1)
    expert = route[:, ROUTE_E:ROUTE_E + MOE_TOPK].astype(jnp.int32)
    rank = route[:, ROUTE_R:ROUTE_R + MOE_TOPK].astype(jnp.int32)
    start_of = jnp.sum(jnp.where(expert[..., None] == jnp.arange(MOE_EXPERTS, dtype=jnp.int32), pad_start, 0), axis=-1)
    return start_of + rank, block_expert, used, n_blocks


def _idx_blocks(dest, k, n_tiles):
    return dest[:, k].reshape(n_tiles, 1, TM)


def _dispatch_kernel(d0_ref, d1_ref, x_ref, mod_ref, rows_in, rows_out, h_ref, sem, *, d):
    del rows_in
    step = pl.program_id(0) * pl.num_programs(1) + pl.program_id(1)
    n_steps = pl.num_programs(0) * pl.num_programs(1)
    slot = step % 2

    def wait_slot(s):
        for _ in range(MOE_TOPK):
            pltpu.make_async_copy(h_ref.at[s], rows_out.at[pl.ds(0, TM)], sem.at[s]).wait()

    h = _modulate(x_ref[0], mod_ref[0, :, 3 * d:4 * d], mod_ref[0, :, 4 * d:5 * d])
    h_ref[slot] = _pack_bf16_pairs(h)

    def issue(r, carry):
        src = h_ref.at[slot, pl.ds(r, 1)]
        pltpu.make_async_copy(src, rows_out.at[pl.ds(d0_ref[0, 0, r], 1)], sem.at[slot]).start()
        pltpu.make_async_copy(src, rows_out.at[pl.ds(d1_ref[0, 0, r], 1)], sem.at[slot]).start()
        return carry

    lax.fori_loop(0, TM, issue, 0, unroll=DMA_ISSUE_UNROLL)

    @pl.when(step > 0)
    def _():
        wait_slot(1 - slot)

    @pl.when(step == n_steps - 1)
    def _():
        wait_slot(slot)


def _dispatch(xx, modsel, dest, n_rows, n_lat_tiles):
    b, nt, d = xx.shape
    tiles = nt // TM
    idx_spec = pl.BlockSpec((1, 1, TM), lambda i, j: (i * tiles + j, 0, 0), memory_space=pltpu.SMEM)
    return pl.pallas_call(
        functools.partial(_dispatch_kernel, d=d),
        grid=(b, tiles),
        in_specs=[idx_spec, idx_spec, pl.BlockSpec((1, TM, d), lambda i, j: (i, j, 0)),
                  _mod_spec(6 * d, n_lat_tiles), pl.BlockSpec(memory_space=pl.ANY)],
        out_specs=pl.BlockSpec(memory_space=pl.ANY),
        out_shape=jax.ShapeDtypeStruct((n_rows, d // 2), jnp.uint32),
        scratch_shapes=[pltpu.VMEM((2, TM, d // 2), jnp.uint32), pltpu.SemaphoreType.DMA((2,))],
        input_output_aliases={4: 0},
        compiler_params=_cparams(("arbitrary", "arbitrary")),
    )(_idx_blocks(dest, 0, b * tiles), _idx_blocks(dest, 1, b * tiles), xx, modsel,
      jnp.zeros((n_rows, d // 2), jnp.uint32))


def _ffn_kernel(be_ref, used_ref, x_ref, w1_ref, w3_ref, w2_ref, y_ref, w1b_ref, w3b_ref, w2b_ref):
    i = pl.program_id(0)

    @pl.when((i == 0) | (be_ref[i] != be_ref[jnp.maximum(i - 1, 0)]))
    def _():
        w1b_ref[...] = w1_ref[0, 0].astype(BF16)
        w3b_ref[...] = w3_ref[0, 0].astype(BF16)
        w2b_ref[...] = w2_ref[0, 0].astype(BF16)

    @pl.when(i < used_ref[0])
    def _():
        x = _unpack_bf16_pairs(x_ref[...])
        mid = _silu(_dot(x, w1b_ref[...])) * _dot(x, w3b_ref[...])
        y_ref[...] = _dot(mid.astype(BF16), w2b_ref[...])

    @pl.when(i >= used_ref[0])
    def _():
        y_ref[...] = jnp.zeros_like(y_ref)


def _moe_ffn(x_rows, block_expert, used, n_blocks, layer, w1, w3, w2):
    d = w1.shape[2]
    ff = w1.shape[3]
    grid_spec = pltpu.PrefetchScalarGridSpec(
        num_scalar_prefetch=2,
        grid=(n_blocks,),
        in_specs=[pl.BlockSpec((MOE_BLOCK, d // 2), lambda i, be, nu: (i, 0)),
                  pl.BlockSpec((1, 1, d, ff), lambda i, be, nu: (layer, be[i], 0, 0)),
                  pl.BlockSpec((1, 1, d, ff), lambda i, be, nu: (layer, be[i], 0, 0)),
                  pl.BlockSpec((1, 1, ff, d), lambda i, be, nu: (layer, be[i], 0, 0))],
        out_specs=pl.BlockSpec((MOE_BLOCK, d), lambda i, be, nu: (i, 0)),
        scratch_shapes=[pltpu.VMEM((d, ff), BF16), pltpu.VMEM((d, ff), BF16), pltpu.VMEM((ff, d), BF16)],
    )
    return pl.pallas_call(
        _ffn_kernel,
        grid_spec=grid_spec,
        out_shape=jax.ShapeDtypeStruct((x_rows.shape[0], d), F32),
        compiler_params=_cparams(("arbitrary",)),
    )(block_expert, used, x_rows, w1, w3, w2)


def _row_gather(src_hbm, idx_ref, dst_ref, sem, n):
    def issue(r, carry):
        pltpu.make_async_copy(src_hbm.at[pl.ds(idx_ref[0, 0, r], 1)], dst_ref.at[pl.ds(r, 1)], sem).start()
        return carry

    lax.fori_loop(0, n, issue, 0, unroll=DMA_ISSUE_UNROLL)


def _row_gather_wait(src_hbm, dst_ref, sem, n):
    pltpu.make_async_copy(src_hbm.at[pl.ds(0, n)], dst_ref, sem).wait()


def _combine_kernel(d0_ref, d1_ref, n0_ref, n1_ref, x_ref, mod_ref, r_ref, y_hbm, o_ref, y_ref, sem, *, d):
    step = pl.program_id(0) * pl.num_programs(1) + pl.program_id(1)
    n_steps = pl.num_programs(0) * pl.num_programs(1)
    slot = step % 2

    def gather(idx_refs, s):
        for k, idx_ref in enumerate(idx_refs):
            _row_gather(y_hbm, idx_ref, y_ref.at[s, k], sem.at[s, k], TM)

    @pl.when(step == 0)
    def _():
        gather((d0_ref, d1_ref), slot)

    @pl.when(step + 1 < n_steps)
    def _():
        gather((n0_ref, n1_ref), 1 - slot)

    route = r_ref[...]
    lane = lax.broadcasted_iota(jnp.int32, route.shape, 1)
    g0 = jnp.sum(jnp.where(lane == ROUTE_G, route, 0.0), axis=-1, keepdims=True)
    g1 = jnp.sum(jnp.where(lane == ROUTE_G + 1, route, 0.0), axis=-1, keepdims=True)
    for k in range(MOE_TOPK):
        _row_gather_wait(y_hbm, y_ref.at[slot, k], sem.at[slot, k], TM)
    o_ref[0] = x_ref[0] + mod_ref[0, :, 5 * d:6 * d] * (y_ref[slot, 0] * g0 + y_ref[slot, 1] * g1)


def _combine(xx, modsel, route, y_rows, dest, n_lat_tiles, out_tiles):
    b, nt, d = xx.shape
    tiles = nt // TM
    idx_spec = pl.BlockSpec((1, 1, TM), lambda i, j: (i * tiles + j, 0, 0), memory_space=pltpu.SMEM)

    def next_block(i, j):
        wrap = j + 1 >= out_tiles
        return (jnp.where(wrap, jnp.minimum(i + 1, b - 1) * tiles, i * tiles + j + 1), 0, 0)

    next_spec = pl.BlockSpec((1, 1, TM), next_block, memory_space=pltpu.SMEM)
    d0, d1 = _idx_blocks(dest, 0, b * tiles), _idx_blocks(dest, 1, b * tiles)
    return pl.pallas_call(
        functools.partial(_combine_kernel, d=d),
        grid=(b, out_tiles),
        in_specs=[idx_spec, idx_spec, next_spec, next_spec, pl.BlockSpec((1, TM, d), lambda i, j: (i, j, 0)),
                  _mod_spec(6 * d, n_lat_tiles),
                  pl.BlockSpec((TM, LANES), lambda i, j: (i * tiles + j, 0)),
                  pl.BlockSpec(memory_space=pl.ANY)],
        out_specs=pl.BlockSpec((1, TM, d), lambda i, j: (i, j, 0)),
        out_shape=jax.ShapeDtypeStruct((b, out_tiles * TM, d), F32),
        scratch_shapes=[pltpu.VMEM((2, MOE_TOPK, TM, d), F32), pltpu.SemaphoreType.DMA((2, MOE_TOPK))],
        compiler_params=_cparams(("arbitrary", "arbitrary")),
    )(d0, d1, d0, d1, xx, modsel, route, y_rows)


def _pad_heads_cols(w):
    lead = w.shape[:-1]
    w = w.reshape(*lead, N_HEADS, HEAD_DIM)
    w = jnp.concatenate([w, jnp.zeros_like(w)], axis=-1)
    return w.reshape(*lead, PAD_W)


def _pad_heads_rows(w):
    return _pad_heads_cols(w.T).T


def _seg_mean_matrix(width, segments):
    m = np.zeros((width, width), np.float32)
    for g in range(width // LANES):
        for start, length in segments:
            a = g * LANES + start
            m[a:a + length, a:a + length] = 1.0 / length
    return jnp.asarray(m, BF16)


def _rope_tables(n_lat, n_ctx):
    pos = jnp.arange(n_lat)
    rows = (pos // GRID_W).astype(F32)
    cols = (pos % GRID_W).astype(F32)
    per_axis = MLA_ROPE // 2
    inv_freq = ROPE_THETA ** (-jnp.arange(0, per_axis, 2, dtype=F32) / per_axis)
    ang = jnp.concatenate([rows[:, None] * inv_freq, cols[:, None] * inv_freq], axis=-1)
    i = np.arange(MLA_ROPE)
    src = (i // 16) * 8 + (i % 8)
    sign = np.where((i % 16) < 8, -1.0, 1.0).astype(np.float32)
    cos = jnp.ones((n_lat, LANES), F32).at[:, HEAD_DIM:HEAD_DIM + MLA_ROPE].set(jnp.cos(ang)[:, src])
    sin = jnp.zeros((n_lat, LANES), F32).at[:, HEAD_DIM:HEAD_DIM + MLA_ROPE].set(jnp.sin(ang)[:, src] * sign)
    cos = jnp.concatenate([cos, jnp.ones((n_ctx, LANES), F32)], axis=0)
    sin = jnp.concatenate([sin, jnp.zeros((n_ctx, LANES), F32)], axis=0)
    return cos, sin


def _na_bias_table(rpb):
    w = np.arange(GRID_W)
    col_start = np.clip(w - NA_KW // 2, 0, GRID_W - NA_KW)
    valid = (w[None, :] >= col_start[:, None]) & (w[None, :] < col_start[:, None] + NA_KW)
    dc = np.clip(w[None, :] - w[:, None], 1 - NA_KW, NA_KW - 1) + (NA_KW - 1)
    onehot = jnp.asarray(dc[None, :, :] == np.arange(2 * NA_KW - 1)[:, None, None], F32)
    t = jnp.einsum('hrd,dqk->hrqk', rpb.astype(F32), onehot, precision=lax.Precision.HIGHEST)
    t = jnp.where(jnp.asarray(valid)[None, None, :, :], t * LOG2E, NEG_BIG)
    masked = jnp.full((N_HEADS, GRID_W, GRID_W), NEG_BIG, F32)
    q_off = (0, NA_KH // 2, NA_KH)
    first = ([0] * NA_TILE_ROWS, list(range(NA_TILE_ROWS)), [NA_KH // 2] * NA_TILE_ROWS)
    cases = []
    for c in range(3):
        row_blocks = []
        for rr in range(NA_TILE_ROWS):
            blocks = []
            for jj in range(NA_WIN_ROWS):
                live = first[c][rr] <= jj < first[c][rr] + NA_KH
                dr = jj - (q_off[c] + rr) + (NA_KH - 1)
                blocks.append(t[:, dr] if live else masked)
            row_blocks.append(jnp.concatenate(blocks, axis=-1))
        cases.append(jnp.concatenate(row_blocks, axis=-2))
    return jnp.stack(cases, axis=0)


def _block_diag_mask(block):
    i = np.arange(GROUP_W) // block
    return (i[:, None] == i[None, :]).astype(np.float32)


def _retention_consts():
    c = RET_CHUNK
    j = np.arange(2 * N_HEADS, dtype=np.float64)
    lg = np.log1p(-np.exp2(-5.0 - j))
    lg_f, lg_b = lg[0::2], lg[1::2]
    pos = np.arange(c, dtype=np.float64)
    diff = pos[:, None] - pos[None, :]
    k_scale = HEAD_DIM ** -0.5
    dm = np.zeros((N_HEADS, c, c))
    for h in range(N_HEADS):
        dm[h] = (np.where(diff >= 0, np.exp(np.maximum(diff, 0.0) * lg_f[h]), 0.0)
                 + np.where(diff <= 0, np.exp(np.maximum(-diff, 0.0) * lg_b[h]), 0.0)) * k_scale
    lanes = lambda per_head: np.repeat(per_head, HEAD_DIM, axis=-1)
    out = {
        'dm': dm,
        'qw_f': lanes(np.exp((pos + 1)[:, None] * lg_f[None, :])),
        'kw_f': lanes(np.exp((c - 1 - pos)[:, None] * lg_f[None, :])) * k_scale,
        'cd_f': lanes(np.exp(c * lg_f)[None, :]),
        'qw_b': lanes(np.exp((c - pos)[:, None] * lg_b[None, :])),
        'kw_b': lanes(np.exp(pos[:, None] * lg_b[None, :])) * k_scale,
        'cd_b': lanes(np.exp(c * lg_b)[None, :]),
        'bd': _block_diag_mask(HEAD_DIM),
    }
    out = {k: jnp.asarray(v, F32) for k, v in out.items()}
    out['ms'] = jnp.asarray(_block_diag_mask(HEAD_DIM) / HEAD_DIM, BF16)
    return out


def _hgrn_consts():
    t = np.arange(TM)
    same = (t[:, None] // HG_CHUNK) == (t[None, :] // HG_CHUNK)
    lincl = same & (t[None, :] <= t[:, None])
    lexcl = same & (t[None, :] < t[:, None])
    return {
        'lincl': jnp.asarray(lincl, BF16), 'lexcl': jnp.asarray(lexcl, BF16),
        'uincl': jnp.asarray(lincl.T, BF16), 'uexcl': jnp.asarray(lexcl.T, BF16),
        'bseg': jnp.asarray(_block_diag_mask(HEAD_DIM), BF16),
        'bd': jnp.asarray(_block_diag_mask(HEAD_DIM), F32),
        'ms': jnp.asarray(_block_diag_mask(HEAD_DIM) / HEAD_DIM, BF16),
    }


def _layer_weights(l, w_in, w_out, mla_g_cq, mla_g_ckv, mla_w_uq, mla_w_ukv, mla_g_qn, mla_g_qr, mla_g_kn,
                   mla_g_kr, na_g_q, na_g_k, moe_w_rg, moe_b_rg, moe_w_re, moe_b_re):
    d = w_in.shape[1]
    w = w_in[l]
    z = lambda n: jnp.zeros((d, n), F32)
    o = 0
    cq, o = w[:, o:o + MLA_Q_LORA], o + MLA_Q_LORA
    ckv, o = w[:, o:o + MLA_KV_LORA], o + MLA_KV_LORA
    kr, o = w[:, o:o + MLA_ROPE], o + MLA_ROPE
    naq, o = w[:, o:o + GROUP_W], o + GROUP_W
    nak, o = w[:, o:o + GROUP_W], o + GROUP_W
    nav, o = w[:, o:o + GROUP_W], o + GROUP_W
    rest = w[:, o:]
    w_in_p = jnp.concatenate([cq, z(GROUP_W - MLA_Q_LORA), ckv, z(HEAD_DIM), kr, z(LANES - HEAD_DIM - MLA_ROPE),
                              _pad_heads_cols(naq), _pad_heads_cols(nak), _pad_heads_cols(nav), rest],
                             axis=1).astype(BF16)

    qk_dim = HEAD_DIM + MLA_ROPE
    wuq = mla_w_uq[l].reshape(MLA_Q_LORA, N_HEADS, qk_dim)
    wuq = jnp.concatenate([wuq, jnp.zeros((MLA_Q_LORA, N_HEADS, LANES - qk_dim), F32)], axis=-1)
    wuq = jnp.concatenate([wuq.reshape(MLA_Q_LORA, PAD_W), jnp.zeros((GROUP_W - MLA_Q_LORA, PAD_W), F32)], axis=0)
    wukv = mla_w_ukv[l].reshape(MLA_KV_LORA, N_HEADS, 2 * HEAD_DIM)
    pad64 = jnp.zeros((MLA_KV_LORA, N_HEADS, HEAD_DIM), F32)
    wk = jnp.concatenate([wukv[:, :, :HEAD_DIM], pad64], axis=-1).reshape(MLA_KV_LORA, PAD_W)
    wv = jnp.concatenate([wukv[:, :, HEAD_DIM:], pad64], axis=-1).reshape(MLA_KV_LORA, PAD_W)

    def per_head(parts):
        row = jnp.concatenate(parts + [jnp.zeros((LANES - sum(p.shape[0] for p in parts),), F32)])
        return jnp.tile(row, N_HEADS)[None, :]

    prep = {
        'wuq': wuq.astype(BF16), 'wk': wk.astype(BF16), 'wv': wv.astype(BF16),
        'gcq': jnp.concatenate([mla_g_cq[l], jnp.zeros((GROUP_W - MLA_Q_LORA,), F32)])[None, :],
        'gckv': mla_g_ckv[l][None, :],
        'gkr': jnp.concatenate([jnp.zeros((HEAD_DIM,), F32), mla_g_kr[l],
                                jnp.zeros((LANES - HEAD_DIM - MLA_ROPE,), F32)])[None, :],
        'gq': per_head([mla_g_qn[l], mla_g_qr[l]]),
        'gk': per_head([mla_g_kn[l]]),
        'mq': _seg_mean_matrix(PAD_W, [(0, HEAD_DIM), (HEAD_DIM, MLA_ROPE)]),
        'mk': _seg_mean_matrix(PAD_W, [(0, HEAD_DIM)]),
        'gnq': per_head([na_g_q[l]]),
        'gnk': per_head([na_g_k[l]]),
        'mn': _seg_mean_matrix(PAD_W, [(0, HEAD_DIM)]),
    }
    wo = w_out[l]
    ow = {
        'mla': _pad_heads_rows(wo[0:GROUP_W]).astype(BF16),
        'na': _pad_heads_rows(wo[GROUP_W:2 * GROUP_W]).astype(BF16),
        'ret': wo[2 * GROUP_W:3 * GROUP_W].astype(BF16),
        'hg': wo[3 * GROUP_W:4 * GROUP_W].astype(BF16),
    }
    n_r = MOE_GROUPS + MOE_EXPERTS
    wr = jnp.concatenate([moe_w_rg[l], moe_w_re[l], jnp.zeros((d, LANES - n_r), F32)], axis=1)
    wr_hi = wr.astype(BF16)
    rw = {
        'hi': wr_hi, 'lo': (wr - wr_hi.astype(F32)).astype(BF16),
        'b': jnp.concatenate([moe_b_rg[l], moe_b_re[l], jnp.zeros((LANES - n_r,), F32)])[None, :],
    }
    return w_in_p, prep, ow, rw


def _layer(xx, modsel, lw, rope_c, rope_s, na_bias, rc, hc, hg_lb_l, ret_go, hg_go, layer, w1, w3, w2,
           n_lat, n_ctx, last):
    w_in_p, prep_w, ow, rw = lw
    b, nt, d = xx.shape
    n_lat_tiles = n_lat // TM
    p = _inproj(xx, modsel, w_in_p, n_lat_tiles)
    qm, km, vm, qn, kn, vn = _prep(p, rope_c, rope_s, prep_w)
    y_mla = _mla_attn(qm, km, vm, n_lat, n_ctx)
    y_na = _na_attn(qn, kn, vn, na_bias, n_lat, n_ctx)
    y_ret = _retention(p, rc, ret_go, n_lat, n_ctx)
    y_hg = _hgrn2(p, hc, hg_lb_l, hg_go, n_lat, n_ctx)
    xx = _outproj(xx, y_mla, y_na, y_ret, y_hg, modsel, ow, n_lat_tiles)
    route, counts = _router(xx, modsel, rw, n_lat_tiles)
    dest, block_expert, used, n_blocks = _moe_plan(route, counts, b * nt)
    x_rows = _dispatch(xx, modsel, dest, n_blocks * MOE_BLOCK, n_lat_tiles)
    y_rows = _moe_ffn(x_rows, block_expert, used, n_blocks, layer, w1, w3, w2)
    return _combine(xx, modsel, route, y_rows, dest, n_lat_tiles, n_lat_tiles if last else nt // TM)


def kernel(x, c, ctx, c_ctx, w_ada, b_ada, w_in, w_out, mla_g_cq, mla_g_ckv, mla_w_uq, mla_w_ukv, mla_g_qn, mla_g_qr, mla_g_kn, mla_g_kr, na_g_q, na_g_k, na_rpb, ret_g_out, hg_lb_raw, hg_g_out, moe_w_rg, moe_b_rg, moe_w_re, moe_b_re, moe_w1, moe_w3, moe_w2):
    b, n_lat, d = x.shape
    n_ctx = ctx.shape[1]
    depth = w_in.shape[0]
    assert n_lat % TM == 0 and n_ctx % TM == 0 and TM % GRID_W == 0
    assert n_lat // TM >= 3 and n_lat // GRID_W >= NA_WIN_ROWS
    assert w_in.shape[2] == MLA_Q_LORA + MLA_KV_LORA + MLA_ROPE + 12 * GROUP_W

    cc = jnp.concatenate([c, c_ctx[None, :], jnp.zeros((16 - b - 1, d), F32)], axis=0)
    mods = _ada_all(cc, w_ada, b_ada)
    rope_c, rope_s = _rope_tables(n_lat, n_ctx)
    rc = _retention_consts()
    hc = _hgrn_consts()
    lb_w = jax.nn.softmax(hg_lb_raw.astype(F32), axis=0)
    hg_lb = jnp.cumsum(lb_w, axis=0) - lb_w[0:1]

    xx = jnp.concatenate([x, ctx], axis=1)
    tile_go = lambda g: jnp.tile(g, N_HEADS)[None, :]
    for l in range(depth):
        modsel = jnp.stack([mods[l, :b], jnp.broadcast_to(mods[l, b], (b, 6 * d))], axis=1).reshape(2 * b, 1, 6 * d)
        lw = _layer_weights(l, w_in, w_out, mla_g_cq, mla_g_ckv, mla_w_uq, mla_w_ukv, mla_g_qn, mla_g_qr,
                            mla_g_kn, mla_g_kr, na_g_q, na_g_k, moe_w_rg, moe_b_rg, moe_w_re, moe_b_re)
        xx = _layer(xx, modsel, lw, rope_c, rope_s, _na_bias_table(na_rpb[l]), rc, hc, hg_lb[l][None, :],
                    tile_go(ret_g_out[l]), tile_go(hg_g_out[l]),
                    l, moe_w1, moe_w3, moe_w2,
                    n_lat, n_ctx, l == depth - 1)
    return xx
```

```python
import functools

import numpy as np
import jax
import jax.numpy as jnp
from jax import lax
from jax.experimental import pallas as pl
from jax.experimental.pallas import tpu as pltpu

F32 = jnp.float32
BF16 = jnp.bfloat16

EPS = 1e-6
ROPE_THETA = 10000.0
NEG_BIG = -1e30
F_FLOOR = 1e-20
GRID_W = 64
N_HEADS = 4
HEAD_DIM = 64
LANES = 128
GROUP_W = N_HEADS * HEAD_DIM
PAD_W = N_HEADS * LANES
MLA_Q_LORA = 192
MLA_KV_LORA = 128
MLA_ROPE = 32
MLA_SCALE = (HEAD_DIM + MLA_ROPE) ** -0.5
LOG2E = 1.4426950408889634
NA_KH = 8
NA_KW = 16
NA_SCALE = HEAD_DIM ** -0.5
RET_CHUNK = 128
HG_CHUNK = 16
MOE_GROUPS = 4
MOE_PER_GROUP = 8
MOE_EXPERTS = MOE_GROUPS * MOE_PER_GROUP
MOE_TOPK = 2
MOE_BLOCK = 256
TM = 256
NA_TILE_ROWS = TM // GRID_W
NA_WIN_ROWS = NA_TILE_ROWS + NA_KH
VMEM_LIMIT = 56 * 1024 * 1024
DMA_ISSUE_UNROLL = 8

P_ATTN_COLS = 2 * GROUP_W + 3 * PAD_W
COL_RET_Q, COL_RET_K, COL_RET_V, COL_RET_G = 0, 1, 2, 3
COL_HG_Q, COL_HG_FF, COL_HG_FB, COL_HG_I, COL_HG_G = 4, 5, 6, 7, 8


def _cparams(sem):
    return pltpu.CompilerParams(dimension_semantics=sem, vmem_limit_bytes=VMEM_LIMIT)


def _sigmoid(x):
    return 1.0 / (1.0 + jnp.exp(-x))


def _silu(x):
    return x * _sigmoid(x)


def _dot(a, b):
    return jnp.dot(a, b, preferred_element_type=F32)


def _dot_nt(a, b):
    return lax.dot_general(a, b, (((1,), (1,)), ((), ())), preferred_element_type=F32)


def _split_dot_l(x, m, n):
    acc = None
    rem = x
    for i in range(n):
        piece = rem.astype(BF16)
        d = _dot(piece, m)
        acc = d if acc is None else acc + d
        if i + 1 < n:
            rem = rem - piece.astype(F32)
    return acc


def _split_dot_r(m, x, n):
    acc = None
    rem = x
    for i in range(n):
        piece = rem.astype(BF16)
        d = _dot(m, piece)
        acc = d if acc is None else acc + d
        if i + 1 < n:
            rem = rem - piece.astype(F32)
    return acc


def _pack_bf16_pairs(x):
    half = x.shape[1] // 2
    bits = lax.bitcast_convert_type(x.astype(BF16).astype(F32), jnp.uint32)
    return (bits[:, :half] >> 16) | (bits[:, half:] & jnp.uint32(0xFFFF0000))


def _unpack_bf16_pairs(p):
    lo = lax.bitcast_convert_type(p << 16, F32)
    hi = lax.bitcast_convert_type(p & jnp.uint32(0xFFFF0000), F32)
    return jnp.concatenate([lo, hi], axis=1).astype(BF16)


def _seg_rms(x, m, gain):
    return x * lax.rsqrt(_split_dot_l(x * x, m, 2) + EPS) * gain


def _ada_kernel(c_ref, w_ref, b_ref, o_ref):
    s = _silu(c_ref[...])
    o_ref[0] = jnp.dot(s, w_ref[0], preferred_element_type=F32,
                       precision=lax.Precision.HIGHEST) + b_ref[0]


def _ada_all(cc, w_ada, b_ada):
    n_layers, d, d6 = w_ada.shape
    bn = 512
    rows = cc.shape[0]
    return pl.pallas_call(
        _ada_kernel,
        grid=(n_layers, d6 // bn),
        in_specs=[pl.BlockSpec((rows, d), lambda l, j: (0, 0)),
                  pl.BlockSpec((1, d, bn), lambda l, j: (l, 0, j)),
                  pl.BlockSpec((1, 1, bn), lambda l, j: (l, 0, j))],
        out_specs=pl.BlockSpec((1, rows, bn), lambda l, j: (l, 0, j)),
        out_shape=jax.ShapeDtypeStruct((n_layers, rows, d6), F32),
        compiler_params=_cparams(("arbitrary", "arbitrary")),
    )(cc, w_ada, b_ada.reshape(n_layers, 1, d6))


def _mod_spec(d6, n_lat_tiles):
    return pl.BlockSpec((1, 1, d6), lambda b, j: (2 * b + (j >= n_lat_tiles).astype(jnp.int32), 0, 0))


def _modulate(x, shift, scale):
    xn = x * lax.rsqrt(jnp.mean(x * x, axis=-1, keepdims=True) + EPS)
    return xn * (1.0 + scale) + shift


def _inproj_prep_kernel(x_ref, mod_ref, w_ref, c_ref, s_ref, wuq_ref, wk_ref, wv_ref, gcq_ref, gckv_ref, gkr_ref,
                        gq_ref, gk_ref, mq_ref, mk_ref, gnq_ref, gnk_ref, mn_ref,
                        p_ref, qm_ref, km_ref, vm_ref, qn_ref, kn_ref, vn_ref, *, d):
    xm = _modulate(x_ref[0], mod_ref[0, :, 0:d], mod_ref[0, :, d:2 * d]).astype(BF16)
    p_ref[0] = _dot(xm, w_ref[:, P_ATTN_COLS:])
    pa = _dot(xm, w_ref[:, 0:P_ATTN_COLS])

    cq = pa[:, 0:256]
    ckv = pa[:, 256:384]
    kr = pa[:, 384:512]
    cqn = cq * lax.rsqrt(jnp.sum(cq * cq, axis=-1, keepdims=True) * (1.0 / MLA_Q_LORA) + EPS) * gcq_ref[...]
    ckvn = (ckv * lax.rsqrt(jnp.mean(ckv * ckv, axis=-1, keepdims=True) + EPS) * gckv_ref[...]).astype(BF16)
    krn = kr * lax.rsqrt(jnp.sum(kr * kr, axis=-1, keepdims=True) * (1.0 / MLA_ROPE) + EPS) * gkr_ref[...]
    q = _seg_rms(_dot(cqn.astype(BF16), wuq_ref[...]), mq_ref[...], gq_ref[...])
    kk = _seg_rms(_dot(ckvn, wk_ref[...]), mk_ref[...], gk_ref[...])
    vv = _dot(ckvn, wv_ref[...])

    cos = c_ref[...]
    sin = s_ref[...]
    lane = lax.broadcasted_iota(jnp.int32, (TM, LANES), 1)
    first = (lane % 16) < 8

    def rope(x):
        partner = jnp.where(first, pltpu.roll(x, LANES - 8, 1), pltpu.roll(x, 8, 1))
        return x * cos + partner * sin

    krr = rope(krn)
    nq = _seg_rms(pa[:, 512:1024], mn_ref[...], gnq_ref[...])
    nk = _seg_rms(pa[:, 1024:1536], mn_ref[...], gnk_ref[...])
    for h in range(N_HEADS):
        sl = slice(h * LANES, (h + 1) * LANES)
        qm_ref[0, h] = (rope(q[:, sl]) * (MLA_SCALE * LOG2E)).astype(BF16)
        km_ref[0, h] = (kk[:, sl] + krr).astype(BF16)
        vm_ref[0, h] = vv[:, sl].T.astype(BF16)
        qn_ref[0, h] = (nq[:, sl] * (NA_SCALE * LOG2E)).astype(BF16)
        kn_ref[0, h] = nk[:, sl].astype(BF16)
        vn_ref[0, h] = pa[:, 1536 + h * LANES:1536 + (h + 1) * LANES].astype(BF16)


def _inproj_prep(xx, modsel, w_in_p, rope_c, rope_s, pw, n_lat_tiles):
    b, nt, d = xx.shape
    rest = w_in_p.shape[1] - P_ATTN_COLS
    full = lambda a: pl.BlockSpec(a.shape, lambda i, j: (0,) * a.ndim)
    consts = [pw['wuq'], pw['wk'], pw['wv'], pw['gcq'], pw['gckv'], pw['gkr'], pw['gq'], pw['gk'],
              pw['mq'], pw['mk'], pw['gnq'], pw['gnk'], pw['mn']]
    head_spec = pl.BlockSpec((1, N_HEADS, TM, LANES), lambda i, j: (i, 0, j, 0))
    head_shape = jax.ShapeDtypeStruct((b, N_HEADS, nt, LANES), BF16)
    head_t_spec = pl.BlockSpec((1, N_HEADS, LANES, TM), lambda i, j: (i, 0, 0, j))
    head_t_shape = jax.ShapeDtypeStruct((b, N_HEADS, LANES, nt), BF16)
    return pl.pallas_call(
        functools.partial(_inproj_prep_kernel, d=d),
        grid=(b, nt // TM),
        in_specs=[pl.BlockSpec((1, TM, d), lambda i, j: (i, j, 0)),
                  _mod_spec(6 * d, n_lat_tiles),
                  full(w_in_p),
                  pl.BlockSpec((TM, LANES), lambda i, j: (j, 0)),
                  pl.BlockSpec((TM, LANES), lambda i, j: (j, 0))] + [full(a) for a in consts],
        out_specs=[pl.BlockSpec((1, TM, rest), lambda i, j: (i, j, 0)),
                   head_spec, head_spec, head_t_spec, head_spec, head_spec, head_spec],
        out_shape=[jax.ShapeDtypeStruct((b, nt, rest), F32),
                   head_shape, head_shape, head_t_shape, head_shape, head_shape, head_shape],
        compiler_params=_cparams(("arbitrary", "arbitrary")),
    )(xx, modsel, w_in_p, rope_c, rope_s, *consts)


def _softmax2_pv(s, v):
    m = jnp.max(s, axis=-1, keepdims=True)
    e = jnp.exp2(s - m)
    l = jnp.sum(e, axis=-1, keepdims=True)
    return _dot(e.astype(BF16), v) / l


MLA_KEY_BLOCK = 2176
MLA_Q_TILE = 512


MLA_HEADS_PER_STEP = 4


def _mla_attend(q_ref, k_ref, vt_ref, o_ref, n_q, k0, k1):
    items = [(hh, s0) for hh in range(MLA_HEADS_PER_STEP) for s0 in range(k0, k1, MLA_KEY_BLOCK)]
    score = lambda hh, s0: _dot_nt(k_ref[0, hh, s0:min(s0 + MLA_KEY_BLOCK, k1), :], q_ref[0, hh, 0:n_q, :])
    st = score(*items[0])
    m = l = acc = None
    for i, (hh, s0) in enumerate(items):
        st_next = score(*items[i + 1]) if i + 1 < len(items) else None
        vt_blk = vt_ref[0, hh, :, s0:min(s0 + MLA_KEY_BLOCK, k1)]
        bm = jnp.max(st, axis=0, keepdims=True)
        if s0 == k0:
            m = bm
            e = jnp.exp2(st - m)
            l = jnp.sum(e, axis=0, keepdims=True)
            acc = _dot(vt_blk, e.astype(BF16))
        else:
            m_new = jnp.maximum(m, bm)
            alpha = jnp.exp2(m - m_new)
            e = jnp.exp2(st - m_new)
            l = l * alpha + jnp.sum(e, axis=0, keepdims=True)
            acc = acc * alpha + _dot(vt_blk, e.astype(BF16))
            m = m_new
        if s0 + MLA_KEY_BLOCK >= k1:
            o_ref[0, 0:n_q, hh * LANES:(hh + 1) * LANES] = (acc / l).T
        st = st_next


def _mla_kernel(q_ref, k_ref, vt_ref, o_ref, *, n_lat, n_ctx):
    j = pl.program_id(2)

    @pl.when(j < n_lat // MLA_Q_TILE)
    def _():
        _mla_attend(q_ref, k_ref, vt_ref, o_ref, MLA_Q_TILE, 0, n_lat + n_ctx)

    @pl.when(j >= n_lat // MLA_Q_TILE)
    def _():
        _mla_attend(q_ref, k_ref, vt_ref, o_ref, n_ctx, n_lat, n_lat + n_ctx)


def _mla_attn(qm, km, vmt, n_lat, n_ctx):
    b, h, nt, _ = qm.shape
    hp = MLA_HEADS_PER_STEP
    tq = MLA_Q_TILE
    assert n_lat % tq == 0 and n_ctx <= tq
    kv_spec = pl.BlockSpec((1, hp, nt, LANES), lambda i, hh, j: (i, hh, 0, 0))
    vt_spec = pl.BlockSpec((1, hp, LANES, nt), lambda i, hh, j: (i, hh, 0, 0))
    return pl.pallas_call(
        functools.partial(_mla_kernel, n_lat=n_lat, n_ctx=n_ctx),
        grid=(b, h // hp, n_lat // tq + 1),
        in_specs=[pl.BlockSpec((1, hp, tq, LANES), lambda i, hh, j: (i, hh, j, 0)), kv_spec, vt_spec],
        out_specs=pl.BlockSpec((1, tq, hp * LANES), lambda i, hh, j: (i, j, hh)),
        out_shape=jax.ShapeDtypeStruct((b, nt, PAD_W), F32),
        compiler_params=_cparams(("arbitrary", "arbitrary", "arbitrary")),
    )(qm, km, vmt)


def _na_kernel(q_ref, k_ref, v_ref, bias_ref, o_ref, *, n_lat, n_ctx):
    j = pl.program_id(1)
    rows = n_lat // GRID_W
    n_tiles = n_lat // TM
    win = NA_WIN_ROWS * GRID_W

    @pl.when(j < n_tiles)
    def _():
        start = jnp.clip(j * NA_TILE_ROWS - NA_KH // 2, 0, rows - NA_WIN_ROWS)
        case = jnp.where(j == 0, 0, jnp.where(j == n_tiles - 1, 2, 1))
        tok0 = pl.multiple_of(start * GRID_W, GRID_W)
        for h in range(N_HEADS):
            q = q_ref[0, h]
            s1 = _dot_nt(q, k_ref[0, h, pl.ds(tok0, win), :]) + bias_ref[case, h]
            s2 = _dot_nt(q, k_ref[0, h, pl.ds(n_lat, n_ctx), :])
            m = jnp.maximum(jnp.max(s1, axis=-1, keepdims=True), jnp.max(s2, axis=-1, keepdims=True))
            e1 = jnp.exp2(s1 - m)
            e2 = jnp.exp2(s2 - m)
            l = jnp.sum(e1, axis=-1, keepdims=True) + jnp.sum(e2, axis=-1, keepdims=True)
            o = _dot(e1.astype(BF16), v_ref[0, h, pl.ds(tok0, win), :])
            o = o + _dot(e2.astype(BF16), v_ref[0, h, pl.ds(n_lat, n_ctx), :])
            o_ref[0, :, h * LANES:(h + 1) * LANES] = o / l

    @pl.when(j >= n_lat // TM)
    def _():
        for h in range(N_HEADS):
            s = _dot_nt(q_ref[0, h], k_ref[0, h, pl.ds(n_lat, n_ctx), :])
            o_ref[0, :, h * LANES:(h + 1) * LANES] = _softmax2_pv(s, v_ref[0, h, pl.ds(n_lat, n_ctx), :])


def _na_attn(qn, kn, vn, bias, n_lat, n_ctx):
    b, h, nt, _ = qn.shape
    kv_spec = pl.BlockSpec((1, h, nt, LANES), lambda i, j: (i, 0, 0, 0))
    return pl.pallas_call(
        functools.partial(_na_kernel, n_lat=n_lat, n_ctx=n_ctx),
        grid=(b, nt // TM),
        in_specs=[pl.BlockSpec((1, h, TM, LANES), lambda i, j: (i, 0, j, 0)), kv_spec, kv_spec,
                  pl.BlockSpec(bias.shape, lambda i, j: (0, 0, 0, 0))],
        out_specs=pl.BlockSpec((1, TM, PAD_W), lambda i, j: (i, j, 0)),
        out_shape=jax.ShapeDtypeStruct((b, nt, PAD_W), F32),
        compiler_params=_cparams(("arbitrary", "arbitrary")),
    )(qn, kn, vn, bias)


def _head_mask(h, shape):
    return (lax.broadcasted_iota(jnp.int32, shape, 1) // HEAD_DIM) == h


def _ret_state_step(s_ref, q, k, v, qw, kw, cd, bd):
    state = s_ref[...]
    o = _dot((q * qw).astype(BF16), state.astype(BF16))
    upd = _dot((k * kw).T.astype(BF16), v.astype(BF16))
    s_ref[...] = state * cd + upd * bd
    return o


def _ret_fwd_kernel(q_ref, k_ref, v_ref, dm_ref, qw_ref, kw_ref, cd_ref, bd_ref, o_ref, s_ref):
    @pl.when(pl.program_id(1) == 0)
    def _():
        s_ref[...] = jnp.zeros_like(s_ref)

    for c in range(TM // RET_CHUNK):
        rows = slice(c * RET_CHUNK, (c + 1) * RET_CHUNK)
        q = q_ref[0, rows]
        k = k_ref[0, rows]
        v = v_ref[0, rows]
        o = _ret_state_step(s_ref, q, k, v, qw_ref[...], kw_ref[...], cd_ref[...], bd_ref[...])
        kb = k.astype(BF16)
        for h in range(N_HEADS):
            hm = _head_mask(h, q.shape)
            sc = _dot_nt(jnp.where(hm, q, 0.0).astype(BF16), kb) * dm_ref[h]
            o = o + _dot(sc.astype(BF16), jnp.where(hm, v, 0.0).astype(BF16))
        o_ref[0, rows] = o


def _ret_bwd_kernel(q_ref, k_ref, v_ref, g_ref, op_ref, qw_ref, kw_ref, cd_ref, bd_ref, ms_ref, go_ref,
                    y_ref, s_ref):
    @pl.when(pl.program_id(1) == 0)
    def _():
        s_ref[...] = jnp.zeros_like(s_ref)

    for c in reversed(range(TM // RET_CHUNK)):
        rows = slice(c * RET_CHUNK, (c + 1) * RET_CHUNK)
        o = op_ref[0, rows] + _ret_state_step(s_ref, q_ref[0, rows], k_ref[0, rows], v_ref[0, rows], qw_ref[...],
                                              kw_ref[...], cd_ref[...], bd_ref[...])
        y_ref[0, rows] = _seg_rms(o, ms_ref[...], go_ref[...]) * _silu(g_ref[0, rows])


def _scan_order(n_lat_t, n_ctx_t, reverse):
    if reverse:
        return lambda i: jnp.where(i < n_ctx_t, n_lat_t + n_ctx_t - 1 - i, n_lat_t + n_ctx_t - 1 - i)
    return lambda i: jnp.where(i < n_ctx_t, n_lat_t + i, i - n_ctx_t)


def _retention(p, rc, go, n_lat, n_ctx):
    b, nt, _ = p.shape
    c = TM
    n_lat_t, n_ctx_t = n_lat // c, n_ctx // c
    fwd = _scan_order(n_lat_t, n_ctx_t, False)
    bwd = _scan_order(n_lat_t, n_ctx_t, True)
    col = lambda order, cb: pl.BlockSpec((1, c, GROUP_W), lambda i, j: (i, order(j), cb))
    full = lambda a: pl.BlockSpec(a.shape, lambda i, j: (0,) * a.ndim)
    out_shape = jax.ShapeDtypeStruct((b, nt, GROUP_W), F32)
    scratch = [pltpu.VMEM((GROUP_W, GROUP_W), F32)]
    consts_f = [rc['dm'], rc['qw_f'], rc['kw_f'], rc['cd_f'], rc['bd']]
    o_part = pl.pallas_call(
        _ret_fwd_kernel,
        grid=(b, nt // c),
        in_specs=[col(fwd, COL_RET_Q), col(fwd, COL_RET_K), col(fwd, COL_RET_V)] + [full(a) for a in consts_f],
        out_specs=pl.BlockSpec((1, c, GROUP_W), lambda i, j: (i, fwd(j), 0)),
        out_shape=out_shape,
        scratch_shapes=scratch,
        compiler_params=_cparams(("arbitrary", "arbitrary")),
    )(p, p, p, *consts_f)
    consts_b = [rc['qw_b'], rc['kw_b'], rc['cd_b'], rc['bd'], rc['ms'], go]
    return pl.pallas_call(
        _ret_bwd_kernel,
        grid=(b, nt // c),
        in_specs=[col(bwd, COL_RET_Q), col(bwd, COL_RET_K), col(bwd, COL_RET_V), col(bwd, COL_RET_G),
                  pl.BlockSpec((1, c, GROUP_W), lambda i, j: (i, bwd(j), 0))] + [full(a) for a in consts_b],
        out_specs=pl.BlockSpec((1, c, GROUP_W), lambda i, j: (i, bwd(j), 0)),
        out_shape=out_shape,
        scratch_shapes=scratch,
        compiler_params=_cparams(("arbitrary", "arbitrary")),
    )(p, p, p, p, o_part, *consts_b)


def _hg_direction(q_ref, f_ref, v_ref, lb_ref, ain_ref, aex_ref, bseg_ref, bd_ref, st_ref, sh_ref, *, reverse):
    n_chunks = TM // HG_CHUNK
    assert n_chunks == HG_CHUNK
    qh = _silu(q_ref[0])
    lb = lb_ref[...]
    f = jnp.maximum(lb + (1.0 - lb) * _sigmoid(f_ref[0]), F_FLOOR)
    lf = jnp.log(f) * LOG2E
    k = 1.0 - f
    v = v_ref[0]
    row = lax.broadcasted_iota(jnp.int32, (TM, 1), 0)
    pos = row % HG_CHUNK
    row_chunk = row // HG_CHUNK

    a_in = _split_dot_r(ain_ref[...], lf, 3)
    a_ex = _split_dot_r(aex_ref[...], lf, 3)
    width = v.shape[1]
    key_exp = a_in - jnp.log(k) * LOG2E
    for slot, val in enumerate((key_exp, v)):
        sh_ref[slot] = val.reshape(n_chunks, HG_CHUNK, width)

    def key_row(slot, s):
        return jnp.broadcast_to(sh_ref[slot, :, s:s + 1, :], (n_chunks, HG_CHUNK, width)).reshape(TM, width)
    qp = (qh * jnp.exp2(a_in)).astype(BF16)
    kdec = k * jnp.exp2(a_ex)
    lam_all = jnp.exp2(a_in + a_ex)
    vt = v.T.astype(BF16)
    bd = bd_ref[...]
    bseg = bseg_ref[...]
    state = st_ref[...]
    parts = [None] * n_chunks
    o_band = jnp.zeros_like(v)
    upds = [_dot(vt, jnp.where(row_chunk == c, kdec, 0.0).astype(BF16)) * bd for c in range(n_chunks)]
    for step in range(n_chunks):
        c = n_chunks - 1 - step if reverse else step
        r0 = c * HG_CHUNK
        parts[c] = _dot_nt(qp[r0:r0 + HG_CHUNK], state.astype(BF16))
        state = state * lam_all[r0:r0 + 1] + upds[c]
        s = step
        valid = (pos <= s) if reverse else (pos >= s)
        w = jnp.where(valid, qh * jnp.exp2(a_in - key_row(0, s)), 0.0)
        o_band = o_band + _dot(w.astype(BF16), bseg) * key_row(1, s)
    st_ref[...] = state
    return jnp.concatenate(parts, axis=0) + o_band


def _hg_fwd_kernel(q_ref, f_ref, v_ref, lb_ref, ain_ref, aex_ref, bseg_ref, bd_ref, o_ref, st_ref, sh_ref):
    @pl.when(pl.program_id(1) == 0)
    def _():
        st_ref[...] = jnp.zeros_like(st_ref)

    o_ref[0] = _hg_direction(q_ref, f_ref, v_ref, lb_ref, ain_ref, aex_ref, bseg_ref, bd_ref, st_ref, sh_ref,
                             reverse=False)


def _hg_bwd_kernel(q_ref, f_ref, v_ref, g_ref, op_ref, lb_ref, ain_ref, aex_ref, bseg_ref, bd_ref,
                   ms_ref, go_ref, y_ref, st_ref, sh_ref):
    @pl.when(pl.program_id(1) == 0)
    def _():
        st_ref[...] = jnp.zeros_like(st_ref)

    o = op_ref[0] + _hg_direction(q_ref, f_ref, v_ref, lb_ref, ain_ref, aex_ref, bseg_ref, bd_ref, st_ref, sh_ref,
                                  reverse=True)
    y_ref[0] = _seg_rms(o, ms_ref[...], go_ref[...]) * _silu(g_ref[0])


def _hgrn2(p, hc, lb, go, n_lat, n_ctx):
    b, nt, _ = p.shape
    n_lat_t, n_ctx_t = n_lat // TM, n_ctx // TM
    fwd = _scan_order(n_lat_t, n_ctx_t, False)
    bwd = _scan_order(n_lat_t, n_ctx_t, True)
    col = lambda order, cb: pl.BlockSpec((1, TM, GROUP_W), lambda i, j: (i, order(j), cb))
    full = lambda a: pl.BlockSpec(a.shape, lambda i, j: (0,) * a.ndim)
    out_shape = jax.ShapeDtypeStruct((b, nt, GROUP_W), F32)
    scratch = [pltpu.VMEM((GROUP_W, GROUP_W), F32), pltpu.VMEM((2, TM // HG_CHUNK, HG_CHUNK, GROUP_W), F32)]
    consts_f = [lb, hc['lincl'], hc['uexcl'], hc['bseg'], hc['bd']]
    o_part = pl.pallas_call(
        _hg_fwd_kernel,
        grid=(b, nt // TM),
        in_specs=[col(fwd, COL_HG_Q), col(fwd, COL_HG_FF), col(fwd, COL_HG_I)] + [full(a) for a in consts_f],
        out_specs=pl.BlockSpec((1, TM, GROUP_W), lambda i, j: (i, fwd(j), 0)),
        out_shape=out_shape,
        scratch_shapes=scratch,
        compiler_params=_cparams(("arbitrary", "arbitrary")),
    )(p, p, p, *consts_f)
    consts_b = [lb, hc['uincl'], hc['lexcl'], hc['bseg'], hc['bd'], hc['ms'], go]
    return pl.pallas_call(
        _hg_bwd_kernel,
        grid=(b, nt // TM),
        in_specs=[col(bwd, COL_HG_Q), col(bwd, COL_HG_FB), col(bwd, COL_HG_I), col(bwd, COL_HG_G),
                  pl.BlockSpec((1, TM, GROUP_W), lambda i, j: (i, bwd(j), 0))] + [full(a) for a in consts_b],
        out_specs=pl.BlockSpec((1, TM, GROUP_W), lambda i, j: (i, bwd(j), 0)),
        out_shape=out_shape,
        scratch_shapes=scratch,
        compiler_params=_cparams(("arbitrary", "arbitrary")),
    )(p, p, p, p, o_part, *consts_b)


def _outproj_router_kernel(x_ref, ym_ref, yn_ref, yr_ref, yh_ref, mod_ref, wm_ref, wn_ref, wr_ref, wh_ref,
                           whi_ref, wlo_ref, br_ref, ltri_ref, o_ref, r_ref, cnt_ref, *, d):
    acc = _dot(ym_ref[0].astype(BF16), wm_ref[...])
    acc = acc + _dot(yn_ref[0].astype(BF16), wn_ref[...])
    acc = acc + _dot(yr_ref[0].astype(BF16), wr_ref[...])
    acc = acc + _dot(yh_ref[0].astype(BF16), wh_ref[...])
    x_new = x_ref[0] + mod_ref[0, :, 2 * d:3 * d] * acc
    o_ref[0] = x_new
    h = _modulate(x_new, mod_ref[0, :, 3 * d:4 * d], mod_ref[0, :, 4 * d:5 * d])
    _route(h, whi_ref, wlo_ref, br_ref, ltri_ref, r_ref, cnt_ref)


def _outproj_router(xx, y_mla, y_na, y_ret, y_hg, modsel, ow, rw, n_lat_tiles):
    b, nt, d = xx.shape
    tiles = nt // TM
    tile = lambda w: pl.BlockSpec((1, TM, w), lambda i, j: (i, j, 0))
    full = lambda a: pl.BlockSpec(a.shape, lambda i, j: (0, 0))
    ltri = jnp.asarray(np.tril(np.ones((TM, TM), np.float32), -1), BF16)
    ws = [ow['mla'], ow['na'], ow['ret'], ow['hg'], rw['hi'], rw['lo'], rw['b'], ltri]
    return pl.pallas_call(
        functools.partial(_outproj_router_kernel, d=d),
        grid=(b, tiles),
        in_specs=[tile(d), tile(PAD_W), tile(PAD_W), tile(GROUP_W), tile(GROUP_W),
                  _mod_spec(6 * d, n_lat_tiles)] + [full(a) for a in ws],
        out_specs=[tile(d), pl.BlockSpec((TM, LANES), lambda i, j: (i * tiles + j, 0)),
                   pl.BlockSpec((1, LANES), lambda i, j: (0, 0))],
        out_shape=[jax.ShapeDtypeStruct((b, nt, d), F32), jax.ShapeDtypeStruct((b * nt, LANES), F32),
                   jax.ShapeDtypeStruct((1, LANES), F32)],
        compiler_params=_cparams(("arbitrary", "arbitrary")),
    )(xx, y_mla, y_na, y_ret, y_hg, modsel, *ws)


def _route(h, whi_ref, wlo_ref, br_ref, ltri_ref, r_ref, cnt_ref):
    @pl.when((pl.program_id(0) == 0) & (pl.program_id(1) == 0))
    def _():
        cnt_ref[...] = jnp.zeros_like(cnt_ref)

    h_hi = h.astype(BF16)
    h_lo = (h - h_hi.astype(F32)).astype(BF16)
    lg = _dot(h_hi, whi_ref[...]) + _dot(h_lo, whi_ref[...]) + _dot(h_hi, wlo_ref[...]) + br_ref[...]

    lane = lax.broadcasted_iota(jnp.int32, lg.shape, 1).astype(F32)
    far = 1e9

    def first_argmax(vals, vmax):
        return jnp.min(jnp.where(vals == vmax, lane, far), axis=-1, keepdims=True)

    gl = jnp.where(lane < MOE_GROUPS, lg, NEG_BIG)
    gmax = jnp.max(gl, axis=-1, keepdims=True)
    pg_top = 1.0 / jnp.sum(jnp.exp(gl - gmax), axis=-1, keepdims=True)
    lo = MOE_GROUPS + MOE_PER_GROUP * first_argmax(gl, gmax)
    fl = jnp.where((lane >= lo) & (lane < lo + MOE_PER_GROUP), lg, NEG_BIG)
    fmax = jnp.max(fl, axis=-1, keepdims=True)
    fsum = jnp.sum(jnp.exp(fl - fmax), axis=-1, keepdims=True)
    i1 = first_argmax(fl, fmax)
    fl2 = jnp.where(lane == i1, NEG_BIG, fl)
    f2max = jnp.max(fl2, axis=-1, keepdims=True)
    i2 = first_argmax(fl2, f2max)
    p1 = 1.0 / fsum
    p2 = jnp.exp(f2max - fmax) / fsum
    g1 = pg_top * p1 / (p1 + p2)
    g2 = pg_top * p2 / (p1 + p2)
    e1 = i1 - MOE_GROUPS
    e2 = i2 - MOE_GROUPS

    onehot = jnp.where(lane == e1, 1.0, 0.0) + jnp.where(lane == e2, 1.0, 0.0)
    before = cnt_ref[...] + _dot(ltri_ref[...], onehot.astype(BF16))
    r1 = jnp.sum(jnp.where(lane == e1, before, 0.0), axis=-1, keepdims=True)
    r2 = jnp.sum(jnp.where(lane == e2, before, 0.0), axis=-1, keepdims=True)
    cnt_ref[...] += jnp.sum(onehot, axis=0, keepdims=True)

    out = jnp.zeros_like(lg)
    for col, val in enumerate((e1, e2, g1, g2, r1, r2)):
        out = jnp.where(lane == col, val, out)
    r_ref[...] = out


ROUTE_E, ROUTE_G, ROUTE_R = 0, 2, 4


def _moe_plan(route, counts_f, n_tok):
    counts = counts_f[0, :MOE_EXPERTS].astype(jnp.int32)
    padded = (counts + MOE_BLOCK - 1) // MOE_BLOCK * MOE_BLOCK
    pad_end = jnp.cumsum(padded)
    pad_start = pad_end - padded
    n_blocks = -(-(n_tok * MOE_TOPK) // MOE_BLOCK) + MOE_EXPERTS
    blk0 = jnp.arange(n_blocks, dtype=jnp.int32) * MOE_BLOCK
    block_expert = jnp.minimum(jnp.sum((pad_end[None, :] <= blk0[:, None]).astype(jnp.int32), axis=1),
                               MOE_EXPERTS - 1)
    used = (pad_end[-1] // MOE_BLOCK).reshape(1)
    expert = route[:, ROUTE_E:ROUTE_E + MOE_TOPK].astype(jnp.int32)
    rank = route[:, ROUTE_R:ROUTE_R + MOE_TOPK].astype(jnp.int32)
    start_of = jnp.sum(jnp.where(expert[..., None] == jnp.arange(MOE_EXPERTS, dtype=jnp.int32), pad_start, 0), axis=-1)
    return start_of + rank, block_expert, used, n_blocks


def _idx_blocks(dest, k, n_tiles):
    return dest[:, k].reshape(n_tiles, 1, TM)


def _dispatch_kernel(d0_ref, d1_ref, x_ref, mod_ref, rows_in, rows_out, h_ref, sem, *, d):
    del rows_in
    step = pl.program_id(0) * pl.num_programs(1) + pl.program_id(1)
    n_steps = pl.num_programs(0) * pl.num_programs(1)
    slot = step % 2

    def wait_slot(s):
        for _ in range(MOE_TOPK):
            pltpu.make_async_copy(h_ref.at[s], rows_out.at[pl.ds(0, TM)], sem.at[s]).wait()

    h = _modulate(x_ref[0], mod_ref[0, :, 3 * d:4 * d], mod_ref[0, :, 4 * d:5 * d])
    h_ref[slot] = _pack_bf16_pairs(h)

    def issue(r, carry):
        src = h_ref.at[slot, pl.ds(r, 1)]
        pltpu.make_async_copy(src, rows_out.at[pl.ds(d0_ref[0, 0, r], 1)], sem.at[slot]).start()
        pltpu.make_async_copy(src, rows_out.at[pl.ds(d1_ref[0, 0, r], 1)], sem.at[slot]).start()
        return carry

    lax.fori_loop(0, TM, issue, 0, unroll=DMA_ISSUE_UNROLL)

    @pl.when(step > 0)
    def _():
        wait_slot(1 - slot)

    @pl.when(step == n_steps - 1)
    def _():
        wait_slot(slot)


def _dispatch(xx, modsel, dest, n_rows, n_lat_tiles):
    b, nt, d = xx.shape
    tiles = nt // TM
    idx_spec = pl.BlockSpec((1, 1, TM), lambda i, j: (i * tiles + j, 0, 0), memory_space=pltpu.SMEM)
    return pl.pallas_call(
        functools.partial(_dispatch_kernel, d=d),
        grid=(b, tiles),
        in_specs=[idx_spec, idx_spec, pl.BlockSpec((1, TM, d), lambda i, j: (i, j, 0)),
                  _mod_spec(6 * d, n_lat_tiles), pl.BlockSpec(memory_space=pl.ANY)],
        out_specs=pl.BlockSpec(memory_space=pl.ANY),
        out_shape=jax.ShapeDtypeStruct((n_rows, d // 2), jnp.uint32),
        scratch_shapes=[pltpu.VMEM((2, TM, d // 2), jnp.uint32), pltpu.SemaphoreType.DMA((2,))],
        input_output_aliases={4: 0},
        compiler_params=_cparams(("arbitrary", "arbitrary")),
    )(_idx_blocks(dest, 0, b * tiles), _idx_blocks(dest, 1, b * tiles), xx, modsel,
      jnp.zeros((n_rows, d // 2), jnp.uint32))


def _ffn_kernel(be_ref, used_ref, x_ref, w1_ref, w3_ref, w2_ref, y_ref, w1b_ref, w3b_ref, w2b_ref):
    i = pl.program_id(0)

    @pl.when((i == 0) | (be_ref[i] != be_ref[jnp.maximum(i - 1, 0)]))
    def _():
        w1b_ref[...] = w1_ref[0, 0].astype(BF16)
        w3b_ref[...] = w3_ref[0, 0].astype(BF16)
        w2b_ref[...] = w2_ref[0, 0].astype(BF16)

    @pl.when(i < used_ref[0])
    def _():
        x = _unpack_bf16_pairs(x_ref[...])
        mid = _silu(_dot(x, w1b_ref[...])) * _dot(x, w3b_ref[...])
        y_ref[...] = _dot(mid.astype(BF16), w2b_ref[...])

    @pl.when(i >= used_ref[0])
    def _():
        y_ref[...] = jnp.zeros_like(y_ref)


def _moe_ffn(x_rows, block_expert, used, n_blocks, layer, w1, w3, w2):
    d = w1.shape[2]
    ff = w1.shape[3]
    grid_spec = pltpu.PrefetchScalarGridSpec(
        num_scalar_prefetch=2,
        grid=(n_blocks,),
        in_specs=[pl.BlockSpec((MOE_BLOCK, d // 2), lambda i, be, nu: (i, 0)),
                  pl.BlockSpec((1, 1, d, ff), lambda i, be, nu: (layer, be[i], 0, 0)),
                  pl.BlockSpec((1, 1, d, ff), lambda i, be, nu: (layer, be[i], 0, 0)),
                  pl.BlockSpec((1, 1, ff, d), lambda i, be, nu: (layer, be[i], 0, 0))],
        out_specs=pl.BlockSpec((MOE_BLOCK, d), lambda i, be, nu: (i, 0)),
        scratch_shapes=[pltpu.VMEM((d, ff), BF16), pltpu.VMEM((d, ff), BF16), pltpu.VMEM((ff, d), BF16)],
    )
    return pl.pallas_call(
        _ffn_kernel,
        grid_spec=grid_spec,
        out_shape=jax.ShapeDtypeStruct((x_rows.shape[0], d), F32),
        compiler_params=_cparams(("arbitrary",)),
    )(block_expert, used, x_rows, w1, w3, w2)


def _row_gather(src_hbm, idx_ref, dst_ref, sem, n):
    def issue(r, carry):
        pltpu.make_async_copy(src_hbm.at[pl.ds(idx_ref[0, 0, r], 1)], dst_ref.at[pl.ds(r, 1)], sem).start()
        return carry

    lax.fori_loop(0, n, issue, 0, unroll=DMA_ISSUE_UNROLL)


def _row_gather_wait(src_hbm, dst_ref, sem, n):
    pltpu.make_async_copy(src_hbm.at[pl.ds(0, n)], dst_ref, sem).wait()


def _combine_kernel(d0_ref, d1_ref, n0_ref, n1_ref, x_ref, mod_ref, r_ref, y_hbm, o_ref, y_ref, sem, *, d):
    step = pl.program_id(0) * pl.num_programs(1) + pl.program_id(1)
    n_steps = pl.num_programs(0) * pl.num_programs(1)
    slot = step % 2

    def gather(idx_refs, s):
        for k, idx_ref in enumerate(idx_refs):
            _row_gather(y_hbm, idx_ref, y_ref.at[s, k], sem.at[s, k], TM)

    @pl.when(step == 0)
    def _():
        gather((d0_ref, d1_ref), slot)

    @pl.when(step + 1 < n_steps)
    def _():
        gather((n0_ref, n1_ref), 1 - slot)

    route = r_ref[...]
    lane = lax.broadcasted_iota(jnp.int32, route.shape, 1)
    g0 = jnp.sum(jnp.where(lane == ROUTE_G, route, 0.0), axis=-1, keepdims=True)
    g1 = jnp.sum(jnp.where(lane == ROUTE_G + 1, route, 0.0), axis=-1, keepdims=True)
    for k in range(MOE_TOPK):
        _row_gather_wait(y_hbm, y_ref.at[slot, k], sem.at[slot, k], TM)
    o_ref[0] = x_ref[0] + mod_ref[0, :, 5 * d:6 * d] * (y_ref[slot, 0] * g0 + y_ref[slot, 1] * g1)


def _combine(xx, modsel, route, y_rows, dest, n_lat_tiles, out_tiles):
    b, nt, d = xx.shape
    tiles = nt // TM
    idx_spec = pl.BlockSpec((1, 1, TM), lambda i, j: (i * tiles + j, 0, 0), memory_space=pltpu.SMEM)

    def next_block(i, j):
        wrap = j + 1 >= out_tiles
        return (jnp.where(wrap, jnp.minimum(i + 1, b - 1) * tiles, i * tiles + j + 1), 0, 0)

    next_spec = pl.BlockSpec((1, 1, TM), next_block, memory_space=pltpu.SMEM)
    d0, d1 = _idx_blocks(dest, 0, b * tiles), _idx_blocks(dest, 1, b * tiles)
    return pl.pallas_call(
        functools.partial(_combine_kernel, d=d),
        grid=(b, out_tiles),
        in_specs=[idx_spec, idx_spec, next_spec, next_spec, pl.BlockSpec((1, TM, d), lambda i, j: (i, j, 0)),
                  _mod_spec(6 * d, n_lat_tiles),
                  pl.BlockSpec((TM, LANES), lambda i, j: (i * tiles + j, 0)),
                  pl.BlockSpec(memory_space=pl.ANY)],
        out_specs=pl.BlockSpec((1, TM, d), lambda i, j: (i, j, 0)),
        out_shape=jax.ShapeDtypeStruct((b, out_tiles * TM, d), F32),
        scratch_shapes=[pltpu.VMEM((2, MOE_TOPK, TM, d), F32), pltpu.SemaphoreType.DMA((2, MOE_TOPK))],
        compiler_params=_cparams(("arbitrary", "arbitrary")),
    )(d0, d1, d0, d1, xx, modsel, route, y_rows)


def _pad_heads_cols(w):
    lead = w.shape[:-1]
    w = w.reshape(*lead, N_HEADS, HEAD_DIM)
    w = jnp.concatenate([w, jnp.zeros_like(w)], axis=-1)
    return w.reshape(*lead, PAD_W)


def _pad_heads_rows(w):
    return _pad_heads_cols(w.T).T


def _seg_mean_matrix(width, segments):
    m = np.zeros((width, width), np.float32)
    for g in range(width // LANES):
        for start, length in segments:
            a = g * LANES + start
            m[a:a + length, a:a + length] = 1.0 / length
    return jnp.asarray(m, BF16)


def _rope_tables(n_lat, n_ctx):
    pos = jnp.arange(n_lat)
    rows = (pos // GRID_W).astype(F32)
    cols = (pos % GRID_W).astype(F32)
    per_axis = MLA_ROPE // 2
    inv_freq = ROPE_THETA ** (-jnp.arange(0, per_axis, 2, dtype=F32) / per_axis)
    ang = jnp.concatenate([rows[:, None] * inv_freq, cols[:, None] * inv_freq], axis=-1)
    i = np.arange(MLA_ROPE)
    src = (i // 16) * 8 + (i % 8)
    sign = np.where((i % 16) < 8, -1.0, 1.0).astype(np.float32)
    cos = jnp.ones((n_lat, LANES), F32).at[:, HEAD_DIM:HEAD_DIM + MLA_ROPE].set(jnp.cos(ang)[:, src])
    sin = jnp.zeros((n_lat, LANES), F32).at[:, HEAD_DIM:HEAD_DIM + MLA_ROPE].set(jnp.sin(ang)[:, src] * sign)
    cos = jnp.concatenate([cos, jnp.ones((n_ctx, LANES), F32)], axis=0)
    sin = jnp.concatenate([sin, jnp.zeros((n_ctx, LANES), F32)], axis=0)
    return cos, sin


def _na_bias_table(rpb):
    w = np.arange(GRID_W)
    col_start = np.clip(w - NA_KW // 2, 0, GRID_W - NA_KW)
    valid = (w[None, :] >= col_start[:, None]) & (w[None, :] < col_start[:, None] + NA_KW)
    dc = np.clip(w[None, :] - w[:, None], 1 - NA_KW, NA_KW - 1) + (NA_KW - 1)
    onehot = jnp.asarray(dc[None, :, :] == np.arange(2 * NA_KW - 1)[:, None, None], F32)
    t = jnp.einsum('hrd,dqk->hrqk', rpb.astype(F32), onehot, precision=lax.Precision.HIGHEST)
    t = jnp.where(jnp.asarray(valid)[None, None, :, :], t * LOG2E, NEG_BIG)
    masked = jnp.full((N_HEADS, GRID_W, GRID_W), NEG_BIG, F32)
    q_off = (0, NA_KH // 2, NA_KH)
    first = ([0] * NA_TILE_ROWS, list(range(NA_TILE_ROWS)), [NA_KH // 2] * NA_TILE_ROWS)
    cases = []
    for c in range(3):
        row_blocks = []
        for rr in range(NA_TILE_ROWS):
            blocks = []
            for jj in range(NA_WIN_ROWS):
                live = first[c][rr] <= jj < first[c][rr] + NA_KH
                dr = jj - (q_off[c] + rr) + (NA_KH - 1)
                blocks.append(t[:, dr] if live else masked)
            row_blocks.append(jnp.concatenate(blocks, axis=-1))
        cases.append(jnp.concatenate(row_blocks, axis=-2))
    return jnp.stack(cases, axis=0)


def _block_diag_mask(block):
    i = np.arange(GROUP_W) // block
    return (i[:, None] == i[None, :]).astype(np.float32)


def _retention_consts():
    c = RET_CHUNK
    j = np.arange(2 * N_HEADS, dtype=np.float64)
    lg = np.log1p(-np.exp2(-5.0 - j))
    lg_f, lg_b = lg[0::2], lg[1::2]
    pos = np.arange(c, dtype=np.float64)
    diff = pos[:, None] - pos[None, :]
    k_scale = HEAD_DIM ** -0.5
    dm = np.zeros((N_HEADS, c, c))
    for h in range(N_HEADS):
        dm[h] = (np.where(diff >= 0, np.exp(np.maximum(diff, 0.0) * lg_f[h]), 0.0)
                 + np.where(diff <= 0, np.exp(np.maximum(-diff, 0.0) * lg_b[h]), 0.0)) * k_scale
    lanes = lambda per_head: np.repeat(per_head, HEAD_DIM, axis=-1)
    out = {
        'dm': dm,
        'qw_f': lanes(np.exp((pos + 1)[:, None] * lg_f[None, :])),
        'kw_f': lanes(np.exp((c - 1 - pos)[:, None] * lg_f[None, :])) * k_scale,
        'cd_f': lanes(np.exp(c * lg_f)[None, :]),
        'qw_b': lanes(np.exp((c - pos)[:, None] * lg_b[None, :])),
        'kw_b': lanes(np.exp(pos[:, None] * lg_b[None, :])) * k_scale,
        'cd_b': lanes(np.exp(c * lg_b)[None, :]),
        'bd': _block_diag_mask(HEAD_DIM),
    }
    out = {k: jnp.asarray(v, F32) for k, v in out.items()}
    out['ms'] = jnp.asarray(_block_diag_mask(HEAD_DIM) / HEAD_DIM, BF16)
    return out


def _hgrn_consts():
    t = np.arange(TM)
    same = (t[:, None] // HG_CHUNK) == (t[None, :] // HG_CHUNK)
    lincl = same & (t[None, :] <= t[:, None])
    lexcl = same & (t[None, :] < t[:, None])
    return {
        'lincl': jnp.asarray(lincl, BF16), 'lexcl': jnp.asarray(lexcl, BF16),
        'uincl': jnp.asarray(lincl.T, BF16), 'uexcl': jnp.asarray(lexcl.T, BF16),
        'bseg': jnp.asarray(_block_diag_mask(HEAD_DIM), BF16),
        'bd': jnp.asarray(_block_diag_mask(HEAD_DIM), F32),
        'ms': jnp.asarray(_block_diag_mask(HEAD_DIM) / HEAD_DIM, BF16),
    }


def _layer_weights(l, w_in, w_out, mla_g_cq, mla_g_ckv, mla_w_uq, mla_w_ukv, mla_g_qn, mla_g_qr, mla_g_kn,
                   mla_g_kr, na_g_q, na_g_k, moe_w_rg, moe_b_rg, moe_w_re, moe_b_re):
    d = w_in.shape[1]
    w = w_in[l]
    z = lambda n: jnp.zeros((d, n), F32)
    o = 0
    cq, o = w[:, o:o + MLA_Q_LORA], o + MLA_Q_LORA
    ckv, o = w[:, o:o + MLA_KV_LORA], o + MLA_KV_LORA
    kr, o = w[:, o:o + MLA_ROPE], o + MLA_ROPE
    naq, o = w[:, o:o + GROUP_W], o + GROUP_W
    nak, o = w[:, o:o + GROUP_W], o + GROUP_W
    nav, o = w[:, o:o + GROUP_W], o + GROUP_W
    rest = w[:, o:]
    w_in_p = jnp.concatenate([cq, z(GROUP_W - MLA_Q_LORA), ckv, z(HEAD_DIM), kr, z(LANES - HEAD_DIM - MLA_ROPE),
                              _pad_heads_cols(naq), _pad_heads_cols(nak), _pad_heads_cols(nav), rest],
                             axis=1).astype(BF16)

    qk_dim = HEAD_DIM + MLA_ROPE
    wuq = mla_w_uq[l].reshape(MLA_Q_LORA, N_HEADS, qk_dim)
    wuq = jnp.concatenate([wuq, jnp.zeros((MLA_Q_LORA, N_HEADS, LANES - qk_dim), F32)], axis=-1)
    wuq = jnp.concatenate([wuq.reshape(MLA_Q_LORA, PAD_W), jnp.zeros((GROUP_W - MLA_Q_LORA, PAD_W), F32)], axis=0)
    wukv = mla_w_ukv[l].reshape(MLA_KV_LORA, N_HEADS, 2 * HEAD_DIM)
    pad64 = jnp.zeros((MLA_KV_LORA, N_HEADS, HEAD_DIM), F32)
    wk = jnp.concatenate([wukv[:, :, :HEAD_DIM], pad64], axis=-1).reshape(MLA_KV_LORA, PAD_W)
    wv = jnp.concatenate([wukv[:, :, HEAD_DIM:], pad64], axis=-1).reshape(MLA_KV_LORA, PAD_W)

    def per_head(parts):
        row = jnp.concatenate(parts + [jnp.zeros((LANES - sum(p.shape[0] for p in parts),), F32)])
        return jnp.tile(row, N_HEADS)[None, :]

    prep = {
        'wuq': wuq.astype(BF16), 'wk': wk.astype(BF16), 'wv': wv.astype(BF16),
        'gcq': jnp.concatenate([mla_g_cq[l], jnp.zeros((GROUP_W - MLA_Q_LORA,), F32)])[None, :],
        'gckv': mla_g_ckv[l][None, :],
        'gkr': jnp.concatenate([jnp.zeros((HEAD_DIM,), F32), mla_g_kr[l],
                                jnp.zeros((LANES - HEAD_DIM - MLA_ROPE,), F32)])[None, :],
        'gq': per_head([mla_g_qn[l], mla_g_qr[l]]),
        'gk': per_head([mla_g_kn[l]]),
        'mq': _seg_mean_matrix(PAD_W, [(0, HEAD_DIM), (HEAD_DIM, MLA_ROPE)]),
        'mk': _seg_mean_matrix(PAD_W, [(0, HEAD_DIM)]),
        'gnq': per_head([na_g_q[l]]),
        'gnk': per_head([na_g_k[l]]),
        'mn': _seg_mean_matrix(PAD_W, [(0, HEAD_DIM)]),
    }
    wo = w_out[l]
    ow = {
        'mla': _pad_heads_rows(wo[0:GROUP_W]).astype(BF16),
        'na': _pad_heads_rows(wo[GROUP_W:2 * GROUP_W]).astype(BF16),
        'ret': wo[2 * GROUP_W:3 * GROUP_W].astype(BF16),
        'hg': wo[3 * GROUP_W:4 * GROUP_W].astype(BF16),
    }
    n_r = MOE_GROUPS + MOE_EXPERTS
    wr = jnp.concatenate([moe_w_rg[l], moe_w_re[l], jnp.zeros((d, LANES - n_r), F32)], axis=1)
    wr_hi = wr.astype(BF16)
    rw = {
        'hi': wr_hi, 'lo': (wr - wr_hi.astype(F32)).astype(BF16),
        'b': jnp.concatenate([moe_b_rg[l], moe_b_re[l], jnp.zeros((LANES - n_r,), F32)])[None, :],
    }
    return w_in_p, prep, ow, rw


def _layer(xx, modsel, lw, rope_c, rope_s, na_bias, rc, hc, hg_lb_l, ret_go, hg_go, layer, w1, w3, w2,
           n_lat, n_ctx, last):
    w_in_p, prep_w, ow, rw = lw
    b, nt, d = xx.shape
    n_lat_tiles = n_lat // TM
    p, qm, km, vm, qn, kn, vn = _inproj_prep(xx, modsel, w_in_p, rope_c, rope_s, prep_w, n_lat_tiles)
    y_mla = _mla_attn(qm, km, vm, n_lat, n_ctx)
    y_na = _na_attn(qn, kn, vn, na_bias, n_lat, n_ctx)
    y_ret = _retention(p, rc, ret_go, n_lat, n_ctx)
    y_hg = _hgrn2(p, hc, hg_lb_l, hg_go, n_lat, n_ctx)
    xx, route, counts = _outproj_router(xx, y_mla, y_na, y_ret, y_hg, modsel, ow, rw, n_lat_tiles)
    dest, block_expert, used, n_blocks = _moe_plan(route, counts, b * nt)
    x_rows = _dispatch(xx, modsel, dest, n_blocks * MOE_BLOCK, n_lat_tiles)
    y_rows = _moe_ffn(x_rows, block_expert, used, n_blocks, layer, w1, w3, w2)
    return _combine(xx, modsel, route, y_rows, dest, n_lat_tiles, n_lat_tiles if last else nt // TM)


def kernel(x, c, ctx, c_ctx, w_ada, b_ada, w_in, w_out, mla_g_cq, mla_g_ckv, mla_w_uq, mla_w_ukv, mla_g_qn, mla_g_qr, mla_g_kn, mla_g_kr, na_g_q, na_g_k, na_rpb, ret_g_out, hg_lb_raw, hg_g_out, moe_w_rg, moe_b_rg, moe_w_re, moe_b_re, moe_w1, moe_w3, moe_w2):
    b, n_lat, d = x.shape
    n_ctx = ctx.shape[1]
    depth = w_in.shape[0]
    assert n_lat % TM == 0 and n_ctx % TM == 0 and TM % GRID_W == 0
    assert n_lat // TM >= 3 and n_lat // GRID_W >= NA_WIN_ROWS
    assert w_in.shape[2] == MLA_Q_LORA + MLA_KV_LORA + MLA_ROPE + 12 * GROUP_W

    cc = jnp.concatenate([c, c_ctx[None, :], jnp.zeros((16 - b - 1, d), F32)], axis=0)
    mods = _ada_all(cc, w_ada, b_ada)
    rope_c, rope_s = _rope_tables(n_lat, n_ctx)
    rc = _retention_consts()
    hc = _hgrn_consts()
    lb_w = jax.nn.softmax(hg_lb_raw.astype(F32), axis=0)
    hg_lb = jnp.cumsum(lb_w, axis=0) - lb_w[0:1]

    xx = jnp.concatenate([x, ctx], axis=1)
    tile_go = lambda g: jnp.tile(g, N_HEADS)[None, :]
    for l in range(depth):
        modsel = jnp.stack([mods[l, :b], jnp.broadcast_to(mods[l, b], (b, 6 * d))], axis=1).reshape(2 * b, 1, 6 * d)
        lw = _layer_weights(l, w_in, w_out, mla_g_cq, mla_g_ckv, mla_w_uq, mla_w_ukv, mla_g_qn, mla_g_qr,
                            mla_g_kn, mla_g_kr, na_g_q, na_g_k, moe_w_rg, moe_b_rg, moe_w_re, moe_b_re)
        xx = _layer(xx, modsel, lw, rope_c, rope_s, _na_bias_table(na_rpb[l]), rc, hc, hg_lb[l][None, :],
                    tile_go(ret_g_out[l]), tile_go(hg_g_out[l]),
                    l, moe_w1, moe_w3, moe_w2,
                    n_lat, n_ctx, l == depth - 1)
    return xx
```

```python
import functools

import numpy as np
import jax
import jax.numpy as jnp
from jax import lax
from jax.experimental import pallas as pl
from jax.experimental.pallas import tpu as pltpu

F32 = jnp.float32
BF16 = jnp.bfloat16

EPS = 1e-6
ROPE_THETA = 10000.0
NEG_BIG = -1e30
F_FLOOR = 1e-20
GRID_W = 64
N_HEADS = 4
HEAD_DIM = 64
LANES = 128
GROUP_W = N_HEADS * HEAD_DIM
PAD_W = N_HEADS * LANES
MLA_Q_LORA = 192
MLA_KV_LORA = 128
MLA_ROPE = 32
MLA_SCALE = (HEAD_DIM + MLA_ROPE) ** -0.5
LOG2E = 1.4426950408889634
NA_KH = 8
NA_KW = 16
NA_SCALE = HEAD_DIM ** -0.5
RET_CHUNK = 128
HG_CHUNK = 16
MOE_GROUPS = 4
MOE_PER_GROUP = 8
MOE_EXPERTS = MOE_GROUPS * MOE_PER_GROUP
MOE_TOPK = 2
MOE_BLOCK = 512
TM = 256
NA_TILE_ROWS = TM // GRID_W
NA_WIN_ROWS = NA_TILE_ROWS + NA_KH
VMEM_LIMIT = 56 * 1024 * 1024
DMA_ISSUE_UNROLL = 8

P_ATTN_COLS = 2 * GROUP_W + 3 * PAD_W
COL_RET_Q, COL_RET_K, COL_RET_V, COL_RET_G = 0, 1, 2, 3
COL_HG_Q, COL_HG_FF, COL_HG_FB, COL_HG_I, COL_HG_G = 4, 5, 6, 7, 8


def _cparams(sem):
    return pltpu.CompilerParams(dimension_semantics=sem, vmem_limit_bytes=VMEM_LIMIT)


def _sigmoid(x):
    return 1.0 / (1.0 + jnp.exp(-x))


def _silu(x):
    return x * _sigmoid(x)


def _dot(a, b):
    return jnp.dot(a, b, preferred_element_type=F32)


def _dot_nt(a, b):
    return lax.dot_general(a, b, (((1,), (1,)), ((), ())), preferred_element_type=F32)


def _split_dot_l(x, m, n):
    acc = None
    rem = x
    for i in range(n):
        piece = rem.astype(BF16)
        d = _dot(piece, m)
        acc = d if acc is None else acc + d
        if i + 1 < n:
            rem = rem - piece.astype(F32)
    return acc


def _split_dot_r(m, x, n):
    acc = None
    rem = x
    for i in range(n):
        piece = rem.astype(BF16)
        d = _dot(m, piece)
        acc = d if acc is None else acc + d
        if i + 1 < n:
            rem = rem - piece.astype(F32)
    return acc


def _pack_bf16_pairs(x):
    half = x.shape[1] // 2
    bits = lax.bitcast_convert_type(x.astype(BF16).astype(F32), jnp.uint32)
    return (bits[:, :half] >> 16) | (bits[:, half:] & jnp.uint32(0xFFFF0000))


def _unpack_bf16_pairs(p):
    lo = lax.bitcast_convert_type(p << 16, F32)
    hi = lax.bitcast_convert_type(p & jnp.uint32(0xFFFF0000), F32)
    return jnp.concatenate([lo, hi], axis=1).astype(BF16)


def _seg_rms(x, m, gain):
    return x * lax.rsqrt(_split_dot_l(x * x, m, 2) + EPS) * gain


def _ada_kernel(c_ref, w_ref, b_ref, o_ref):
    s = _silu(c_ref[...])
    o_ref[0] = jnp.dot(s, w_ref[0], preferred_element_type=F32,
                       precision=lax.Precision.HIGHEST) + b_ref[0]


def _ada_all(cc, w_ada, b_ada):
    n_layers, d, d6 = w_ada.shape
    bn = 512
    rows = cc.shape[0]
    return pl.pallas_call(
        _ada_kernel,
        grid=(n_layers, d6 // bn),
        in_specs=[pl.BlockSpec((rows, d), lambda l, j: (0, 0)),
                  pl.BlockSpec((1, d, bn), lambda l, j: (l, 0, j)),
                  pl.BlockSpec((1, 1, bn), lambda l, j: (l, 0, j))],
        out_specs=pl.BlockSpec((1, rows, bn), lambda l, j: (l, 0, j)),
        out_shape=jax.ShapeDtypeStruct((n_layers, rows, d6), F32),
        compiler_params=_cparams(("arbitrary", "arbitrary")),
    )(cc, w_ada, b_ada.reshape(n_layers, 1, d6))


def _mod_spec(d6, n_lat_tiles):
    return pl.BlockSpec((1, 1, d6), lambda b, j: (2 * b + (j >= n_lat_tiles).astype(jnp.int32), 0, 0))


def _modulate(x, shift, scale):
    xn = x * lax.rsqrt(jnp.mean(x * x, axis=-1, keepdims=True) + EPS)
    return xn * (1.0 + scale) + shift


def _inproj_prep_kernel(x_ref, mod_ref, w_ref, c_ref, s_ref, wuq_ref, wk_ref, wv_ref, gcq_ref, gckv_ref, gkr_ref,
                        gq_ref, gk_ref, mq_ref, mk_ref, gnq_ref, gnk_ref, mn_ref,
                        p_ref, qm_ref, km_ref, vm_ref, qn_ref, kn_ref, vn_ref, *, d):
    xm = _modulate(x_ref[0], mod_ref[0, :, 0:d], mod_ref[0, :, d:2 * d]).astype(BF16)
    p_ref[0] = _dot(xm, w_ref[:, P_ATTN_COLS:])
    pa = _dot(xm, w_ref[:, 0:P_ATTN_COLS])

    cq = pa[:, 0:256]
    ckv = pa[:, 256:384]
    kr = pa[:, 384:512]
    cqn = cq * lax.rsqrt(jnp.sum(cq * cq, axis=-1, keepdims=True) * (1.0 / MLA_Q_LORA) + EPS) * gcq_ref[...]
    ckvn = (ckv * lax.rsqrt(jnp.mean(ckv * ckv, axis=-1, keepdims=True) + EPS) * gckv_ref[...]).astype(BF16)
    krn = kr * lax.rsqrt(jnp.sum(kr * kr, axis=-1, keepdims=True) * (1.0 / MLA_ROPE) + EPS) * gkr_ref[...]
    q = _seg_rms(_dot(cqn.astype(BF16), wuq_ref[...]), mq_ref[...], gq_ref[...])
    kk = _seg_rms(_dot(ckvn, wk_ref[...]), mk_ref[...], gk_ref[...])
    vv = _dot(ckvn, wv_ref[...])

    cos = c_ref[...]
    sin = s_ref[...]
    lane = lax.broadcasted_iota(jnp.int32, (TM, LANES), 1)
    first = (lane % 16) < 8

    def rope(x):
        partner = jnp.where(first, pltpu.roll(x, LANES - 8, 1), pltpu.roll(x, 8, 1))
        return x * cos + partner * sin

    krr = rope(krn)
    nq = _seg_rms(pa[:, 512:1024], mn_ref[...], gnq_ref[...])
    nk = _seg_rms(pa[:, 1024:1536], mn_ref[...], gnk_ref[...])
    for h in range(N_HEADS):
        sl = slice(h * LANES, (h + 1) * LANES)
        qm_ref[0, h] = (rope(q[:, sl]) * (MLA_SCALE * LOG2E)).astype(BF16)
        km_ref[0, h] = (kk[:, sl] + krr).astype(BF16)
        vm_ref[0, h] = vv[:, sl].T.astype(BF16)
        qn_ref[0, h] = (nq[:, sl] * (NA_SCALE * LOG2E)).astype(BF16)
        kn_ref[0, h] = nk[:, sl].astype(BF16)
        vn_ref[0, h] = pa[:, 1536 + h * LANES:1536 + (h + 1) * LANES].astype(BF16)


def _inproj_prep(xx, modsel, w_in_p, rope_c, rope_s, pw, n_lat_tiles):
    b, nt, d = xx.shape
    rest = w_in_p.shape[1] - P_ATTN_COLS
    full = lambda a: pl.BlockSpec(a.shape, lambda i, j: (0,) * a.ndim)
    consts = [pw['wuq'], pw['wk'], pw['wv'], pw['gcq'], pw['gckv'], pw['gkr'], pw['gq'], pw['gk'],
              pw['mq'], pw['mk'], pw['gnq'], pw['gnk'], pw['mn']]
    head_spec = pl.BlockSpec((1, N_HEADS, TM, LANES), lambda i, j: (i, 0, j, 0))
    head_shape = jax.ShapeDtypeStruct((b, N_HEADS, nt, LANES), BF16)
    head_t_spec = pl.BlockSpec((1, N_HEADS, LANES, TM), lambda i, j: (i, 0, 0, j))
    head_t_shape = jax.ShapeDtypeStruct((b, N_HEADS, LANES, nt), BF16)
    return pl.pallas_call(
        functools.partial(_inproj_prep_kernel, d=d),
        grid=(b, nt // TM),
        in_specs=[pl.BlockSpec((1, TM, d), lambda i, j: (i, j, 0)),
                  _mod_spec(6 * d, n_lat_tiles),
                  full(w_in_p),
                  pl.BlockSpec((TM, LANES), lambda i, j: (j, 0)),
                  pl.BlockSpec((TM, LANES), lambda i, j: (j, 0))] + [full(a) for a in consts],
        out_specs=[pl.BlockSpec((1, TM, rest), lambda i, j: (i, j, 0)),
                   head_spec, head_spec, head_t_spec, head_spec, head_spec, head_spec],
        out_shape=[jax.ShapeDtypeStruct((b, nt, rest), F32),
                   head_shape, head_shape, head_t_shape, head_shape, head_shape, head_shape],
        compiler_params=_cparams(("arbitrary", "arbitrary")),
    )(xx, modsel, w_in_p, rope_c, rope_s, *consts)


def _softmax2_pv(s, v):
    m = jnp.max(s, axis=-1, keepdims=True)
    e = jnp.exp2(s - m)
    l = jnp.sum(e, axis=-1, keepdims=True)
    return _dot(e.astype(BF16), v) / l


MLA_KEY_BLOCK = 2176
MLA_Q_TILE = 512


MLA_HEADS_PER_STEP = 4


def _mla_attend(q_ref, k_ref, vt_ref, o_ref, n_q, k0, k1):
    items = [(hh, s0) for hh in range(MLA_HEADS_PER_STEP) for s0 in range(k0, k1, MLA_KEY_BLOCK)]
    score = lambda hh, s0: _dot_nt(k_ref[0, hh, s0:min(s0 + MLA_KEY_BLOCK, k1), :], q_ref[0, hh, 0:n_q, :])
    st = score(*items[0])
    m = l = acc = None
    for i, (hh, s0) in enumerate(items):
        st_next = score(*items[i + 1]) if i + 1 < len(items) else None
        vt_blk = vt_ref[0, hh, :, s0:min(s0 + MLA_KEY_BLOCK, k1)]
        bm = jnp.max(st, axis=0, keepdims=True)
        if s0 == k0:
            m = bm
            e = jnp.exp2(st - m)
            l = jnp.sum(e, axis=0, keepdims=True)
            acc = _dot(vt_blk, e.astype(BF16))
        else:
            m_new = jnp.maximum(m, bm)
            alpha = jnp.exp2(m - m_new)
            e = jnp.exp2(st - m_new)
            l = l * alpha + jnp.sum(e, axis=0, keepdims=True)
            acc = acc * alpha + _dot(vt_blk, e.astype(BF16))
            m = m_new
        if s0 + MLA_KEY_BLOCK >= k1:
            o_ref[0, 0:n_q, hh * LANES:(hh + 1) * LANES] = (acc / l).T
        st = st_next


def _mla_kernel(q_ref, k_ref, vt_ref, o_ref, *, n_lat, n_ctx):
    j = pl.program_id(2)

    @pl.when(j < n_lat // MLA_Q_TILE)
    def _():
        _mla_attend(q_ref, k_ref, vt_ref, o_ref, MLA_Q_TILE, 0, n_lat + n_ctx)

    @pl.when(j >= n_lat // MLA_Q_TILE)
    def _():
        _mla_attend(q_ref, k_ref, vt_ref, o_ref, n_ctx, n_lat, n_lat + n_ctx)


def _mla_attn(qm, km, vmt, n_lat, n_ctx):
    b, h, nt, _ = qm.shape
    hp = MLA_HEADS_PER_STEP
    tq = MLA_Q_TILE
    assert n_lat % tq == 0 and n_ctx <= tq
    kv_spec = pl.BlockSpec((1, hp, nt, LANES), lambda i, hh, j: (i, hh, 0, 0))
    vt_spec = pl.BlockSpec((1, hp, LANES, nt), lambda i, hh, j: (i, hh, 0, 0))
    return pl.pallas_call(
        functools.partial(_mla_kernel, n_lat=n_lat, n_ctx=n_ctx),
        grid=(b, h // hp, n_lat // tq + 1),
        in_specs=[pl.BlockSpec((1, hp, tq, LANES), lambda i, hh, j: (i, hh, j, 0)), kv_spec, vt_spec],
        out_specs=pl.BlockSpec((1, tq, hp * LANES), lambda i, hh, j: (i, j, hh)),
        out_shape=jax.ShapeDtypeStruct((b, nt, PAD_W), F32),
        compiler_params=_cparams(("arbitrary", "arbitrary", "arbitrary")),
    )(qm, km, vmt)


def _na_kernel(q_ref, k_ref, v_ref, bias_ref, o_ref, *, n_lat, n_ctx):
    j = pl.program_id(1)
    rows = n_lat // GRID_W
    n_tiles = n_lat // TM
    win = NA_WIN_ROWS * GRID_W

    @pl.when(j < n_tiles)
    def _():
        start = jnp.clip(j * NA_TILE_ROWS - NA_KH // 2, 0, rows - NA_WIN_ROWS)
        case = jnp.where(j == 0, 0, jnp.where(j == n_tiles - 1, 2, 1))
        tok0 = pl.multiple_of(start * GRID_W, GRID_W)
        def scores(h):
            q = q_ref[0, h]
            return (_dot_nt(q, k_ref[0, h, pl.ds(tok0, win), :]) + bias_ref[case, h],
                    _dot_nt(q, k_ref[0, h, pl.ds(n_lat, n_ctx), :]))

        nxt = scores(0)
        for h in range(N_HEADS):
            s1, s2 = nxt
            if h + 1 < N_HEADS:
                nxt = scores(h + 1)
            m = jnp.maximum(jnp.max(s1, axis=-1, keepdims=True), jnp.max(s2, axis=-1, keepdims=True))
            e1 = jnp.exp2(s1 - m)
            e2 = jnp.exp2(s2 - m)
            l = jnp.sum(e1, axis=-1, keepdims=True) + jnp.sum(e2, axis=-1, keepdims=True)
            o = _dot(e1.astype(BF16), v_ref[0, h, pl.ds(tok0, win), :])
            o = o + _dot(e2.astype(BF16), v_ref[0, h, pl.ds(n_lat, n_ctx), :])
            o_ref[0, :, h * LANES:(h + 1) * LANES] = o / l

    @pl.when(j >= n_lat // TM)
    def _():
        for h in range(N_HEADS):
            s = _dot_nt(q_ref[0, h], k_ref[0, h, pl.ds(n_lat, n_ctx), :])
            o_ref[0, :, h * LANES:(h + 1) * LANES] = _softmax2_pv(s, v_ref[0, h, pl.ds(n_lat, n_ctx), :])


def _na_attn(qn, kn, vn, bias, n_lat, n_ctx):
    b, h, nt, _ = qn.shape
    kv_spec = pl.BlockSpec((1, h, nt, LANES), lambda i, j: (i, 0, 0, 0))
    return pl.pallas_call(
        functools.partial(_na_kernel, n_lat=n_lat, n_ctx=n_ctx),
        grid=(b, nt // TM),
        in_specs=[pl.BlockSpec((1, h, TM, LANES), lambda i, j: (i, 0, j, 0)), kv_spec, kv_spec,
                  pl.BlockSpec(bias.shape, lambda i, j: (0, 0, 0, 0))],
        out_specs=pl.BlockSpec((1, TM, PAD_W), lambda i, j: (i, j, 0)),
        out_shape=jax.ShapeDtypeStruct((b, nt, PAD_W), F32),
        compiler_params=_cparams(("arbitrary", "arbitrary")),
    )(qn, kn, vn, bias)


def _head_mask(h, shape):
    return (lax.broadcasted_iota(jnp.int32, shape, 1) // HEAD_DIM) == h


def _ret_state_step(s_ref, q, k, v, qw, kw, cd, bd):
    state = s_ref[...]
    o = _dot((q * qw).astype(BF16), state.astype(BF16))
    upd = _dot((k * kw).T.astype(BF16), v.astype(BF16))
    s_ref[...] = state * cd + upd * bd
    return o


def _ret_fwd_body(q_ref, k_ref, v_ref, dm_ref, qw_ref, kw_ref, cd_ref, bd_ref, o_ref, s_ref):
    for c in range(TM // RET_CHUNK):
        rows = slice(c * RET_CHUNK, (c + 1) * RET_CHUNK)
        q = q_ref[0, rows]
        k = k_ref[0, rows]
        v = v_ref[0, rows]
        o = _ret_state_step(s_ref, q, k, v, qw_ref[...], kw_ref[...], cd_ref[...], bd_ref[...])
        kb = k.astype(BF16)
        for h in range(N_HEADS):
            hm = _head_mask(h, q.shape)
            sc = _dot_nt(jnp.where(hm, q, 0.0).astype(BF16), kb) * dm_ref[h]
            o = o + _dot(sc.astype(BF16), jnp.where(hm, v, 0.0).astype(BF16))
        o_ref[0, rows] = o


def _ret_bwd_body(q_ref, k_ref, v_ref, g_ref, op_ref, qw_ref, kw_ref, cd_ref, bd_ref, ms_ref, go_ref,
                  y_ref, s_ref):
    for c in reversed(range(TM // RET_CHUNK)):
        rows = slice(c * RET_CHUNK, (c + 1) * RET_CHUNK)
        o = op_ref[0, rows] + _ret_state_step(s_ref, q_ref[0, rows], k_ref[0, rows], v_ref[0, rows], qw_ref[...],
                                              kw_ref[...], cd_ref[...], bd_ref[...])
        y_ref[0, rows] = _seg_rms(o, ms_ref[...], go_ref[...]) * _silu(g_ref[0, rows])


def _scan_order(n_lat_t, n_ctx_t, reverse):
    if reverse:
        return lambda i: jnp.where(i < n_ctx_t, n_lat_t + n_ctx_t - 1 - i, n_lat_t + n_ctx_t - 1 - i)
    return lambda i: jnp.where(i < n_ctx_t, n_lat_t + i, i - n_ctx_t)


def _hg_direction(q_ref, f_ref, v_ref, lb_ref, ain_ref, aex_ref, bseg_ref, bd_ref, st_ref, sh_ref, *, reverse):
    n_chunks = TM // HG_CHUNK
    assert n_chunks == HG_CHUNK
    qh = _silu(q_ref[0])
    lb = lb_ref[...]
    f = jnp.maximum(lb + (1.0 - lb) * _sigmoid(f_ref[0]), F_FLOOR)
    lf = jnp.log(f) * LOG2E
    k = 1.0 - f
    v = v_ref[0]
    row = lax.broadcasted_iota(jnp.int32, (TM, 1), 0)
    pos = row % HG_CHUNK
    row_chunk = row // HG_CHUNK

    a_in = _split_dot_r(ain_ref[...], lf, 3)
    a_ex = _split_dot_r(aex_ref[...], lf, 3)
    width = v.shape[1]
    key_exp = a_in - jnp.log(k) * LOG2E
    for slot, val in enumerate((key_exp, v)):
        sh_ref[slot] = val.reshape(n_chunks, HG_CHUNK, width)

    def key_row(slot, s):
        return jnp.broadcast_to(sh_ref[slot, :, s:s + 1, :], (n_chunks, HG_CHUNK, width)).reshape(TM, width)
    qp = (qh * jnp.exp2(a_in)).astype(BF16)
    kdec = k * jnp.exp2(a_ex)
    lam_all = jnp.exp2(a_in + a_ex)
    vt = v.T.astype(BF16)
    bd = bd_ref[...]
    bseg = bseg_ref[...]
    state = st_ref[...]
    parts = [None] * n_chunks
    o_band = jnp.zeros_like(v)
    upds = [_dot(vt, jnp.where(row_chunk == c, kdec, 0.0).astype(BF16)) * bd for c in range(n_chunks)]
    for step in range(n_chunks):
        c = n_chunks - 1 - step if reverse else step
        r0 = c * HG_CHUNK
        parts[c] = _dot_nt(qp[r0:r0 + HG_CHUNK], state.astype(BF16))
        state = state * lam_all[r0:r0 + 1] + upds[c]
        s = step
        valid = (pos <= s) if reverse else (pos >= s)
        w = jnp.where(valid, qh * jnp.exp2(a_in - key_row(0, s)), 0.0)
        o_band = o_band + _dot(w.astype(BF16), bseg) * key_row(1, s)
    st_ref[...] = state
    return jnp.concatenate(parts, axis=0) + o_band


def _scan_fwd_kernel(rq_ref, rk_ref, rv_ref, hq_ref, hf_ref, hv_ref, dm_ref, qw_ref, kw_ref, cd_ref, bd_ref,
                     lb_ref, ain_ref, aex_ref, bseg_ref, ro_ref, ho_ref, rs_ref, hs_ref, sh_ref):
    @pl.when(pl.program_id(1) == 0)
    def _():
        rs_ref[...] = jnp.zeros_like(rs_ref)
        hs_ref[...] = jnp.zeros_like(hs_ref)

    _ret_fwd_body(rq_ref, rk_ref, rv_ref, dm_ref, qw_ref, kw_ref, cd_ref, bd_ref, ro_ref, rs_ref)
    ho_ref[0] = _hg_direction(hq_ref, hf_ref, hv_ref, lb_ref, ain_ref, aex_ref, bseg_ref, bd_ref, hs_ref, sh_ref,
                              reverse=False)


def _scan_bwd_kernel(rq_ref, rk_ref, rv_ref, rg_ref, rop_ref, hq_ref, hf_ref, hv_ref, hg_ref, hop_ref,
                     qw_ref, kw_ref, cd_ref, bd_ref, ms_ref, rgo_ref, lb_ref, ain_ref, aex_ref, bseg_ref, hgo_ref,
                     ry_ref, hy_ref, rs_ref, hs_ref, sh_ref):
    @pl.when(pl.program_id(1) == 0)
    def _():
        rs_ref[...] = jnp.zeros_like(rs_ref)
        hs_ref[...] = jnp.zeros_like(hs_ref)

    _ret_bwd_body(rq_ref, rk_ref, rv_ref, rg_ref, rop_ref, qw_ref, kw_ref, cd_ref, bd_ref, ms_ref, rgo_ref,
                  ry_ref, rs_ref)
    o = hop_ref[0] + _hg_direction(hq_ref, hf_ref, hv_ref, lb_ref, ain_ref, aex_ref, bseg_ref, bd_ref, hs_ref,
                                   sh_ref, reverse=True)
    hy_ref[0] = _seg_rms(o, ms_ref[...], hgo_ref[...]) * _silu(hg_ref[0])


def _recurrent_mixers(p, rc, hc, lb, ret_go, hg_go, n_lat, n_ctx):
    b, nt, _ = p.shape
    n_lat_t, n_ctx_t = n_lat // TM, n_ctx // TM
    fwd = _scan_order(n_lat_t, n_ctx_t, False)
    bwd = _scan_order(n_lat_t, n_ctx_t, True)
    col = lambda order, cb: pl.BlockSpec((1, TM, GROUP_W), lambda i, j: (i, order(j), cb))
    full = lambda a: pl.BlockSpec(a.shape, lambda i, j: (0,) * a.ndim)
    out_shape = jax.ShapeDtypeStruct((b, nt, GROUP_W), F32)
    scratch = [pltpu.VMEM((GROUP_W, GROUP_W), F32), pltpu.VMEM((GROUP_W, GROUP_W), F32),
               pltpu.VMEM((2, TM // HG_CHUNK, HG_CHUNK, GROUP_W), F32)]
    consts_f = [rc['dm'], rc['qw_f'], rc['kw_f'], rc['cd_f'], rc['bd'], lb, hc['lincl'], hc['uexcl'], hc['bseg']]
    out_f = pl.BlockSpec((1, TM, GROUP_W), lambda i, j: (i, fwd(j), 0))
    ret_part, hg_part = pl.pallas_call(
        _scan_fwd_kernel,
        grid=(b, nt // TM),
        in_specs=[col(fwd, COL_RET_Q), col(fwd, COL_RET_K), col(fwd, COL_RET_V),
                  col(fwd, COL_HG_Q), col(fwd, COL_HG_FF), col(fwd, COL_HG_I)] + [full(a) for a in consts_f],
        out_specs=[out_f, out_f],
        out_shape=[out_shape, out_shape],
        scratch_shapes=scratch,
        compiler_params=_cparams(("arbitrary", "arbitrary")),
    )(p, p, p, p, p, p, *consts_f)
    consts_b = [rc['qw_b'], rc['kw_b'], rc['cd_b'], rc['bd'], rc['ms'], ret_go,
                lb, hc['uincl'], hc['lexcl'], hc['bseg'], hg_go]
    out_b = pl.BlockSpec((1, TM, GROUP_W), lambda i, j: (i, bwd(j), 0))
    return pl.pallas_call(
        _scan_bwd_kernel,
        grid=(b, nt // TM),
        in_specs=[col(bwd, COL_RET_Q), col(bwd, COL_RET_K), col(bwd, COL_RET_V), col(bwd, COL_RET_G), out_b,
                  col(bwd, COL_HG_Q), col(bwd, COL_HG_FB), col(bwd, COL_HG_I), col(bwd, COL_HG_G), out_b]
                 + [full(a) for a in consts_b],
        out_specs=[out_b, out_b],
        out_shape=[out_shape, out_shape],
        scratch_shapes=scratch,
        compiler_params=_cparams(("arbitrary", "arbitrary")),
    )(p, p, p, p, ret_part, p, p, p, p, hg_part, *consts_b)


def _outproj_router_kernel(x_ref, ym_ref, yn_ref, yr_ref, yh_ref, mod_ref, wm_ref, wn_ref, wr_ref, wh_ref,
                           whi_ref, wlo_ref, br_ref, ltri_ref, o_ref, r_ref, cnt_ref, *, d):
    acc = _dot(ym_ref[0].astype(BF16), wm_ref[...])
    acc = acc + _dot(yn_ref[0].astype(BF16), wn_ref[...])
    acc = acc + _dot(yr_ref[0].astype(BF16), wr_ref[...])
    acc = acc + _dot(yh_ref[0].astype(BF16), wh_ref[...])
    x_new = x_ref[0] + mod_ref[0, :, 2 * d:3 * d] * acc
    o_ref[0] = x_new
    h = _modulate(x_new, mod_ref[0, :, 3 * d:4 * d], mod_ref[0, :, 4 * d:5 * d])
    _route(h, whi_ref, wlo_ref, br_ref, ltri_ref, r_ref, cnt_ref)


def _outproj_router(xx, y_mla, y_na, y_ret, y_hg, modsel, ow, rw, n_lat_tiles):
    b, nt, d = xx.shape
    tiles = nt // TM
    tile = lambda w: pl.BlockSpec((1, TM, w), lambda i, j: (i, j, 0))
    full = lambda a: pl.BlockSpec(a.shape, lambda i, j: (0, 0))
    ltri = jnp.asarray(np.tril(np.ones((TM, TM), np.float32), -1), BF16)
    ws = [ow['mla'], ow['na'], ow['ret'], ow['hg'], rw['hi'], rw['lo'], rw['b'], ltri]
    return pl.pallas_call(
        functools.partial(_outproj_router_kernel, d=d),
        grid=(b, tiles),
        in_specs=[tile(d), tile(PAD_W), tile(PAD_W), tile(GROUP_W), tile(GROUP_W),
                  _mod_spec(6 * d, n_lat_tiles)] + [full(a) for a in ws],
        out_specs=[tile(d), pl.BlockSpec((TM, LANES), lambda i, j: (i * tiles + j, 0)),
                   pl.BlockSpec((1, LANES), lambda i, j: (0, 0))],
        out_shape=[jax.ShapeDtypeStruct((b, nt, d), F32), jax.ShapeDtypeStruct((b * nt, LANES), F32),
                   jax.ShapeDtypeStruct((1, LANES), F32)],
        compiler_params=_cparams(("arbitrary", "arbitrary")),
    )(xx, y_mla, y_na, y_ret, y_hg, modsel, *ws)


def _route(h, whi_ref, wlo_ref, br_ref, ltri_ref, r_ref, cnt_ref):
    @pl.when((pl.program_id(0) == 0) & (pl.program_id(1) == 0))
    def _():
        cnt_ref[...] = jnp.zeros_like(cnt_ref)

    h_hi = h.astype(BF16)
    h_lo = (h - h_hi.astype(F32)).astype(BF16)
    lg = _dot(h_hi, whi_ref[...]) + _dot(h_lo, whi_ref[...]) + _dot(h_hi, wlo_ref[...]) + br_ref[...]

    lane = lax.broadcasted_iota(jnp.int32, lg.shape, 1).astype(F32)
    far = 1e9

    def first_argmax(vals, vmax):
        return jnp.min(jnp.where(vals == vmax, lane, far), axis=-1, keepdims=True)

    gl = jnp.where(lane < MOE_GROUPS, lg, NEG_BIG)
    gmax = jnp.max(gl, axis=-1, keepdims=True)
    pg_top = 1.0 / jnp.sum(jnp.exp(gl - gmax), axis=-1, keepdims=True)
    lo = MOE_GROUPS + MOE_PER_GROUP * first_argmax(gl, gmax)
    fl = jnp.where((lane >= lo) & (lane < lo + MOE_PER_GROUP), lg, NEG_BIG)
    fmax = jnp.max(fl, axis=-1, keepdims=True)
    fsum = jnp.sum(jnp.exp(fl - fmax), axis=-1, keepdims=True)
    i1 = first_argmax(fl, fmax)
    fl2 = jnp.where(lane == i1, NEG_BIG, fl)
    f2max = jnp.max(fl2, axis=-1, keepdims=True)
    i2 = first_argmax(fl2, f2max)
    p1 = 1.0 / fsum
    p2 = jnp.exp(f2max - fmax) / fsum
    g1 = pg_top * p1 / (p1 + p2)
    g2 = pg_top * p2 / (p1 + p2)
    e1 = i1 - MOE_GROUPS
    e2 = i2 - MOE_GROUPS

    onehot = jnp.where(lane == e1, 1.0, 0.0) + jnp.where(lane == e2, 1.0, 0.0)
    before = cnt_ref[...] + _dot(ltri_ref[...], onehot.astype(BF16))
    r1 = jnp.sum(jnp.where(lane == e1, before, 0.0), axis=-1, keepdims=True)
    r2 = jnp.sum(jnp.where(lane == e2, before, 0.0), axis=-1, keepdims=True)
    cnt_ref[...] += jnp.sum(onehot, axis=0, keepdims=True)

    out = jnp.zeros_like(lg)
    for col, val in enumerate((e1, e2, g1, g2, r1, r2)):
        out = jnp.where(lane == col, val, out)
    r_ref[...] = out


ROUTE_E, ROUTE_G, ROUTE_R = 0, 2, 4


def _moe_plan(route, counts_f, n_tok):
    counts = counts_f[0, :MOE_EXPERTS].astype(jnp.int32)
    padded = (counts + MOE_BLOCK - 1) // MOE_BLOCK * MOE_BLOCK
    pad_end = jnp.cumsum(padded)
    pad_start = pad_end - padded
    n_blocks = -(-(n_tok * MOE_TOPK) // MOE_BLOCK) + MOE_EXPERTS
    blk0 = jnp.arange(n_blocks, dtype=jnp.int32) * MOE_BLOCK
    block_expert = jnp.minimum(jnp.sum((pad_end[None, :] <= blk0[:, None]).astype(jnp.int32), axis=1),
                               MOE_EXPERTS - 1)
    used = (pad_end[-1] // MOE_BLOCK).reshape(1)
    expert = route[:, ROUTE_E:ROUTE_E + MOE_TOPK].astype(jnp.int32)
    rank = route[:, ROUTE_R:ROUTE_R + MOE_TOPK].astype(jnp.int32)
    start_of = jnp.sum(jnp.where(expert[..., None] == jnp.arange(MOE_EXPERTS, dtype=jnp.int32), pad_start, 0), axis=-1)
    return start_of + rank, block_expert, used, n_blocks


def _idx_blocks(dest, k, n_tiles):
    return dest[:, k].reshape(n_tiles, 1, TM)


def _dispatch_kernel(d0_ref, d1_ref, x_ref, mod_ref, rows_in, rows_out, h_ref, sem, *, d):
    del rows_in
    step = pl.program_id(0) * pl.num_programs(1) + pl.program_id(1)
    n_steps = pl.num_programs(0) * pl.num_programs(1)
    slot = step % 2

    def wait_slot(s):
        for _ in range(MOE_TOPK):
            pltpu.make_async_copy(h_ref.at[s], rows_out.at[pl.ds(0, TM)], sem.at[s]).wait()

    h = _modulate(x_ref[0], mod_ref[0, :, 3 * d:4 * d], mod_ref[0, :, 4 * d:5 * d])
    h_ref[slot] = _pack_bf16_pairs(h)

    def issue(r, carry):
        src = h_ref.at[slot, pl.ds(r, 1)]
        pltpu.make_async_copy(src, rows_out.at[pl.ds(d0_ref[0, 0, r], 1)], sem.at[slot]).start()
        pltpu.make_async_copy(src, rows_out.at[pl.ds(d1_ref[0, 0, r], 1)], sem.at[slot]).start()
        return carry

    lax.fori_loop(0, TM, issue, 0, unroll=DMA_ISSUE_UNROLL)

    @pl.when(step > 0)
    def _():
        wait_slot(1 - slot)

    @pl.when(step == n_steps - 1)
    def _():
        wait_slot(slot)


def _dispatch(xx, modsel, dest, n_rows, n_lat_tiles):
    b, nt, d = xx.shape
    tiles = nt // TM
    idx_spec = pl.BlockSpec((1, 1, TM), lambda i, j: (i * tiles + j, 0, 0), memory_space=pltpu.SMEM)
    return pl.pallas_call(
        functools.partial(_dispatch_kernel, d=d),
        grid=(b, tiles),
        in_specs=[idx_spec, idx_spec, pl.BlockSpec((1, TM, d), lambda i, j: (i, j, 0)),
                  _mod_spec(6 * d, n_lat_tiles), pl.BlockSpec(memory_space=pl.ANY)],
        out_specs=pl.BlockSpec(memory_space=pl.ANY),
        out_shape=jax.ShapeDtypeStruct((n_rows, d // 2), jnp.uint32),
        scratch_shapes=[pltpu.VMEM((2, TM, d // 2), jnp.uint32), pltpu.SemaphoreType.DMA((2,))],
        input_output_aliases={4: 0},
        compiler_params=_cparams(("arbitrary", "arbitrary")),
    )(_idx_blocks(dest, 0, b * tiles), _idx_blocks(dest, 1, b * tiles), xx, modsel,
      jnp.zeros((n_rows, d // 2), jnp.uint32))


def _ffn_kernel(be_ref, used_ref, x_ref, w1_ref, w3_ref, w2_ref, y_ref, w1b_ref, w3b_ref, w2b_ref):
    i = pl.program_id(0)

    @pl.when((i == 0) | (be_ref[i] != be_ref[jnp.maximum(i - 1, 0)]))
    def _():
        w1b_ref[...] = w1_ref[0, 0].astype(BF16)
        w3b_ref[...] = w3_ref[0, 0].astype(BF16)
        w2b_ref[...] = w2_ref[0, 0].astype(BF16)

    @pl.when(i < used_ref[0])
    def _():
        x = _unpack_bf16_pairs(x_ref[...])
        mid = _silu(_dot(x, w1b_ref[...])) * _dot(x, w3b_ref[...])
        y_ref[...] = _dot(mid.astype(BF16), w2b_ref[...])

    @pl.when(i >= used_ref[0])
    def _():
        y_ref[...] = jnp.zeros_like(y_ref)


def _moe_ffn(x_rows, block_expert, used, n_blocks, layer, w1, w3, w2):
    d = w1.shape[2]
    ff = w1.shape[3]
    grid_spec = pltpu.PrefetchScalarGridSpec(
        num_scalar_prefetch=2,
        grid=(n_blocks,),
        in_specs=[pl.BlockSpec((MOE_BLOCK, d // 2), lambda i, be, nu: (i, 0)),
                  pl.BlockSpec((1, 1, d, ff), lambda i, be, nu: (layer, be[i], 0, 0)),
                  pl.BlockSpec((1, 1, d, ff), lambda i, be, nu: (layer, be[i], 0, 0)),
                  pl.BlockSpec((1, 1, ff, d), lambda i, be, nu: (layer, be[i], 0, 0))],
        out_specs=pl.BlockSpec((MOE_BLOCK, d), lambda i, be, nu: (i, 0)),
        scratch_shapes=[pltpu.VMEM((d, ff), BF16), pltpu.VMEM((d, ff), BF16), pltpu.VMEM((ff, d), BF16)],
    )
    return pl.pallas_call(
        _ffn_kernel,
        grid_spec=grid_spec,
        out_shape=jax.ShapeDtypeStruct((x_rows.shape[0], d), F32),
        compiler_params=_cparams(("arbitrary",)),
    )(block_expert, used, x_rows, w1, w3, w2)


def _row_gather(src_hbm, idx_ref, dst_ref, sem, n):
    def issue(r, carry):
        pltpu.make_async_copy(src_hbm.at[pl.ds(idx_ref[0, 0, r], 1)], dst_ref.at[pl.ds(r, 1)], sem).start()
        return carry

    lax.fori_loop(0, n, issue, 0, unroll=DMA_ISSUE_UNROLL)


def _row_gather_wait(src_hbm, dst_ref, sem, n):
    pltpu.make_async_copy(src_hbm.at[pl.ds(0, n)], dst_ref, sem).wait()


def _combine_kernel(d0_ref, d1_ref, n0_ref, n1_ref, x_ref, mod_ref, r_ref, y_hbm, o_ref, y_ref, sem, *, d):
    step = pl.program_id(0) * pl.num_programs(1) + pl.program_id(1)
    n_steps = pl.num_programs(0) * pl.num_programs(1)
    slot = step % 2

    def gather(idx_refs, s):
        for k, idx_ref in enumerate(idx_refs):
            _row_gather(y_hbm, idx_ref, y_ref.at[s, k], sem.at[s, k], TM)

    @pl.when(step == 0)
    def _():
        gather((d0_ref, d1_ref), slot)

    @pl.when(step + 1 < n_steps)
    def _():
        gather((n0_ref, n1_ref), 1 - slot)

    route = r_ref[...]
    lane = lax.broadcasted_iota(jnp.int32, route.shape, 1)
    g0 = jnp.sum(jnp.where(lane == ROUTE_G, route, 0.0), axis=-1, keepdims=True)
    g1 = jnp.sum(jnp.where(lane == ROUTE_G + 1, route, 0.0), axis=-1, keepdims=True)
    for k in range(MOE_TOPK):
        _row_gather_wait(y_hbm, y_ref.at[slot, k], sem.at[slot, k], TM)
    o_ref[0] = x_ref[0] + mod_ref[0, :, 5 * d:6 * d] * (y_ref[slot, 0] * g0 + y_ref[slot, 1] * g1)


def _combine(xx, modsel, route, y_rows, dest, n_lat_tiles, out_tiles):
    b, nt, d = xx.shape
    tiles = nt // TM
    idx_spec = pl.BlockSpec((1, 1, TM), lambda i, j: (i * tiles + j, 0, 0), memory_space=pltpu.SMEM)

    def next_block(i, j):
        wrap = j + 1 >= out_tiles
        return (jnp.where(wrap, jnp.minimum(i + 1, b - 1) * tiles, i * tiles + j + 1), 0, 0)

    next_spec = pl.BlockSpec((1, 1, TM), next_block, memory_space=pltpu.SMEM)
    d0, d1 = _idx_blocks(dest, 0, b * tiles), _idx_blocks(dest, 1, b * tiles)
    return pl.pallas_call(
        functools.partial(_combine_kernel, d=d),
        grid=(b, out_tiles),
        in_specs=[idx_spec, idx_spec, next_spec, next_spec, pl.BlockSpec((1, TM, d), lambda i, j: (i, j, 0)),
                  _mod_spec(6 * d, n_lat_tiles),
                  pl.BlockSpec((TM, LANES), lambda i, j: (i * tiles + j, 0)),
                  pl.BlockSpec(memory_space=pl.ANY)],
        out_specs=pl.BlockSpec((1, TM, d), lambda i, j: (i, j, 0)),
        out_shape=jax.ShapeDtypeStruct((b, out_tiles * TM, d), F32),
        scratch_shapes=[pltpu.VMEM((2, MOE_TOPK, TM, d), F32), pltpu.SemaphoreType.DMA((2, MOE_TOPK))],
        compiler_params=_cparams(("arbitrary", "arbitrary")),
    )(d0, d1, d0, d1, xx, modsel, route, y_rows)


def _pad_heads_cols(w):
    lead = w.shape[:-1]
    w = w.reshape(*lead, N_HEADS, HEAD_DIM)
    w = jnp.concatenate([w, jnp.zeros_like(w)], axis=-1)
    return w.reshape(*lead, PAD_W)


def _pad_heads_rows(w):
    return _pad_heads_cols(w.T).T


def _seg_mean_matrix(width, segments):
    m = np.zeros((width, width), np.float32)
    for g in range(width // LANES):
        for start, length in segments:
            a = g * LANES + start
            m[a:a + length, a:a + length] = 1.0 / length
    return jnp.asarray(m, BF16)


def _rope_tables(n_lat, n_ctx):
    pos = jnp.arange(n_lat)
    rows = (pos // GRID_W).astype(F32)
    cols = (pos % GRID_W).astype(F32)
    per_axis = MLA_ROPE // 2
    inv_freq = ROPE_THETA ** (-jnp.arange(0, per_axis, 2, dtype=F32) / per_axis)
    ang = jnp.concatenate([rows[:, None] * inv_freq, cols[:, None] * inv_freq], axis=-1)
    i = np.arange(MLA_ROPE)
    src = (i // 16) * 8 + (i % 8)
    sign = np.where((i % 16) < 8, -1.0, 1.0).astype(np.float32)
    cos = jnp.ones((n_lat, LANES), F32).at[:, HEAD_DIM:HEAD_DIM + MLA_ROPE].set(jnp.cos(ang)[:, src])
    sin = jnp.zeros((n_lat, LANES), F32).at[:, HEAD_DIM:HEAD_DIM + MLA_ROPE].set(jnp.sin(ang)[:, src] * sign)
    cos = jnp.concatenate([cos, jnp.ones((n_ctx, LANES), F32)], axis=0)
    sin = jnp.concatenate([sin, jnp.zeros((n_ctx, LANES), F32)], axis=0)
    return cos, sin


def _na_bias_table(rpb):
    w = np.arange(GRID_W)
    col_start = np.clip(w - NA_KW // 2, 0, GRID_W - NA_KW)
    valid = (w[None, :] >= col_start[:, None]) & (w[None, :] < col_start[:, None] + NA_KW)
    dc = np.clip(w[None, :] - w[:, None], 1 - NA_KW, NA_KW - 1) + (NA_KW - 1)
    onehot = jnp.asarray(dc[None, :, :] == np.arange(2 * NA_KW - 1)[:, None, None], F32)
    t = jnp.einsum('hrd,dqk->hrqk', rpb.astype(F32), onehot, precision=lax.Precision.HIGHEST)
    t = jnp.where(jnp.asarray(valid)[None, None, :, :], t * LOG2E, NEG_BIG)
    masked = jnp.full((N_HEADS, GRID_W, GRID_W), NEG_BIG, F32)
    q_off = (0, NA_KH // 2, NA_KH)
    first = ([0] * NA_TILE_ROWS, list(range(NA_TILE_ROWS)), [NA_KH // 2] * NA_TILE_ROWS)
    cases = []
    for c in range(3):
        row_blocks = []
        for rr in range(NA_TILE_ROWS):
            blocks = []
            for jj in range(NA_WIN_ROWS):
                live = first[c][rr] <= jj < first[c][rr] + NA_KH
                dr = jj - (q_off[c] + rr) + (NA_KH - 1)
                blocks.append(t[:, dr] if live else masked)
            row_blocks.append(jnp.concatenate(blocks, axis=-1))
        cases.append(jnp.concatenate(row_blocks, axis=-2))
    return jnp.stack(cases, axis=0)


def _block_diag_mask(block):
    i = np.arange(GROUP_W) // block
    return (i[:, None] == i[None, :]).astype(np.float32)


def _retention_consts():
    c = RET_CHUNK
    j = np.arange(2 * N_HEADS, dtype=np.float64)
    lg = np.log1p(-np.exp2(-5.0 - j))
    lg_f, lg_b = lg[0::2], lg[1::2]
    pos = np.arange(c, dtype=np.float64)
    diff = pos[:, None] - pos[None, :]
    k_scale = HEAD_DIM ** -0.5
    dm = np.zeros((N_HEADS, c, c))
    for h in range(N_HEADS):
        dm[h] = (np.where(diff >= 0, np.exp(np.maximum(diff, 0.0) * lg_f[h]), 0.0)
                 + np.where(diff <= 0, np.exp(np.maximum(-diff, 0.0) * lg_b[h]), 0.0)) * k_scale
    lanes = lambda per_head: np.repeat(per_head, HEAD_DIM, axis=-1)
    out = {
        'dm': dm,
        'qw_f': lanes(np.exp((pos + 1)[:, None] * lg_f[None, :])),
        'kw_f': lanes(np.exp((c - 1 - pos)[:, None] * lg_f[None, :])) * k_scale,
        'cd_f': lanes(np.exp(c * lg_f)[None, :]),
        'qw_b': lanes(np.exp((c - pos)[:, None] * lg_b[None, :])),
        'kw_b': lanes(np.exp(pos[:, None] * lg_b[None, :])) * k_scale,
        'cd_b': lanes(np.exp(c * lg_b)[None, :]),
        'bd': _block_diag_mask(HEAD_DIM),
    }
    out = {k: jnp.asarray(v, F32) for k, v in out.items()}
    out['ms'] = jnp.asarray(_block_diag_mask(HEAD_DIM) / HEAD_DIM, BF16)
    return out


def _hgrn_consts():
    t = np.arange(TM)
    same = (t[:, None] // HG_CHUNK) == (t[None, :] // HG_CHUNK)
    lincl = same & (t[None, :] <= t[:, None])
    lexcl = same & (t[None, :] < t[:, None])
    return {
        'lincl': jnp.asarray(lincl, BF16), 'lexcl': jnp.asarray(lexcl, BF16),
        'uincl': jnp.asarray(lincl.T, BF16), 'uexcl': jnp.asarray(lexcl.T, BF16),
        'bseg': jnp.asarray(_block_diag_mask(HEAD_DIM), BF16),
        'bd': jnp.asarray(_block_diag_mask(HEAD_DIM), F32),
        'ms': jnp.asarray(_block_diag_mask(HEAD_DIM) / HEAD_DIM, BF16),
    }


def _layer_weights(l, w_in, w_out, mla_g_cq, mla_g_ckv, mla_w_uq, mla_w_ukv, mla_g_qn, mla_g_qr, mla_g_kn,
                   mla_g_kr, na_g_q, na_g_k, moe_w_rg, moe_b_rg, moe_w_re, moe_b_re):
    d = w_in.shape[1]
    w = w_in[l]
    z = lambda n: jnp.zeros((d, n), F32)
    o = 0
    cq, o = w[:, o:o + MLA_Q_LORA], o + MLA_Q_LORA
    ckv, o = w[:, o:o + MLA_KV_LORA], o + MLA_KV_LORA
    kr, o = w[:, o:o + MLA_ROPE], o + MLA_ROPE
    naq, o = w[:, o:o + GROUP_W], o + GROUP_W
    nak, o = w[:, o:o + GROUP_W], o + GROUP_W
    nav, o = w[:, o:o + GROUP_W], o + GROUP_W
    rest = w[:, o:]
    w_in_p = jnp.concatenate([cq, z(GROUP_W - MLA_Q_LORA), ckv, z(HEAD_DIM), kr, z(LANES - HEAD_DIM - MLA_ROPE),
                              _pad_heads_cols(naq), _pad_heads_cols(nak), _pad_heads_cols(nav), rest],
                             axis=1).astype(BF16)

    qk_dim = HEAD_DIM + MLA_ROPE
    wuq = mla_w_uq[l].reshape(MLA_Q_LORA, N_HEADS, qk_dim)
    wuq = jnp.concatenate([wuq, jnp.zeros((MLA_Q_LORA, N_HEADS, LANES - qk_dim), F32)], axis=-1)
    wuq = jnp.concatenate([wuq.reshape(MLA_Q_LORA, PAD_W), jnp.zeros((GROUP_W - MLA_Q_LORA, PAD_W), F32)], axis=0)
    wukv = mla_w_ukv[l].reshape(MLA_KV_LORA, N_HEADS, 2 * HEAD_DIM)
    pad64 = jnp.zeros((MLA_KV_LORA, N_HEADS, HEAD_DIM), F32)
    wk = jnp.concatenate([wukv[:, :, :HEAD_DIM], pad64], axis=-1).reshape(MLA_KV_LORA, PAD_W)
    wv = jnp.concatenate([wukv[:, :, HEAD_DIM:], pad64], axis=-1).reshape(MLA_KV_LORA, PAD_W)

    def per_head(parts):
        row = jnp.concatenate(parts + [jnp.zeros((LANES - sum(p.shape[0] for p in parts),), F32)])
        return jnp.tile(row, N_HEADS)[None, :]

    prep = {
        'wuq': wuq.astype(BF16), 'wk': wk.astype(BF16), 'wv': wv.astype(BF16),
        'gcq': jnp.concatenate([mla_g_cq[l], jnp.zeros((GROUP_W - MLA_Q_LORA,), F32)])[None, :],
        'gckv': mla_g_ckv[l][None, :],
        'gkr': jnp.concatenate([jnp.zeros((HEAD_DIM,), F32), mla_g_kr[l],
                                jnp.zeros((LANES - HEAD_DIM - MLA_ROPE,), F32)])[None, :],
        'gq': per_head([mla_g_qn[l], mla_g_qr[l]]),
        'gk': per_head([mla_g_kn[l]]),
        'mq': _seg_mean_matrix(PAD_W, [(0, HEAD_DIM), (HEAD_DIM, MLA_ROPE)]),
        'mk': _seg_mean_matrix(PAD_W, [(0, HEAD_DIM)]),
        'gnq': per_head([na_g_q[l]]),
        'gnk': per_head([na_g_k[l]]),
        'mn': _seg_mean_matrix(PAD_W, [(0, HEAD_DIM)]),
    }
    wo = w_out[l]
    ow = {
        'mla': _pad_heads_rows(wo[0:GROUP_W]).astype(BF16),
        'na': _pad_heads_rows(wo[GROUP_W:2 * GROUP_W]).astype(BF16),
        'ret': wo[2 * GROUP_W:3 * GROUP_W].astype(BF16),
        'hg': wo[3 * GROUP_W:4 * GROUP_W].astype(BF16),
    }
    n_r = MOE_GROUPS + MOE_EXPERTS
    wr = jnp.concatenate([moe_w_rg[l], moe_w_re[l], jnp.zeros((d, LANES - n_r), F32)], axis=1)
    wr_hi = wr.astype(BF16)
    rw = {
        'hi': wr_hi, 'lo': (wr - wr_hi.astype(F32)).astype(BF16),
        'b': jnp.concatenate([moe_b_rg[l], moe_b_re[l], jnp.zeros((LANES - n_r,), F32)])[None, :],
    }
    return w_in_p, prep, ow, rw


def _layer(xx, modsel, lw, rope_c, rope_s, na_bias, rc, hc, hg_lb_l, ret_go, hg_go, layer, w1, w3, w2,
           n_lat, n_ctx, last):
    w_in_p, prep_w, ow, rw = lw
    b, nt, d = xx.shape
    n_lat_tiles = n_lat // TM
    p, qm, km, vm, qn, kn, vn = _inproj_prep(xx, modsel, w_in_p, rope_c, rope_s, prep_w, n_lat_tiles)
    y_mla = _mla_attn(qm, km, vm, n_lat, n_ctx)
    y_na = _na_attn(qn, kn, vn, na_bias, n_lat, n_ctx)
    y_ret, y_hg = _recurrent_mixers(p, rc, hc, hg_lb_l, ret_go, hg_go, n_lat, n_ctx)
    xx, route, counts = _outproj_router(xx, y_mla, y_na, y_ret, y_hg, modsel, ow, rw, n_lat_tiles)
    dest, block_expert, used, n_blocks = _moe_plan(route, counts, b * nt)
    x_rows = _dispatch(xx, modsel, dest, n_blocks * MOE_BLOCK, n_lat_tiles)
    y_rows = _moe_ffn(x_rows, block_expert, used, n_blocks, layer, w1, w3, w2)
    return _combine(xx, modsel, route, y_rows, dest, n_lat_tiles, n_lat_tiles if last else nt // TM)


def kernel(x, c, ctx, c_ctx, w_ada, b_ada, w_in, w_out, mla_g_cq, mla_g_ckv, mla_w_uq, mla_w_ukv, mla_g_qn, mla_g_qr, mla_g_kn, mla_g_kr, na_g_q, na_g_k, na_rpb, ret_g_out, hg_lb_raw, hg_g_out, moe_w_rg, moe_b_rg, moe_w_re, moe_b_re, moe_w1, moe_w3, moe_w2):
    b, n_lat, d = x.shape
    n_ctx = ctx.shape[1]
    depth = w_in.shape[0]
    assert n_lat % TM == 0 and n_ctx % TM == 0 and TM % GRID_W == 0
    assert n_lat // TM >= 3 and n_lat // GRID_W >= NA_WIN_ROWS
    assert w_in.shape[2] == MLA_Q_LORA + MLA_KV_LORA + MLA_ROPE + 12 * GROUP_W

    cc = jnp.concatenate([c, c_ctx[None, :], jnp.zeros((16 - b - 1, d), F32)], axis=0)
    mods = _ada_all(cc, w_ada, b_ada)
    rope_c, rope_s = _rope_tables(n_lat, n_ctx)
    rc = _retention_consts()
    hc = _hgrn_consts()
    lb_w = jax.nn.softmax(hg_lb_raw.astype(F32), axis=0)
    hg_lb = jnp.cumsum(lb_w, axis=0) - lb_w[0:1]

    xx = jnp.concatenate([x, ctx], axis=1)
    tile_go = lambda g: jnp.tile(g, N_HEADS)[None, :]
    for l in range(depth):
        modsel = jnp.stack([mods[l, :b], jnp.broadcast_to(mods[l, b], (b, 6 * d))], axis=1).reshape(2 * b, 1, 6 * d)
        lw = _layer_weights(l, w_in, w_out, mla_g_cq, mla_g_ckv, mla_w_uq, mla_w_ukv, mla_g_qn, mla_g_qr,
                            mla_g_kn, mla_g_kr, na_g_q, na_g_k, moe_w_rg, moe_b_rg, moe_w_re, moe_b_re)
        xx = _layer(xx, modsel, lw, rope_c, rope_s, _na_bias_table(na_rpb[l]), rc, hc, hg_lb[l][None, :],
                    tile_go(ret_g_out[l]), tile_go(hg_g_out[l]),
                    l, moe_w1, moe_w3, moe_w2,
                    n_lat, n_ctx, l == depth - 1)
    return xx
```

```python
import functools

import numpy as np
import jax
import jax.numpy as jnp
from jax import lax
from jax.experimental import pallas as pl
from jax.experimental.pallas import tpu as pltpu

F32 = jnp.float32
BF16 = jnp.bfloat16

EPS = 1e-6
ROPE_THETA = 10000.0
NEG_BIG = -1e30
F_FLOOR = 1e-20
GRID_W = 64
N_HEADS = 4
HEAD_DIM = 64
LANES = 128
GROUP_W = N_HEADS * HEAD_DIM
PAD_W = N_HEADS * LANES
MLA_Q_LORA = 192
MLA_KV_LORA = 128
MLA_ROPE = 32
MLA_SCALE = (HEAD_DIM + MLA_ROPE) ** -0.5
LOG2E = 1.4426950408889634
NA_KH = 8
NA_KW = 16
NA_SCALE = HEAD_DIM ** -0.5
RET_CHUNK = 128
HG_CHUNK = 16
MOE_GROUPS = 4
MOE_PER_GROUP = 8
MOE_EXPERTS = MOE_GROUPS * MOE_PER_GROUP
MOE_TOPK = 2
MOE_BLOCK = 512
TM = 256
NA_TILE_ROWS = TM // GRID_W
NA_WIN_ROWS = NA_TILE_ROWS + NA_KH
VMEM_LIMIT = 56 * 1024 * 1024
DMA_ISSUE_UNROLL = 8

P_ATTN_COLS = 2 * GROUP_W + 3 * PAD_W
COL_RET_Q, COL_RET_K, COL_RET_V, COL_RET_G = 0, 1, 2, 3
COL_HG_Q, COL_HG_FF, COL_HG_FB, COL_HG_I, COL_HG_G = 4, 5, 6, 7, 8


def _cparams(sem):
    return pltpu.CompilerParams(dimension_semantics=sem, vmem_limit_bytes=VMEM_LIMIT)


def _sigmoid(x):
    return 1.0 / (1.0 + jnp.exp(-x))


def _silu(x):
    return x * _sigmoid(x)


def _dot(a, b):
    return jnp.dot(a, b, preferred_element_type=F32)


def _dot_nt(a, b):
    return lax.dot_general(a, b, (((1,), (1,)), ((), ())), preferred_element_type=F32)


def _split_dot_l(x, m, n):
    acc = None
    rem = x
    for i in range(n):
        piece = rem.astype(BF16)
        d = _dot(piece, m)
        acc = d if acc is None else acc + d
        if i + 1 < n:
            rem = rem - piece.astype(F32)
    return acc


def _split_dot_r(m, x, n):
    acc = None
    rem = x
    for i in range(n):
        piece = rem.astype(BF16)
        d = _dot(m, piece)
        acc = d if acc is None else acc + d
        if i + 1 < n:
            rem = rem - piece.astype(F32)
    return acc


def _pack_bf16_pairs(x):
    half = x.shape[1] // 2
    bits = lax.bitcast_convert_type(x.astype(BF16).astype(F32), jnp.uint32)
    return (bits[:, :half] >> 16) | (bits[:, half:] & jnp.uint32(0xFFFF0000))


def _unpack_bf16_pairs(p, dtype=BF16):
    lo = lax.bitcast_convert_type(p << 16, F32)
    hi = lax.bitcast_convert_type(p & jnp.uint32(0xFFFF0000), F32)
    return jnp.concatenate([lo, hi], axis=1).astype(dtype)


def _seg_rms(x, m, gain):
    return x * lax.rsqrt(_split_dot_l(x * x, m, 2) + EPS) * gain


def _ada_kernel(c_ref, w_ref, b_ref, o_ref):
    s = _silu(c_ref[...])
    o_ref[0] = jnp.dot(s, w_ref[0], preferred_element_type=F32,
                       precision=lax.Precision.HIGHEST) + b_ref[0]


def _ada_all(cc, w_ada, b_ada):
    n_layers, d, d6 = w_ada.shape
    bn = 512
    rows = cc.shape[0]
    return pl.pallas_call(
        _ada_kernel,
        grid=(n_layers, d6 // bn),
        in_specs=[pl.BlockSpec((rows, d), lambda l, j: (0, 0)),
                  pl.BlockSpec((1, d, bn), lambda l, j: (l, 0, j)),
                  pl.BlockSpec((1, 1, bn), lambda l, j: (l, 0, j))],
        out_specs=pl.BlockSpec((1, rows, bn), lambda l, j: (l, 0, j)),
        out_shape=jax.ShapeDtypeStruct((n_layers, rows, d6), F32),
        compiler_params=_cparams(("arbitrary", "arbitrary")),
    )(cc, w_ada, b_ada.reshape(n_layers, 1, d6))


def _mod_spec(d6, n_lat_tiles):
    return pl.BlockSpec((1, 1, d6), lambda b, j: (2 * b + (j >= n_lat_tiles).astype(jnp.int32), 0, 0))


def _modulate(x, shift, scale):
    xn = x * lax.rsqrt(jnp.mean(x * x, axis=-1, keepdims=True) + EPS)
    return xn * (1.0 + scale) + shift


def _inproj_prep_kernel(x_ref, mod_ref, w_ref, c_ref, s_ref, wuq_ref, wk_ref, wv_ref, gcq_ref, gckv_ref, gkr_ref,
                        gq_ref, gk_ref, mq_ref, mk_ref, gnq_ref, gnk_ref, mn_ref,
                        p_ref, qm_ref, km_ref, vm_ref, qn_ref, kn_ref, vn_ref, *, d):
    xm = _modulate(x_ref[0], mod_ref[0, :, 0:d], mod_ref[0, :, d:2 * d]).astype(BF16)
    p_ref[0] = _dot(xm, w_ref[:, P_ATTN_COLS:])
    pa = _dot(xm, w_ref[:, 0:P_ATTN_COLS])

    cq = pa[:, 0:256]
    ckv = pa[:, 256:384]
    kr = pa[:, 384:512]
    cqn = cq * lax.rsqrt(jnp.sum(cq * cq, axis=-1, keepdims=True) * (1.0 / MLA_Q_LORA) + EPS) * gcq_ref[...]
    ckvn = (ckv * lax.rsqrt(jnp.mean(ckv * ckv, axis=-1, keepdims=True) + EPS) * gckv_ref[...]).astype(BF16)
    krn = kr * lax.rsqrt(jnp.sum(kr * kr, axis=-1, keepdims=True) * (1.0 / MLA_ROPE) + EPS) * gkr_ref[...]
    q = _seg_rms(_dot(cqn.astype(BF16), wuq_ref[...]), mq_ref[...], gq_ref[...])
    kk = _seg_rms(_dot(ckvn, wk_ref[...]), mk_ref[...], gk_ref[...])
    vv = _dot(ckvn, wv_ref[...])

    cos = c_ref[...]
    sin = s_ref[...]
    lane = lax.broadcasted_iota(jnp.int32, (TM, LANES), 1)
    first = (lane % 16) < 8

    def rope(x):
        partner = jnp.where(first, pltpu.roll(x, LANES - 8, 1), pltpu.roll(x, 8, 1))
        return x * cos + partner * sin

    krr = rope(krn)
    nq = _seg_rms(pa[:, 512:1024], mn_ref[...], gnq_ref[...])
    nk = _seg_rms(pa[:, 1024:1536], mn_ref[...], gnk_ref[...])
    for h in range(N_HEADS):
        sl = slice(h * LANES, (h + 1) * LANES)
        qm_ref[0, h] = (rope(q[:, sl]) * (MLA_SCALE * LOG2E)).astype(BF16)
        km_ref[0, h] = (kk[:, sl] + krr).astype(BF16)
        vm_ref[0, h] = vv[:, sl].T.astype(BF16)
        qn_ref[0, h] = (nq[:, sl] * (NA_SCALE * LOG2E)).astype(BF16)
        kn_ref[0, h] = nk[:, sl].astype(BF16)
        vn_ref[0, h] = pa[:, 1536 + h * LANES:1536 + (h + 1) * LANES].astype(BF16)


def _inproj_prep(xx, modsel, w_in_p, rope_c, rope_s, pw, n_lat_tiles):
    b, nt, d = xx.shape
    rest = w_in_p.shape[1] - P_ATTN_COLS
    full = lambda a: pl.BlockSpec(a.shape, lambda i, j: (0,) * a.ndim)
    consts = [pw['wuq'], pw['wk'], pw['wv'], pw['gcq'], pw['gckv'], pw['gkr'], pw['gq'], pw['gk'],
              pw['mq'], pw['mk'], pw['gnq'], pw['gnk'], pw['mn']]
    head_spec = pl.BlockSpec((1, N_HEADS, TM, LANES), lambda i, j: (i, 0, j, 0))
    head_shape = jax.ShapeDtypeStruct((b, N_HEADS, nt, LANES), BF16)
    head_t_spec = pl.BlockSpec((1, N_HEADS, LANES, TM), lambda i, j: (i, 0, 0, j))
    head_t_shape = jax.ShapeDtypeStruct((b, N_HEADS, LANES, nt), BF16)
    return pl.pallas_call(
        functools.partial(_inproj_prep_kernel, d=d),
        grid=(b, nt // TM),
        in_specs=[pl.BlockSpec((1, TM, d), lambda i, j: (i, j, 0)),
                  _mod_spec(6 * d, n_lat_tiles),
                  full(w_in_p),
                  pl.BlockSpec((TM, LANES), lambda i, j: (j, 0)),
                  pl.BlockSpec((TM, LANES), lambda i, j: (j, 0))] + [full(a) for a in consts],
        out_specs=[pl.BlockSpec((1, TM, rest), lambda i, j: (i, j, 0)),
                   head_spec, head_spec, head_t_spec, head_spec, head_spec, head_spec],
        out_shape=[jax.ShapeDtypeStruct((b, nt, rest), F32),
                   head_shape, head_shape, head_t_shape, head_shape, head_shape, head_shape],
        compiler_params=_cparams(("arbitrary", "arbitrary")),
    )(xx, modsel, w_in_p, rope_c, rope_s, *consts)


def _softmax2_pv(s, v):
    m = jnp.max(s, axis=-1, keepdims=True)
    e = jnp.exp2(s - m)
    l = jnp.sum(e, axis=-1, keepdims=True)
    return _dot(e.astype(BF16), v) / l


MLA_KEY_BLOCK = 2176
MLA_Q_TILE = 512


MLA_HEADS_PER_STEP = 4


def _mla_attend(q_ref, k_ref, vt_ref, o_ref, n_q, k0, k1):
    items = [(hh, s0) for hh in range(MLA_HEADS_PER_STEP) for s0 in range(k0, k1, MLA_KEY_BLOCK)]
    score = lambda hh, s0: _dot_nt(k_ref[0, hh, s0:min(s0 + MLA_KEY_BLOCK, k1), :], q_ref[0, hh, 0:n_q, :])
    st = score(*items[0])
    m = l = acc = None
    for i, (hh, s0) in enumerate(items):
        st_next = score(*items[i + 1]) if i + 1 < len(items) else None
        vt_blk = vt_ref[0, hh, :, s0:min(s0 + MLA_KEY_BLOCK, k1)]
        bm = jnp.max(st, axis=0, keepdims=True)
        if s0 == k0:
            m = bm
            e = jnp.exp2(st - m)
            l = jnp.sum(e, axis=0, keepdims=True)
            acc = _dot(vt_blk, e.astype(BF16))
        else:
            m_new = jnp.maximum(m, bm)
            alpha = jnp.exp2(m - m_new)
            e = jnp.exp2(st - m_new)
            l = l * alpha + jnp.sum(e, axis=0, keepdims=True)
            acc = acc * alpha + _dot(vt_blk, e.astype(BF16))
            m = m_new
        if s0 + MLA_KEY_BLOCK >= k1:
            o_ref[0, 0:n_q, hh * LANES:(hh + 1) * LANES] = (acc / l).T
        st = st_next


def _mla_kernel(q_ref, k_ref, vt_ref, o_ref, *, n_lat, n_ctx):
    j = pl.program_id(2)

    @pl.when(j < n_lat // MLA_Q_TILE)
    def _():
        _mla_attend(q_ref, k_ref, vt_ref, o_ref, MLA_Q_TILE, 0, n_lat + n_ctx)

    @pl.when(j >= n_lat // MLA_Q_TILE)
    def _():
        _mla_attend(q_ref, k_ref, vt_ref, o_ref, n_ctx, n_lat, n_lat + n_ctx)


def _mla_attn(qm, km, vmt, n_lat, n_ctx):
    b, h, nt, _ = qm.shape
    hp = MLA_HEADS_PER_STEP
    tq = MLA_Q_TILE
    assert n_lat % tq == 0 and n_ctx <= tq
    kv_spec = pl.BlockSpec((1, hp, nt, LANES), lambda i, hh, j: (i, hh, 0, 0))
    vt_spec = pl.BlockSpec((1, hp, LANES, nt), lambda i, hh, j: (i, hh, 0, 0))
    return pl.pallas_call(
        functools.partial(_mla_kernel, n_lat=n_lat, n_ctx=n_ctx),
        grid=(b, h // hp, n_lat // tq + 1),
        in_specs=[pl.BlockSpec((1, hp, tq, LANES), lambda i, hh, j: (i, hh, j, 0)), kv_spec, vt_spec],
        out_specs=pl.BlockSpec((1, tq, hp * LANES), lambda i, hh, j: (i, j, hh)),
        out_shape=jax.ShapeDtypeStruct((b, nt, PAD_W), F32),
        compiler_params=_cparams(("arbitrary", "arbitrary", "arbitrary")),
    )(qm, km, vmt)


def _na_kernel(q_ref, k_ref, v_ref, bias_ref, o_ref, *, n_lat, n_ctx):
    j = pl.program_id(1)
    rows = n_lat // GRID_W
    n_tiles = n_lat // TM
    win = NA_WIN_ROWS * GRID_W

    @pl.when(j < n_tiles)
    def _():
        start = jnp.clip(j * NA_TILE_ROWS - NA_KH // 2, 0, rows - NA_WIN_ROWS)
        case = jnp.where(j == 0, 0, jnp.where(j == n_tiles - 1, 2, 1))
        tok0 = pl.multiple_of(start * GRID_W, GRID_W)
        def scores(h):
            q = q_ref[0, h]
            return (_dot_nt(q, k_ref[0, h, pl.ds(tok0, win), :]) + bias_ref[case, h],
                    _dot_nt(q, k_ref[0, h, pl.ds(n_lat, n_ctx), :]))

        nxt = scores(0)
        for h in range(N_HEADS):
            s1, s2 = nxt
            if h + 1 < N_HEADS:
                nxt = scores(h + 1)
            m = jnp.maximum(jnp.max(s1, axis=-1, keepdims=True), jnp.max(s2, axis=-1, keepdims=True))
            e1 = jnp.exp2(s1 - m)
            e2 = jnp.exp2(s2 - m)
            l = jnp.sum(e1, axis=-1, keepdims=True) + jnp.sum(e2, axis=-1, keepdims=True)
            o = _dot(e1.astype(BF16), v_ref[0, h, pl.ds(tok0, win), :])
            o = o + _dot(e2.astype(BF16), v_ref[0, h, pl.ds(n_lat, n_ctx), :])
            o_ref[0, :, h * LANES:(h + 1) * LANES] = o / l

    @pl.when(j >= n_lat // TM)
    def _():
        for h in range(N_HEADS):
            s = _dot_nt(q_ref[0, h], k_ref[0, h, pl.ds(n_lat, n_ctx), :])
            o_ref[0, :, h * LANES:(h + 1) * LANES] = _softmax2_pv(s, v_ref[0, h, pl.ds(n_lat, n_ctx), :])


def _na_attn(qn, kn, vn, bias, n_lat, n_ctx):
    b, h, nt, _ = qn.shape
    kv_spec = pl.BlockSpec((1, h, nt, LANES), lambda i, j: (i, 0, 0, 0))
    return pl.pallas_call(
        functools.partial(_na_kernel, n_lat=n_lat, n_ctx=n_ctx),
        grid=(b, nt // TM),
        in_specs=[pl.BlockSpec((1, h, TM, LANES), lambda i, j: (i, 0, j, 0)), kv_spec, kv_spec,
                  pl.BlockSpec(bias.shape, lambda i, j: (0, 0, 0, 0))],
        out_specs=pl.BlockSpec((1, TM, PAD_W), lambda i, j: (i, j, 0)),
        out_shape=jax.ShapeDtypeStruct((b, nt, PAD_W), F32),
        compiler_params=_cparams(("arbitrary", "arbitrary")),
    )(qn, kn, vn, bias)


def _head_mask(h, shape):
    return (lax.broadcasted_iota(jnp.int32, shape, 1) // HEAD_DIM) == h


def _ret_state_step(s_ref, q, k, v, qw, kw, cd, bd):
    state = s_ref[...]
    o = _dot((q * qw).astype(BF16), state.astype(BF16))
    upd = _dot((k * kw).T.astype(BF16), v.astype(BF16))
    s_ref[...] = state * cd + upd * bd
    return o


def _ret_fwd_body(q_ref, k_ref, v_ref, dm_ref, qw_ref, kw_ref, cd_ref, bd_ref, o_ref, s_ref):
    for c in range(TM // RET_CHUNK):
        rows = slice(c * RET_CHUNK, (c + 1) * RET_CHUNK)
        q = q_ref[0, rows]
        k = k_ref[0, rows]
        v = v_ref[0, rows]
        o = _ret_state_step(s_ref, q, k, v, qw_ref[...], kw_ref[...], cd_ref[...], bd_ref[...])
        kb = k.astype(BF16)
        for h in range(N_HEADS):
            hm = _head_mask(h, q.shape)
            sc = _dot_nt(jnp.where(hm, q, 0.0).astype(BF16), kb) * dm_ref[h]
            o = o + _dot(sc.astype(BF16), jnp.where(hm, v, 0.0).astype(BF16))
        o_ref[0, rows] = o


def _ret_bwd_body(q_ref, k_ref, v_ref, g_ref, op_ref, qw_ref, kw_ref, cd_ref, bd_ref, ms_ref, go_ref,
                  y_ref, s_ref):
    for c in reversed(range(TM // RET_CHUNK)):
        rows = slice(c * RET_CHUNK, (c + 1) * RET_CHUNK)
        o = op_ref[0, rows] + _ret_state_step(s_ref, q_ref[0, rows], k_ref[0, rows], v_ref[0, rows], qw_ref[...],
                                              kw_ref[...], cd_ref[...], bd_ref[...])
        y_ref[0, rows] = _seg_rms(o, ms_ref[...], go_ref[...]) * _silu(g_ref[0, rows])


def _scan_order(n_lat_t, n_ctx_t, reverse):
    if reverse:
        return lambda i: jnp.where(i < n_ctx_t, n_lat_t + n_ctx_t - 1 - i, n_lat_t + n_ctx_t - 1 - i)
    return lambda i: jnp.where(i < n_ctx_t, n_lat_t + i, i - n_ctx_t)


def _hg_direction(q_ref, f_ref, v_ref, lb_ref, ain_ref, aex_ref, bseg_ref, bd_ref, st_ref, sh_ref, *, reverse):
    n_chunks = TM // HG_CHUNK
    assert n_chunks == HG_CHUNK
    qh = _silu(q_ref[0])
    lb = lb_ref[...]
    f = jnp.maximum(lb + (1.0 - lb) * _sigmoid(f_ref[0]), F_FLOOR)
    lf = jnp.log(f) * LOG2E
    k = 1.0 - f
    v = v_ref[0]
    row = lax.broadcasted_iota(jnp.int32, (TM, 1), 0)
    pos = row % HG_CHUNK
    row_chunk = row // HG_CHUNK

    a_in = _split_dot_r(ain_ref[...], lf, 3)
    a_ex = _split_dot_r(aex_ref[...], lf, 3)
    width = v.shape[1]
    key_exp = a_in - jnp.log(k) * LOG2E
    for slot, val in enumerate((key_exp, v)):
        sh_ref[slot] = val.reshape(n_chunks, HG_CHUNK, width)

    def key_row(slot, s):
        return jnp.broadcast_to(sh_ref[slot, :, s:s + 1, :], (n_chunks, HG_CHUNK, width)).reshape(TM, width)
    qp = (qh * jnp.exp2(a_in)).astype(BF16)
    kdec = k * jnp.exp2(a_ex)
    lam_all = jnp.exp2(a_in + a_ex)
    vt = v.T.astype(BF16)
    bd = bd_ref[...]
    bseg = bseg_ref[...]
    state = st_ref[...]
    parts = [None] * n_chunks
    o_band = jnp.zeros_like(v)
    upds = [_dot(vt, jnp.where(row_chunk == c, kdec, 0.0).astype(BF16)) * bd for c in range(n_chunks)]
    for step in range(n_chunks):
        c = n_chunks - 1 - step if reverse else step
        r0 = c * HG_CHUNK
        parts[c] = _dot_nt(qp[r0:r0 + HG_CHUNK], state.astype(BF16))
        state = state * lam_all[r0:r0 + 1] + upds[c]
        s = step
        valid = (pos <= s) if reverse else (pos >= s)
        w = jnp.where(valid, qh * jnp.exp2(a_in - key_row(0, s)), 0.0)
        o_band = o_band + _dot(w.astype(BF16), bseg) * key_row(1, s)
    st_ref[...] = state
    return jnp.concatenate(parts, axis=0) + o_band


def _scan_fwd_kernel(rq_ref, rk_ref, rv_ref, hq_ref, hf_ref, hv_ref, dm_ref, qw_ref, kw_ref, cd_ref, bd_ref,
                     lb_ref, ain_ref, aex_ref, bseg_ref, ro_ref, ho_ref, rs_ref, hs_ref, sh_ref):
    @pl.when(pl.program_id(1) == 0)
    def _():
        rs_ref[...] = jnp.zeros_like(rs_ref)
        hs_ref[...] = jnp.zeros_like(hs_ref)

    _ret_fwd_body(rq_ref, rk_ref, rv_ref, dm_ref, qw_ref, kw_ref, cd_ref, bd_ref, ro_ref, rs_ref)
    ho_ref[0] = _hg_direction(hq_ref, hf_ref, hv_ref, lb_ref, ain_ref, aex_ref, bseg_ref, bd_ref, hs_ref, sh_ref,
                              reverse=False)


def _scan_bwd_kernel(rq_ref, rk_ref, rv_ref, rg_ref, rop_ref, hq_ref, hf_ref, hv_ref, hg_ref, hop_ref,
                     qw_ref, kw_ref, cd_ref, bd_ref, ms_ref, rgo_ref, lb_ref, ain_ref, aex_ref, bseg_ref, hgo_ref,
                     ry_ref, hy_ref, rs_ref, hs_ref, sh_ref):
    @pl.when(pl.program_id(1) == 0)
    def _():
        rs_ref[...] = jnp.zeros_like(rs_ref)
        hs_ref[...] = jnp.zeros_like(hs_ref)

    _ret_bwd_body(rq_ref, rk_ref, rv_ref, rg_ref, rop_ref, qw_ref, kw_ref, cd_ref, bd_ref, ms_ref, rgo_ref,
                  ry_ref, rs_ref)
    o = hop_ref[0] + _hg_direction(hq_ref, hf_ref, hv_ref, lb_ref, ain_ref, aex_ref, bseg_ref, bd_ref, hs_ref,
                                   sh_ref, reverse=True)
    hy_ref[0] = _seg_rms(o, ms_ref[...], hgo_ref[...]) * _silu(hg_ref[0])


def _recurrent_mixers(p, rc, hc, lb, ret_go, hg_go, n_lat, n_ctx):
    b, nt, _ = p.shape
    n_lat_t, n_ctx_t = n_lat // TM, n_ctx // TM
    fwd = _scan_order(n_lat_t, n_ctx_t, False)
    bwd = _scan_order(n_lat_t, n_ctx_t, True)
    col = lambda order, cb: pl.BlockSpec((1, TM, GROUP_W), lambda i, j: (i, order(j), cb))
    full = lambda a: pl.BlockSpec(a.shape, lambda i, j: (0,) * a.ndim)
    out_shape = jax.ShapeDtypeStruct((b, nt, GROUP_W), F32)
    scratch = [pltpu.VMEM((GROUP_W, GROUP_W), F32), pltpu.VMEM((GROUP_W, GROUP_W), F32),
               pltpu.VMEM((2, TM // HG_CHUNK, HG_CHUNK, GROUP_W), F32)]
    consts_f = [rc['dm'], rc['qw_f'], rc['kw_f'], rc['cd_f'], rc['bd'], lb, hc['lincl'], hc['uexcl'], hc['bseg']]
    out_f = pl.BlockSpec((1, TM, GROUP_W), lambda i, j: (i, fwd(j), 0))
    ret_part, hg_part = pl.pallas_call(
        _scan_fwd_kernel,
        grid=(b, nt // TM),
        in_specs=[col(fwd, COL_RET_Q), col(fwd, COL_RET_K), col(fwd, COL_RET_V),
                  col(fwd, COL_HG_Q), col(fwd, COL_HG_FF), col(fwd, COL_HG_I)] + [full(a) for a in consts_f],
        out_specs=[out_f, out_f],
        out_shape=[out_shape, out_shape],
        scratch_shapes=scratch,
        compiler_params=_cparams(("arbitrary", "arbitrary")),
    )(p, p, p, p, p, p, *consts_f)
    consts_b = [rc['qw_b'], rc['kw_b'], rc['cd_b'], rc['bd'], rc['ms'], ret_go,
                lb, hc['uincl'], hc['lexcl'], hc['bseg'], hg_go]
    out_b = pl.BlockSpec((1, TM, GROUP_W), lambda i, j: (i, bwd(j), 0))
    return pl.pallas_call(
        _scan_bwd_kernel,
        grid=(b, nt // TM),
        in_specs=[col(bwd, COL_RET_Q), col(bwd, COL_RET_K), col(bwd, COL_RET_V), col(bwd, COL_RET_G), out_b,
                  col(bwd, COL_HG_Q), col(bwd, COL_HG_FB), col(bwd, COL_HG_I), col(bwd, COL_HG_G), out_b]
                 + [full(a) for a in consts_b],
        out_specs=[out_b, out_b],
        out_shape=[out_shape, out_shape],
        scratch_shapes=scratch,
        compiler_params=_cparams(("arbitrary", "arbitrary")),
    )(p, p, p, p, ret_part, p, p, p, p, hg_part, *consts_b)


def _outproj_router_kernel(x_ref, ym_ref, yn_ref, yr_ref, yh_ref, mod_ref, wm_ref, wn_ref, wr_ref, wh_ref,
                           whl_ref, br_ref, ltri_ref, o_ref, r_ref, cnt_ref, *, d):
    acc = _dot(ym_ref[0].astype(BF16), wm_ref[...])
    acc = acc + _dot(yn_ref[0].astype(BF16), wn_ref[...])
    acc = acc + _dot(yr_ref[0].astype(BF16), wr_ref[...])
    acc = acc + _dot(yh_ref[0].astype(BF16), wh_ref[...])
    x_new = x_ref[0] + mod_ref[0, :, 2 * d:3 * d] * acc
    o_ref[0] = x_new
    h = _modulate(x_new, mod_ref[0, :, 3 * d:4 * d], mod_ref[0, :, 4 * d:5 * d])
    _route(h, whl_ref, br_ref, ltri_ref, r_ref, cnt_ref)


def _outproj_router(xx, y_mla, y_na, y_ret, y_hg, modsel, ow, rw, n_lat_tiles):
    b, nt, d = xx.shape
    tiles = nt // TM
    tile = lambda w: pl.BlockSpec((1, TM, w), lambda i, j: (i, j, 0))
    full = lambda a: pl.BlockSpec(a.shape, lambda i, j: (0, 0))
    ltri = jnp.asarray(np.tril(np.ones((TM, TM), np.float32), -1), BF16)
    ws = [ow['mla'], ow['na'], ow['ret'], ow['hg'], rw['hi_lo'], rw['b'], ltri]
    return pl.pallas_call(
        functools.partial(_outproj_router_kernel, d=d),
        grid=(b, tiles),
        in_specs=[tile(d), tile(PAD_W), tile(PAD_W), tile(GROUP_W), tile(GROUP_W),
                  _mod_spec(6 * d, n_lat_tiles)] + [full(a) for a in ws],
        out_specs=[tile(d), pl.BlockSpec((TM, LANES), lambda i, j: (i * tiles + j, 0)),
                   pl.BlockSpec((1, LANES), lambda i, j: (0, 0))],
        out_shape=[jax.ShapeDtypeStruct((b, nt, d), F32), jax.ShapeDtypeStruct((b * nt, LANES), F32),
                   jax.ShapeDtypeStruct((1, LANES), F32)],
        compiler_params=_cparams(("arbitrary", "arbitrary")),
    )(xx, y_mla, y_na, y_ret, y_hg, modsel, *ws)


def _route(h, whl_ref, br_ref, ltri_ref, r_ref, cnt_ref):
    @pl.when((pl.program_id(0) == 0) & (pl.program_id(1) == 0))
    def _():
        cnt_ref[...] = jnp.zeros_like(cnt_ref)

    h_hi = h.astype(BF16)
    h_lo = (h - h_hi.astype(F32)).astype(BF16)
    hi_terms = _dot(h_hi, whl_ref[...])
    lg = hi_terms[:, :LANES] + hi_terms[:, LANES:] + _dot(h_lo, whl_ref[:, :LANES]) + br_ref[...]

    lane = lax.broadcasted_iota(jnp.int32, lg.shape, 1).astype(F32)
    far = 1e9

    def first_argmax(vals, vmax):
        return jnp.min(jnp.where(vals == vmax, lane, far), axis=-1, keepdims=True)

    gl = jnp.where(lane < MOE_GROUPS, lg, NEG_BIG)
    gmax = jnp.max(gl, axis=-1, keepdims=True)
    pg_top = 1.0 / jnp.sum(jnp.exp(gl - gmax), axis=-1, keepdims=True)
    lo = MOE_GROUPS + MOE_PER_GROUP * first_argmax(gl, gmax)
    fl = jnp.where((lane >= lo) & (lane < lo + MOE_PER_GROUP), lg, NEG_BIG)
    fmax = jnp.max(fl, axis=-1, keepdims=True)
    fsum = jnp.sum(jnp.exp(fl - fmax), axis=-1, keepdims=True)
    i1 = first_argmax(fl, fmax)
    fl2 = jnp.where(lane == i1, NEG_BIG, fl)
    f2max = jnp.max(fl2, axis=-1, keepdims=True)
    i2 = first_argmax(fl2, f2max)
    p1 = 1.0 / fsum
    p2 = jnp.exp(f2max - fmax) / fsum
    g1 = pg_top * p1 / (p1 + p2)
    g2 = pg_top * p2 / (p1 + p2)
    e1 = i1 - MOE_GROUPS
    e2 = i2 - MOE_GROUPS

    onehot = jnp.where(lane == e1, 1.0, 0.0) + jnp.where(lane == e2, 1.0, 0.0)
    before = cnt_ref[...] + _dot(ltri_ref[...], onehot.astype(BF16))
    r1 = jnp.sum(jnp.where(lane == e1, before, 0.0), axis=-1, keepdims=True)
    r2 = jnp.sum(jnp.where(lane == e2, before, 0.0), axis=-1, keepdims=True)
    cnt_ref[...] += jnp.sum(onehot, axis=0, keepdims=True)

    out = jnp.zeros_like(lg)
    for col, val in enumerate((e1, e2, g1, g2, r1, r2)):
        out = jnp.where(lane == col, val, out)
    r_ref[...] = out


ROUTE_E, ROUTE_G, ROUTE_R = 0, 2, 4


def _moe_plan(route, counts_f, n_tok):
    counts = counts_f[0, :MOE_EXPERTS].astype(jnp.int32)
    padded = (counts + MOE_BLOCK - 1) // MOE_BLOCK * MOE_BLOCK
    pad_end = jnp.cumsum(padded)
    pad_start = pad_end - padded
    n_blocks = -(-(n_tok * MOE_TOPK) // MOE_BLOCK) + MOE_EXPERTS
    blk0 = jnp.arange(n_blocks, dtype=jnp.int32) * MOE_BLOCK
    block_expert = jnp.minimum(jnp.sum((pad_end[None, :] <= blk0[:, None]).astype(jnp.int32), axis=1),
                               MOE_EXPERTS - 1)
    used = (pad_end[-1] // MOE_BLOCK).reshape(1)
    expert = route[:, ROUTE_E:ROUTE_E + MOE_TOPK].astype(jnp.int32)
    rank = route[:, ROUTE_R:ROUTE_R + MOE_TOPK].astype(jnp.int32)
    start_of = jnp.sum(jnp.where(expert[..., None] == jnp.arange(MOE_EXPERTS, dtype=jnp.int32), pad_start, 0), axis=-1)
    return start_of + rank, block_expert, used, n_blocks


def _idx_blocks(dest, k, n_tiles):
    return dest[:, k].reshape(n_tiles, 1, TM)


def _dispatch_kernel(d0_ref, d1_ref, x_ref, mod_ref, rows_in, rows_out, h_ref, sem, *, d):
    del rows_in
    step = pl.program_id(0) * pl.num_programs(1) + pl.program_id(1)
    n_steps = pl.num_programs(0) * pl.num_programs(1)
    slot = step % 2

    def wait_slot(s):
        for _ in range(MOE_TOPK):
            pltpu.make_async_copy(h_ref.at[s], rows_out.at[pl.ds(0, TM)], sem.at[s]).wait()

    h = _modulate(x_ref[0], mod_ref[0, :, 3 * d:4 * d], mod_ref[0, :, 4 * d:5 * d])
    h_ref[slot] = _pack_bf16_pairs(h)

    def issue(r, carry):
        src = h_ref.at[slot, pl.ds(r, 1)]
        pltpu.make_async_copy(src, rows_out.at[pl.ds(d0_ref[0, 0, r], 1)], sem.at[slot]).start()
        pltpu.make_async_copy(src, rows_out.at[pl.ds(d1_ref[0, 0, r], 1)], sem.at[slot]).start()
        return carry

    lax.fori_loop(0, TM, issue, 0, unroll=DMA_ISSUE_UNROLL)

    @pl.when(step > 0)
    def _():
        wait_slot(1 - slot)

    @pl.when(step == n_steps - 1)
    def _():
        wait_slot(slot)


def _dispatch(xx, modsel, dest, n_rows, n_lat_tiles):
    b, nt, d = xx.shape
    tiles = nt // TM
    idx_spec = pl.BlockSpec((1, 1, TM), lambda i, j: (i * tiles + j, 0, 0), memory_space=pltpu.SMEM)
    return pl.pallas_call(
        functools.partial(_dispatch_kernel, d=d),
        grid=(b, tiles),
        in_specs=[idx_spec, idx_spec, pl.BlockSpec((1, TM, d), lambda i, j: (i, j, 0)),
                  _mod_spec(6 * d, n_lat_tiles), pl.BlockSpec(memory_space=pl.ANY)],
        out_specs=pl.BlockSpec(memory_space=pl.ANY),
        out_shape=jax.ShapeDtypeStruct((n_rows, d // 2), jnp.uint32),
        scratch_shapes=[pltpu.VMEM((2, TM, d // 2), jnp.uint32), pltpu.SemaphoreType.DMA((2,))],
        input_output_aliases={4: 0},
        compiler_params=_cparams(("arbitrary", "arbitrary")),
    )(_idx_blocks(dest, 0, b * tiles), _idx_blocks(dest, 1, b * tiles), xx, modsel,
      jnp.zeros((n_rows, d // 2), jnp.uint32))


def _ffn_kernel(be_ref, used_ref, x_ref, w1_ref, w3_ref, w2_ref, y_ref, w1b_ref, w3b_ref, w2b_ref):
    i = pl.program_id(0)

    @pl.when((i == 0) | (be_ref[i] != be_ref[jnp.maximum(i - 1, 0)]))
    def _():
        w1b_ref[...] = w1_ref[0, 0].astype(BF16)
        w3b_ref[...] = w3_ref[0, 0].astype(BF16)
        w2b_ref[...] = w2_ref[0, 0].astype(BF16)

    @pl.when(i < used_ref[0])
    def _():
        x = _unpack_bf16_pairs(x_ref[...])
        mid = _silu(_dot(x, w1b_ref[...])) * _dot(x, w3b_ref[...])
        y_ref[...] = _pack_bf16_pairs(_dot(mid.astype(BF16), w2b_ref[...]))

    @pl.when(i >= used_ref[0])
    def _():
        y_ref[...] = jnp.zeros_like(y_ref)


def _moe_ffn(x_rows, block_expert, used, n_blocks, layer, w1, w3, w2):
    d = w1.shape[2]
    ff = w1.shape[3]
    grid_spec = pltpu.PrefetchScalarGridSpec(
        num_scalar_prefetch=2,
        grid=(n_blocks,),
        in_specs=[pl.BlockSpec((MOE_BLOCK, d // 2), lambda i, be, nu: (i, 0)),
                  pl.BlockSpec((1, 1, d, ff), lambda i, be, nu: (layer, be[i], 0, 0)),
                  pl.BlockSpec((1, 1, d, ff), lambda i, be, nu: (layer, be[i], 0, 0)),
                  pl.BlockSpec((1, 1, ff, d), lambda i, be, nu: (layer, be[i], 0, 0))],
        out_specs=pl.BlockSpec((MOE_BLOCK, d // 2), lambda i, be, nu: (i, 0)),
        scratch_shapes=[pltpu.VMEM((d, ff), BF16), pltpu.VMEM((d, ff), BF16), pltpu.VMEM((ff, d), BF16)],
    )
    return pl.pallas_call(
        _ffn_kernel,
        grid_spec=grid_spec,
        out_shape=jax.ShapeDtypeStruct((x_rows.shape[0], d // 2), jnp.uint32),
        compiler_params=_cparams(("arbitrary",)),
    )(block_expert, used, x_rows, w1, w3, w2)


def _row_gather(src_hbm, idx_ref, dst_ref, sem, n):
    def issue(r, carry):
        pltpu.make_async_copy(src_hbm.at[pl.ds(idx_ref[0, 0, r], 1)], dst_ref.at[pl.ds(r, 1)], sem).start()
        return carry

    lax.fori_loop(0, n, issue, 0, unroll=DMA_ISSUE_UNROLL)


def _row_gather_wait(src_hbm, dst_ref, sem, n):
    pltpu.make_async_copy(src_hbm.at[pl.ds(0, n)], dst_ref, sem).wait()


def _combine_kernel(d0_ref, d1_ref, n0_ref, n1_ref, x_ref, mod_ref, r_ref, y_hbm, o_ref, y_ref, sem, *, d):
    step = pl.program_id(0) * pl.num_programs(1) + pl.program_id(1)
    n_steps = pl.num_programs(0) * pl.num_programs(1)
    slot = step % 2

    def gather(idx_refs, s):
        for k, idx_ref in enumerate(idx_refs):
            _row_gather(y_hbm, idx_ref, y_ref.at[s, k], sem.at[s, k], TM)

    @pl.when(step == 0)
    def _():
        gather((d0_ref, d1_ref), slot)

    @pl.when(step + 1 < n_steps)
    def _():
        gather((n0_ref, n1_ref), 1 - slot)

    route = r_ref[...]
    lane = lax.broadcasted_iota(jnp.int32, route.shape, 1)
    g0 = jnp.sum(jnp.where(lane == ROUTE_G, route, 0.0), axis=-1, keepdims=True)
    g1 = jnp.sum(jnp.where(lane == ROUTE_G + 1, route, 0.0), axis=-1, keepdims=True)
    for k in range(MOE_TOPK):
        _row_gather_wait(y_hbm, y_ref.at[slot, k], sem.at[slot, k], TM)
    y0 = _unpack_bf16_pairs(y_ref[slot, 0], F32)
    y1 = _unpack_bf16_pairs(y_ref[slot, 1], F32)
    o_ref[0] = x_ref[0] + mod_ref[0, :, 5 * d:6 * d] * (y0 * g0 + y1 * g1)


def _combine(xx, modsel, route, y_rows, dest, n_lat_tiles, out_tiles):
    b, nt, d = xx.shape
    tiles = nt // TM
    idx_spec = pl.BlockSpec((1, 1, TM), lambda i, j: (i * tiles + j, 0, 0), memory_space=pltpu.SMEM)

    def next_block(i, j):
        wrap = j + 1 >= out_tiles
        return (jnp.where(wrap, jnp.minimum(i + 1, b - 1) * tiles, i * tiles + j + 1), 0, 0)

    next_spec = pl.BlockSpec((1, 1, TM), next_block, memory_space=pltpu.SMEM)
    d0, d1 = _idx_blocks(dest, 0, b * tiles), _idx_blocks(dest, 1, b * tiles)
    return pl.pallas_call(
        functools.partial(_combine_kernel, d=d),
        grid=(b, out_tiles),
        in_specs=[idx_spec, idx_spec, next_spec, next_spec, pl.BlockSpec((1, TM, d), lambda i, j: (i, j, 0)),
                  _mod_spec(6 * d, n_lat_tiles),
                  pl.BlockSpec((TM, LANES), lambda i, j: (i * tiles + j, 0)),
                  pl.BlockSpec(memory_space=pl.ANY)],
        out_specs=pl.BlockSpec((1, TM, d), lambda i, j: (i, j, 0)),
        out_shape=jax.ShapeDtypeStruct((b, out_tiles * TM, d), F32),
        scratch_shapes=[pltpu.VMEM((2, MOE_TOPK, TM, d // 2), jnp.uint32), pltpu.SemaphoreType.DMA((2, MOE_TOPK))],
        compiler_params=_cparams(("arbitrary", "arbitrary")),
    )(d0, d1, d0, d1, xx, modsel, route, y_rows)


def _pad_heads_cols(w):
    lead = w.shape[:-1]
    w = w.reshape(*lead, N_HEADS, HEAD_DIM)
    w = jnp.concatenate([w, jnp.zeros_like(w)], axis=-1)
    return w.reshape(*lead, PAD_W)


def _pad_heads_rows(w):
    return _pad_heads_cols(w.T).T


def _seg_mean_matrix(width, segments):
    m = np.zeros((width, width), np.float32)
    for g in range(width // LANES):
        for start, length in segments:
            a = g * LANES + start
            m[a:a + length, a:a + length] = 1.0 / length
    return jnp.asarray(m, BF16)


def _rope_tables(n_lat, n_ctx):
    pos = jnp.arange(n_lat)
    rows = (pos // GRID_W).astype(F32)
    cols = (pos % GRID_W).astype(F32)
    per_axis = MLA_ROPE // 2
    inv_freq = ROPE_THETA ** (-jnp.arange(0, per_axis, 2, dtype=F32) / per_axis)
    ang = jnp.concatenate([rows[:, None] * inv_freq, cols[:, None] * inv_freq], axis=-1)
    i = np.arange(MLA_ROPE)
    src = (i // 16) * 8 + (i % 8)
    sign = np.where((i % 16) < 8, -1.0, 1.0).astype(np.float32)
    cos = jnp.ones((n_lat, LANES), F32).at[:, HEAD_DIM:HEAD_DIM + MLA_ROPE].set(jnp.cos(ang)[:, src])
    sin = jnp.zeros((n_lat, LANES), F32).at[:, HEAD_DIM:HEAD_DIM + MLA_ROPE].set(jnp.sin(ang)[:, src] * sign)
    cos = jnp.concatenate([cos, jnp.ones((n_ctx, LANES), F32)], axis=0)
    sin = jnp.concatenate([sin, jnp.zeros((n_ctx, LANES), F32)], axis=0)
    return cos, sin


def _na_bias_table(rpb):
    w = np.arange(GRID_W)
    col_start = np.clip(w - NA_KW // 2, 0, GRID_W - NA_KW)
    valid = (w[None, :] >= col_start[:, None]) & (w[None, :] < col_start[:, None] + NA_KW)
    dc = np.clip(w[None, :] - w[:, None], 1 - NA_KW, NA_KW - 1) + (NA_KW - 1)
    onehot = jnp.asarray(dc[None, :, :] == np.arange(2 * NA_KW - 1)[:, None, None], F32)
    t = jnp.einsum('hrd,dqk->hrqk', rpb.astype(F32), onehot, precision=lax.Precision.HIGHEST)
    t = jnp.where(jnp.asarray(valid)[None, None, :, :], t * LOG2E, NEG_BIG)
    masked = jnp.full((N_HEADS, GRID_W, GRID_W), NEG_BIG, F32)
    q_off = (0, NA_KH // 2, NA_KH)
    first = ([0] * NA_TILE_ROWS, list(range(NA_TILE_ROWS)), [NA_KH // 2] * NA_TILE_ROWS)
    cases = []
    for c in range(3):
        row_blocks = []
        for rr in range(NA_TILE_ROWS):
            blocks = []
            for jj in range(NA_WIN_ROWS):
                live = first[c][rr] <= jj < first[c][rr] + NA_KH
                dr = jj - (q_off[c] + rr) + (NA_KH - 1)
                blocks.append(t[:, dr] if live else masked)
            row_blocks.append(jnp.concatenate(blocks, axis=-1))
        cases.append(jnp.concatenate(row_blocks, axis=-2))
    return jnp.stack(cases, axis=0)


def _block_diag_mask(block):
    i = np.arange(GROUP_W) // block
    return (i[:, None] == i[None, :]).astype(np.float32)


def _retention_consts():
    c = RET_CHUNK
    j = np.arange(2 * N_HEADS, dtype=np.float64)
    lg = np.log1p(-np.exp2(-5.0 - j))
    lg_f, lg_b = lg[0::2], lg[1::2]
    pos = np.arange(c, dtype=np.float64)
    diff = pos[:, None] - pos[None, :]
    k_scale = HEAD_DIM ** -0.5
    dm = np.zeros((N_HEADS, c, c))
    for h in range(N_HEADS):
        dm[h] = (np.where(diff >= 0, np.exp(np.maximum(diff, 0.0) * lg_f[h]), 0.0)
                 + np.where(diff <= 0, np.exp(np.maximum(-diff, 0.0) * lg_b[h]), 0.0)) * k_scale
    lanes = lambda per_head: np.repeat(per_head, HEAD_DIM, axis=-1)
    out = {
        'dm': dm,
        'qw_f': lanes(np.exp((pos + 1)[:, None] * lg_f[None, :])),
        'kw_f': lanes(np.exp((c - 1 - pos)[:, None] * lg_f[None, :])) * k_scale,
        'cd_f': lanes(np.exp(c * lg_f)[None, :]),
        'qw_b': lanes(np.exp((c - pos)[:, None] * lg_b[None, :])),
        'kw_b': lanes(np.exp(pos[:, None] * lg_b[None, :])) * k_scale,
        'cd_b': lanes(np.exp(c * lg_b)[None, :]),
        'bd': _block_diag_mask(HEAD_DIM),
    }
    out = {k: jnp.asarray(v, F32) for k, v in out.items()}
    out['ms'] = jnp.asarray(_block_diag_mask(HEAD_DIM) / HEAD_DIM, BF16)
    return out


def _hgrn_consts():
    t = np.arange(TM)
    same = (t[:, None] // HG_CHUNK) == (t[None, :] // HG_CHUNK)
    lincl = same & (t[None, :] <= t[:, None])
    lexcl = same & (t[None, :] < t[:, None])
    return {
        'lincl': jnp.asarray(lincl, BF16), 'lexcl': jnp.asarray(lexcl, BF16),
        'uincl': jnp.asarray(lincl.T, BF16), 'uexcl': jnp.asarray(lexcl.T, BF16),
        'bseg': jnp.asarray(_block_diag_mask(HEAD_DIM), BF16),
        'bd': jnp.asarray(_block_diag_mask(HEAD_DIM), F32),
        'ms': jnp.asarray(_block_diag_mask(HEAD_DIM) / HEAD_DIM, BF16),
    }


def _layer_weights(l, w_in, w_out, mla_g_cq, mla_g_ckv, mla_w_uq, mla_w_ukv, mla_g_qn, mla_g_qr, mla_g_kn,
                   mla_g_kr, na_g_q, na_g_k, moe_w_rg, moe_b_rg, moe_w_re, moe_b_re):
    d = w_in.shape[1]
    w = w_in[l]
    z = lambda n: jnp.zeros((d, n), F32)
    o = 0
    cq, o = w[:, o:o + MLA_Q_LORA], o + MLA_Q_LORA
    ckv, o = w[:, o:o + MLA_KV_LORA], o + MLA_KV_LORA
    kr, o = w[:, o:o + MLA_ROPE], o + MLA_ROPE
    naq, o = w[:, o:o + GROUP_W], o + GROUP_W
    nak, o = w[:, o:o + GROUP_W], o + GROUP_W
    nav, o = w[:, o:o + GROUP_W], o + GROUP_W
    rest = w[:, o:]
    w_in_p = jnp.concatenate([cq, z(GROUP_W - MLA_Q_LORA), ckv, z(HEAD_DIM), kr, z(LANES - HEAD_DIM - MLA_ROPE),
                              _pad_heads_cols(naq), _pad_heads_cols(nak), _pad_heads_cols(nav), rest],
                             axis=1).astype(BF16)

    qk_dim = HEAD_DIM + MLA_ROPE
    wuq = mla_w_uq[l].reshape(MLA_Q_LORA, N_HEADS, qk_dim)
    wuq = jnp.concatenate([wuq, jnp.zeros((MLA_Q_LORA, N_HEADS, LANES - qk_dim), F32)], axis=-1)
    wuq = jnp.concatenate([wuq.reshape(MLA_Q_LORA, PAD_W), jnp.zeros((GROUP_W - MLA_Q_LORA, PAD_W), F32)], axis=0)
    wukv = mla_w_ukv[l].reshape(MLA_KV_LORA, N_HEADS, 2 * HEAD_DIM)
    pad64 = jnp.zeros((MLA_KV_LORA, N_HEADS, HEAD_DIM), F32)
    wk = jnp.concatenate([wukv[:, :, :HEAD_DIM], pad64], axis=-1).reshape(MLA_KV_LORA, PAD_W)
    wv = jnp.concatenate([wukv[:, :, HEAD_DIM:], pad64], axis=-1).reshape(MLA_KV_LORA, PAD_W)

    def per_head(parts):
        row = jnp.concatenate(parts + [jnp.zeros((LANES - sum(p.shape[0] for p in parts),), F32)])
        return jnp.tile(row, N_HEADS)[None, :]

    prep = {
        'wuq': wuq.astype(BF16), 'wk': wk.astype(BF16), 'wv': wv.astype(BF16),
        'gcq': jnp.concatenate([mla_g_cq[l], jnp.zeros((GROUP_W - MLA_Q_LORA,), F32)])[None, :],
        'gckv': mla_g_ckv[l][None, :],
        'gkr': jnp.concatenate([jnp.zeros((HEAD_DIM,), F32), mla_g_kr[l],
                                jnp.zeros((LANES - HEAD_DIM - MLA_ROPE,), F32)])[None, :],
        'gq': per_head([mla_g_qn[l], mla_g_qr[l]]),
        'gk': per_head([mla_g_kn[l]]),
        'mq': _seg_mean_matrix(PAD_W, [(0, HEAD_DIM), (HEAD_DIM, MLA_ROPE)]),
        'mk': _seg_mean_matrix(PAD_W, [(0, HEAD_DIM)]),
        'gnq': per_head([na_g_q[l]]),
        'gnk': per_head([na_g_k[l]]),
        'mn': _seg_mean_matrix(PAD_W, [(0, HEAD_DIM)]),
    }
    wo = w_out[l]
    ow = {
        'mla': _pad_heads_rows(wo[0:GROUP_W]).astype(BF16),
        'na': _pad_heads_rows(wo[GROUP_W:2 * GROUP_W]).astype(BF16),
        'ret': wo[2 * GROUP_W:3 * GROUP_W].astype(BF16),
        'hg': wo[3 * GROUP_W:4 * GROUP_W].astype(BF16),
    }
    n_r = MOE_GROUPS + MOE_EXPERTS
    wr = jnp.concatenate([moe_w_rg[l], moe_w_re[l], jnp.zeros((d, LANES - n_r), F32)], axis=1)
    wr_hi = wr.astype(BF16)
    rw = {
        'hi_lo': jnp.concatenate([wr_hi, (wr - wr_hi.astype(F32)).astype(BF16)], axis=1),
        'b': jnp.concatenate([moe_b_rg[l], moe_b_re[l], jnp.zeros((LANES - n_r,), F32)])[None, :],
    }
    return w_in_p, prep, ow, rw


def _layer(xx, modsel, lw, rope_c, rope_s, na_bias, rc, hc, hg_lb_l, ret_go, hg_go, layer, w1, w3, w2,
           n_lat, n_ctx, last):
    w_in_p, prep_w, ow, rw = lw
    b, nt, d = xx.shape
    n_lat_tiles = n_lat // TM
    p, qm, km, vm, qn, kn, vn = _inproj_prep(xx, modsel, w_in_p, rope_c, rope_s, prep_w, n_lat_tiles)
    y_mla = _mla_attn(qm, km, vm, n_lat, n_ctx)
    y_na = _na_attn(qn, kn, vn, na_bias, n_lat, n_ctx)
    y_ret, y_hg = _recurrent_mixers(p, rc, hc, hg_lb_l, ret_go, hg_go, n_lat, n_ctx)
    xx, route, counts = _outproj_router(xx, y_mla, y_na, y_ret, y_hg, modsel, ow, rw, n_lat_tiles)
    dest, block_expert, used, n_blocks = _moe_plan(route, counts, b * nt)
    x_rows = _dispatch(xx, modsel, dest, n_blocks * MOE_BLOCK, n_lat_tiles)
    y_rows = _moe_ffn(x_rows, block_expert, used, n_blocks, layer, w1, w3, w2)
    return _combine(xx, modsel, route, y_rows, dest, n_lat_tiles, n_lat_tiles if last else nt // TM)


def kernel(x, c, ctx, c_ctx, w_ada, b_ada, w_in, w_out, mla_g_cq, mla_g_ckv, mla_w_uq, mla_w_ukv, mla_g_qn, mla_g_qr, mla_g_kn, mla_g_kr, na_g_q, na_g_k, na_rpb, ret_g_out, hg_lb_raw, hg_g_out, moe_w_rg, moe_b_rg, moe_w_re, moe_b_re, moe_w1, moe_w3, moe_w2):
    b, n_lat, d = x.shape
    n_ctx = ctx.shape[1]
    depth = w_in.shape[0]
    assert n_lat % TM == 0 and n_ctx % TM == 0 and TM % GRID_W == 0
    assert n_lat // TM >= 3 and n_lat // GRID_W >= NA_WIN_ROWS
    assert w_in.shape[2] == MLA_Q_LORA + MLA_KV_LORA + MLA_ROPE + 12 * GROUP_W

    cc = jnp.concatenate([c, c_ctx[None, :], jnp.zeros((16 - b - 1, d), F32)], axis=0)
    mods = _ada_all(cc, w_ada, b_ada)
    rope_c, rope_s = _rope_tables(n_lat, n_ctx)
    rc = _retention_consts()
    hc = _hgrn_consts()
    lb_w = jax.nn.softmax(hg_lb_raw.astype(F32), axis=0)
    hg_lb = jnp.cumsum(lb_w, axis=0) - lb_w[0:1]

    xx = jnp.concatenate([x, ctx], axis=1)
    tile_go = lambda g: jnp.tile(g, N_HEADS)[None, :]
    for l in range(depth):
        modsel = jnp.stack([mods[l, :b], jnp.broadcast_to(mods[l, b], (b, 6 * d))], axis=1).reshape(2 * b, 1, 6 * d)
        lw = _layer_weights(l, w_in, w_out, mla_g_cq, mla_g_ckv, mla_w_uq, mla_w_ukv, mla_g_qn, mla_g_qr,
                            mla_g_kn, mla_g_kr, na_g_q, na_g_k, moe_w_rg, moe_b_rg, moe_w_re, moe_b_re)
        xx = _layer(xx, modsel, lw, rope_c, rope_s, _na_bias_table(na_rpb[l]), rc, hc, hg_lb[l][None, :],
                    tile_go(ret_g_out[l]), tile_go(hg_g_out[l]),
                    l, moe_w1, moe_w3, moe_w2,
                    n_lat, n_ctx, l == depth - 1)
    return xx
```

```python
import functools

import numpy as np
import jax
import jax.numpy as jnp
from jax import lax
from jax.experimental import pallas as pl
from jax.experimental.pallas import tpu as pltpu

F32 = jnp.float32
BF16 = jnp.bfloat16

EPS = 1e-6
ROPE_THETA = 10000.0
NEG_BIG = -1e30
F_FLOOR = 1e-20
GRID_W = 64
N_HEADS = 4
HEAD_DIM = 64
LANES = 128
GROUP_W = N_HEADS * HEAD_DIM
PAD_W = N_HEADS * LANES
MLA_Q_LORA = 192
MLA_KV_LORA = 128
MLA_ROPE = 32
MLA_SCALE = (HEAD_DIM + MLA_ROPE) ** -0.5
LOG2E = 1.4426950408889634
NA_KH = 8
NA_KW = 16
NA_SCALE = HEAD_DIM ** -0.5
RET_CHUNK = 128
HG_CHUNK = 16
MOE_GROUPS = 4
MOE_PER_GROUP = 8
MOE_EXPERTS = MOE_GROUPS * MOE_PER_GROUP
MOE_TOPK = 2
MOE_BLOCK = 512
TM = 256
NA_TILE_ROWS = TM // GRID_W
NA_WIN_ROWS = NA_TILE_ROWS + NA_KH
VMEM_LIMIT = 56 * 1024 * 1024
SUBLANE_PAD_ROWS = 16
DMA_ISSUE_UNROLL = 8

P_ATTN_COLS = 2 * GROUP_W + 3 * PAD_W
COL_RET_Q, COL_RET_K, COL_RET_V, COL_RET_G = 0, 1, 2, 3
COL_HG_Q, COL_HG_FF, COL_HG_FB, COL_HG_I, COL_HG_G = 4, 5, 6, 7, 8


def _cparams(sem):
    return pltpu.CompilerParams(dimension_semantics=sem, vmem_limit_bytes=VMEM_LIMIT)


def _sigmoid(x):
    return 1.0 / (1.0 + jnp.exp(-x))


def _silu(x):
    return x * _sigmoid(x)


def _dot(a, b):
    return jnp.dot(a, b, preferred_element_type=F32)


def _dot_nt(a, b):
    return lax.dot_general(a, b, (((1,), (1,)), ((), ())), preferred_element_type=F32)


def _split_dot_l(x, m, n):
    acc = None
    rem = x
    for i in range(n):
        piece = rem.astype(BF16)
        d = _dot(piece, m)
        acc = d if acc is None else acc + d
        if i + 1 < n:
            rem = rem - piece.astype(F32)
    return acc


def _split_dot_r(m, x, n):
    acc = None
    rem = x
    for i in range(n):
        piece = rem.astype(BF16)
        d = _dot(m, piece)
        acc = d if acc is None else acc + d
        if i + 1 < n:
            rem = rem - piece.astype(F32)
    return acc


def _pack_bf16_pairs(x):
    half = x.shape[1] // 2
    bits = lax.bitcast_convert_type(x.astype(BF16).astype(F32), jnp.uint32)
    return (bits[:, :half] >> 16) | (bits[:, half:] & jnp.uint32(0xFFFF0000))


def _unpack_bf16_pairs(p, dtype=BF16):
    lo = lax.bitcast_convert_type(p << 16, F32)
    hi = lax.bitcast_convert_type(p & jnp.uint32(0xFFFF0000), F32)
    return jnp.concatenate([lo, hi], axis=1).astype(dtype)


def _seg_rms(x, m, gain):
    return x * lax.rsqrt(_split_dot_l(x * x, m, 2) + EPS) * gain


def _ada_kernel(c_ref, w_ref, b_ref, o_ref):
    s = _silu(c_ref[...])
    o_ref[0] = jnp.dot(s, w_ref[0], preferred_element_type=F32,
                       precision=lax.Precision.HIGHEST) + b_ref[0]


def _ada_all(cc, w_ada, b_ada):
    n_layers, d, d6 = w_ada.shape
    bn = 512
    rows = cc.shape[0]
    return pl.pallas_call(
        _ada_kernel,
        grid=(n_layers, d6 // bn),
        in_specs=[pl.BlockSpec((rows, d), lambda l, j: (0, 0)),
                  pl.BlockSpec((1, d, bn), lambda l, j: (l, 0, j)),
                  pl.BlockSpec((1, 1, bn), lambda l, j: (l, 0, j))],
        out_specs=pl.BlockSpec((1, rows, bn), lambda l, j: (l, 0, j)),
        out_shape=jax.ShapeDtypeStruct((n_layers, rows, d6), F32),
        compiler_params=_cparams(("arbitrary", "arbitrary")),
    )(cc, w_ada, b_ada.reshape(n_layers, 1, d6))


def _mod_spec(d6, n_lat_tiles):
    return pl.BlockSpec((1, 1, d6), lambda b, j: (2 * b + (j >= n_lat_tiles).astype(jnp.int32), 0, 0))


def _modulate(x, shift, scale):
    xn = x * lax.rsqrt(jnp.mean(x * x, axis=-1, keepdims=True) + EPS)
    return xn * (1.0 + scale) + shift


def _inproj_prep_kernel(x_ref, mod_ref, w_ref, c_ref, s_ref, wuq_ref, wk_ref, wv_ref, gcq_ref, gckv_ref, gkr_ref,
                        gq_ref, gk_ref, mq_ref, mk_ref, gnq_ref, gnk_ref, mn_ref,
                        p_ref, qm_ref, km_ref, vm_ref, qn_ref, kn_ref, vn_ref, *, d):
    xm = _modulate(x_ref[0], mod_ref[0, :, 0:d], mod_ref[0, :, d:2 * d]).astype(BF16)
    p_ref[0] = _dot(xm, w_ref[:, P_ATTN_COLS:])
    pa = _dot(xm, w_ref[:, 0:P_ATTN_COLS])

    cq = pa[:, 0:256]
    ckv = pa[:, 256:384]
    kr = pa[:, 384:512]
    cqn = cq * lax.rsqrt(jnp.sum(cq * cq, axis=-1, keepdims=True) * (1.0 / MLA_Q_LORA) + EPS) * gcq_ref[...]
    ckvn = (ckv * lax.rsqrt(jnp.mean(ckv * ckv, axis=-1, keepdims=True) + EPS) * gckv_ref[...]).astype(BF16)
    krn = kr * lax.rsqrt(jnp.sum(kr * kr, axis=-1, keepdims=True) * (1.0 / MLA_ROPE) + EPS) * gkr_ref[...]
    q = _seg_rms(_dot(cqn.astype(BF16), wuq_ref[...]), mq_ref[...], gq_ref[...])
    kk = _seg_rms(_dot(ckvn, wk_ref[...]), mk_ref[...], gk_ref[...])
    vv = _dot(ckvn, wv_ref[...])

    cos = c_ref[...]
    sin = s_ref[...]
    lane = lax.broadcasted_iota(jnp.int32, (TM, LANES), 1)
    first = (lane % 16) < 8

    def rope(x):
        partner = jnp.where(first, pltpu.roll(x, LANES - 8, 1), pltpu.roll(x, 8, 1))
        return x * cos + partner * sin

    krr = rope(krn)
    nq = _seg_rms(pa[:, 512:1024], mn_ref[...], gnq_ref[...])
    nk = _seg_rms(pa[:, 1024:1536], mn_ref[...], gnk_ref[...])
    for h in range(N_HEADS):
        sl = slice(h * LANES, (h + 1) * LANES)
        qm_ref[0, h] = (rope(q[:, sl]) * (MLA_SCALE * LOG2E)).astype(BF16)
        km_ref[0, h] = (kk[:, sl] + krr).astype(BF16)
        vm_ref[0, h] = vv[:, sl].T.astype(BF16)
        qn_ref[0, h] = (nq[:, sl] * (NA_SCALE * LOG2E)).astype(BF16)
        kn_ref[0, h] = nk[:, sl].astype(BF16)
        vn_ref[0, h] = pa[:, 1536 + h * LANES:1536 + (h + 1) * LANES].astype(BF16)


def _inproj_prep(xx, modsel, w_in_p, rope_c, rope_s, pw, n_lat_tiles):
    b, nt, d = xx.shape
    rest = w_in_p.shape[1] - P_ATTN_COLS
    full = lambda a: pl.BlockSpec(a.shape, lambda i, j: (0,) * a.ndim)
    consts = [pw['wuq'], pw['wk'], pw['wv'], pw['gcq'], pw['gckv'], pw['gkr'], pw['gq'], pw['gk'],
              pw['mq'], pw['mk'], pw['gnq'], pw['gnk'], pw['mn']]
    head_spec = pl.BlockSpec((1, N_HEADS, TM, LANES), lambda i, j: (i, 0, j, 0))
    head_shape = jax.ShapeDtypeStruct((b, N_HEADS, nt, LANES), BF16)
    head_t_spec = pl.BlockSpec((1, N_HEADS, LANES, TM), lambda i, j: (i, 0, 0, j))
    head_t_shape = jax.ShapeDtypeStruct((b, N_HEADS, LANES, nt), BF16)
    return pl.pallas_call(
        functools.partial(_inproj_prep_kernel, d=d),
        grid=(b, nt // TM),
        in_specs=[pl.BlockSpec((1, TM, d), lambda i, j: (i, j, 0)),
                  _mod_spec(6 * d, n_lat_tiles),
                  full(w_in_p),
                  pl.BlockSpec((TM, LANES), lambda i, j: (j, 0)),
                  pl.BlockSpec((TM, LANES), lambda i, j: (j, 0))] + [full(a) for a in consts],
        out_specs=[pl.BlockSpec((1, TM, rest), lambda i, j: (i, j, 0)),
                   head_spec, head_spec, head_t_spec, head_spec, head_spec, head_spec],
        out_shape=[jax.ShapeDtypeStruct((b, nt, rest), F32),
                   head_shape, head_shape, head_t_shape, head_shape, head_shape, head_shape],
        compiler_params=_cparams(("arbitrary", "arbitrary")),
    )(xx, modsel, w_in_p, rope_c, rope_s, *consts)


def _softmax2_pv(s, v):
    m = jnp.max(s, axis=-1, keepdims=True)
    e = jnp.exp2(s - m)
    l = jnp.sum(e, axis=-1, keepdims=True)
    return _dot(e.astype(BF16), v) / l


MLA_KEY_BLOCK = 1088
MLA_Q_TILE = 512


MLA_HEADS_PER_STEP = 4


def _mla_attend(q_ref, k_ref, vt_ref, o_ref, n_q, k0, k1):
    items = [(hh, s0) for hh in range(MLA_HEADS_PER_STEP) for s0 in range(k0, k1, MLA_KEY_BLOCK)]
    score = lambda hh, s0: _dot_nt(k_ref[0, hh, s0:min(s0 + MLA_KEY_BLOCK, k1), :], q_ref[0, hh, 0:n_q, :])
    st = score(*items[0])
    m = l = acc = None
    for i, (hh, s0) in enumerate(items):
        st_next = score(*items[i + 1]) if i + 1 < len(items) else None
        vt_blk = vt_ref[0, hh, :, s0:min(s0 + MLA_KEY_BLOCK, k1)]
        bm = jnp.max(st, axis=0, keepdims=True)
        if s0 == k0:
            m = bm
            e = jnp.exp2(st - m)
            l = jnp.sum(e, axis=0, keepdims=True)
            acc = _dot(vt_blk, e.astype(BF16))
        else:
            m_new = jnp.maximum(m, bm)
            alpha = jnp.exp2(m - m_new)
            e = jnp.exp2(st - m_new)
            l = l * alpha + jnp.sum(e, axis=0, keepdims=True)
            acc = acc * alpha + _dot(vt_blk, e.astype(BF16))
            m = m_new
        if s0 + MLA_KEY_BLOCK >= k1:
            o_ref[0, 0:n_q, hh * LANES:(hh + 1) * LANES] = (acc / l).T
        st = st_next


def _mla_kernel(q_ref, k_ref, vt_ref, o_ref, *, n_lat, n_ctx):
    j = pl.program_id(2)

    @pl.when(j < n_lat // MLA_Q_TILE)
    def _():
        _mla_attend(q_ref, k_ref, vt_ref, o_ref, MLA_Q_TILE, 0, n_lat + n_ctx)

    @pl.when(j >= n_lat // MLA_Q_TILE)
    def _():
        _mla_attend(q_ref, k_ref, vt_ref, o_ref, n_ctx, n_lat, n_lat + n_ctx)


def _mla_attn(qm, km, vmt, n_lat, n_ctx):
    b, h, nt, _ = qm.shape
    hp = MLA_HEADS_PER_STEP
    tq = MLA_Q_TILE
    assert n_lat % tq == 0 and n_ctx <= tq
    kv_spec = pl.BlockSpec((1, hp, nt, LANES), lambda i, hh, j: (i, hh, 0, 0))
    vt_spec = pl.BlockSpec((1, hp, LANES, nt), lambda i, hh, j: (i, hh, 0, 0))
    return pl.pallas_call(
        functools.partial(_mla_kernel, n_lat=n_lat, n_ctx=n_ctx),
        grid=(b, h // hp, n_lat // tq + 1),
        in_specs=[pl.BlockSpec((1, hp, tq, LANES), lambda i, hh, j: (i, hh, j, 0)), kv_spec, vt_spec],
        out_specs=pl.BlockSpec((1, tq, hp * LANES), lambda i, hh, j: (i, j, hh)),
        out_shape=jax.ShapeDtypeStruct((b, nt, PAD_W), F32),
        compiler_params=_cparams(("arbitrary", "arbitrary", "arbitrary")),
    )(qm, km, vmt)


def _na_kernel(q_ref, k_ref, v_ref, bias_ref, o_ref, *, n_lat, n_ctx):
    j = pl.program_id(1)
    rows = n_lat // GRID_W
    n_tiles = n_lat // TM
    win = NA_WIN_ROWS * GRID_W

    @pl.when(j < n_tiles)
    def _():
        start = jnp.clip(j * NA_TILE_ROWS - NA_KH // 2, 0, rows - NA_WIN_ROWS)
        case = jnp.where(j == 0, 0, jnp.where(j == n_tiles - 1, 2, 1))
        tok0 = pl.multiple_of(start * GRID_W, GRID_W)
        def scores(h):
            q = q_ref[0, h]
            return (_dot_nt(q, k_ref[0, h, pl.ds(tok0, win), :]) + bias_ref[case, h],
                    _dot_nt(q, k_ref[0, h, pl.ds(n_lat, n_ctx), :]))

        nxt = scores(0)
        for h in range(N_HEADS):
            s1, s2 = nxt
            if h + 1 < N_HEADS:
                nxt = scores(h + 1)
            m = jnp.maximum(jnp.max(s1, axis=-1, keepdims=True), jnp.max(s2, axis=-1, keepdims=True))
            e1 = jnp.exp2(s1 - m)
            e2 = jnp.exp2(s2 - m)
            l = jnp.sum(e1, axis=-1, keepdims=True) + jnp.sum(e2, axis=-1, keepdims=True)
            o = _dot(e1.astype(BF16), v_ref[0, h, pl.ds(tok0, win), :])
            o = o + _dot(e2.astype(BF16), v_ref[0, h, pl.ds(n_lat, n_ctx), :])
            o_ref[0, :, h * LANES:(h + 1) * LANES] = o / l

    @pl.when(j >= n_lat // TM)
    def _():
        for h in range(N_HEADS):
            s = _dot_nt(q_ref[0, h], k_ref[0, h, pl.ds(n_lat, n_ctx), :])
            o_ref[0, :, h * LANES:(h + 1) * LANES] = _softmax2_pv(s, v_ref[0, h, pl.ds(n_lat, n_ctx), :])


def _na_attn(qn, kn, vn, bias, n_lat, n_ctx):
    b, h, nt, _ = qn.shape
    kv_spec = pl.BlockSpec((1, h, nt, LANES), lambda i, j: (i, 0, 0, 0))
    return pl.pallas_call(
        functools.partial(_na_kernel, n_lat=n_lat, n_ctx=n_ctx),
        grid=(b, nt // TM),
        in_specs=[pl.BlockSpec((1, h, TM, LANES), lambda i, j: (i, 0, j, 0)), kv_spec, kv_spec,
                  pl.BlockSpec(bias.shape, lambda i, j: (0, 0, 0, 0))],
        out_specs=pl.BlockSpec((1, TM, PAD_W), lambda i, j: (i, j, 0)),
        out_shape=jax.ShapeDtypeStruct((b, nt, PAD_W), F32),
        compiler_params=_cparams(("arbitrary", "arbitrary")),
    )(qn, kn, vn, bias)


def _head_mask(h, shape):
    return (lax.broadcasted_iota(jnp.int32, shape, 1) // HEAD_DIM) == h


def _ret_state_step(s_ref, q, k, v, qw, kw, cd, bd):
    state = s_ref[...]
    o = _dot((q * qw).astype(BF16), state.astype(BF16))
    upd = _dot((k * kw).T.astype(BF16), v.astype(BF16))
    s_ref[...] = state * cd + upd * bd
    return o


def _ret_fwd_body(q_ref, k_ref, v_ref, dm_ref, qw_ref, kw_ref, cd_ref, bd_ref, o_ref, s_ref):
    for c in range(TM // RET_CHUNK):
        rows = slice(c * RET_CHUNK, (c + 1) * RET_CHUNK)
        q = q_ref[0, rows]
        k = k_ref[0, rows]
        v = v_ref[0, rows]
        o = _ret_state_step(s_ref, q, k, v, qw_ref[...], kw_ref[...], cd_ref[...], bd_ref[...])
        kb = k.astype(BF16)
        for h in range(N_HEADS):
            hm = _head_mask(h, q.shape)
            sc = _dot_nt(jnp.where(hm, q, 0.0).astype(BF16), kb) * dm_ref[h]
            o = o + _dot(sc.astype(BF16), jnp.where(hm, v, 0.0).astype(BF16))
        o_ref[0, rows] = o


def _ret_bwd_body(q_ref, k_ref, v_ref, g_ref, op_ref, qw_ref, kw_ref, cd_ref, bd_ref, ms_ref, go_ref,
                  y_ref, s_ref):
    for c in reversed(range(TM // RET_CHUNK)):
        rows = slice(c * RET_CHUNK, (c + 1) * RET_CHUNK)
        o = op_ref[0, rows] + _ret_state_step(s_ref, q_ref[0, rows], k_ref[0, rows], v_ref[0, rows], qw_ref[...],
                                              kw_ref[...], cd_ref[...], bd_ref[...])
        y_ref[0, rows] = _seg_rms(o, ms_ref[...], go_ref[...]) * _silu(g_ref[0, rows])


def _scan_order(n_lat_t, n_ctx_t, reverse):
    if reverse:
        return lambda i: n_lat_t + n_ctx_t - 1 - i
    return lambda i: jnp.where(i < n_ctx_t, n_lat_t + i, i - n_ctx_t)


def _hg_direction(q_ref, f_ref, v_ref, lb_ref, ain_ref, aex_ref, bseg_ref, bd_ref, st_ref, sh_ref, *, reverse):
    n_chunks = TM // HG_CHUNK
    assert n_chunks == HG_CHUNK
    qh = _silu(q_ref[0])
    lb = lb_ref[...]
    f = jnp.maximum(lb + (1.0 - lb) * _sigmoid(f_ref[0]), F_FLOOR)
    lf = jnp.log(f) * LOG2E
    k = 1.0 - f
    v = v_ref[0]
    row = lax.broadcasted_iota(jnp.int32, (TM, 1), 0)
    pos = row % HG_CHUNK
    row_chunk = row // HG_CHUNK

    a_in = _split_dot_r(ain_ref[...], lf, 3)
    a_ex = _split_dot_r(aex_ref[...], lf, 3)
    width = v.shape[1]
    key_exp = a_in - jnp.log(k) * LOG2E
    for slot, val in enumerate((key_exp, v)):
        sh_ref[slot] = val.reshape(n_chunks, HG_CHUNK, width)

    def key_row(slot, s):
        return jnp.broadcast_to(sh_ref[slot, :, s:s + 1, :], (n_chunks, HG_CHUNK, width)).reshape(TM, width)
    qp = (qh * jnp.exp2(a_in)).astype(BF16)
    kdec = k * jnp.exp2(a_ex)
    lam_all = jnp.exp2(a_in + a_ex)
    vt = v.T.astype(BF16)
    bd = bd_ref[...]
    bseg = bseg_ref[...]
    state = st_ref[...]
    parts = [None] * n_chunks
    o_band = jnp.zeros_like(v)
    upds = [_dot(vt, jnp.where(row_chunk == c, kdec, 0.0).astype(BF16)) * bd for c in range(n_chunks)]
    for step in range(n_chunks):
        c = n_chunks - 1 - step if reverse else step
        r0 = c * HG_CHUNK
        parts[c] = _dot_nt(qp[r0:r0 + HG_CHUNK], state.astype(BF16))
        state = state * lam_all[r0:r0 + 1] + upds[c]
        s = step
        valid = (pos <= s) if reverse else (pos >= s)
        w = jnp.where(valid, qh * jnp.exp2(a_in - key_row(0, s)), 0.0)
        o_band = o_band + _dot(w.astype(BF16), bseg) * key_row(1, s)
    st_ref[...] = state
    return jnp.concatenate(parts, axis=0) + o_band


def _scan_fwd_kernel(rq_ref, rk_ref, rv_ref, hq_ref, hf_ref, hv_ref, dm_ref, qw_ref, kw_ref, cd_ref, bd_ref,
                     lb_ref, ain_ref, aex_ref, bseg_ref, ro_ref, ho_ref, rs_ref, hs_ref, sh_ref):
    @pl.when(pl.program_id(1) == 0)
    def _():
        rs_ref[...] = jnp.zeros_like(rs_ref)
        hs_ref[...] = jnp.zeros_like(hs_ref)

    _ret_fwd_body(rq_ref, rk_ref, rv_ref, dm_ref, qw_ref, kw_ref, cd_ref, bd_ref, ro_ref, rs_ref)
    ho_ref[0] = _hg_direction(hq_ref, hf_ref, hv_ref, lb_ref, ain_ref, aex_ref, bseg_ref, bd_ref, hs_ref, sh_ref,
                              reverse=False)


def _scan_bwd_kernel(rq_ref, rk_ref, rv_ref, rg_ref, rop_ref, hq_ref, hf_ref, hv_ref, hg_ref, hop_ref,
                     qw_ref, kw_ref, cd_ref, bd_ref, ms_ref, rgo_ref, lb_ref, ain_ref, aex_ref, bseg_ref, hgo_ref,
                     ry_ref, hy_ref, rs_ref, hs_ref, sh_ref):
    @pl.when(pl.program_id(1) == 0)
    def _():
        rs_ref[...] = jnp.zeros_like(rs_ref)
        hs_ref[...] = jnp.zeros_like(hs_ref)

    _ret_bwd_body(rq_ref, rk_ref, rv_ref, rg_ref, rop_ref, qw_ref, kw_ref, cd_ref, bd_ref, ms_ref, rgo_ref,
                  ry_ref, rs_ref)
    o = hop_ref[0] + _hg_direction(hq_ref, hf_ref, hv_ref, lb_ref, ain_ref, aex_ref, bseg_ref, bd_ref, hs_ref,
                                   sh_ref, reverse=True)
    hy_ref[0] = _seg_rms(o, ms_ref[...], hgo_ref[...]) * _silu(hg_ref[0])


def _recurrent_mixers(p, rc, hc, lb, ret_go, hg_go, n_lat, n_ctx):
    b, nt, _ = p.shape
    n_lat_t, n_ctx_t = n_lat // TM, n_ctx // TM
    fwd = _scan_order(n_lat_t, n_ctx_t, False)
    bwd = _scan_order(n_lat_t, n_ctx_t, True)
    col = lambda order, cb: pl.BlockSpec((1, TM, GROUP_W), lambda i, j: (i, order(j), cb))
    full = lambda a: pl.BlockSpec(a.shape, lambda i, j: (0,) * a.ndim)
    out_shape = jax.ShapeDtypeStruct((b, nt, GROUP_W), F32)
    scratch = [pltpu.VMEM((GROUP_W, GROUP_W), F32), pltpu.VMEM((GROUP_W, GROUP_W), F32),
               pltpu.VMEM((2, TM // HG_CHUNK, HG_CHUNK, GROUP_W), F32)]
    consts_f = [rc['dm'], rc['qw_f'], rc['kw_f'], rc['cd_f'], rc['bd'], lb, hc['lincl'], hc['uexcl'], hc['bseg']]
    out_f = pl.BlockSpec((1, TM, GROUP_W), lambda i, j: (i, fwd(j), 0))
    ret_part, hg_part = pl.pallas_call(
        _scan_fwd_kernel,
        grid=(b, nt // TM),
        in_specs=[col(fwd, COL_RET_Q), col(fwd, COL_RET_K), col(fwd, COL_RET_V),
                  col(fwd, COL_HG_Q), col(fwd, COL_HG_FF), col(fwd, COL_HG_I)] + [full(a) for a in consts_f],
        out_specs=[out_f, out_f],
        out_shape=[out_shape, out_shape],
        scratch_shapes=scratch,
        compiler_params=_cparams(("arbitrary", "arbitrary")),
    )(p, p, p, p, p, p, *consts_f)
    consts_b = [rc['qw_b'], rc['kw_b'], rc['cd_b'], rc['bd'], rc['ms'], ret_go,
                lb, hc['uincl'], hc['lexcl'], hc['bseg'], hg_go]
    out_b = pl.BlockSpec((1, TM, GROUP_W), lambda i, j: (i, bwd(j), 0))
    return pl.pallas_call(
        _scan_bwd_kernel,
        grid=(b, nt // TM),
        in_specs=[col(bwd, COL_RET_Q), col(bwd, COL_RET_K), col(bwd, COL_RET_V), col(bwd, COL_RET_G), out_b,
                  col(bwd, COL_HG_Q), col(bwd, COL_HG_FB), col(bwd, COL_HG_I), col(bwd, COL_HG_G), out_b]
                 + [full(a) for a in consts_b],
        out_specs=[out_b, out_b],
        out_shape=[out_shape, out_shape],
        scratch_shapes=scratch,
        compiler_params=_cparams(("arbitrary", "arbitrary")),
    )(p, p, p, p, ret_part, p, p, p, p, hg_part, *consts_b)


def _outproj_router_kernel(x_ref, ym_ref, yn_ref, yr_ref, yh_ref, mod_ref, wm_ref, wn_ref, wr_ref, wh_ref,
                           whl_ref, br_ref, ltri_ref, o_ref, r_ref, cnt_ref, *, d):
    acc = _dot(ym_ref[0].astype(BF16), wm_ref[...])
    acc = acc + _dot(yn_ref[0].astype(BF16), wn_ref[...])
    acc = acc + _dot(yr_ref[0].astype(BF16), wr_ref[...])
    acc = acc + _dot(yh_ref[0].astype(BF16), wh_ref[...])
    x_new = x_ref[0] + mod_ref[0, :, 2 * d:3 * d] * acc
    o_ref[0] = x_new
    h = _modulate(x_new, mod_ref[0, :, 3 * d:4 * d], mod_ref[0, :, 4 * d:5 * d])
    _route(h, whl_ref, br_ref, ltri_ref, r_ref, cnt_ref)


def _outproj_router(xx, y_mla, y_na, y_ret, y_hg, modsel, ow, rw, n_lat_tiles):
    b, nt, d = xx.shape
    tiles = nt // TM
    tile = lambda w: pl.BlockSpec((1, TM, w), lambda i, j: (i, j, 0))
    full = lambda a: pl.BlockSpec(a.shape, lambda i, j: (0, 0))
    ltri = jnp.asarray(np.tril(np.ones((TM, TM), np.float32), -1), BF16)
    ws = [ow['mla'], ow['na'], ow['ret'], ow['hg'], rw['hi_lo'], rw['b'], ltri]
    return pl.pallas_call(
        functools.partial(_outproj_router_kernel, d=d),
        grid=(b, tiles),
        in_specs=[tile(d), tile(PAD_W), tile(PAD_W), tile(GROUP_W), tile(GROUP_W),
                  _mod_spec(6 * d, n_lat_tiles)] + [full(a) for a in ws],
        out_specs=[tile(d), pl.BlockSpec((TM, LANES), lambda i, j: (i * tiles + j, 0)),
                   pl.BlockSpec((1, LANES), lambda i, j: (0, 0))],
        out_shape=[jax.ShapeDtypeStruct((b, nt, d), F32), jax.ShapeDtypeStruct((b * nt, LANES), F32),
                   jax.ShapeDtypeStruct((1, LANES), F32)],
        compiler_params=_cparams(("arbitrary", "arbitrary")),
    )(xx, y_mla, y_na, y_ret, y_hg, modsel, *ws)


def _route(h, whl_ref, br_ref, ltri_ref, r_ref, cnt_ref):
    @pl.when((pl.program_id(0) == 0) & (pl.program_id(1) == 0))
    def _():
        cnt_ref[...] = jnp.zeros_like(cnt_ref)

    h_hi = h.astype(BF16)
    h_lo = (h - h_hi.astype(F32)).astype(BF16)
    hi_terms = _dot(h_hi, whl_ref[...])
    lg = hi_terms[:, :LANES] + hi_terms[:, LANES:] + _dot(h_lo, whl_ref[:, :LANES]) + br_ref[...]

    lane = lax.broadcasted_iota(jnp.int32, lg.shape, 1).astype(F32)
    far = 1e9

    def first_argmax(vals, vmax):
        return jnp.min(jnp.where(vals == vmax, lane, far), axis=-1, keepdims=True)

    gl = jnp.where(lane < MOE_GROUPS, lg, NEG_BIG)
    gmax = jnp.max(gl, axis=-1, keepdims=True)
    pg_top = 1.0 / jnp.sum(jnp.exp(gl - gmax), axis=-1, keepdims=True)
    lo = MOE_GROUPS + MOE_PER_GROUP * first_argmax(gl, gmax)
    fl = jnp.where((lane >= lo) & (lane < lo + MOE_PER_GROUP), lg, NEG_BIG)
    fmax = jnp.max(fl, axis=-1, keepdims=True)
    fsum = jnp.sum(jnp.exp(fl - fmax), axis=-1, keepdims=True)
    i1 = first_argmax(fl, fmax)
    fl2 = jnp.where(lane == i1, NEG_BIG, fl)
    f2max = jnp.max(fl2, axis=-1, keepdims=True)
    i2 = first_argmax(fl2, f2max)
    p1 = 1.0 / fsum
    p2 = jnp.exp(f2max - fmax) / fsum
    g1 = pg_top * p1 / (p1 + p2)
    g2 = pg_top * p2 / (p1 + p2)
    e1 = i1 - MOE_GROUPS
    e2 = i2 - MOE_GROUPS

    onehot = jnp.where(lane == e1, 1.0, 0.0) + jnp.where(lane == e2, 1.0, 0.0)
    before = cnt_ref[...] + _dot(ltri_ref[...], onehot.astype(BF16))
    r1 = jnp.sum(jnp.where(lane == e1, before, 0.0), axis=-1, keepdims=True)
    r2 = jnp.sum(jnp.where(lane == e2, before, 0.0), axis=-1, keepdims=True)
    cnt_ref[...] += jnp.sum(onehot, axis=0, keepdims=True)

    out = jnp.zeros_like(lg)
    for col, val in enumerate((e1, e2, g1, g2, r1, r2)):
        out = jnp.where(lane == col, val, out)
    r_ref[...] = out


ROUTE_E, ROUTE_G, ROUTE_R = 0, 2, 4


def _moe_plan(route, counts_f, n_tok):
    counts = counts_f[0, :MOE_EXPERTS].astype(jnp.int32)
    padded = (counts + MOE_BLOCK - 1) // MOE_BLOCK * MOE_BLOCK
    pad_end = jnp.cumsum(padded)
    pad_start = pad_end - padded
    n_blocks = -(-(n_tok * MOE_TOPK) // MOE_BLOCK) + MOE_EXPERTS
    blk0 = jnp.arange(n_blocks, dtype=jnp.int32) * MOE_BLOCK
    block_expert = jnp.minimum(jnp.sum((pad_end[None, :] <= blk0[:, None]).astype(jnp.int32), axis=1),
                               MOE_EXPERTS - 1)
    used = (pad_end[-1] // MOE_BLOCK).reshape(1)
    expert = route[:, ROUTE_E:ROUTE_E + MOE_TOPK].astype(jnp.int32)
    rank = route[:, ROUTE_R:ROUTE_R + MOE_TOPK].astype(jnp.int32)
    start_of = jnp.sum(jnp.where(expert[..., None] == jnp.arange(MOE_EXPERTS, dtype=jnp.int32), pad_start, 0), axis=-1)
    return start_of + rank, block_expert, used, n_blocks


def _idx_blocks(dest, k, n_tiles):
    return dest[:, k].reshape(n_tiles, 1, TM)


def _dispatch_kernel(d0_ref, d1_ref, x_ref, mod_ref, rows_in, rows_out, h_ref, sem, *, d):
    del rows_in
    step = pl.program_id(0) * pl.num_programs(1) + pl.program_id(1)
    n_steps = pl.num_programs(0) * pl.num_programs(1)
    slot = step % 2

    def wait_slot(s):
        for _ in range(MOE_TOPK):
            pltpu.make_async_copy(h_ref.at[s], rows_out.at[pl.ds(0, TM)], sem.at[s]).wait()

    h = _modulate(x_ref[0], mod_ref[0, :, 3 * d:4 * d], mod_ref[0, :, 4 * d:5 * d])
    h_ref[slot] = _pack_bf16_pairs(h)

    def issue(r, carry):
        src = h_ref.at[slot, pl.ds(r, 1)]
        pltpu.make_async_copy(src, rows_out.at[pl.ds(d0_ref[0, 0, r], 1)], sem.at[slot]).start()
        pltpu.make_async_copy(src, rows_out.at[pl.ds(d1_ref[0, 0, r], 1)], sem.at[slot]).start()
        return carry

    lax.fori_loop(0, TM, issue, 0, unroll=DMA_ISSUE_UNROLL)

    @pl.when(step > 0)
    def _():
        wait_slot(1 - slot)

    @pl.when(step == n_steps - 1)
    def _():
        wait_slot(slot)


def _dispatch(xx, modsel, dest, n_rows, n_lat_tiles):
    b, nt, d = xx.shape
    tiles = nt // TM
    idx_spec = pl.BlockSpec((1, 1, TM), lambda i, j: (i * tiles + j, 0, 0), memory_space=pltpu.SMEM)
    return pl.pallas_call(
        functools.partial(_dispatch_kernel, d=d),
        grid=(b, tiles),
        in_specs=[idx_spec, idx_spec, pl.BlockSpec((1, TM, d), lambda i, j: (i, j, 0)),
                  _mod_spec(6 * d, n_lat_tiles), pl.BlockSpec(memory_space=pl.ANY)],
        out_specs=pl.BlockSpec(memory_space=pl.ANY),
        out_shape=jax.ShapeDtypeStruct((n_rows, d // 2), jnp.uint32),
        scratch_shapes=[pltpu.VMEM((2, TM, d // 2), jnp.uint32), pltpu.SemaphoreType.DMA((2,))],
        input_output_aliases={4: 0},
        compiler_params=_cparams(("arbitrary", "arbitrary")),
    )(_idx_blocks(dest, 0, b * tiles), _idx_blocks(dest, 1, b * tiles), xx, modsel,
      jnp.zeros((n_rows, d // 2), jnp.uint32))


def _ffn_kernel(be_ref, used_ref, x_ref, w1_ref, w3_ref, w2_ref, y_ref, w1b_ref, w3b_ref, w2b_ref):
    i = pl.program_id(0)

    @pl.when((i == 0) | (be_ref[i] != be_ref[jnp.maximum(i - 1, 0)]))
    def _():
        w1b_ref[...] = w1_ref[0, 0].astype(BF16)
        w3b_ref[...] = w3_ref[0, 0].astype(BF16)
        w2b_ref[...] = w2_ref[0, 0].astype(BF16)

    @pl.when(i < used_ref[0])
    def _():
        x = _unpack_bf16_pairs(x_ref[...])
        mid = _silu(_dot(x, w1b_ref[...])) * _dot(x, w3b_ref[...])
        y_ref[...] = _pack_bf16_pairs(_dot(mid.astype(BF16), w2b_ref[...]))

    @pl.when(i >= used_ref[0])
    def _():
        y_ref[...] = jnp.zeros_like(y_ref)


def _moe_ffn(x_rows, block_expert, used, n_blocks, layer, w1, w3, w2):
    d = w1.shape[2]
    ff = w1.shape[3]
    grid_spec = pltpu.PrefetchScalarGridSpec(
        num_scalar_prefetch=2,
        grid=(n_blocks,),
        in_specs=[pl.BlockSpec((MOE_BLOCK, d // 2), lambda i, be, nu: (i, 0)),
                  pl.BlockSpec((1, 1, d, ff), lambda i, be, nu: (layer, be[i], 0, 0)),
                  pl.BlockSpec((1, 1, d, ff), lambda i, be, nu: (layer, be[i], 0, 0)),
                  pl.BlockSpec((1, 1, ff, d), lambda i, be, nu: (layer, be[i], 0, 0))],
        out_specs=pl.BlockSpec((MOE_BLOCK, d // 2), lambda i, be, nu: (i, 0)),
        scratch_shapes=[pltpu.VMEM((d, ff), BF16), pltpu.VMEM((d, ff), BF16), pltpu.VMEM((ff, d), BF16)],
    )
    return pl.pallas_call(
        _ffn_kernel,
        grid_spec=grid_spec,
        out_shape=jax.ShapeDtypeStruct((x_rows.shape[0], d // 2), jnp.uint32),
        compiler_params=_cparams(("arbitrary",)),
    )(block_expert, used, x_rows, w1, w3, w2)


def _row_gather(src_hbm, idx_ref, dst_ref, sem, n):
    def issue(r, carry):
        pltpu.make_async_copy(src_hbm.at[pl.ds(idx_ref[0, 0, r], 1)], dst_ref.at[pl.ds(r, 1)], sem).start()
        return carry

    lax.fori_loop(0, n, issue, 0, unroll=DMA_ISSUE_UNROLL)


def _row_gather_wait(src_hbm, dst_ref, sem, n):
    pltpu.make_async_copy(src_hbm.at[pl.ds(0, n)], dst_ref, sem).wait()


def _combine_kernel(d0_ref, d1_ref, n0_ref, n1_ref, x_ref, mod_ref, r_ref, y_hbm, o_ref, y_ref, sem, *, d):
    step = pl.program_id(0) * pl.num_programs(1) + pl.program_id(1)
    n_steps = pl.num_programs(0) * pl.num_programs(1)
    slot = step % 2

    def gather(idx_refs, s):
        for k, idx_ref in enumerate(idx_refs):
            _row_gather(y_hbm, idx_ref, y_ref.at[s, k], sem.at[s, k], TM)

    @pl.when(step == 0)
    def _():
        gather((d0_ref, d1_ref), slot)

    @pl.when(step + 1 < n_steps)
    def _():
        gather((n0_ref, n1_ref), 1 - slot)

    route = r_ref[...]
    lane = lax.broadcasted_iota(jnp.int32, route.shape, 1)
    g0 = jnp.sum(jnp.where(lane == ROUTE_G, route, 0.0), axis=-1, keepdims=True)
    g1 = jnp.sum(jnp.where(lane == ROUTE_G + 1, route, 0.0), axis=-1, keepdims=True)
    for k in range(MOE_TOPK):
        _row_gather_wait(y_hbm, y_ref.at[slot, k], sem.at[slot, k], TM)
    y0 = _unpack_bf16_pairs(y_ref[slot, 0], F32)
    y1 = _unpack_bf16_pairs(y_ref[slot, 1], F32)
    o_ref[0] = x_ref[0] + mod_ref[0, :, 5 * d:6 * d] * (y0 * g0 + y1 * g1)


def _combine(xx, modsel, route, y_rows, dest, n_lat_tiles, out_tiles):
    b, nt, d = xx.shape
    tiles = nt // TM
    idx_spec = pl.BlockSpec((1, 1, TM), lambda i, j: (i * tiles + j, 0, 0), memory_space=pltpu.SMEM)

    def next_block(i, j):
        wrap = j + 1 >= out_tiles
        return (jnp.where(wrap, jnp.minimum(i + 1, b - 1) * tiles, i * tiles + j + 1), 0, 0)

    next_spec = pl.BlockSpec((1, 1, TM), next_block, memory_space=pltpu.SMEM)
    d0, d1 = _idx_blocks(dest, 0, b * tiles), _idx_blocks(dest, 1, b * tiles)
    return pl.pallas_call(
        functools.partial(_combine_kernel, d=d),
        grid=(b, out_tiles),
        in_specs=[idx_spec, idx_spec, next_spec, next_spec, pl.BlockSpec((1, TM, d), lambda i, j: (i, j, 0)),
                  _mod_spec(6 * d, n_lat_tiles),
                  pl.BlockSpec((TM, LANES), lambda i, j: (i * tiles + j, 0)),
                  pl.BlockSpec(memory_space=pl.ANY)],
        out_specs=pl.BlockSpec((1, TM, d), lambda i, j: (i, j, 0)),
        out_shape=jax.ShapeDtypeStruct((b, out_tiles * TM, d), F32),
        scratch_shapes=[pltpu.VMEM((2, MOE_TOPK, TM, d // 2), jnp.uint32), pltpu.SemaphoreType.DMA((2, MOE_TOPK))],
        compiler_params=_cparams(("arbitrary", "arbitrary")),
    )(d0, d1, d0, d1, xx, modsel, route, y_rows)


def _pad_heads_cols(w):
    lead = w.shape[:-1]
    w = w.reshape(*lead, N_HEADS, HEAD_DIM)
    w = jnp.concatenate([w, jnp.zeros_like(w)], axis=-1)
    return w.reshape(*lead, PAD_W)


def _pad_heads_rows(w):
    return _pad_heads_cols(w.T).T


def _seg_mean_matrix(width, segments):
    m = np.zeros((width, width), np.float32)
    for g in range(width // LANES):
        for start, length in segments:
            a = g * LANES + start
            m[a:a + length, a:a + length] = 1.0 / length
    return jnp.asarray(m, BF16)


def _rope_tables(n_lat, n_ctx):
    pos = jnp.arange(n_lat)
    rows = (pos // GRID_W).astype(F32)
    cols = (pos % GRID_W).astype(F32)
    per_axis = MLA_ROPE // 2
    inv_freq = ROPE_THETA ** (-jnp.arange(0, per_axis, 2, dtype=F32) / per_axis)
    ang = jnp.concatenate([rows[:, None] * inv_freq, cols[:, None] * inv_freq], axis=-1)
    i = np.arange(MLA_ROPE)
    src = (i // 16) * 8 + (i % 8)
    sign = np.where((i % 16) < 8, -1.0, 1.0).astype(np.float32)
    cos = jnp.ones((n_lat, LANES), F32).at[:, HEAD_DIM:HEAD_DIM + MLA_ROPE].set(jnp.cos(ang)[:, src])
    sin = jnp.zeros((n_lat, LANES), F32).at[:, HEAD_DIM:HEAD_DIM + MLA_ROPE].set(jnp.sin(ang)[:, src] * sign)
    cos = jnp.concatenate([cos, jnp.ones((n_ctx, LANES), F32)], axis=0)
    sin = jnp.concatenate([sin, jnp.zeros((n_ctx, LANES), F32)], axis=0)
    return cos, sin


def _na_bias_table(rpb):
    w = np.arange(GRID_W)
    col_start = np.clip(w - NA_KW // 2, 0, GRID_W - NA_KW)
    valid = (w[None, :] >= col_start[:, None]) & (w[None, :] < col_start[:, None] + NA_KW)
    dc = np.clip(w[None, :] - w[:, None], 1 - NA_KW, NA_KW - 1) + (NA_KW - 1)
    onehot = jnp.asarray(dc[None, :, :] == np.arange(2 * NA_KW - 1)[:, None, None], F32)
    t = jnp.einsum('hrd,dqk->hrqk', rpb.astype(F32), onehot, precision=lax.Precision.HIGHEST)
    t = jnp.where(jnp.asarray(valid)[None, None, :, :], t * LOG2E, NEG_BIG)
    masked = jnp.full((N_HEADS, GRID_W, GRID_W), NEG_BIG, F32)
    q_off = (0, NA_KH // 2, NA_KH)
    first = ([0] * NA_TILE_ROWS, list(range(NA_TILE_ROWS)), [NA_KH // 2] * NA_TILE_ROWS)
    cases = []
    for c in range(3):
        row_blocks = []
        for rr in range(NA_TILE_ROWS):
            blocks = []
            for jj in range(NA_WIN_ROWS):
                live = first[c][rr] <= jj < first[c][rr] + NA_KH
                dr = jj - (q_off[c] + rr) + (NA_KH - 1)
                blocks.append(t[:, dr] if live else masked)
            row_blocks.append(jnp.concatenate(blocks, axis=-1))
        cases.append(jnp.concatenate(row_blocks, axis=-2))
    return jnp.stack(cases, axis=0)


def _block_diag_mask(block):
    i = np.arange(GROUP_W) // block
    return (i[:, None] == i[None, :]).astype(np.float32)


def _retention_consts():
    c = RET_CHUNK
    j = np.arange(2 * N_HEADS, dtype=np.float64)
    lg = np.log1p(-np.exp2(-5.0 - j))
    lg_f, lg_b = lg[0::2], lg[1::2]
    pos = np.arange(c, dtype=np.float64)
    diff = pos[:, None] - pos[None, :]
    k_scale = HEAD_DIM ** -0.5
    dm = np.zeros((N_HEADS, c, c))
    for h in range(N_HEADS):
        dm[h] = (np.where(diff >= 0, np.exp(np.maximum(diff, 0.0) * lg_f[h]), 0.0)
                 + np.where(diff <= 0, np.exp(np.maximum(-diff, 0.0) * lg_b[h]), 0.0)) * k_scale
    lanes = lambda per_head: np.repeat(per_head, HEAD_DIM, axis=-1)
    out = {
        'dm': dm,
        'qw_f': lanes(np.exp((pos + 1)[:, None] * lg_f[None, :])),
        'kw_f': lanes(np.exp((c - 1 - pos)[:, None] * lg_f[None, :])) * k_scale,
        'cd_f': lanes(np.exp(c * lg_f)[None, :]),
        'qw_b': lanes(np.exp((c - pos)[:, None] * lg_b[None, :])),
        'kw_b': lanes(np.exp(pos[:, None] * lg_b[None, :])) * k_scale,
        'cd_b': lanes(np.exp(c * lg_b)[None, :]),
        'bd': _block_diag_mask(HEAD_DIM),
    }
    out = {k: jnp.asarray(v, F32) for k, v in out.items()}
    out['ms'] = jnp.asarray(_block_diag_mask(HEAD_DIM) / HEAD_DIM, BF16)
    return out


def _hgrn_consts():
    t = np.arange(TM)
    same = (t[:, None] // HG_CHUNK) == (t[None, :] // HG_CHUNK)
    lincl = same & (t[None, :] <= t[:, None])
    lexcl = same & (t[None, :] < t[:, None])
    return {
        'lincl': jnp.asarray(lincl, BF16), 'lexcl': jnp.asarray(lexcl, BF16),
        'uincl': jnp.asarray(lincl.T, BF16), 'uexcl': jnp.asarray(lexcl.T, BF16),
        'bseg': jnp.asarray(_block_diag_mask(HEAD_DIM), BF16),
        'bd': jnp.asarray(_block_diag_mask(HEAD_DIM), F32),
        'ms': jnp.asarray(_block_diag_mask(HEAD_DIM) / HEAD_DIM, BF16),
    }


def _layer_weights(l, w_in, w_out, mla_g_cq, mla_g_ckv, mla_w_uq, mla_w_ukv, mla_g_qn, mla_g_qr, mla_g_kn,
                   mla_g_kr, na_g_q, na_g_k, moe_w_rg, moe_b_rg, moe_w_re, moe_b_re):
    d = w_in.shape[1]
    w = w_in[l]
    z = lambda n: jnp.zeros((d, n), F32)
    o = 0
    cq, o = w[:, o:o + MLA_Q_LORA], o + MLA_Q_LORA
    ckv, o = w[:, o:o + MLA_KV_LORA], o + MLA_KV_LORA
    kr, o = w[:, o:o + MLA_ROPE], o + MLA_ROPE
    naq, o = w[:, o:o + GROUP_W], o + GROUP_W
    nak, o = w[:, o:o + GROUP_W], o + GROUP_W
    nav, o = w[:, o:o + GROUP_W], o + GROUP_W
    rest = w[:, o:]
    w_in_p = jnp.concatenate([cq, z(GROUP_W - MLA_Q_LORA), ckv, z(HEAD_DIM), kr, z(LANES - HEAD_DIM - MLA_ROPE),
                              _pad_heads_cols(naq), _pad_heads_cols(nak), _pad_heads_cols(nav), rest],
                             axis=1).astype(BF16)

    qk_dim = HEAD_DIM + MLA_ROPE
    wuq = mla_w_uq[l].reshape(MLA_Q_LORA, N_HEADS, qk_dim)
    wuq = jnp.concatenate([wuq, jnp.zeros((MLA_Q_LORA, N_HEADS, LANES - qk_dim), F32)], axis=-1)
    wuq = jnp.concatenate([wuq.reshape(MLA_Q_LORA, PAD_W), jnp.zeros((GROUP_W - MLA_Q_LORA, PAD_W), F32)], axis=0)
    wukv = mla_w_ukv[l].reshape(MLA_KV_LORA, N_HEADS, 2 * HEAD_DIM)
    pad64 = jnp.zeros((MLA_KV_LORA, N_HEADS, HEAD_DIM), F32)
    wk = jnp.concatenate([wukv[:, :, :HEAD_DIM], pad64], axis=-1).reshape(MLA_KV_LORA, PAD_W)
    wv = jnp.concatenate([wukv[:, :, HEAD_DIM:], pad64], axis=-1).reshape(MLA_KV_LORA, PAD_W)

    def per_head(parts):
        row = jnp.concatenate(parts + [jnp.zeros((LANES - sum(p.shape[0] for p in parts),), F32)])
        return jnp.tile(row, N_HEADS)[None, :]

    prep = {
        'wuq': wuq.astype(BF16), 'wk': wk.astype(BF16), 'wv': wv.astype(BF16),
        'gcq': jnp.concatenate([mla_g_cq[l], jnp.zeros((GROUP_W - MLA_Q_LORA,), F32)])[None, :],
        'gckv': mla_g_ckv[l][None, :],
        'gkr': jnp.concatenate([jnp.zeros((HEAD_DIM,), F32), mla_g_kr[l],
                                jnp.zeros((LANES - HEAD_DIM - MLA_ROPE,), F32)])[None, :],
        'gq': per_head([mla_g_qn[l], mla_g_qr[l]]),
        'gk': per_head([mla_g_kn[l]]),
        'mq': _seg_mean_matrix(PAD_W, [(0, HEAD_DIM), (HEAD_DIM, MLA_ROPE)]),
        'mk': _seg_mean_matrix(PAD_W, [(0, HEAD_DIM)]),
        'gnq': per_head([na_g_q[l]]),
        'gnk': per_head([na_g_k[l]]),
        'mn': _seg_mean_matrix(PAD_W, [(0, HEAD_DIM)]),
    }
    wo = w_out[l]
    ow = {
        'mla': _pad_heads_rows(wo[0:GROUP_W]).astype(BF16),
        'na': _pad_heads_rows(wo[GROUP_W:2 * GROUP_W]).astype(BF16),
        'ret': wo[2 * GROUP_W:3 * GROUP_W].astype(BF16),
        'hg': wo[3 * GROUP_W:4 * GROUP_W].astype(BF16),
    }
    n_r = MOE_GROUPS + MOE_EXPERTS
    wr = jnp.concatenate([moe_w_rg[l], moe_w_re[l], jnp.zeros((d, LANES - n_r), F32)], axis=1)
    wr_hi = wr.astype(BF16)
    rw = {
        'hi_lo': jnp.concatenate([wr_hi, (wr - wr_hi.astype(F32)).astype(BF16)], axis=1),
        'b': jnp.concatenate([moe_b_rg[l], moe_b_re[l], jnp.zeros((LANES - n_r,), F32)])[None, :],
    }
    return w_in_p, prep, ow, rw


def _layer(xx, modsel, lw, rope_c, rope_s, na_bias, rc, hc, hg_lb_l, ret_go, hg_go, layer, w1, w3, w2,
           n_lat, n_ctx, last):
    w_in_p, prep_w, ow, rw = lw
    b, nt, d = xx.shape
    n_lat_tiles = n_lat // TM
    p, qm, km, vm, qn, kn, vn = _inproj_prep(xx, modsel, w_in_p, rope_c, rope_s, prep_w, n_lat_tiles)
    y_mla = _mla_attn(qm, km, vm, n_lat, n_ctx)
    y_na = _na_attn(qn, kn, vn, na_bias, n_lat, n_ctx)
    y_ret, y_hg = _recurrent_mixers(p, rc, hc, hg_lb_l, ret_go, hg_go, n_lat, n_ctx)
    xx, route, counts = _outproj_router(xx, y_mla, y_na, y_ret, y_hg, modsel, ow, rw, n_lat_tiles)
    dest, block_expert, used, n_blocks = _moe_plan(route, counts, b * nt)
    x_rows = _dispatch(xx, modsel, dest, n_blocks * MOE_BLOCK, n_lat_tiles)
    y_rows = _moe_ffn(x_rows, block_expert, used, n_blocks, layer, w1, w3, w2)
    return _combine(xx, modsel, route, y_rows, dest, n_lat_tiles, n_lat_tiles if last else nt // TM)


def kernel(x, c, ctx, c_ctx, w_ada, b_ada, w_in, w_out, mla_g_cq, mla_g_ckv, mla_w_uq, mla_w_ukv, mla_g_qn, mla_g_qr, mla_g_kn, mla_g_kr, na_g_q, na_g_k, na_rpb, ret_g_out, hg_lb_raw, hg_g_out, moe_w_rg, moe_b_rg, moe_w_re, moe_b_re, moe_w1, moe_w3, moe_w2):
    b, n_lat, d = x.shape
    n_ctx = ctx.shape[1]
    depth = w_in.shape[0]
    assert n_lat % TM == 0 and n_ctx % TM == 0 and TM % GRID_W == 0
    assert n_lat // TM >= 3 and n_lat // GRID_W >= NA_WIN_ROWS
    assert w_in.shape[2] == MLA_Q_LORA + MLA_KV_LORA + MLA_ROPE + 12 * GROUP_W

    assert b + 1 <= SUBLANE_PAD_ROWS
    cc = jnp.concatenate([c, c_ctx[None, :], jnp.zeros((SUBLANE_PAD_ROWS - b - 1, d), F32)], axis=0)
    mods = _ada_all(cc, w_ada, b_ada)
    rope_c, rope_s = _rope_tables(n_lat, n_ctx)
    rc = _retention_consts()
    hc = _hgrn_consts()
    lb_w = jax.nn.softmax(hg_lb_raw.astype(F32), axis=0)
    hg_lb = jnp.cumsum(lb_w, axis=0) - lb_w[0:1]

    xx = jnp.concatenate([x, ctx], axis=1)
    tile_go = lambda g: jnp.tile(g, N_HEADS)[None, :]
    for l in range(depth):
        modsel = jnp.stack([mods[l, :b], jnp.broadcast_to(mods[l, b], (b, 6 * d))], axis=1).reshape(2 * b, 1, 6 * d)
        lw = _layer_weights(l, w_in, w_out, mla_g_cq, mla_g_ckv, mla_w_uq, mla_w_ukv, mla_g_qn, mla_g_qr,
                            mla_g_kn, mla_g_kr, na_g_q, na_g_k, moe_w_rg, moe_b_rg, moe_w_re, moe_b_re)
        xx = _layer(xx, modsel, lw, rope_c, rope_s, _na_bias_table(na_rpb[l]), rc, hc, hg_lb[l][None, :],
                    tile_go(ret_g_out[l]), tile_go(hg_g_out[l]),
                    l, moe_w1, moe_w3, moe_w2,
                    n_lat, n_ctx, l == depth - 1)
    return xx
```

```python
import functools

import numpy as np
import jax
import jax.numpy as jnp
from jax import lax
from jax.experimental import pallas as pl
from jax.experimental.pallas import tpu as pltpu

F32 = jnp.float32
BF16 = jnp.bfloat16

EPS = 1e-6
ROPE_THETA = 10000.0
NEG_BIG = -1e30
F_FLOOR = 1e-20
GRID_W = 64
N_HEADS = 4
HEAD_DIM = 64
LANES = 128
GROUP_W = N_HEADS * HEAD_DIM
PAD_W = N_HEADS * LANES
MLA_Q_LORA = 192
MLA_KV_LORA = 128
MLA_ROPE = 32
MLA_SCALE = (HEAD_DIM + MLA_ROPE) ** -0.5
LOG2E = 1.4426950408889634
NA_KH = 8
NA_KW = 16
NA_SCALE = HEAD_DIM ** -0.5
RET_CHUNK = 128
HG_CHUNK = 16
MOE_GROUPS = 4
MOE_PER_GROUP = 8
MOE_EXPERTS = MOE_GROUPS * MOE_PER_GROUP
MOE_TOPK = 2
MOE_BLOCK = 512
TM = 256
NA_TILE_ROWS = TM // GRID_W
NA_WIN_ROWS = NA_TILE_ROWS + NA_KH
VMEM_LIMIT = 56 * 1024 * 1024
SUBLANE_PAD_ROWS = 16
DMA_ISSUE_UNROLL = 32

P_ATTN_COLS = 2 * GROUP_W + 3 * PAD_W
COL_RET_Q, COL_RET_K, COL_RET_V, COL_RET_G = 0, 1, 2, 3
COL_HG_Q, COL_HG_FF, COL_HG_FB, COL_HG_I, COL_HG_G = 4, 5, 6, 7, 8


def _cparams(sem):
    return pltpu.CompilerParams(dimension_semantics=sem, vmem_limit_bytes=VMEM_LIMIT)


def _sigmoid(x):
    return 1.0 / (1.0 + jnp.exp(-x))


def _silu(x):
    return x * _sigmoid(x)


def _dot(a, b):
    return jnp.dot(a, b, preferred_element_type=F32)


def _dot_nt(a, b):
    return lax.dot_general(a, b, (((1,), (1,)), ((), ())), preferred_element_type=F32)


def _split_dot_l(x, m, n):
    acc = None
    rem = x
    for i in range(n):
        piece = rem.astype(BF16)
        d = _dot(piece, m)
        acc = d if acc is None else acc + d
        if i + 1 < n:
            rem = rem - piece.astype(F32)
    return acc


def _split_dot_r(m, x, n):
    acc = None
    rem = x
    for i in range(n):
        piece = rem.astype(BF16)
        d = _dot(m, piece)
        acc = d if acc is None else acc + d
        if i + 1 < n:
            rem = rem - piece.astype(F32)
    return acc


def _pack_bf16_pairs(x):
    half = x.shape[1] // 2
    bits = lax.bitcast_convert_type(x.astype(BF16).astype(F32), jnp.uint32)
    return (bits[:, :half] >> 16) | (bits[:, half:] & jnp.uint32(0xFFFF0000))


def _unpack_bf16_pairs(p, dtype=BF16):
    lo = lax.bitcast_convert_type(p << 16, F32)
    hi = lax.bitcast_convert_type(p & jnp.uint32(0xFFFF0000), F32)
    return jnp.concatenate([lo, hi], axis=1).astype(dtype)


def _seg_rms(x, m, gain):
    return x * lax.rsqrt(_split_dot_l(x * x, m, 2) + EPS) * gain


def _ada_kernel(c_ref, w_ref, b_ref, o_ref):
    s = _silu(c_ref[...])
    o_ref[0] = jnp.dot(s, w_ref[0], preferred_element_type=F32,
                       precision=lax.Precision.HIGHEST) + b_ref[0]


def _ada_all(cc, w_ada, b_ada):
    n_layers, d, d6 = w_ada.shape
    bn = 512
    rows = cc.shape[0]
    return pl.pallas_call(
        _ada_kernel,
        grid=(n_layers, d6 // bn),
        in_specs=[pl.BlockSpec((rows, d), lambda l, j: (0, 0)),
                  pl.BlockSpec((1, d, bn), lambda l, j: (l, 0, j)),
                  pl.BlockSpec((1, 1, bn), lambda l, j: (l, 0, j))],
        out_specs=pl.BlockSpec((1, rows, bn), lambda l, j: (l, 0, j)),
        out_shape=jax.ShapeDtypeStruct((n_layers, rows, d6), F32),
        compiler_params=_cparams(("arbitrary", "arbitrary")),
    )(cc, w_ada, b_ada.reshape(n_layers, 1, d6))


def _mod_spec(d6, n_lat_tiles):
    return pl.BlockSpec((1, 1, d6), lambda b, j: (2 * b + (j >= n_lat_tiles).astype(jnp.int32), 0, 0))


def _modulate(x, shift, scale):
    xn = x * lax.rsqrt(jnp.mean(x * x, axis=-1, keepdims=True) + EPS)
    return xn * (1.0 + scale) + shift


def _inproj_prep_kernel(x_ref, mod_ref, w_ref, c_ref, s_ref, wuq_ref, wk_ref, wv_ref, gcq_ref, gckv_ref, gkr_ref,
                        gq_ref, gk_ref, mq_ref, mk_ref, gnq_ref, gnk_ref, mn_ref,
                        p_ref, qm_ref, km_ref, vm_ref, qn_ref, kn_ref, vn_ref, *, d):
    xm = _modulate(x_ref[0], mod_ref[0, :, 0:d], mod_ref[0, :, d:2 * d]).astype(BF16)
    p_ref[0] = _dot(xm, w_ref[:, P_ATTN_COLS:])
    pa = _dot(xm, w_ref[:, 0:P_ATTN_COLS])

    cq = pa[:, 0:256]
    ckv = pa[:, 256:384]
    kr = pa[:, 384:512]
    cqn = cq * lax.rsqrt(jnp.sum(cq * cq, axis=-1, keepdims=True) * (1.0 / MLA_Q_LORA) + EPS) * gcq_ref[...]
    ckvn = (ckv * lax.rsqrt(jnp.mean(ckv * ckv, axis=-1, keepdims=True) + EPS) * gckv_ref[...]).astype(BF16)
    krn = kr * lax.rsqrt(jnp.sum(kr * kr, axis=-1, keepdims=True) * (1.0 / MLA_ROPE) + EPS) * gkr_ref[...]
    q = _seg_rms(_dot(cqn.astype(BF16), wuq_ref[...]), mq_ref[...], gq_ref[...])
    kk = _seg_rms(_dot(ckvn, wk_ref[...]), mk_ref[...], gk_ref[...])
    vv = _dot(ckvn, wv_ref[...])

    cos = c_ref[...]
    sin = s_ref[...]
    lane = lax.broadcasted_iota(jnp.int32, (TM, LANES), 1)
    first = (lane % 16) < 8

    def rope(x):
        partner = jnp.where(first, pltpu.roll(x, LANES - 8, 1), pltpu.roll(x, 8, 1))
        return x * cos + partner * sin

    krr = rope(krn)
    nq = _seg_rms(pa[:, 512:1024], mn_ref[...], gnq_ref[...])
    nk = _seg_rms(pa[:, 1024:1536], mn_ref[...], gnk_ref[...])
    for h in range(N_HEADS):
        sl = slice(h * LANES, (h + 1) * LANES)
        qm_ref[0, h] = (rope(q[:, sl]) * (MLA_SCALE * LOG2E)).astype(BF16)
        km_ref[0, h] = (kk[:, sl] + krr).astype(BF16)
        vm_ref[0, h] = vv[:, sl].T.astype(BF16)
        qn_ref[0, h] = (nq[:, sl] * (NA_SCALE * LOG2E)).astype(BF16)
        kn_ref[0, h] = nk[:, sl].astype(BF16)
        vn_ref[0, h] = pa[:, 1536 + h * LANES:1536 + (h + 1) * LANES].astype(BF16)


def _inproj_prep(xx, modsel, w_in_p, rope_c, rope_s, pw, n_lat_tiles):
    b, nt, d = xx.shape
    rest = w_in_p.shape[1] - P_ATTN_COLS
    full = lambda a: pl.BlockSpec(a.shape, lambda i, j: (0,) * a.ndim)
    consts = [pw['wuq'], pw['wk'], pw['wv'], pw['gcq'], pw['gckv'], pw['gkr'], pw['gq'], pw['gk'],
              pw['mq'], pw['mk'], pw['gnq'], pw['gnk'], pw['mn']]
    head_spec = pl.BlockSpec((1, N_HEADS, TM, LANES), lambda i, j: (i, 0, j, 0))
    head_shape = jax.ShapeDtypeStruct((b, N_HEADS, nt, LANES), BF16)
    head_t_spec = pl.BlockSpec((1, N_HEADS, LANES, TM), lambda i, j: (i, 0, 0, j))
    head_t_shape = jax.ShapeDtypeStruct((b, N_HEADS, LANES, nt), BF16)
    return pl.pallas_call(
        functools.partial(_inproj_prep_kernel, d=d),
        grid=(b, nt // TM),
        in_specs=[pl.BlockSpec((1, TM, d), lambda i, j: (i, j, 0)),
                  _mod_spec(6 * d, n_lat_tiles),
                  full(w_in_p),
                  pl.BlockSpec((TM, LANES), lambda i, j: (j, 0)),
                  pl.BlockSpec((TM, LANES), lambda i, j: (j, 0))] + [full(a) for a in consts],
        out_specs=[pl.BlockSpec((1, TM, rest), lambda i, j: (i, j, 0)),
                   head_spec, head_spec, head_t_spec, head_spec, head_spec, head_spec],
        out_shape=[jax.ShapeDtypeStruct((b, nt, rest), F32),
                   head_shape, head_shape, head_t_shape, head_shape, head_shape, head_shape],
        compiler_params=_cparams(("arbitrary", "arbitrary")),
    )(xx, modsel, w_in_p, rope_c, rope_s, *consts)


def _softmax2_pv(s, v):
    m = jnp.max(s, axis=-1, keepdims=True)
    e = jnp.exp2(s - m)
    l = jnp.sum(e, axis=-1, keepdims=True)
    return _dot(e.astype(BF16), v) / l


MLA_KEY_BLOCK = 1088
MLA_Q_TILE = 512


MLA_HEADS_PER_STEP = 4


def _mla_attend(q_ref, k_ref, vt_ref, o_ref, n_q, k0, k1):
    items = [(hh, s0) for hh in range(MLA_HEADS_PER_STEP) for s0 in range(k0, k1, MLA_KEY_BLOCK)]
    score = lambda hh, s0: _dot_nt(k_ref[0, hh, s0:min(s0 + MLA_KEY_BLOCK, k1), :], q_ref[0, hh, 0:n_q, :])
    st = score(*items[0])
    m = l = acc = None
    for i, (hh, s0) in enumerate(items):
        st_next = score(*items[i + 1]) if i + 1 < len(items) else None
        vt_blk = vt_ref[0, hh, :, s0:min(s0 + MLA_KEY_BLOCK, k1)]
        bm = jnp.max(st, axis=0, keepdims=True)
        if s0 == k0:
            m = bm
            e = jnp.exp2(st - m)
            l = jnp.sum(e, axis=0, keepdims=True)
            acc = _dot(vt_blk, e.astype(BF16))
        else:
            m_new = jnp.maximum(m, bm)
            alpha = jnp.exp2(m - m_new)
            e = jnp.exp2(st - m_new)
            l = l * alpha + jnp.sum(e, axis=0, keepdims=True)
            acc = acc * alpha + _dot(vt_blk, e.astype(BF16))
            m = m_new
        if s0 + MLA_KEY_BLOCK >= k1:
            o_ref[0, 0:n_q, hh * LANES:(hh + 1) * LANES] = (acc / l).T
        st = st_next


def _mla_kernel(q_ref, k_ref, vt_ref, o_ref, *, n_lat, n_ctx):
    j = pl.program_id(2)

    @pl.when(j < n_lat // MLA_Q_TILE)
    def _():
        _mla_attend(q_ref, k_ref, vt_ref, o_ref, MLA_Q_TILE, 0, n_lat + n_ctx)

    @pl.when(j >= n_lat // MLA_Q_TILE)
    def _():
        _mla_attend(q_ref, k_ref, vt_ref, o_ref, n_ctx, n_lat, n_lat + n_ctx)


def _mla_attn(qm, km, vmt, n_lat, n_ctx):
    b, h, nt, _ = qm.shape
    hp = MLA_HEADS_PER_STEP
    tq = MLA_Q_TILE
    assert n_lat % tq == 0 and n_ctx <= tq
    kv_spec = pl.BlockSpec((1, hp, nt, LANES), lambda i, hh, j: (i, hh, 0, 0))
    vt_spec = pl.BlockSpec((1, hp, LANES, nt), lambda i, hh, j: (i, hh, 0, 0))
    return pl.pallas_call(
        functools.partial(_mla_kernel, n_lat=n_lat, n_ctx=n_ctx),
        grid=(b, h // hp, n_lat // tq + 1),
        in_specs=[pl.BlockSpec((1, hp, tq, LANES), lambda i, hh, j: (i, hh, j, 0)), kv_spec, vt_spec],
        out_specs=pl.BlockSpec((1, tq, hp * LANES), lambda i, hh, j: (i, j, hh)),
        out_shape=jax.ShapeDtypeStruct((b, nt, PAD_W), F32),
        compiler_params=_cparams(("arbitrary", "arbitrary", "arbitrary")),
    )(qm, km, vmt)


def _na_kernel(q_ref, k_ref, v_ref, bias_ref, o_ref, *, n_lat, n_ctx):
    j = pl.program_id(1)
    rows = n_lat // GRID_W
    n_tiles = n_lat // TM
    win = NA_WIN_ROWS * GRID_W

    @pl.when(j < n_tiles)
    def _():
        start = jnp.clip(j * NA_TILE_ROWS - NA_KH // 2, 0, rows - NA_WIN_ROWS)
        case = jnp.where(j == 0, 0, jnp.where(j == n_tiles - 1, 2, 1))
        tok0 = pl.multiple_of(start * GRID_W, GRID_W)
        def scores(h):
            q = q_ref[0, h]
            return (_dot_nt(q, k_ref[0, h, pl.ds(tok0, win), :]) + bias_ref[case, h],
                    _dot_nt(q, k_ref[0, h, pl.ds(n_lat, n_ctx), :]))

        nxt = scores(0)
        for h in range(N_HEADS):
            s1, s2 = nxt
            if h + 1 < N_HEADS:
                nxt = scores(h + 1)
            m = jnp.maximum(jnp.max(s1, axis=-1, keepdims=True), jnp.max(s2, axis=-1, keepdims=True))
            e1 = jnp.exp2(s1 - m)
            e2 = jnp.exp2(s2 - m)
            l = jnp.sum(e1, axis=-1, keepdims=True) + jnp.sum(e2, axis=-1, keepdims=True)
            o = _dot(e1.astype(BF16), v_ref[0, h, pl.ds(tok0, win), :])
            o = o + _dot(e2.astype(BF16), v_ref[0, h, pl.ds(n_lat, n_ctx), :])
            o_ref[0, :, h * LANES:(h + 1) * LANES] = o / l

    @pl.when(j >= n_lat // TM)
    def _():
        for h in range(N_HEADS):
            s = _dot_nt(q_ref[0, h], k_ref[0, h, pl.ds(n_lat, n_ctx), :])
            o_ref[0, :, h * LANES:(h + 1) * LANES] = _softmax2_pv(s, v_ref[0, h, pl.ds(n_lat, n_ctx), :])


def _na_attn(qn, kn, vn, bias, n_lat, n_ctx):
    b, h, nt, _ = qn.shape
    kv_spec = pl.BlockSpec((1, h, nt, LANES), lambda i, j: (i, 0, 0, 0))
    return pl.pallas_call(
        functools.partial(_na_kernel, n_lat=n_lat, n_ctx=n_ctx),
        grid=(b, nt // TM),
        in_specs=[pl.BlockSpec((1, h, TM, LANES), lambda i, j: (i, 0, j, 0)), kv_spec, kv_spec,
                  pl.BlockSpec(bias.shape, lambda i, j: (0, 0, 0, 0))],
        out_specs=pl.BlockSpec((1, TM, PAD_W), lambda i, j: (i, j, 0)),
        out_shape=jax.ShapeDtypeStruct((b, nt, PAD_W), F32),
        compiler_params=_cparams(("arbitrary", "arbitrary")),
    )(qn, kn, vn, bias)


def _head_mask(h, shape):
    return (lax.broadcasted_iota(jnp.int32, shape, 1) // HEAD_DIM) == h


def _ret_state_step(s_ref, q, k, v, qw, kw, cd, bd):
    state = s_ref[...]
    o = _dot((q * qw).astype(BF16), state.astype(BF16))
    upd = _dot((k * kw).T.astype(BF16), v.astype(BF16))
    s_ref[...] = state * cd + upd * bd
    return o


def _ret_fwd_body(q_ref, k_ref, v_ref, dm_ref, qw_ref, kw_ref, cd_ref, bd_ref, o_ref, s_ref):
    for c in range(TM // RET_CHUNK):
        rows = slice(c * RET_CHUNK, (c + 1) * RET_CHUNK)
        q = q_ref[0, rows]
        k = k_ref[0, rows]
        v = v_ref[0, rows]
        o = _ret_state_step(s_ref, q, k, v, qw_ref[...], kw_ref[...], cd_ref[...], bd_ref[...])
        kb = k.astype(BF16)
        for h in range(N_HEADS):
            hm = _head_mask(h, q.shape)
            sc = _dot_nt(jnp.where(hm, q, 0.0).astype(BF16), kb) * dm_ref[h]
            o = o + _dot(sc.astype(BF16), jnp.where(hm, v, 0.0).astype(BF16))
        o_ref[0, rows] = o


def _ret_bwd_body(q_ref, k_ref, v_ref, g_ref, op_ref, qw_ref, kw_ref, cd_ref, bd_ref, ms_ref, go_ref,
                  y_ref, s_ref):
    for c in reversed(range(TM // RET_CHUNK)):
        rows = slice(c * RET_CHUNK, (c + 1) * RET_CHUNK)
        o = op_ref[0, rows] + _ret_state_step(s_ref, q_ref[0, rows], k_ref[0, rows], v_ref[0, rows], qw_ref[...],
                                              kw_ref[...], cd_ref[...], bd_ref[...])
        y_ref[0, rows] = _seg_rms(o, ms_ref[...], go_ref[...]) * _silu(g_ref[0, rows])


def _scan_order(n_lat_t, n_ctx_t, reverse):
    if reverse:
        return lambda i: n_lat_t + n_ctx_t - 1 - i
    return lambda i: jnp.where(i < n_ctx_t, n_lat_t + i, i - n_ctx_t)


def _hg_direction(q_ref, f_ref, v_ref, lb_ref, ain_ref, aex_ref, bseg_ref, bd_ref, st_ref, sh_ref, *, reverse):
    n_chunks = TM // HG_CHUNK
    assert n_chunks == HG_CHUNK
    qh = _silu(q_ref[0])
    lb = lb_ref[...]
    f = jnp.maximum(lb + (1.0 - lb) * _sigmoid(f_ref[0]), F_FLOOR)
    lf = jnp.log(f) * LOG2E
    k = 1.0 - f
    v = v_ref[0]
    row = lax.broadcasted_iota(jnp.int32, (TM, 1), 0)
    pos = row % HG_CHUNK
    row_chunk = row // HG_CHUNK

    a_in = _split_dot_r(ain_ref[...], lf, 3)
    a_ex = _split_dot_r(aex_ref[...], lf, 3)
    width = v.shape[1]
    key_exp = a_in - jnp.log(k) * LOG2E
    for slot, val in enumerate((key_exp, v)):
        sh_ref[slot] = val.reshape(n_chunks, HG_CHUNK, width)

    def key_row(slot, s):
        return jnp.broadcast_to(sh_ref[slot, :, s:s + 1, :], (n_chunks, HG_CHUNK, width)).reshape(TM, width)
    qp = (qh * jnp.exp2(a_in)).astype(BF16)
    kdec = k * jnp.exp2(a_ex)
    lam_all = jnp.exp2(a_in + a_ex)
    vt = v.T.astype(BF16)
    bd = bd_ref[...]
    bseg = bseg_ref[...]
    state = st_ref[...]
    parts = [None] * n_chunks
    o_band = jnp.zeros_like(v)
    upds = [_dot(vt, jnp.where(row_chunk == c, kdec, 0.0).astype(BF16)) * bd for c in range(n_chunks)]
    for step in range(n_chunks):
        c = n_chunks - 1 - step if reverse else step
        r0 = c * HG_CHUNK
        parts[c] = _dot_nt(qp[r0:r0 + HG_CHUNK], state.astype(BF16))
        state = state * lam_all[r0:r0 + 1] + upds[c]
        s = step
        valid = (pos <= s) if reverse else (pos >= s)
        w = jnp.where(valid, qh * jnp.exp2(a_in - key_row(0, s)), 0.0)
        o_band = o_band + _dot(w.astype(BF16), bseg) * key_row(1, s)
    st_ref[...] = state
    return jnp.concatenate(parts, axis=0) + o_band


def _scan_fwd_kernel(rq_ref, rk_ref, rv_ref, hq_ref, hf_ref, hv_ref, dm_ref, qw_ref, kw_ref, cd_ref, bd_ref,
                     lb_ref, ain_ref, aex_ref, bseg_ref, ro_ref, ho_ref, rs_ref, hs_ref, sh_ref):
    @pl.when(pl.program_id(1) == 0)
    def _():
        rs_ref[...] = jnp.zeros_like(rs_ref)
        hs_ref[...] = jnp.zeros_like(hs_ref)

    _ret_fwd_body(rq_ref, rk_ref, rv_ref, dm_ref, qw_ref, kw_ref, cd_ref, bd_ref, ro_ref, rs_ref)
    ho_ref[0] = _hg_direction(hq_ref, hf_ref, hv_ref, lb_ref, ain_ref, aex_ref, bseg_ref, bd_ref, hs_ref, sh_ref,
                              reverse=False)


def _scan_bwd_kernel(rq_ref, rk_ref, rv_ref, rg_ref, rop_ref, hq_ref, hf_ref, hv_ref, hg_ref, hop_ref,
                     qw_ref, kw_ref, cd_ref, bd_ref, ms_ref, rgo_ref, lb_ref, ain_ref, aex_ref, bseg_ref, hgo_ref,
                     ry_ref, hy_ref, rs_ref, hs_ref, sh_ref):
    @pl.when(pl.program_id(1) == 0)
    def _():
        rs_ref[...] = jnp.zeros_like(rs_ref)
        hs_ref[...] = jnp.zeros_like(hs_ref)

    _ret_bwd_body(rq_ref, rk_ref, rv_ref, rg_ref, rop_ref, qw_ref, kw_ref, cd_ref, bd_ref, ms_ref, rgo_ref,
                  ry_ref, rs_ref)
    o = hop_ref[0] + _hg_direction(hq_ref, hf_ref, hv_ref, lb_ref, ain_ref, aex_ref, bseg_ref, bd_ref, hs_ref,
                                   sh_ref, reverse=True)
    hy_ref[0] = _seg_rms(o, ms_ref[...], hgo_ref[...]) * _silu(hg_ref[0])


def _recurrent_mixers(p, rc, hc, lb, ret_go, hg_go, n_lat, n_ctx):
    b, nt, _ = p.shape
    n_lat_t, n_ctx_t = n_lat // TM, n_ctx // TM
    fwd = _scan_order(n_lat_t, n_ctx_t, False)
    bwd = _scan_order(n_lat_t, n_ctx_t, True)
    col = lambda order, cb: pl.BlockSpec((1, TM, GROUP_W), lambda i, j: (i, order(j), cb))
    full = lambda a: pl.BlockSpec(a.shape, lambda i, j: (0,) * a.ndim)
    out_shape = jax.ShapeDtypeStruct((b, nt, GROUP_W), F32)
    scratch = [pltpu.VMEM((GROUP_W, GROUP_W), F32), pltpu.VMEM((GROUP_W, GROUP_W), F32),
               pltpu.VMEM((2, TM // HG_CHUNK, HG_CHUNK, GROUP_W), F32)]
    consts_f = [rc['dm'], rc['qw_f'], rc['kw_f'], rc['cd_f'], rc['bd'], lb, hc['lincl'], hc['uexcl'], hc['bseg']]
    out_f = pl.BlockSpec((1, TM, GROUP_W), lambda i, j: (i, fwd(j), 0))
    ret_part, hg_part = pl.pallas_call(
        _scan_fwd_kernel,
        grid=(b, nt // TM),
        in_specs=[col(fwd, COL_RET_Q), col(fwd, COL_RET_K), col(fwd, COL_RET_V),
                  col(fwd, COL_HG_Q), col(fwd, COL_HG_FF), col(fwd, COL_HG_I)] + [full(a) for a in consts_f],
        out_specs=[out_f, out_f],
        out_shape=[out_shape, out_shape],
        scratch_shapes=scratch,
        compiler_params=_cparams(("arbitrary", "arbitrary")),
    )(p, p, p, p, p, p, *consts_f)
    consts_b = [rc['qw_b'], rc['kw_b'], rc['cd_b'], rc['bd'], rc['ms'], ret_go,
                lb, hc['uincl'], hc['lexcl'], hc['bseg'], hg_go]
    out_b = pl.BlockSpec((1, TM, GROUP_W), lambda i, j: (i, bwd(j), 0))
    return pl.pallas_call(
        _scan_bwd_kernel,
        grid=(b, nt // TM),
        in_specs=[col(bwd, COL_RET_Q), col(bwd, COL_RET_K), col(bwd, COL_RET_V), col(bwd, COL_RET_G), out_b,
                  col(bwd, COL_HG_Q), col(bwd, COL_HG_FB), col(bwd, COL_HG_I), col(bwd, COL_HG_G), out_b]
                 + [full(a) for a in consts_b],
        out_specs=[out_b, out_b],
        out_shape=[out_shape, out_shape],
        scratch_shapes=scratch,
        compiler_params=_cparams(("arbitrary", "arbitrary")),
    )(p, p, p, p, ret_part, p, p, p, p, hg_part, *consts_b)


def _outproj_router_kernel(x_ref, ym_ref, yn_ref, yr_ref, yh_ref, mod_ref, wm_ref, wn_ref, wr_ref, wh_ref,
                           whl_ref, br_ref, ltri_ref, o_ref, r_ref, cnt_ref, *, d):
    acc = _dot(ym_ref[0].astype(BF16), wm_ref[...])
    acc = acc + _dot(yn_ref[0].astype(BF16), wn_ref[...])
    acc = acc + _dot(yr_ref[0].astype(BF16), wr_ref[...])
    acc = acc + _dot(yh_ref[0].astype(BF16), wh_ref[...])
    x_new = x_ref[0] + mod_ref[0, :, 2 * d:3 * d] * acc
    o_ref[0] = x_new
    h = _modulate(x_new, mod_ref[0, :, 3 * d:4 * d], mod_ref[0, :, 4 * d:5 * d])
    _route(h, whl_ref, br_ref, ltri_ref, r_ref, cnt_ref)


def _outproj_router(xx, y_mla, y_na, y_ret, y_hg, modsel, ow, rw, n_lat_tiles):
    b, nt, d = xx.shape
    tiles = nt // TM
    tile = lambda w: pl.BlockSpec((1, TM, w), lambda i, j: (i, j, 0))
    full = lambda a: pl.BlockSpec(a.shape, lambda i, j: (0, 0))
    ltri = jnp.asarray(np.tril(np.ones((TM, TM), np.float32), -1), BF16)
    ws = [ow['mla'], ow['na'], ow['ret'], ow['hg'], rw['hi_lo'], rw['b'], ltri]
    return pl.pallas_call(
        functools.partial(_outproj_router_kernel, d=d),
        grid=(b, tiles),
        in_specs=[tile(d), tile(PAD_W), tile(PAD_W), tile(GROUP_W), tile(GROUP_W),
                  _mod_spec(6 * d, n_lat_tiles)] + [full(a) for a in ws],
        out_specs=[tile(d), pl.BlockSpec((TM, LANES), lambda i, j: (i * tiles + j, 0)),
                   pl.BlockSpec((1, LANES), lambda i, j: (0, 0))],
        out_shape=[jax.ShapeDtypeStruct((b, nt, d), F32), jax.ShapeDtypeStruct((b * nt, LANES), F32),
                   jax.ShapeDtypeStruct((1, LANES), F32)],
        compiler_params=_cparams(("arbitrary", "arbitrary")),
    )(xx, y_mla, y_na, y_ret, y_hg, modsel, *ws)


def _route(h, whl_ref, br_ref, ltri_ref, r_ref, cnt_ref):
    @pl.when((pl.program_id(0) == 0) & (pl.program_id(1) == 0))
    def _():
        cnt_ref[...] = jnp.zeros_like(cnt_ref)

    h_hi = h.astype(BF16)
    h_lo = (h - h_hi.astype(F32)).astype(BF16)
    hi_terms = _dot(h_hi, whl_ref[...])
    lg = hi_terms[:, :LANES] + hi_terms[:, LANES:] + _dot(h_lo, whl_ref[:, :LANES]) + br_ref[...]

    lane = lax.broadcasted_iota(jnp.int32, lg.shape, 1).astype(F32)
    far = 1e9

    def first_argmax(vals, vmax):
        return jnp.min(jnp.where(vals == vmax, lane, far), axis=-1, keepdims=True)

    gl = jnp.where(lane < MOE_GROUPS, lg, NEG_BIG)
    gmax = jnp.max(gl, axis=-1, keepdims=True)
    pg_top = 1.0 / jnp.sum(jnp.exp(gl - gmax), axis=-1, keepdims=True)
    lo = MOE_GROUPS + MOE_PER_GROUP * first_argmax(gl, gmax)
    fl = jnp.where((lane >= lo) & (lane < lo + MOE_PER_GROUP), lg, NEG_BIG)
    fmax = jnp.max(fl, axis=-1, keepdims=True)
    fsum = jnp.sum(jnp.exp(fl - fmax), axis=-1, keepdims=True)
    i1 = first_argmax(fl, fmax)
    fl2 = jnp.where(lane == i1, NEG_BIG, fl)
    f2max = jnp.max(fl2, axis=-1, keepdims=True)
    i2 = first_argmax(fl2, f2max)
    p1 = 1.0 / fsum
    p2 = jnp.exp(f2max - fmax) / fsum
    g1 = pg_top * p1 / (p1 + p2)
    g2 = pg_top * p2 / (p1 + p2)
    e1 = i1 - MOE_GROUPS
    e2 = i2 - MOE_GROUPS

    onehot = jnp.where(lane == e1, 1.0, 0.0) + jnp.where(lane == e2, 1.0, 0.0)
    before = cnt_ref[...] + _dot(ltri_ref[...], onehot.astype(BF16))
    r1 = jnp.sum(jnp.where(lane == e1, before, 0.0), axis=-1, keepdims=True)
    r2 = jnp.sum(jnp.where(lane == e2, before, 0.0), axis=-1, keepdims=True)
    cnt_ref[...] += jnp.sum(onehot, axis=0, keepdims=True)

    out = jnp.zeros_like(lg)
    for col, val in enumerate((e1, e2, g1, g2, r1, r2)):
        out = jnp.where(lane == col, val, out)
    r_ref[...] = out


ROUTE_E, ROUTE_G, ROUTE_R = 0, 2, 4


def _moe_plan(route, counts_f, n_tok):
    counts = counts_f[0, :MOE_EXPERTS].astype(jnp.int32)
    padded = (counts + MOE_BLOCK - 1) // MOE_BLOCK * MOE_BLOCK
    pad_end = jnp.cumsum(padded)
    pad_start = pad_end - padded
    n_blocks = -(-(n_tok * MOE_TOPK) // MOE_BLOCK) + MOE_EXPERTS
    blk0 = jnp.arange(n_blocks, dtype=jnp.int32) * MOE_BLOCK
    block_expert = jnp.minimum(jnp.sum((pad_end[None, :] <= blk0[:, None]).astype(jnp.int32), axis=1),
                               MOE_EXPERTS - 1)
    used = (pad_end[-1] // MOE_BLOCK).reshape(1)
    expert = route[:, ROUTE_E:ROUTE_E + MOE_TOPK].astype(jnp.int32)
    rank = route[:, ROUTE_R:ROUTE_R + MOE_TOPK].astype(jnp.int32)
    start_of = jnp.sum(jnp.where(expert[..., None] == jnp.arange(MOE_EXPERTS, dtype=jnp.int32), pad_start, 0), axis=-1)
    return start_of + rank, block_expert, used, n_blocks


def _idx_blocks(dest, k, n_tiles):
    return dest[:, k].reshape(n_tiles, 1, TM)


def _dispatch_kernel(d0_ref, d1_ref, x_ref, mod_ref, rows_in, rows_out, h_ref, sem, *, d):
    del rows_in
    step = pl.program_id(0) * pl.num_programs(1) + pl.program_id(1)
    n_steps = pl.num_programs(0) * pl.num_programs(1)
    slot = step % 2

    def wait_slot(s):
        for _ in range(MOE_TOPK):
            pltpu.make_async_copy(h_ref.at[s], rows_out.at[pl.ds(0, TM)], sem.at[s]).wait()

    h = _modulate(x_ref[0], mod_ref[0, :, 3 * d:4 * d], mod_ref[0, :, 4 * d:5 * d])
    h_ref[slot] = _pack_bf16_pairs(h)

    def issue(r, carry):
        src = h_ref.at[slot, pl.ds(r, 1)]
        pltpu.make_async_copy(src, rows_out.at[pl.ds(d0_ref[0, 0, r], 1)], sem.at[slot]).start()
        pltpu.make_async_copy(src, rows_out.at[pl.ds(d1_ref[0, 0, r], 1)], sem.at[slot]).start()
        return carry

    lax.fori_loop(0, TM, issue, 0, unroll=DMA_ISSUE_UNROLL)

    @pl.when(step > 0)
    def _():
        wait_slot(1 - slot)

    @pl.when(step == n_steps - 1)
    def _():
        wait_slot(slot)


def _dispatch(xx, modsel, dest, n_rows, n_lat_tiles):
    b, nt, d = xx.shape
    tiles = nt // TM
    idx_spec = pl.BlockSpec((1, 1, TM), lambda i, j: (i * tiles + j, 0, 0), memory_space=pltpu.SMEM)
    return pl.pallas_call(
        functools.partial(_dispatch_kernel, d=d),
        grid=(b, tiles),
        in_specs=[idx_spec, idx_spec, pl.BlockSpec((1, TM, d), lambda i, j: (i, j, 0)),
                  _mod_spec(6 * d, n_lat_tiles), pl.BlockSpec(memory_space=pl.ANY)],
        out_specs=pl.BlockSpec(memory_space=pl.ANY),
        out_shape=jax.ShapeDtypeStruct((n_rows, d // 2), jnp.uint32),
        scratch_shapes=[pltpu.VMEM((2, TM, d // 2), jnp.uint32), pltpu.SemaphoreType.DMA((2,))],
        input_output_aliases={4: 0},
        compiler_params=_cparams(("arbitrary", "arbitrary")),
    )(_idx_blocks(dest, 0, b * tiles), _idx_blocks(dest, 1, b * tiles), xx, modsel,
      jnp.zeros((n_rows, d // 2), jnp.uint32))


def _ffn_kernel(be_ref, used_ref, x_ref, w1_ref, w3_ref, w2_ref, y_ref, w1b_ref, w3b_ref, w2b_ref):
    i = pl.program_id(0)

    @pl.when((i == 0) | (be_ref[i] != be_ref[jnp.maximum(i - 1, 0)]))
    def _():
        w1b_ref[...] = w1_ref[0, 0].astype(BF16)
        w3b_ref[...] = w3_ref[0, 0].astype(BF16)
        w2b_ref[...] = w2_ref[0, 0].astype(BF16)

    @pl.when(i < used_ref[0])
    def _():
        x = _unpack_bf16_pairs(x_ref[...])
        mid = _silu(_dot(x, w1b_ref[...])) * _dot(x, w3b_ref[...])
        y_ref[...] = _pack_bf16_pairs(_dot(mid.astype(BF16), w2b_ref[...]))

    @pl.when(i >= used_ref[0])
    def _():
        y_ref[...] = jnp.zeros_like(y_ref)


def _moe_ffn(x_rows, block_expert, used, n_blocks, layer, w1, w3, w2):
    d = w1.shape[2]
    ff = w1.shape[3]
    grid_spec = pltpu.PrefetchScalarGridSpec(
        num_scalar_prefetch=2,
        grid=(n_blocks,),
        in_specs=[pl.BlockSpec((MOE_BLOCK, d // 2), lambda i, be, nu: (i, 0)),
                  pl.BlockSpec((1, 1, d, ff), lambda i, be, nu: (layer, be[i], 0, 0)),
                  pl.BlockSpec((1, 1, d, ff), lambda i, be, nu: (layer, be[i], 0, 0)),
                  pl.BlockSpec((1, 1, ff, d), lambda i, be, nu: (layer, be[i], 0, 0))],
        out_specs=pl.BlockSpec((MOE_BLOCK, d // 2), lambda i, be, nu: (i, 0)),
        scratch_shapes=[pltpu.VMEM((d, ff), BF16), pltpu.VMEM((d, ff), BF16), pltpu.VMEM((ff, d), BF16)],
    )
    return pl.pallas_call(
        _ffn_kernel,
        grid_spec=grid_spec,
        out_shape=jax.ShapeDtypeStruct((x_rows.shape[0], d // 2), jnp.uint32),
        compiler_params=_cparams(("arbitrary",)),
    )(block_expert, used, x_rows, w1, w3, w2)


def _row_gather(src_hbm, idx_ref, dst_ref, sem, n):
    def issue(r, carry):
        pltpu.make_async_copy(src_hbm.at[pl.ds(idx_ref[0, 0, r], 1)], dst_ref.at[pl.ds(r, 1)], sem).start()
        return carry

    lax.fori_loop(0, n, issue, 0, unroll=DMA_ISSUE_UNROLL)


def _row_gather_wait(src_hbm, dst_ref, sem, n):
    pltpu.make_async_copy(src_hbm.at[pl.ds(0, n)], dst_ref, sem).wait()


def _combine_kernel(d0_ref, d1_ref, n0_ref, n1_ref, x_ref, mod_ref, r_ref, y_hbm, o_ref, y_ref, sem, *, d):
    step = pl.program_id(0) * pl.num_programs(1) + pl.program_id(1)
    n_steps = pl.num_programs(0) * pl.num_programs(1)
    slot = step % 2

    def gather(idx_refs, s):
        for k, idx_ref in enumerate(idx_refs):
            _row_gather(y_hbm, idx_ref, y_ref.at[s, k], sem.at[s, k], TM)

    @pl.when(step == 0)
    def _():
        gather((d0_ref, d1_ref), slot)

    @pl.when(step + 1 < n_steps)
    def _():
        gather((n0_ref, n1_ref), 1 - slot)

    route = r_ref[...]
    lane = lax.broadcasted_iota(jnp.int32, route.shape, 1)
    g0 = jnp.sum(jnp.where(lane == ROUTE_G, route, 0.0), axis=-1, keepdims=True)
    g1 = jnp.sum(jnp.where(lane == ROUTE_G + 1, route, 0.0), axis=-1, keepdims=True)
    for k in range(MOE_TOPK):
        _row_gather_wait(y_hbm, y_ref.at[slot, k], sem.at[slot, k], TM)
    y0 = _unpack_bf16_pairs(y_ref[slot, 0], F32)
    y1 = _unpack_bf16_pairs(y_ref[slot, 1], F32)
    o_ref[0] = x_ref[0] + mod_ref[0, :, 5 * d:6 * d] * (y0 * g0 + y1 * g1)


def _combine(xx, modsel, route, y_rows, dest, n_lat_tiles, out_tiles):
    b, nt, d = xx.shape
    tiles = nt // TM
    idx_spec = pl.BlockSpec((1, 1, TM), lambda i, j: (i * tiles + j, 0, 0), memory_space=pltpu.SMEM)

    def next_block(i, j):
        wrap = j + 1 >= out_tiles
        return (jnp.where(wrap, jnp.minimum(i + 1, b - 1) * tiles, i * tiles + j + 1), 0, 0)

    next_spec = pl.BlockSpec((1, 1, TM), next_block, memory_space=pltpu.SMEM)
    d0, d1 = _idx_blocks(dest, 0, b * tiles), _idx_blocks(dest, 1, b * tiles)
    return pl.pallas_call(
        functools.partial(_combine_kernel, d=d),
        grid=(b, out_tiles),
        in_specs=[idx_spec, idx_spec, next_spec, next_spec, pl.BlockSpec((1, TM, d), lambda i, j: (i, j, 0)),
                  _mod_spec(6 * d, n_lat_tiles),
                  pl.BlockSpec((TM, LANES), lambda i, j: (i * tiles + j, 0)),
                  pl.BlockSpec(memory_space=pl.ANY)],
        out_specs=pl.BlockSpec((1, TM, d), lambda i, j: (i, j, 0)),
        out_shape=jax.ShapeDtypeStruct((b, out_tiles * TM, d), F32),
        scratch_shapes=[pltpu.VMEM((2, MOE_TOPK, TM, d // 2), jnp.uint32), pltpu.SemaphoreType.DMA((2, MOE_TOPK))],
        compiler_params=_cparams(("arbitrary", "arbitrary")),
    )(d0, d1, d0, d1, xx, modsel, route, y_rows)


def _pad_heads_cols(w):
    lead = w.shape[:-1]
    w = w.reshape(*lead, N_HEADS, HEAD_DIM)
    w = jnp.concatenate([w, jnp.zeros_like(w)], axis=-1)
    return w.reshape(*lead, PAD_W)


def _pad_heads_rows(w):
    return _pad_heads_cols(w.T).T


def _seg_mean_matrix(width, segments):
    m = np.zeros((width, width), np.float32)
    for g in range(width // LANES):
        for start, length in segments:
            a = g * LANES + start
            m[a:a + length, a:a + length] = 1.0 / length
    return jnp.asarray(m, BF16)


def _rope_tables(n_lat, n_ctx):
    pos = jnp.arange(n_lat)
    rows = (pos // GRID_W).astype(F32)
    cols = (pos % GRID_W).astype(F32)
    per_axis = MLA_ROPE // 2
    inv_freq = ROPE_THETA ** (-jnp.arange(0, per_axis, 2, dtype=F32) / per_axis)
    ang = jnp.concatenate([rows[:, None] * inv_freq, cols[:, None] * inv_freq], axis=-1)
    i = np.arange(MLA_ROPE)
    src = (i // 16) * 8 + (i % 8)
    sign = np.where((i % 16) < 8, -1.0, 1.0).astype(np.float32)
    cos = jnp.ones((n_lat, LANES), F32).at[:, HEAD_DIM:HEAD_DIM + MLA_ROPE].set(jnp.cos(ang)[:, src])
    sin = jnp.zeros((n_lat, LANES), F32).at[:, HEAD_DIM:HEAD_DIM + MLA_ROPE].set(jnp.sin(ang)[:, src] * sign)
    cos = jnp.concatenate([cos, jnp.ones((n_ctx, LANES), F32)], axis=0)
    sin = jnp.concatenate([sin, jnp.zeros((n_ctx, LANES), F32)], axis=0)
    return cos, sin


def _na_bias_table(rpb):
    w = np.arange(GRID_W)
    col_start = np.clip(w - NA_KW // 2, 0, GRID_W - NA_KW)
    valid = (w[None, :] >= col_start[:, None]) & (w[None, :] < col_start[:, None] + NA_KW)
    dc = np.clip(w[None, :] - w[:, None], 1 - NA_KW, NA_KW - 1) + (NA_KW - 1)
    onehot = jnp.asarray(dc[None, :, :] == np.arange(2 * NA_KW - 1)[:, None, None], F32)
    t = jnp.einsum('hrd,dqk->hrqk', rpb.astype(F32), onehot, precision=lax.Precision.HIGHEST)
    t = jnp.where(jnp.asarray(valid)[None, None, :, :], t * LOG2E, NEG_BIG)
    masked = jnp.full((N_HEADS, GRID_W, GRID_W), NEG_BIG, F32)
    q_off = (0, NA_KH // 2, NA_KH)
    first = ([0] * NA_TILE_ROWS, list(range(NA_TILE_ROWS)), [NA_KH // 2] * NA_TILE_ROWS)
    cases = []
    for c in range(3):
        row_blocks = []
        for rr in range(NA_TILE_ROWS):
            blocks = []
            for jj in range(NA_WIN_ROWS):
                live = first[c][rr] <= jj < first[c][rr] + NA_KH
                dr = jj - (q_off[c] + rr) + (NA_KH - 1)
                blocks.append(t[:, dr] if live else masked)
            row_blocks.append(jnp.concatenate(blocks, axis=-1))
        cases.append(jnp.concatenate(row_blocks, axis=-2))
    return jnp.stack(cases, axis=0)


def _block_diag_mask(block):
    i = np.arange(GROUP_W) // block
    return (i[:, None] == i[None, :]).astype(np.float32)


def _retention_consts():
    c = RET_CHUNK
    j = np.arange(2 * N_HEADS, dtype=np.float64)
    lg = np.log1p(-np.exp2(-5.0 - j))
    lg_f, lg_b = lg[0::2], lg[1::2]
    pos = np.arange(c, dtype=np.float64)
    diff = pos[:, None] - pos[None, :]
    k_scale = HEAD_DIM ** -0.5
    dm = np.zeros((N_HEADS, c, c))
    for h in range(N_HEADS):
        dm[h] = (np.where(diff >= 0, np.exp(np.maximum(diff, 0.0) * lg_f[h]), 0.0)
                 + np.where(diff <= 0, np.exp(np.maximum(-diff, 0.0) * lg_b[h]), 0.0)) * k_scale
    lanes = lambda per_head: np.repeat(per_head, HEAD_DIM, axis=-1)
    out = {
        'dm': dm,
        'qw_f': lanes(np.exp((pos + 1)[:, None] * lg_f[None, :])),
        'kw_f': lanes(np.exp((c - 1 - pos)[:, None] * lg_f[None, :])) * k_scale,
        'cd_f': lanes(np.exp(c * lg_f)[None, :]),
        'qw_b': lanes(np.exp((c - pos)[:, None] * lg_b[None, :])),
        'kw_b': lanes(np.exp(pos[:, None] * lg_b[None, :])) * k_scale,
        'cd_b': lanes(np.exp(c * lg_b)[None, :]),
        'bd': _block_diag_mask(HEAD_DIM),
    }
    out = {k: jnp.asarray(v, F32) for k, v in out.items()}
    out['ms'] = jnp.asarray(_block_diag_mask(HEAD_DIM) / HEAD_DIM, BF16)
    return out


def _hgrn_consts():
    t = np.arange(TM)
    same = (t[:, None] // HG_CHUNK) == (t[None, :] // HG_CHUNK)
    lincl = same & (t[None, :] <= t[:, None])
    lexcl = same & (t[None, :] < t[:, None])
    return {
        'lincl': jnp.asarray(lincl, BF16), 'lexcl': jnp.asarray(lexcl, BF16),
        'uincl': jnp.asarray(lincl.T, BF16), 'uexcl': jnp.asarray(lexcl.T, BF16),
        'bseg': jnp.asarray(_block_diag_mask(HEAD_DIM), BF16),
        'bd': jnp.asarray(_block_diag_mask(HEAD_DIM), F32),
        'ms': jnp.asarray(_block_diag_mask(HEAD_DIM) / HEAD_DIM, BF16),
    }


def _layer_weights(l, w_in, w_out, mla_g_cq, mla_g_ckv, mla_w_uq, mla_w_ukv, mla_g_qn, mla_g_qr, mla_g_kn,
                   mla_g_kr, na_g_q, na_g_k, moe_w_rg, moe_b_rg, moe_w_re, moe_b_re):
    d = w_in.shape[1]
    w = w_in[l]
    z = lambda n: jnp.zeros((d, n), F32)
    o = 0
    cq, o = w[:, o:o + MLA_Q_LORA], o + MLA_Q_LORA
    ckv, o = w[:, o:o + MLA_KV_LORA], o + MLA_KV_LORA
    kr, o = w[:, o:o + MLA_ROPE], o + MLA_ROPE
    naq, o = w[:, o:o + GROUP_W], o + GROUP_W
    nak, o = w[:, o:o + GROUP_W], o + GROUP_W
    nav, o = w[:, o:o + GROUP_W], o + GROUP_W
    rest = w[:, o:]
    w_in_p = jnp.concatenate([cq, z(GROUP_W - MLA_Q_LORA), ckv, z(HEAD_DIM), kr, z(LANES - HEAD_DIM - MLA_ROPE),
                              _pad_heads_cols(naq), _pad_heads_cols(nak), _pad_heads_cols(nav), rest],
                             axis=1).astype(BF16)

    qk_dim = HEAD_DIM + MLA_ROPE
    wuq = mla_w_uq[l].reshape(MLA_Q_LORA, N_HEADS, qk_dim)
    wuq = jnp.concatenate([wuq, jnp.zeros((MLA_Q_LORA, N_HEADS, LANES - qk_dim), F32)], axis=-1)
    wuq = jnp.concatenate([wuq.reshape(MLA_Q_LORA, PAD_W), jnp.zeros((GROUP_W - MLA_Q_LORA, PAD_W), F32)], axis=0)
    wukv = mla_w_ukv[l].reshape(MLA_KV_LORA, N_HEADS, 2 * HEAD_DIM)
    pad64 = jnp.zeros((MLA_KV_LORA, N_HEADS, HEAD_DIM), F32)
    wk = jnp.concatenate([wukv[:, :, :HEAD_DIM], pad64], axis=-1).reshape(MLA_KV_LORA, PAD_W)
    wv = jnp.concatenate([wukv[:, :, HEAD_DIM:], pad64], axis=-1).reshape(MLA_KV_LORA, PAD_W)

    def per_head(parts):
        row = jnp.concatenate(parts + [jnp.zeros((LANES - sum(p.shape[0] for p in parts),), F32)])
        return jnp.tile(row, N_HEADS)[None, :]

    prep = {
        'wuq': wuq.astype(BF16), 'wk': wk.astype(BF16), 'wv': wv.astype(BF16),
        'gcq': jnp.concatenate([mla_g_cq[l], jnp.zeros((GROUP_W - MLA_Q_LORA,), F32)])[None, :],
        'gckv': mla_g_ckv[l][None, :],
        'gkr': jnp.concatenate([jnp.zeros((HEAD_DIM,), F32), mla_g_kr[l],
                                jnp.zeros((LANES - HEAD_DIM - MLA_ROPE,), F32)])[None, :],
        'gq': per_head([mla_g_qn[l], mla_g_qr[l]]),
        'gk': per_head([mla_g_kn[l]]),
        'mq': _seg_mean_matrix(PAD_W, [(0, HEAD_DIM), (HEAD_DIM, MLA_ROPE)]),
        'mk': _seg_mean_matrix(PAD_W, [(0, HEAD_DIM)]),
        'gnq': per_head([na_g_q[l]]),
        'gnk': per_head([na_g_k[l]]),
        'mn': _seg_mean_matrix(PAD_W, [(0, HEAD_DIM)]),
    }
    wo = w_out[l]
    ow = {
        'mla': _pad_heads_rows(wo[0:GROUP_W]).astype(BF16),
        'na': _pad_heads_rows(wo[GROUP_W:2 * GROUP_W]).astype(BF16),
        'ret': wo[2 * GROUP_W:3 * GROUP_W].astype(BF16),
        'hg': wo[3 * GROUP_W:4 * GROUP_W].astype(BF16),
    }
    n_r = MOE_GROUPS + MOE_EXPERTS
    wr = jnp.concatenate([moe_w_rg[l], moe_w_re[l], jnp.zeros((d, LANES - n_r), F32)], axis=1)
    wr_hi = wr.astype(BF16)
    rw = {
        'hi_lo': jnp.concatenate([wr_hi, (wr - wr_hi.astype(F32)).astype(BF16)], axis=1),
        'b': jnp.concatenate([moe_b_rg[l], moe_b_re[l], jnp.zeros((LANES - n_r,), F32)])[None, :],
    }
    return w_in_p, prep, ow, rw


def _layer(xx, modsel, lw, rope_c, rope_s, na_bias, rc, hc, hg_lb_l, ret_go, hg_go, layer, w1, w3, w2,
           n_lat, n_ctx, last):
    w_in_p, prep_w, ow, rw = lw
    b, nt, d = xx.shape
    n_lat_tiles = n_lat // TM
    p, qm, km, vm, qn, kn, vn = _inproj_prep(xx, modsel, w_in_p, rope_c, rope_s, prep_w, n_lat_tiles)
    y_mla = _mla_attn(qm, km, vm, n_lat, n_ctx)
    y_na = _na_attn(qn, kn, vn, na_bias, n_lat, n_ctx)
    y_ret, y_hg = _recurrent_mixers(p, rc, hc, hg_lb_l, ret_go, hg_go, n_lat, n_ctx)
    xx, route, counts = _outproj_router(xx, y_mla, y_na, y_ret, y_hg, modsel, ow, rw, n_lat_tiles)
    dest, block_expert, used, n_blocks = _moe_plan(route, counts, b * nt)
    x_rows = _dispatch(xx, modsel, dest, n_blocks * MOE_BLOCK, n_lat_tiles)
    y_rows = _moe_ffn(x_rows, block_expert, used, n_blocks, layer, w1, w3, w2)
    return _combine(xx, modsel, route, y_rows, dest, n_lat_tiles, n_lat_tiles if last else nt // TM)


def kernel(x, c, ctx, c_ctx, w_ada, b_ada, w_in, w_out, mla_g_cq, mla_g_ckv, mla_w_uq, mla_w_ukv, mla_g_qn, mla_g_qr, mla_g_kn, mla_g_kr, na_g_q, na_g_k, na_rpb, ret_g_out, hg_lb_raw, hg_g_out, moe_w_rg, moe_b_rg, moe_w_re, moe_b_re, moe_w1, moe_w3, moe_w2):
    b, n_lat, d = x.shape
    n_ctx = ctx.shape[1]
    depth = w_in.shape[0]
    assert n_lat % TM == 0 and n_ctx % TM == 0 and TM % GRID_W == 0
    assert n_lat // TM >= 3 and n_lat // GRID_W >= NA_WIN_ROWS
    assert w_in.shape[2] == MLA_Q_LORA + MLA_KV_LORA + MLA_ROPE + 12 * GROUP_W

    assert b + 1 <= SUBLANE_PAD_ROWS
    cc = jnp.concatenate([c, c_ctx[None, :], jnp.zeros((SUBLANE_PAD_ROWS - b - 1, d), F32)], axis=0)
    mods = _ada_all(cc, w_ada, b_ada)
    rope_c, rope_s = _rope_tables(n_lat, n_ctx)
    rc = _retention_consts()
    hc = _hgrn_consts()
    lb_w = jax.nn.softmax(hg_lb_raw.astype(F32), axis=0)
    hg_lb = jnp.cumsum(lb_w, axis=0) - lb_w[0:1]

    xx = jnp.concatenate([x, ctx], axis=1)
    tile_go = lambda g: jnp.tile(g, N_HEADS)[None, :]
    for l in range(depth):
        modsel = jnp.stack([mods[l, :b], jnp.broadcast_to(mods[l, b], (b, 6 * d))], axis=1).reshape(2 * b, 1, 6 * d)
        lw = _layer_weights(l, w_in, w_out, mla_g_cq, mla_g_ckv, mla_w_uq, mla_w_ukv, mla_g_qn, mla_g_qr,
                            mla_g_kn, mla_g_kr, na_g_q, na_g_k, moe_w_rg, moe_b_rg, moe_w_re, moe_b_re)
        xx = _layer(xx, modsel, lw, rope_c, rope_s, _na_bias_table(na_rpb[l]), rc, hc, hg_lb[l][None, :],
                    tile_go(ret_g_out[l]), tile_go(hg_g_out[l]),
                    l, moe_w1, moe_w3, moe_w2,
                    n_lat, n_ctx, l == depth - 1)
    return xx
```

```python
import functools

import numpy as np
import jax
import jax.numpy as jnp
from jax import lax
from jax.experimental import pallas as pl
from jax.experimental.pallas import tpu as pltpu

F32 = jnp.float32
BF16 = jnp.bfloat16

EPS = 1e-6
ROPE_THETA = 10000.0
NEG_BIG = -1e30
F_FLOOR = 1e-20
GRID_W = 64
N_HEADS = 4
HEAD_DIM = 64
LANES = 128
GROUP_W = N_HEADS * HEAD_DIM
PAD_W = N_HEADS * LANES
MLA_Q_LORA = 192
MLA_KV_LORA = 128
MLA_ROPE = 32
MLA_SCALE = (HEAD_DIM + MLA_ROPE) ** -0.5
LOG2E = 1.4426950408889634
NA_KH = 8
NA_KW = 16
NA_SCALE = HEAD_DIM ** -0.5
RET_CHUNK = 128
HG_CHUNK = 16
MOE_GROUPS = 4
MOE_PER_GROUP = 8
MOE_EXPERTS = MOE_GROUPS * MOE_PER_GROUP
MOE_TOPK = 2
MOE_BLOCK = 512
TM = 256
NA_TILE_ROWS = TM // GRID_W
NA_WIN_ROWS = NA_TILE_ROWS + NA_KH
VMEM_LIMIT = 56 * 1024 * 1024
SUBLANE_PAD_ROWS = 16
DMA_ISSUE_UNROLL = 32

P_ATTN_COLS = 2 * GROUP_W + 3 * PAD_W
COL_RET_Q, COL_RET_K, COL_RET_V, COL_RET_G = 0, 1, 2, 3
COL_HG_Q, COL_HG_FF, COL_HG_FB, COL_HG_I, COL_HG_G = 4, 5, 6, 7, 8


def _cparams(sem):
    return pltpu.CompilerParams(dimension_semantics=sem, vmem_limit_bytes=VMEM_LIMIT)


def _sigmoid(x):
    return 1.0 / (1.0 + jnp.exp(-x))


def _silu(x):
    return x * _sigmoid(x)


def _dot(a, b):
    return jnp.dot(a, b, preferred_element_type=F32)


def _dot_nt(a, b):
    return lax.dot_general(a, b, (((1,), (1,)), ((), ())), preferred_element_type=F32)


def _split_dot_l(x, m, n):
    acc = None
    rem = x
    for i in range(n):
        piece = rem.astype(BF16)
        d = _dot(piece, m)
        acc = d if acc is None else acc + d
        if i + 1 < n:
            rem = rem - piece.astype(F32)
    return acc


def _split_dot_r(m, x, n):
    acc = None
    rem = x
    for i in range(n):
        piece = rem.astype(BF16)
        d = _dot(m, piece)
        acc = d if acc is None else acc + d
        if i + 1 < n:
            rem = rem - piece.astype(F32)
    return acc


def _pack_bf16_pairs(x):
    half = x.shape[1] // 2
    bits = lax.bitcast_convert_type(x.astype(BF16).astype(F32), jnp.uint32)
    return (bits[:, :half] >> 16) | (bits[:, half:] & jnp.uint32(0xFFFF0000))


def _unpack_bf16_pairs(p, dtype=BF16):
    lo = lax.bitcast_convert_type(p << 16, F32)
    hi = lax.bitcast_convert_type(p & jnp.uint32(0xFFFF0000), F32)
    return jnp.concatenate([lo, hi], axis=1).astype(dtype)


def _seg_rms(x, m, gain):
    return x * lax.rsqrt(_split_dot_l(x * x, m, 2) + EPS) * gain


def _ada_kernel(c_ref, w_ref, b_ref, o_ref):
    s = _silu(c_ref[...])
    o_ref[0] = jnp.dot(s, w_ref[0], preferred_element_type=F32,
                       precision=lax.Precision.HIGHEST) + b_ref[0]


def _ada_all(cc, w_ada, b_ada):
    n_layers, d, d6 = w_ada.shape
    bn = 512
    rows = cc.shape[0]
    return pl.pallas_call(
        _ada_kernel,
        grid=(n_layers, d6 // bn),
        in_specs=[pl.BlockSpec((rows, d), lambda l, j: (0, 0)),
                  pl.BlockSpec((1, d, bn), lambda l, j: (l, 0, j)),
                  pl.BlockSpec((1, 1, bn), lambda l, j: (l, 0, j))],
        out_specs=pl.BlockSpec((1, rows, bn), lambda l, j: (l, 0, j)),
        out_shape=jax.ShapeDtypeStruct((n_layers, rows, d6), F32),
        compiler_params=_cparams(("arbitrary", "arbitrary")),
    )(cc, w_ada, b_ada.reshape(n_layers, 1, d6))


def _mod_spec(d6, n_lat_tiles):
    return pl.BlockSpec((1, 1, d6), lambda b, j: (2 * b + (j >= n_lat_tiles).astype(jnp.int32), 0, 0))


def _modulate(x, shift, scale):
    xn = x * lax.rsqrt(jnp.mean(x * x, axis=-1, keepdims=True) + EPS)
    return xn * (1.0 + scale) + shift


def _inproj_prep_kernel(x_ref, mod_ref, w_ref, c_ref, s_ref, wuq_ref, wk_ref, wv_ref, gcq_ref, gckv_ref, gkr_ref,
                        gq_ref, gk_ref, mq_ref, mk_ref, gnq_ref, gnk_ref, mn_ref,
                        p_ref, qm_ref, km_ref, vm_ref, qn_ref, kn_ref, vn_ref, *, d):
    xm = _modulate(x_ref[0], mod_ref[0, :, 0:d], mod_ref[0, :, d:2 * d]).astype(BF16)
    p_ref[0] = _dot(xm, w_ref[:, P_ATTN_COLS:])
    pa = _dot(xm, w_ref[:, 0:P_ATTN_COLS])

    cq = pa[:, 0:256]
    ckv = pa[:, 256:384]
    kr = pa[:, 384:512]
    cqn = cq * lax.rsqrt(jnp.sum(cq * cq, axis=-1, keepdims=True) * (1.0 / MLA_Q_LORA) + EPS) * gcq_ref[...]
    ckvn = (ckv * lax.rsqrt(jnp.mean(ckv * ckv, axis=-1, keepdims=True) + EPS) * gckv_ref[...]).astype(BF16)
    krn = kr * lax.rsqrt(jnp.sum(kr * kr, axis=-1, keepdims=True) * (1.0 / MLA_ROPE) + EPS) * gkr_ref[...]
    q = _seg_rms(_dot(cqn.astype(BF16), wuq_ref[...]), mq_ref[...], gq_ref[...])
    kk = _seg_rms(_dot(ckvn, wk_ref[...]), mk_ref[...], gk_ref[...])
    vv = _dot(ckvn, wv_ref[...])

    cos = c_ref[...]
    sin = s_ref[...]
    lane = lax.broadcasted_iota(jnp.int32, (TM, LANES), 1)
    first = (lane % 16) < 8

    def rope(x):
        partner = jnp.where(first, pltpu.roll(x, LANES - 8, 1), pltpu.roll(x, 8, 1))
        return x * cos + partner * sin

    krr = rope(krn)
    nq = _seg_rms(pa[:, 512:1024], mn_ref[...], gnq_ref[...])
    nk = _seg_rms(pa[:, 1024:1536], mn_ref[...], gnk_ref[...])
    for h in range(N_HEADS):
        sl = slice(h * LANES, (h + 1) * LANES)
        qm_ref[0, h] = (rope(q[:, sl]) * (MLA_SCALE * LOG2E)).astype(BF16)
        km_ref[0, h] = (kk[:, sl] + krr).astype(BF16)
        vm_ref[0, h] = vv[:, sl].T.astype(BF16)
        qn_ref[0, h] = (nq[:, sl] * (NA_SCALE * LOG2E)).astype(BF16)
        kn_ref[0, h] = nk[:, sl].astype(BF16)
        vn_ref[0, h] = pa[:, 1536 + h * LANES:1536 + (h + 1) * LANES].astype(BF16)


def _inproj_prep(xx, modsel, w_in_p, rope_c, rope_s, pw, n_lat_tiles):
    b, nt, d = xx.shape
    rest = w_in_p.shape[1] - P_ATTN_COLS
    full = lambda a: pl.BlockSpec(a.shape, lambda i, j: (0,) * a.ndim)
    consts = [pw['wuq'], pw['wk'], pw['wv'], pw['gcq'], pw['gckv'], pw['gkr'], pw['gq'], pw['gk'],
              pw['mq'], pw['mk'], pw['gnq'], pw['gnk'], pw['mn']]
    head_spec = pl.BlockSpec((1, N_HEADS, TM, LANES), lambda i, j: (i, 0, j, 0))
    head_shape = jax.ShapeDtypeStruct((b, N_HEADS, nt, LANES), BF16)
    head_t_spec = pl.BlockSpec((1, N_HEADS, LANES, TM), lambda i, j: (i, 0, 0, j))
    head_t_shape = jax.ShapeDtypeStruct((b, N_HEADS, LANES, nt), BF16)
    return pl.pallas_call(
        functools.partial(_inproj_prep_kernel, d=d),
        grid=(b, nt // TM),
        in_specs=[pl.BlockSpec((1, TM, d), lambda i, j: (i, j, 0)),
                  _mod_spec(6 * d, n_lat_tiles),
                  full(w_in_p),
                  pl.BlockSpec((TM, LANES), lambda i, j: (j, 0)),
                  pl.BlockSpec((TM, LANES), lambda i, j: (j, 0))] + [full(a) for a in consts],
        out_specs=[pl.BlockSpec((1, TM, rest), lambda i, j: (i, j, 0)),
                   head_spec, head_spec, head_t_spec, head_spec, head_spec, head_spec],
        out_shape=[jax.ShapeDtypeStruct((b, nt, rest), F32),
                   head_shape, head_shape, head_t_shape, head_shape, head_shape, head_shape],
        compiler_params=_cparams(("arbitrary", "arbitrary")),
    )(xx, modsel, w_in_p, rope_c, rope_s, *consts)


def _softmax2_pv(s, v):
    m = jnp.max(s, axis=-1, keepdims=True)
    e = jnp.exp2(s - m)
    l = jnp.sum(e, axis=-1, keepdims=True)
    return _dot(e.astype(BF16), v) / l


MLA_KEY_BLOCK = 1088
MLA_Q_TILE = 512


MLA_HEADS_PER_STEP = 4


def _mla_attend(q_ref, k_ref, vt_ref, o_ref, n_q, k0, k1):
    items = [(hh, s0) for hh in range(MLA_HEADS_PER_STEP) for s0 in range(k0, k1, MLA_KEY_BLOCK)]
    score = lambda hh, s0: _dot_nt(k_ref[0, hh, s0:min(s0 + MLA_KEY_BLOCK, k1), :], q_ref[0, hh, 0:n_q, :])
    st = score(*items[0])
    m = l = acc = None
    for i, (hh, s0) in enumerate(items):
        st_next = score(*items[i + 1]) if i + 1 < len(items) else None
        vt_blk = vt_ref[0, hh, :, s0:min(s0 + MLA_KEY_BLOCK, k1)]
        bm = jnp.max(st, axis=0, keepdims=True)
        if s0 == k0:
            m = bm
            e = jnp.exp2(st - m)
            l = jnp.sum(e, axis=0, keepdims=True)
            acc = _dot(vt_blk, e.astype(BF16))
        else:
            m_new = jnp.maximum(m, bm)
            alpha = jnp.exp2(m - m_new)
            e = jnp.exp2(st - m_new)
            l = l * alpha + jnp.sum(e, axis=0, keepdims=True)
            acc = acc * alpha + _dot(vt_blk, e.astype(BF16))
            m = m_new
        if s0 + MLA_KEY_BLOCK >= k1:
            o_ref[0, 0:n_q, hh * LANES:(hh + 1) * LANES] = (acc / l).T
        st = st_next


def _mla_kernel(q_ref, k_ref, vt_ref, o_ref, *, n_lat, n_ctx):
    j = pl.program_id(2)

    @pl.when(j < n_lat // MLA_Q_TILE)
    def _():
        _mla_attend(q_ref, k_ref, vt_ref, o_ref, MLA_Q_TILE, 0, n_lat + n_ctx)

    @pl.when(j >= n_lat // MLA_Q_TILE)
    def _():
        _mla_attend(q_ref, k_ref, vt_ref, o_ref, n_ctx, n_lat, n_lat + n_ctx)


def _mla_attn(qm, km, vmt, n_lat, n_ctx):
    b, h, nt, _ = qm.shape
    hp = MLA_HEADS_PER_STEP
    tq = MLA_Q_TILE
    assert n_lat % tq == 0 and n_ctx <= tq
    kv_spec = pl.BlockSpec((1, hp, nt, LANES), lambda i, hh, j: (i, hh, 0, 0))
    vt_spec = pl.BlockSpec((1, hp, LANES, nt), lambda i, hh, j: (i, hh, 0, 0))
    return pl.pallas_call(
        functools.partial(_mla_kernel, n_lat=n_lat, n_ctx=n_ctx),
        grid=(b, h // hp, n_lat // tq + 1),
        in_specs=[pl.BlockSpec((1, hp, tq, LANES), lambda i, hh, j: (i, hh, j, 0)), kv_spec, vt_spec],
        out_specs=pl.BlockSpec((1, tq, hp * LANES), lambda i, hh, j: (i, j, hh)),
        out_shape=jax.ShapeDtypeStruct((b, nt, PAD_W), F32),
        compiler_params=_cparams(("arbitrary", "arbitrary", "arbitrary")),
    )(qm, km, vmt)


def _na_kernel(q_ref, k_ref, v_ref, bias_ref, o_ref, *, n_lat, n_ctx):
    j = pl.program_id(1)
    rows = n_lat // GRID_W
    n_tiles = n_lat // TM
    win = NA_WIN_ROWS * GRID_W

    @pl.when(j < n_tiles)
    def _():
        start = jnp.clip(j * NA_TILE_ROWS - NA_KH // 2, 0, rows - NA_WIN_ROWS)
        case = jnp.where(j == 0, 0, jnp.where(j == n_tiles - 1, 2, 1))
        tok0 = pl.multiple_of(start * GRID_W, GRID_W)
        def scores(h):
            q = q_ref[0, h]
            return (_dot_nt(q, k_ref[0, h, pl.ds(tok0, win), :]) + bias_ref[case, h],
                    _dot_nt(q, k_ref[0, h, pl.ds(n_lat, n_ctx), :]))

        nxt = scores(0)
        for h in range(N_HEADS):
            s1, s2 = nxt
            if h + 1 < N_HEADS:
                nxt = scores(h + 1)
            m = jnp.maximum(jnp.max(s1, axis=-1, keepdims=True), jnp.max(s2, axis=-1, keepdims=True))
            e1 = jnp.exp2(s1 - m)
            e2 = jnp.exp2(s2 - m)
            l = jnp.sum(e1, axis=-1, keepdims=True) + jnp.sum(e2, axis=-1, keepdims=True)
            o = _dot(e1.astype(BF16), v_ref[0, h, pl.ds(tok0, win), :])
            o = o + _dot(e2.astype(BF16), v_ref[0, h, pl.ds(n_lat, n_ctx), :])
            o_ref[0, :, h * LANES:(h + 1) * LANES] = o / l

    @pl.when(j >= n_lat // TM)
    def _():
        for h in range(N_HEADS):
            s = _dot_nt(q_ref[0, h], k_ref[0, h, pl.ds(n_lat, n_ctx), :])
            o_ref[0, :, h * LANES:(h + 1) * LANES] = _softmax2_pv(s, v_ref[0, h, pl.ds(n_lat, n_ctx), :])


def _na_attn(qn, kn, vn, bias, n_lat, n_ctx):
    b, h, nt, _ = qn.shape
    kv_spec = pl.BlockSpec((1, h, nt, LANES), lambda i, j: (i, 0, 0, 0))
    return pl.pallas_call(
        functools.partial(_na_kernel, n_lat=n_lat, n_ctx=n_ctx),
        grid=(b, nt // TM),
        in_specs=[pl.BlockSpec((1, h, TM, LANES), lambda i, j: (i, 0, j, 0)), kv_spec, kv_spec,
                  pl.BlockSpec(bias.shape, lambda i, j: (0, 0, 0, 0))],
        out_specs=pl.BlockSpec((1, TM, PAD_W), lambda i, j: (i, j, 0)),
        out_shape=jax.ShapeDtypeStruct((b, nt, PAD_W), F32),
        compiler_params=_cparams(("arbitrary", "arbitrary")),
    )(qn, kn, vn, bias)


def _head_mask(h, shape):
    return (lax.broadcasted_iota(jnp.int32, shape, 1) // HEAD_DIM) == h


def _ret_state_step(s_ref, q, k, v, qw, kw, cd, bd):
    state = s_ref[...]
    o = _dot((q * qw).astype(BF16), state.astype(BF16))
    upd = _dot((k * kw).T.astype(BF16), v.astype(BF16))
    s_ref[...] = state * cd + upd * bd
    return o


def _ret_fwd_body(q_ref, k_ref, v_ref, dm_ref, qw_ref, kw_ref, cd_ref, bd_ref, o_ref, s_ref):
    for c in range(TM // RET_CHUNK):
        rows = slice(c * RET_CHUNK, (c + 1) * RET_CHUNK)
        q = q_ref[0, rows]
        k = k_ref[0, rows]
        v = v_ref[0, rows]
        o = _ret_state_step(s_ref, q, k, v, qw_ref[...], kw_ref[...], cd_ref[...], bd_ref[...])
        kb = k.astype(BF16)
        for h in range(N_HEADS):
            hm = _head_mask(h, q.shape)
            sc = _dot_nt(jnp.where(hm, q, 0.0).astype(BF16), kb) * dm_ref[h]
            o = o + _dot(sc.astype(BF16), jnp.where(hm, v, 0.0).astype(BF16))
        o_ref[0, rows] = o


def _ret_bwd_body(q_ref, k_ref, v_ref, g_ref, op_ref, qw_ref, kw_ref, cd_ref, bd_ref, ms_ref, go_ref,
                  y_ref, s_ref):
    for c in reversed(range(TM // RET_CHUNK)):
        rows = slice(c * RET_CHUNK, (c + 1) * RET_CHUNK)
        o = op_ref[0, rows] + _ret_state_step(s_ref, q_ref[0, rows], k_ref[0, rows], v_ref[0, rows], qw_ref[...],
                                              kw_ref[...], cd_ref[...], bd_ref[...])
        y_ref[0, rows] = _seg_rms(o, ms_ref[...], go_ref[...]) * _silu(g_ref[0, rows])


def _scan_order(n_lat_t, n_ctx_t, reverse):
    if reverse:
        return lambda i: n_lat_t + n_ctx_t - 1 - i
    return lambda i: jnp.where(i < n_ctx_t, n_lat_t + i, i - n_ctx_t)


def _hg_direction(q_ref, f_ref, v_ref, lb_ref, ain_ref, aex_ref, bseg_ref, bd_ref, st_ref, sh_ref, *, reverse):
    n_chunks = TM // HG_CHUNK
    assert n_chunks == HG_CHUNK
    qh = _silu(q_ref[0])
    lb = lb_ref[...]
    f = jnp.maximum(lb + (1.0 - lb) * _sigmoid(f_ref[0]), F_FLOOR)
    lf = jnp.log(f) * LOG2E
    k = 1.0 - f
    v = v_ref[0]
    row = lax.broadcasted_iota(jnp.int32, (TM, 1), 0)
    pos = row % HG_CHUNK
    row_chunk = row // HG_CHUNK

    a_in = _split_dot_r(ain_ref[...], lf, 3)
    a_ex = _split_dot_r(aex_ref[...], lf, 3)
    width = v.shape[1]
    for slot, val in enumerate((k, a_in, v)):
        sh_ref[slot] = val.reshape(n_chunks, HG_CHUNK, width)

    def key_row(slot, s):
        return jnp.broadcast_to(sh_ref[slot, :, s:s + 1, :], (n_chunks, HG_CHUNK, width)).reshape(TM, width)
    qp = (qh * jnp.exp2(a_in)).astype(BF16)
    kdec = k * jnp.exp2(a_ex)
    lam_all = jnp.exp2(a_in + a_ex)
    vt = v.T.astype(BF16)
    bd = bd_ref[...]
    bseg = bseg_ref[...]
    state = st_ref[...]
    parts = [None] * n_chunks
    o_band = jnp.zeros_like(v)
    upds = [_dot(vt, jnp.where(row_chunk == c, kdec, 0.0).astype(BF16)) * bd for c in range(n_chunks)]
    for step in range(n_chunks):
        c = n_chunks - 1 - step if reverse else step
        r0 = c * HG_CHUNK
        parts[c] = _dot_nt(qp[r0:r0 + HG_CHUNK], state.astype(BF16))
        state = state * lam_all[r0:r0 + 1] + upds[c]
        s = step
        valid = (pos <= s) if reverse else (pos >= s)
        w = jnp.where(valid, qh * key_row(0, s) * jnp.exp2(a_in - key_row(1, s)), 0.0)
        o_band = o_band + _dot(w.astype(BF16), bseg) * key_row(2, s)
    st_ref[...] = state
    return jnp.concatenate(parts, axis=0) + o_band


def _scan_fwd_kernel(rq_ref, rk_ref, rv_ref, hq_ref, hf_ref, hv_ref, dm_ref, qw_ref, kw_ref, cd_ref, bd_ref,
                     lb_ref, ain_ref, aex_ref, bseg_ref, ro_ref, ho_ref, rs_ref, hs_ref, sh_ref):
    @pl.when(pl.program_id(1) == 0)
    def _():
        rs_ref[...] = jnp.zeros_like(rs_ref)
        hs_ref[...] = jnp.zeros_like(hs_ref)

    _ret_fwd_body(rq_ref, rk_ref, rv_ref, dm_ref, qw_ref, kw_ref, cd_ref, bd_ref, ro_ref, rs_ref)
    ho_ref[0] = _hg_direction(hq_ref, hf_ref, hv_ref, lb_ref, ain_ref, aex_ref, bseg_ref, bd_ref, hs_ref, sh_ref,
                              reverse=False)


def _scan_bwd_kernel(rq_ref, rk_ref, rv_ref, rg_ref, rop_ref, hq_ref, hf_ref, hv_ref, hg_ref, hop_ref,
                     qw_ref, kw_ref, cd_ref, bd_ref, ms_ref, rgo_ref, lb_ref, ain_ref, aex_ref, bseg_ref, hgo_ref,
                     ry_ref, hy_ref, rs_ref, hs_ref, sh_ref):
    @pl.when(pl.program_id(1) == 0)
    def _():
        rs_ref[...] = jnp.zeros_like(rs_ref)
        hs_ref[...] = jnp.zeros_like(hs_ref)

    _ret_bwd_body(rq_ref, rk_ref, rv_ref, rg_ref, rop_ref, qw_ref, kw_ref, cd_ref, bd_ref, ms_ref, rgo_ref,
                  ry_ref, rs_ref)
    o = hop_ref[0] + _hg_direction(hq_ref, hf_ref, hv_ref, lb_ref, ain_ref, aex_ref, bseg_ref, bd_ref, hs_ref,
                                   sh_ref, reverse=True)
    hy_ref[0] = _seg_rms(o, ms_ref[...], hgo_ref[...]) * _silu(hg_ref[0])


def _recurrent_mixers(p, rc, hc, lb, ret_go, hg_go, n_lat, n_ctx):
    b, nt, _ = p.shape
    n_lat_t, n_ctx_t = n_lat // TM, n_ctx // TM
    fwd = _scan_order(n_lat_t, n_ctx_t, False)
    bwd = _scan_order(n_lat_t, n_ctx_t, True)
    col = lambda order, cb: pl.BlockSpec((1, TM, GROUP_W), lambda i, j: (i, order(j), cb))
    full = lambda a: pl.BlockSpec(a.shape, lambda i, j: (0,) * a.ndim)
    out_shape = jax.ShapeDtypeStruct((b, nt, GROUP_W), F32)
    scratch = [pltpu.VMEM((GROUP_W, GROUP_W), F32), pltpu.VMEM((GROUP_W, GROUP_W), F32),
               pltpu.VMEM((3, TM // HG_CHUNK, HG_CHUNK, GROUP_W), F32)]
    consts_f = [rc['dm'], rc['qw_f'], rc['kw_f'], rc['cd_f'], rc['bd'], lb, hc['lincl'], hc['uexcl'], hc['bseg']]
    out_f = pl.BlockSpec((1, TM, GROUP_W), lambda i, j: (i, fwd(j), 0))
    ret_part, hg_part = pl.pallas_call(
        _scan_fwd_kernel,
        grid=(b, nt // TM),
        in_specs=[col(fwd, COL_RET_Q), col(fwd, COL_RET_K), col(fwd, COL_RET_V),
                  col(fwd, COL_HG_Q), col(fwd, COL_HG_FF), col(fwd, COL_HG_I)] + [full(a) for a in consts_f],
        out_specs=[out_f, out_f],
        out_shape=[out_shape, out_shape],
        scratch_shapes=scratch,
        compiler_params=_cparams(("arbitrary", "arbitrary")),
    )(p, p, p, p, p, p, *consts_f)
    consts_b = [rc['qw_b'], rc['kw_b'], rc['cd_b'], rc['bd'], rc['ms'], ret_go,
                lb, hc['uincl'], hc['lexcl'], hc['bseg'], hg_go]
    out_b = pl.BlockSpec((1, TM, GROUP_W), lambda i, j: (i, bwd(j), 0))
    return pl.pallas_call(
        _scan_bwd_kernel,
        grid=(b, nt // TM),
        in_specs=[col(bwd, COL_RET_Q), col(bwd, COL_RET_K), col(bwd, COL_RET_V), col(bwd, COL_RET_G), out_b,
                  col(bwd, COL_HG_Q), col(bwd, COL_HG_FB), col(bwd, COL_HG_I), col(bwd, COL_HG_G), out_b]
                 + [full(a) for a in consts_b],
        out_specs=[out_b, out_b],
        out_shape=[out_shape, out_shape],
        scratch_shapes=scratch,
        compiler_params=_cparams(("arbitrary", "arbitrary")),
    )(p, p, p, p, ret_part, p, p, p, p, hg_part, *consts_b)


def _outproj_router_kernel(x_ref, ym_ref, yn_ref, yr_ref, yh_ref, mod_ref, wm_ref, wn_ref, wr_ref, wh_ref,
                           whl_ref, br_ref, ltri_ref, o_ref, r_ref, cnt_ref, *, d):
    acc = _dot(ym_ref[0].astype(BF16), wm_ref[...])
    acc = acc + _dot(yn_ref[0].astype(BF16), wn_ref[...])
    acc = acc + _dot(yr_ref[0].astype(BF16), wr_ref[...])
    acc = acc + _dot(yh_ref[0].astype(BF16), wh_ref[...])
    x_new = x_ref[0] + mod_ref[0, :, 2 * d:3 * d] * acc
    o_ref[0] = x_new
    h = _modulate(x_new, mod_ref[0, :, 3 * d:4 * d], mod_ref[0, :, 4 * d:5 * d])
    _route(h, whl_ref, br_ref, ltri_ref, r_ref, cnt_ref)


def _outproj_router(xx, y_mla, y_na, y_ret, y_hg, modsel, ow, rw, n_lat_tiles):
    b, nt, d = xx.shape
    tiles = nt // TM
    tile = lambda w: pl.BlockSpec((1, TM, w), lambda i, j: (i, j, 0))
    full = lambda a: pl.BlockSpec(a.shape, lambda i, j: (0, 0))
    ltri = jnp.asarray(np.tril(np.ones((TM, TM), np.float32), -1), BF16)
    ws = [ow['mla'], ow['na'], ow['ret'], ow['hg'], rw['hi_lo'], rw['b'], ltri]
    return pl.pallas_call(
        functools.partial(_outproj_router_kernel, d=d),
        grid=(b, tiles),
        in_specs=[tile(d), tile(PAD_W), tile(PAD_W), tile(GROUP_W), tile(GROUP_W),
                  _mod_spec(6 * d, n_lat_tiles)] + [full(a) for a in ws],
        out_specs=[tile(d), pl.BlockSpec((TM, LANES), lambda i, j: (i * tiles + j, 0)),
                   pl.BlockSpec((1, LANES), lambda i, j: (0, 0))],
        out_shape=[jax.ShapeDtypeStruct((b, nt, d), F32), jax.ShapeDtypeStruct((b * nt, LANES), F32),
                   jax.ShapeDtypeStruct((1, LANES), F32)],
        compiler_params=_cparams(("arbitrary", "arbitrary")),
    )(xx, y_mla, y_na, y_ret, y_hg, modsel, *ws)


def _route(h, whl_ref, br_ref, ltri_ref, r_ref, cnt_ref):
    @pl.when((pl.program_id(0) == 0) & (pl.program_id(1) == 0))
    def _():
        cnt_ref[...] = jnp.zeros_like(cnt_ref)

    h_hi = h.astype(BF16)
    h_lo = (h - h_hi.astype(F32)).astype(BF16)
    hi_terms = _dot(h_hi, whl_ref[...])
    lg = hi_terms[:, :LANES] + hi_terms[:, LANES:] + _dot(h_lo, whl_ref[:, :LANES]) + br_ref[...]

    lane = lax.broadcasted_iota(jnp.int32, lg.shape, 1).astype(F32)
    far = 1e9

    def first_argmax(vals, vmax):
        return jnp.min(jnp.where(vals == vmax, lane, far), axis=-1, keepdims=True)

    gl = jnp.where(lane < MOE_GROUPS, lg, NEG_BIG)
    gmax = jnp.max(gl, axis=-1, keepdims=True)
    pg_top = 1.0 / jnp.sum(jnp.exp(gl - gmax), axis=-1, keepdims=True)
    lo = MOE_GROUPS + MOE_PER_GROUP * first_argmax(gl, gmax)
    fl = jnp.where((lane >= lo) & (lane < lo + MOE_PER_GROUP), lg, NEG_BIG)
    fmax = jnp.max(fl, axis=-1, keepdims=True)
    fsum = jnp.sum(jnp.exp(fl - fmax), axis=-1, keepdims=True)
    i1 = first_argmax(fl, fmax)
    fl2 = jnp.where(lane == i1, NEG_BIG, fl)
    f2max = jnp.max(fl2, axis=-1, keepdims=True)
    i2 = first_argmax(fl2, f2max)
    p1 = 1.0 / fsum
    p2 = jnp.exp(f2max - fmax) / fsum
    g1 = pg_top * p1 / (p1 + p2)
    g2 = pg_top * p2 / (p1 + p2)
    e1 = i1 - MOE_GROUPS
    e2 = i2 - MOE_GROUPS

    onehot = jnp.where(lane == e1, 1.0, 0.0) + jnp.where(lane == e2, 1.0, 0.0)
    before = cnt_ref[...] + _dot(ltri_ref[...], onehot.astype(BF16))
    r1 = jnp.sum(jnp.where(lane == e1, before, 0.0), axis=-1, keepdims=True)
    r2 = jnp.sum(jnp.where(lane == e2, before, 0.0), axis=-1, keepdims=True)
    cnt_ref[...] += jnp.sum(onehot, axis=0, keepdims=True)

    out = jnp.zeros_like(lg)
    for col, val in enumerate((e1, e2, g1, g2, r1, r2)):
        out = jnp.where(lane == col, val, out)
    r_ref[...] = out


ROUTE_E, ROUTE_G, ROUTE_R = 0, 2, 4


def _moe_plan(route, counts_f, n_tok):
    counts = counts_f[0, :MOE_EXPERTS].astype(jnp.int32)
    padded = (counts + MOE_BLOCK - 1) // MOE_BLOCK * MOE_BLOCK
    pad_end = jnp.cumsum(padded)
    pad_start = pad_end - padded
    n_blocks = -(-(n_tok * MOE_TOPK) // MOE_BLOCK) + MOE_EXPERTS
    blk0 = jnp.arange(n_blocks, dtype=jnp.int32) * MOE_BLOCK
    block_expert = jnp.minimum(jnp.sum((pad_end[None, :] <= blk0[:, None]).astype(jnp.int32), axis=1),
                               MOE_EXPERTS - 1)
    used = (pad_end[-1] // MOE_BLOCK).reshape(1)
    expert = route[:, ROUTE_E:ROUTE_E + MOE_TOPK].astype(jnp.int32)
    rank = route[:, ROUTE_R:ROUTE_R + MOE_TOPK].astype(jnp.int32)
    start_of = jnp.sum(jnp.where(expert[..., None] == jnp.arange(MOE_EXPERTS, dtype=jnp.int32), pad_start, 0), axis=-1)
    return start_of + rank, block_expert, used, n_blocks


def _idx_blocks(dest, k, n_tiles):
    return dest[:, k].reshape(n_tiles, 1, TM)


def _dispatch_kernel(d0_ref, d1_ref, x_ref, mod_ref, rows_in, rows_out, h_ref, sem, *, d):
    del rows_in
    step = pl.program_id(0) * pl.num_programs(1) + pl.program_id(1)
    n_steps = pl.num_programs(0) * pl.num_programs(1)
    slot = step % 2

    def wait_slot(s):
        for _ in range(MOE_TOPK):
            pltpu.make_async_copy(h_ref.at[s], rows_out.at[pl.ds(0, TM)], sem.at[s]).wait()

    h = _modulate(x_ref[0], mod_ref[0, :, 3 * d:4 * d], mod_ref[0, :, 4 * d:5 * d])
    h_ref[slot] = _pack_bf16_pairs(h)

    def issue(r, carry):
        src = h_ref.at[slot, pl.ds(r, 1)]
        pltpu.make_async_copy(src, rows_out.at[pl.ds(d0_ref[0, 0, r], 1)], sem.at[slot]).start()
        pltpu.make_async_copy(src, rows_out.at[pl.ds(d1_ref[0, 0, r], 1)], sem.at[slot]).start()
        return carry

    lax.fori_loop(0, TM, issue, 0, unroll=DMA_ISSUE_UNROLL)

    @pl.when(step > 0)
    def _():
        wait_slot(1 - slot)

    @pl.when(step == n_steps - 1)
    def _():
        wait_slot(slot)


def _dispatch(xx, modsel, dest, n_rows, n_lat_tiles):
    b, nt, d = xx.shape
    tiles = nt // TM
    idx_spec = pl.BlockSpec((1, 1, TM), lambda i, j: (i * tiles + j, 0, 0), memory_space=pltpu.SMEM)
    return pl.pallas_call(
        functools.partial(_dispatch_kernel, d=d),
        grid=(b, tiles),
        in_specs=[idx_spec, idx_spec, pl.BlockSpec((1, TM, d), lambda i, j: (i, j, 0)),
                  _mod_spec(6 * d, n_lat_tiles), pl.BlockSpec(memory_space=pl.ANY)],
        out_specs=pl.BlockSpec(memory_space=pl.ANY),
        out_shape=jax.ShapeDtypeStruct((n_rows, d // 2), jnp.uint32),
        scratch_shapes=[pltpu.VMEM((2, TM, d // 2), jnp.uint32), pltpu.SemaphoreType.DMA((2,))],
        input_output_aliases={4: 0},
        compiler_params=_cparams(("arbitrary", "arbitrary")),
    )(_idx_blocks(dest, 0, b * tiles), _idx_blocks(dest, 1, b * tiles), xx, modsel,
      jnp.zeros((n_rows, d // 2), jnp.uint32))


def _ffn_kernel(be_ref, used_ref, x_ref, w1_ref, w3_ref, w2_ref, y_ref, w1b_ref, w3b_ref, w2b_ref):
    i = pl.program_id(0)

    @pl.when((i == 0) | (be_ref[i] != be_ref[jnp.maximum(i - 1, 0)]))
    def _():
        w1b_ref[...] = w1_ref[0, 0].astype(BF16)
        w3b_ref[...] = w3_ref[0, 0].astype(BF16)
        w2b_ref[...] = w2_ref[0, 0].astype(BF16)

    @pl.when(i < used_ref[0])
    def _():
        x = _unpack_bf16_pairs(x_ref[...])
        mid = _silu(_dot(x, w1b_ref[...])) * _dot(x, w3b_ref[...])
        y_ref[...] = _pack_bf16_pairs(_dot(mid.astype(BF16), w2b_ref[...]))

    @pl.when(i >= used_ref[0])
    def _():
        y_ref[...] = jnp.zeros_like(y_ref)


def _moe_ffn(x_rows, block_expert, used, n_blocks, layer, w1, w3, w2):
    d = w1.shape[2]
    ff = w1.shape[3]
    grid_spec = pltpu.PrefetchScalarGridSpec(
        num_scalar_prefetch=2,
        grid=(n_blocks,),
        in_specs=[pl.BlockSpec((MOE_BLOCK, d // 2), lambda i, be, nu: (i, 0)),
                  pl.BlockSpec((1, 1, d, ff), lambda i, be, nu: (layer, be[i], 0, 0)),
                  pl.BlockSpec((1, 1, d, ff), lambda i, be, nu: (layer, be[i], 0, 0)),
                  pl.BlockSpec((1, 1, ff, d), lambda i, be, nu: (layer, be[i], 0, 0))],
        out_specs=pl.BlockSpec((MOE_BLOCK, d // 2), lambda i, be, nu: (i, 0)),
        scratch_shapes=[pltpu.VMEM((d, ff), BF16), pltpu.VMEM((d, ff), BF16), pltpu.VMEM((ff, d), BF16)],
    )
    return pl.pallas_call(
        _ffn_kernel,
        grid_spec=grid_spec,
        out_shape=jax.ShapeDtypeStruct((x_rows.shape[0], d // 2), jnp.uint32),
        compiler_params=_cparams(("arbitrary",)),
    )(block_expert, used, x_rows, w1, w3, w2)


def _row_gather(src_hbm, idx_ref, dst_ref, sem, n):
    def issue(r, carry):
        pltpu.make_async_copy(src_hbm.at[pl.ds(idx_ref[0, 0, r], 1)], dst_ref.at[pl.ds(r, 1)], sem).start()
        return carry

    lax.fori_loop(0, n, issue, 0, unroll=DMA_ISSUE_UNROLL)


def _row_gather_wait(src_hbm, dst_ref, sem, n):
    pltpu.make_async_copy(src_hbm.at[pl.ds(0, n)], dst_ref, sem).wait()


def _combine_kernel(d0_ref, d1_ref, n0_ref, n1_ref, x_ref, mod_ref, r_ref, y_hbm, o_ref, y_ref, sem, *, d):
    step = pl.program_id(0) * pl.num_programs(1) + pl.program_id(1)
    n_steps = pl.num_programs(0) * pl.num_programs(1)
    slot = step % 2

    def gather(idx_refs, s):
        for k, idx_ref in enumerate(idx_refs):
            _row_gather(y_hbm, idx_ref, y_ref.at[s, k], sem.at[s, k], TM)

    @pl.when(step == 0)
    def _():
        gather((d0_ref, d1_ref), slot)

    @pl.when(step + 1 < n_steps)
    def _():
        gather((n0_ref, n1_ref), 1 - slot)

    route = r_ref[...]
    lane = lax.broadcasted_iota(jnp.int32, route.shape, 1)
    g0 = jnp.sum(jnp.where(lane == ROUTE_G, route, 0.0), axis=-1, keepdims=True)
    g1 = jnp.sum(jnp.where(lane == ROUTE_G + 1, route, 0.0), axis=-1, keepdims=True)
    for k in range(MOE_TOPK):
        _row_gather_wait(y_hbm, y_ref.at[slot, k], sem.at[slot, k], TM)
    y0 = _unpack_bf16_pairs(y_ref[slot, 0], F32)
    y1 = _unpack_bf16_pairs(y_ref[slot, 1], F32)
    o_ref[0] = x_ref[0] + mod_ref[0, :, 5 * d:6 * d] * (y0 * g0 + y1 * g1)


def _combine(xx, modsel, route, y_rows, dest, n_lat_tiles, out_tiles):
    b, nt, d = xx.shape
    tiles = nt // TM
    idx_spec = pl.BlockSpec((1, 1, TM), lambda i, j: (i * tiles + j, 0, 0), memory_space=pltpu.SMEM)

    def next_block(i, j):
        wrap = j + 1 >= out_tiles
        return (jnp.where(wrap, jnp.minimum(i + 1, b - 1) * tiles, i * tiles + j + 1), 0, 0)

    next_spec = pl.BlockSpec((1, 1, TM), next_block, memory_space=pltpu.SMEM)
    d0, d1 = _idx_blocks(dest, 0, b * tiles), _idx_blocks(dest, 1, b * tiles)
    return pl.pallas_call(
        functools.partial(_combine_kernel, d=d),
        grid=(b, out_tiles),
        in_specs=[idx_spec, idx_spec, next_spec, next_spec, pl.BlockSpec((1, TM, d), lambda i, j: (i, j, 0)),
                  _mod_spec(6 * d, n_lat_tiles),
                  pl.BlockSpec((TM, LANES), lambda i, j: (i * tiles + j, 0)),
                  pl.BlockSpec(memory_space=pl.ANY)],
        out_specs=pl.BlockSpec((1, TM, d), lambda i, j: (i, j, 0)),
        out_shape=jax.ShapeDtypeStruct((b, out_tiles * TM, d), F32),
        scratch_shapes=[pltpu.VMEM((2, MOE_TOPK, TM, d // 2), jnp.uint32), pltpu.SemaphoreType.DMA((2, MOE_TOPK))],
        compiler_params=_cparams(("arbitrary", "arbitrary")),
    )(d0, d1, d0, d1, xx, modsel, route, y_rows)


def _pad_heads_cols(w):
    lead = w.shape[:-1]
    w = w.reshape(*lead, N_HEADS, HEAD_DIM)
    w = jnp.concatenate([w, jnp.zeros_like(w)], axis=-1)
    return w.reshape(*lead, PAD_W)


def _pad_heads_rows(w):
    return _pad_heads_cols(w.T).T


def _seg_mean_matrix(width, segments):
    m = np.zeros((width, width), np.float32)
    for g in range(width // LANES):
        for start, length in segments:
            a = g * LANES + start
            m[a:a + length, a:a + length] = 1.0 / length
    return jnp.asarray(m, BF16)


def _rope_tables(n_lat, n_ctx):
    pos = jnp.arange(n_lat)
    rows = (pos // GRID_W).astype(F32)
    cols = (pos % GRID_W).astype(F32)
    per_axis = MLA_ROPE // 2
    inv_freq = ROPE_THETA ** (-jnp.arange(0, per_axis, 2, dtype=F32) / per_axis)
    ang = jnp.concatenate([rows[:, None] * inv_freq, cols[:, None] * inv_freq], axis=-1)
    i = np.arange(MLA_ROPE)
    src = (i // 16) * 8 + (i % 8)
    sign = np.where((i % 16) < 8, -1.0, 1.0).astype(np.float32)
    cos = jnp.ones((n_lat, LANES), F32).at[:, HEAD_DIM:HEAD_DIM + MLA_ROPE].set(jnp.cos(ang)[:, src])
    sin = jnp.zeros((n_lat, LANES), F32).at[:, HEAD_DIM:HEAD_DIM + MLA_ROPE].set(jnp.sin(ang)[:, src] * sign)
    cos = jnp.concatenate([cos, jnp.ones((n_ctx, LANES), F32)], axis=0)
    sin = jnp.concatenate([sin, jnp.zeros((n_ctx, LANES), F32)], axis=0)
    return cos, sin


def _na_bias_table(rpb):
    w = np.arange(GRID_W)
    col_start = np.clip(w - NA_KW // 2, 0, GRID_W - NA_KW)
    valid = (w[None, :] >= col_start[:, None]) & (w[None, :] < col_start[:, None] + NA_KW)
    dc = np.clip(w[None, :] - w[:, None], 1 - NA_KW, NA_KW - 1) + (NA_KW - 1)
    onehot = jnp.asarray(dc[None, :, :] == np.arange(2 * NA_KW - 1)[:, None, None], F32)
    t = jnp.einsum('hrd,dqk->hrqk', rpb.astype(F32), onehot, precision=lax.Precision.HIGHEST)
    t = jnp.where(jnp.asarray(valid)[None, None, :, :], t * LOG2E, NEG_BIG)
    masked = jnp.full((N_HEADS, GRID_W, GRID_W), NEG_BIG, F32)
    q_off = (0, NA_KH // 2, NA_KH)
    first = ([0] * NA_TILE_ROWS, list(range(NA_TILE_ROWS)), [NA_KH // 2] * NA_TILE_ROWS)
    cases = []
    for c in range(3):
        row_blocks = []
        for rr in range(NA_TILE_ROWS):
            blocks = []
            for jj in range(NA_WIN_ROWS):
                live = first[c][rr] <= jj < first[c][rr] + NA_KH
                dr = jj - (q_off[c] + rr) + (NA_KH - 1)
                blocks.append(t[:, dr] if live else masked)
            row_blocks.append(jnp.concatenate(blocks, axis=-1))
        cases.append(jnp.concatenate(row_blocks, axis=-2))
    return jnp.stack(cases, axis=0)


def _block_diag_mask(block):
    i = np.arange(GROUP_W) // block
    return (i[:, None] == i[None, :]).astype(np.float32)


def _retention_consts():
    c = RET_CHUNK
    j = np.arange(2 * N_HEADS, dtype=np.float64)
    lg = np.log1p(-np.exp2(-5.0 - j))
    lg_f, lg_b = lg[0::2], lg[1::2]
    pos = np.arange(c, dtype=np.float64)
    diff = pos[:, None] - pos[None, :]
    k_scale = HEAD_DIM ** -0.5
    dm = np.zeros((N_HEADS, c, c))
    for h in range(N_HEADS):
        dm[h] = (np.where(diff >= 0, np.exp(np.maximum(diff, 0.0) * lg_f[h]), 0.0)
                 + np.where(diff <= 0, np.exp(np.maximum(-diff, 0.0) * lg_b[h]), 0.0)) * k_scale
    lanes = lambda per_head: np.repeat(per_head, HEAD_DIM, axis=-1)
    out = {
        'dm': dm,
        'qw_f': lanes(np.exp((pos + 1)[:, None] * lg_f[None, :])),
        'kw_f': lanes(np.exp((c - 1 - pos)[:, None] * lg_f[None, :])) * k_scale,
        'cd_f': lanes(np.exp(c * lg_f)[None, :]),
        'qw_b': lanes(np.exp((c - pos)[:, None] * lg_b[None, :])),
        'kw_b': lanes(np.exp(pos[:, None] * lg_b[None, :])) * k_scale,
        'cd_b': lanes(np.exp(c * lg_b)[None, :]),
        'bd': _block_diag_mask(HEAD_DIM),
    }
    out = {k: jnp.asarray(v, F32) for k, v in out.items()}
    out['ms'] = jnp.asarray(_block_diag_mask(HEAD_DIM) / HEAD_DIM, BF16)
    return out


def _hgrn_consts():
    t = np.arange(TM)
    same = (t[:, None] // HG_CHUNK) == (t[None, :] // HG_CHUNK)
    lincl = same & (t[None, :] <= t[:, None])
    lexcl = same & (t[None, :] < t[:, None])
    return {
        'lincl': jnp.asarray(lincl, BF16), 'lexcl': jnp.asarray(lexcl, BF16),
        'uincl': jnp.asarray(lincl.T, BF16), 'uexcl': jnp.asarray(lexcl.T, BF16),
        'bseg': jnp.asarray(_block_diag_mask(HEAD_DIM), BF16),
        'bd': jnp.asarray(_block_diag_mask(HEAD_DIM), F32),
        'ms': jnp.asarray(_block_diag_mask(HEAD_DIM) / HEAD_DIM, BF16),
    }


def _layer_weights(l, w_in, w_out, mla_g_cq, mla_g_ckv, mla_w_uq, mla_w_ukv, mla_g_qn, mla_g_qr, mla_g_kn,
                   mla_g_kr, na_g_q, na_g_k, moe_w_rg, moe_b_rg, moe_w_re, moe_b_re):
    d = w_in.shape[1]
    w = w_in[l]
    z = lambda n: jnp.zeros((d, n), F32)
    o = 0
    cq, o = w[:, o:o + MLA_Q_LORA], o + MLA_Q_LORA
    ckv, o = w[:, o:o + MLA_KV_LORA], o + MLA_KV_LORA
    kr, o = w[:, o:o + MLA_ROPE], o + MLA_ROPE
    naq, o = w[:, o:o + GROUP_W], o + GROUP_W
    nak, o = w[:, o:o + GROUP_W], o + GROUP_W
    nav, o = w[:, o:o + GROUP_W], o + GROUP_W
    rest = w[:, o:]
    w_in_p = jnp.concatenate([cq, z(GROUP_W - MLA_Q_LORA), ckv, z(HEAD_DIM), kr, z(LANES - HEAD_DIM - MLA_ROPE),
                              _pad_heads_cols(naq), _pad_heads_cols(nak), _pad_heads_cols(nav), rest],
                             axis=1).astype(BF16)

    qk_dim = HEAD_DIM + MLA_ROPE
    wuq = mla_w_uq[l].reshape(MLA_Q_LORA, N_HEADS, qk_dim)
    wuq = jnp.concatenate([wuq, jnp.zeros((MLA_Q_LORA, N_HEADS, LANES - qk_dim), F32)], axis=-1)
    wuq = jnp.concatenate([wuq.reshape(MLA_Q_LORA, PAD_W), jnp.zeros((GROUP_W - MLA_Q_LORA, PAD_W), F32)], axis=0)
    wukv = mla_w_ukv[l].reshape(MLA_KV_LORA, N_HEADS, 2 * HEAD_DIM)
    pad64 = jnp.zeros((MLA_KV_LORA, N_HEADS, HEAD_DIM), F32)
    wk = jnp.concatenate([wukv[:, :, :HEAD_DIM], pad64], axis=-1).reshape(MLA_KV_LORA, PAD_W)
    wv = jnp.concatenate([wukv[:, :, HEAD_DIM:], pad64], axis=-1).reshape(MLA_KV_LORA, PAD_W)

    def per_head(parts):
        row = jnp.concatenate(parts + [jnp.zeros((LANES - sum(p.shape[0] for p in parts),), F32)])
        return jnp.tile(row, N_HEADS)[None, :]

    prep = {
        'wuq': wuq.astype(BF16), 'wk': wk.astype(BF16), 'wv': wv.astype(BF16),
        'gcq': jnp.concatenate([mla_g_cq[l], jnp.zeros((GROUP_W - MLA_Q_LORA,), F32)])[None, :],
        'gckv': mla_g_ckv[l][None, :],
        'gkr': jnp.concatenate([jnp.zeros((HEAD_DIM,), F32), mla_g_kr[l],
                                jnp.zeros((LANES - HEAD_DIM - MLA_ROPE,), F32)])[None, :],
        'gq': per_head([mla_g_qn[l], mla_g_qr[l]]),
        'gk': per_head([mla_g_kn[l]]),
        'mq': _seg_mean_matrix(PAD_W, [(0, HEAD_DIM), (HEAD_DIM, MLA_ROPE)]),
        'mk': _seg_mean_matrix(PAD_W, [(0, HEAD_DIM)]),
        'gnq': per_head([na_g_q[l]]),
        'gnk': per_head([na_g_k[l]]),
        'mn': _seg_mean_matrix(PAD_W, [(0, HEAD_DIM)]),
    }
    wo = w_out[l]
    ow = {
        'mla': _pad_heads_rows(wo[0:GROUP_W]).astype(BF16),
        'na': _pad_heads_rows(wo[GROUP_W:2 * GROUP_W]).astype(BF16),
        'ret': wo[2 * GROUP_W:3 * GROUP_W].astype(BF16),
        'hg': wo[3 * GROUP_W:4 * GROUP_W].astype(BF16),
    }
    n_r = MOE_GROUPS + MOE_EXPERTS
    wr = jnp.concatenate([moe_w_rg[l], moe_w_re[l], jnp.zeros((d, LANES - n_r), F32)], axis=1)
    wr_hi = wr.astype(BF16)
    rw = {
        'hi_lo': jnp.concatenate([wr_hi, (wr - wr_hi.astype(F32)).astype(BF16)], axis=1),
        'b': jnp.concatenate([moe_b_rg[l], moe_b_re[l], jnp.zeros((LANES - n_r,), F32)])[None, :],
    }
    return w_in_p, prep, ow, rw


def _layer(xx, modsel, lw, rope_c, rope_s, na_bias, rc, hc, hg_lb_l, ret_go, hg_go, layer, w1, w3, w2,
           n_lat, n_ctx, last):
    w_in_p, prep_w, ow, rw = lw
    b, nt, d = xx.shape
    n_lat_tiles = n_lat // TM
    p, qm, km, vm, qn, kn, vn = _inproj_prep(xx, modsel, w_in_p, rope_c, rope_s, prep_w, n_lat_tiles)
    y_mla = _mla_attn(qm, km, vm, n_lat, n_ctx)
    y_na = _na_attn(qn, kn, vn, na_bias, n_lat, n_ctx)
    y_ret, y_hg = _recurrent_mixers(p, rc, hc, hg_lb_l, ret_go, hg_go, n_lat, n_ctx)
    xx, route, counts = _outproj_router(xx, y_mla, y_na, y_ret, y_hg, modsel, ow, rw, n_lat_tiles)
    dest, block_expert, used, n_blocks = _moe_plan(route, counts, b * nt)
    x_rows = _dispatch(xx, modsel, dest, n_blocks * MOE_BLOCK, n_lat_tiles)
    y_rows = _moe_ffn(x_rows, block_expert, used, n_blocks, layer, w1, w3, w2)
    return _combine(xx, modsel, route, y_rows, dest, n_lat_tiles, n_lat_tiles if last else nt // TM)


def kernel(x, c, ctx, c_ctx, w_ada, b_ada, w_in, w_out, mla_g_cq, mla_g_ckv, mla_w_uq, mla_w_ukv, mla_g_qn, mla_g_qr, mla_g_kn, mla_g_kr, na_g_q, na_g_k, na_rpb, ret_g_out, hg_lb_raw, hg_g_out, moe_w_rg, moe_b_rg, moe_w_re, moe_b_re, moe_w1, moe_w3, moe_w2):
    b, n_lat, d = x.shape
    n_ctx = ctx.shape[1]
    depth = w_in.shape[0]
    assert n_lat % TM == 0 and n_ctx % TM == 0 and TM % GRID_W == 0
    assert n_lat // TM >= 3 and n_lat // GRID_W >= NA_WIN_ROWS
    assert w_in.shape[2] == MLA_Q_LORA + MLA_KV_LORA + MLA_ROPE + 12 * GROUP_W

    assert b + 1 <= SUBLANE_PAD_ROWS
    cc = jnp.concatenate([c, c_ctx[None, :], jnp.zeros((SUBLANE_PAD_ROWS - b - 1, d), F32)], axis=0)
    mods = _ada_all(cc, w_ada, b_ada)
    rope_c, rope_s = _rope_tables(n_lat, n_ctx)
    rc = _retention_consts()
    hc = _hgrn_consts()
    lb_w = jax.nn.softmax(hg_lb_raw.astype(F32), axis=0)
    hg_lb = jnp.cumsum(lb_w, axis=0) - lb_w[0:1]

    xx = jnp.concatenate([x, ctx], axis=1)
    tile_go = lambda g: jnp.tile(g, N_HEADS)[None, :]
    for l in range(depth):
        modsel = jnp.stack([mods[l, :b], jnp.broadcast_to(mods[l, b], (b, 6 * d))], axis=1).reshape(2 * b, 1, 6 * d)
        lw = _layer_weights(l, w_in, w_out, mla_g_cq, mla_g_ckv, mla_w_uq, mla_w_ukv, mla_g_qn, mla_g_qr,
                            mla_g_kn, mla_g_kr, na_g_q, na_g_k, moe_w_rg, moe_b_rg, moe_w_re, moe_b_re)
        xx = _layer(xx, modsel, lw, rope_c, rope_s, _na_bias_table(na_rpb[l]), rc, hc, hg_lb[l][None, :],
                    tile_go(ret_g_out[l]), tile_go(hg_g_out[l]),
                    l, moe_w1, moe_w3, moe_w2,
                    n_lat, n_ctx, l == depth - 1)
    return xx
```
